```python
import math
import jax
import jax.numpy as jnp
from jax import lax
import numpy as np

D_MODEL = 1024
BATCH = 8
SEQ = 2048
DEPTH = 2

D_MIX = D_MODEL
GROUP_W = D_MIX // 4
HEAD_DIM = 64
EPS = 1e-6
NEG_INF = -1e30

MLA_HEADS = GROUP_W // HEAD_DIM
MLA_NOPE = 64
MLA_ROPE = 32
MLA_V = GROUP_W // MLA_HEADS
MLA_Q_RANK = GROUP_W
MLA_KV_RANK = GROUP_W // 2
ROPE_THETA = 10000.0
Q_BLOCK = 128

S5_GROUP_CH = 16
S5_GROUPS = GROUP_W // S5_GROUP_CH
S5_STATE = 64
S5_DT_MIN = 1e-3
S5_DT_MAX = 1e-1

DIL_HEADS = GROUP_W // HEAD_DIM
DIL_PAIRS = ((128, 1), (512, 4), (2048, 16))
T5_BUCKETS = 32
T5_MAX_DIST = 2048

DN_HEADS = GROUP_W // HEAD_DIM
DN_DK = HEAD_DIM
DN_DV = HEAD_DIM
DN_CONV = 4
DN_CHUNK = 64
DN_DT_MIN = 1e-3
DN_DT_MAX = 1e-1

FFN_HIDDEN = -(-8 * D_MODEL // (3 * 256)) * 256

IN_SPLITS = (MLA_Q_RANK, MLA_KV_RANK, MLA_ROPE, GROUP_W, 3 * GROUP_W, 3 * GROUP_W, DN_HEADS, DN_HEADS, GROUP_W)
IN_COLS = MLA_Q_RANK + MLA_KV_RANK + MLA_ROPE + 8 * GROUP_W + 2 * DN_HEADS

kernel_name = 'hybrid_parallel_mixer_trunk'


def rms_norm(x, g):
    xf = x.astype(jnp.float32)
    y = xf * lax.rsqrt(jnp.mean(xf * xf, axis=-1, keepdims=True) + EPS)
    return (y * g.astype(jnp.float32)).astype(x.dtype)


def l2_norm(x):
    return x * lax.rsqrt(jnp.sum(x * x, axis=-1, keepdims=True) + EPS)


def split_cols(t, sizes):
    out, start = [], 0
    for s in sizes:
        out.append(t[..., start:start + s])
        start += s
    return out


def apply_rope(x, pos):
    half = x.shape[-1] // 2
    freqs = ROPE_THETA ** (-jnp.arange(half, dtype=jnp.float32) / half)
    ang = pos[:, None] * freqs[None, :]
    cos = jnp.cos(ang)[None, :, None, :]
    sin = jnp.sin(ang)[None, :, None, :]
    xf = x.astype(jnp.float32)
    x1, x2 = xf[..., :half], xf[..., half:]
    return jnp.concatenate([x1 * cos - x2 * sin, x1 * sin + x2 * cos], axis=-1).astype(x.dtype)


def causal_block_attention(q, k, v, scale):
    B, S, H, Dq = q.shape
    nb = S // Q_BLOCK
    qb = q.reshape(B, nb, Q_BLOCK, H, Dq).transpose(1, 0, 2, 3, 4)
    starts = jnp.arange(nb, dtype=jnp.int32) * Q_BLOCK
    kpos = jnp.arange(S, dtype=jnp.int32)

    def block(args):
        q_blk, s0 = args
        logits = jnp.einsum('bqhd,bkhd->bhqk', q_blk, k).astype(jnp.float32) * scale
        qpos = s0 + jnp.arange(Q_BLOCK, dtype=jnp.int32)
        mask = kpos[None, :] <= qpos[:, None]
        logits = jnp.where(mask[None, None], logits, NEG_INF)
        p = jax.nn.softmax(logits, axis=-1).astype(v.dtype)
        return jnp.einsum('bhqk,bkhd->bqhd', p, v)

    out = lax.map(block, (qb, starts))
    return out.transpose(1, 0, 2, 3, 4).reshape(B, S, H, v.shape[-1])


def mla_mixer(c_q, c_kv, k_rope, q_norm, kv_norm, w_uq, w_ukv, qk_q, qk_k):
    B, S, _ = c_q.shape
    H = MLA_HEADS
    dqk = MLA_NOPE + MLA_ROPE
    q = (rms_norm(c_q, q_norm) @ w_uq).reshape(B, S, H, dqk)
    kv = (rms_norm(c_kv, kv_norm) @ w_ukv).reshape(B, S, H, MLA_NOPE + MLA_V)
    k_nope, v = kv[..., :MLA_NOPE], kv[..., MLA_NOPE:]
    k = jnp.concatenate([k_nope, jnp.broadcast_to(k_rope[:, :, None, :], (B, S, H, MLA_ROPE))], axis=-1)
    q = rms_norm(q, qk_q)
    k = rms_norm(k, qk_k)
    pos = jnp.arange(S, dtype=jnp.float32)
    q = jnp.concatenate([q[..., :MLA_NOPE], apply_rope(q[..., MLA_NOPE:], pos)], axis=-1)
    k = jnp.concatenate([k[..., :MLA_NOPE], apply_rope(k[..., MLA_NOPE:], pos)], axis=-1)
    out = causal_block_attention(q, k, v, dqk ** -0.5)
    return out.reshape(B, S, H * MLA_V)


def complex_affine_combine(e1, e2):
    a1r, a1i, b1r, b1i = e1
    a2r, a2i, b2r, b2i = e2
    ar = a1r * a2r - a1i * a2i
    ai = a1r * a2i + a1i * a2r
    br = a2r * b1r - a2i * b1i + b2r
    bi = a2r * b1i + a2i * b1r + b2i
    return ar, ai, br, bi


def s5_mixer(u, lam_re, lam_im, log_dt, b_re, b_im, c_re, c_im, d_skip, w_glu):
    B, S, W = u.shape
    G, CG = S5_GROUPS, S5_GROUP_CH
    f32 = jnp.float32
    uf = u.astype(f32).reshape(B, S, G, CG)
    lr, li = lam_re.astype(f32), lam_im.astype(f32)
    dt = jnp.exp(log_dt.astype(f32))[:, None]
    mag = jnp.exp(lr * dt)
    ar, ai = mag * jnp.cos(li * dt), mag * jnp.sin(li * dt)
    den = lr * lr + li * li
    nr, ni = ar - 1.0, ai
    zr = (nr * lr + ni * li) / den
    zi = (ni * lr - nr * li) / den
    br, bi = b_re.astype(f32), b_im.astype(f32)
    bbr = zr[..., None] * br - zi[..., None] * bi
    bbi = zr[..., None] * bi + zi[..., None] * br
    xr = jnp.einsum('gpc,bsgc->bsgp', bbr, uf)
    xi = jnp.einsum('gpc,bsgc->bsgp', bbi, uf)
    ar_f = jnp.broadcast_to(ar, xr.shape)
    ai_f = jnp.broadcast_to(ai, xr.shape)
    _, _, hr, hi = lax.associative_scan(complex_affine_combine, (ar_f, ai_f, xr, xi), axis=1)
    y = jnp.einsum('gcp,bsgp->bsgc', c_re.astype(f32), hr) - jnp.einsum('gcp,bsgp->bsgc', c_im.astype(f32), hi)
    y = y.reshape(B, S, W) + d_skip.astype(f32) * u.astype(f32)
    y = y.astype(u.dtype)
    val, gate = jnp.split(y @ w_glu, 2, axis=-1)
    return val * jax.nn.sigmoid(gate)


def t5_bucket(dist):
    exact = T5_BUCKETS // 2
    df = jnp.maximum(dist, 1).astype(jnp.float32)
    large = exact + (jnp.log(df / exact) / math.log(T5_MAX_DIST / exact) * (T5_BUCKETS - exact)).astype(jnp.int32)
    large = jnp.minimum(large, T5_BUCKETS - 1)
    return jnp.where(dist < exact, dist, large)


def dilated_branch(q, k, v, bias_table, window, dilation, scale):
    B, S, H, D = q.shape
    span = window // dilation
    L = S // dilation
    nb = -(-L // span)
    Lp = nb * span

    def to_blocks(t):
        t = t.reshape(B, L, dilation, H, D).transpose(0, 2, 1, 3, 4)
        t = jnp.pad(t, ((0, 0), (0, 0), (0, Lp - L), (0, 0), (0, 0)))
        return t.reshape(B, dilation, nb, span, H, D)

    def with_prev(t):
        prev = jnp.pad(t, ((0, 0), (0, 0), (1, 0), (0, 0), (0, 0), (0, 0)))[:, :, :-1]
        return jnp.concatenate([prev, t], axis=3)

    qb = to_blocks(q)
    kk = with_prev(to_blocks(k))
    vv = with_prev(to_blocks(v))
    qi = jnp.arange(span, dtype=jnp.int32)[:, None] + span
    kj = jnp.arange(2 * span, dtype=jnp.int32)[None, :]
    delta = qi - kj
    band = (delta >= 0) & (delta <= span)
    before_start = (jnp.arange(nb)[:, None, None] == 0) & (kj < span)[None]
    valid = band[None] & (~before_start)
    bias = bias_table[t5_bucket(jnp.clip(delta, 0, span) * dilation)]
    bias = bias.transpose(2, 0, 1).astype(jnp.float32)
    logits = jnp.einsum('bgnqhd,bgnkhd->bgnhqk', qb, kk).astype(jnp.float32) * scale + bias
    logits = jnp.where(valid[None, None, :, None], logits, NEG_INF)
    m = jnp.max(logits, axis=-1)
    p = jnp.exp(logits - m[..., None])
    l = jnp.sum(p, axis=-1)
    o = jnp.einsum('bgnhqk,bgnkhd->bgnqhd', p.astype(v.dtype), vv).astype(jnp.float32)

    def from_blocks(t):
        t = t.reshape(B, dilation, Lp, *t.shape[4:])[:, :, :L]
        t = jnp.moveaxis(t, 1, 2)
        return t.reshape(B, S, *t.shape[3:])

    return from_blocks(o), from_blocks(jnp.swapaxes(m, -1, -2)), from_blocks(jnp.swapaxes(l, -1, -2))


def dilated_mixer(qkv, q_norm, k_norm, bias_table):
    B, S, _ = qkv.shape
    q, k, v = [t.reshape(B, S, DIL_HEADS, HEAD_DIM) for t in jnp.split(qkv, 3, axis=-1)]
    q = rms_norm(q, q_norm)
    k = rms_norm(k, k_norm)
    branches = [dilated_branch(q, k, v, bias_table, w, d, HEAD_DIM ** -0.5) for (w, d) in DIL_PAIRS]
    m_all = jnp.stack([br[1] for br in branches])
    l_all = jnp.stack([br[2] for br in branches])
    o_all = jnp.stack([br[0] for br in branches])
    wts = jnp.exp(m_all - jnp.max(m_all, axis=0, keepdims=True))
    num = jnp.sum(wts[..., None] * o_all, axis=0)
    den = jnp.sum(wts * l_all, axis=0)
    return (num / den[..., None]).astype(qkv.dtype).reshape(B, S, DIL_HEADS * HEAD_DIM)


def causal_depthwise_conv(x, w):
    K, C = w.shape
    return lax.conv_general_dilated(x, w[:, None, :].astype(x.dtype), window_strides=(1,),
                                    padding=[(K - 1, 0)], dimension_numbers=('NWC', 'WIO', 'NWC'),
                                    feature_group_count=C)


def chunked_gated_delta(q, k, v, g, beta):
    B, S, H, DK = q.shape
    DV = v.shape[-1]
    C = DN_CHUNK
    N = S // C

    def chunks(t):
        return jnp.moveaxis(t.reshape(B, N, C, H, *t.shape[3:]), 3, 1)

    q, k, v, g, beta = chunks(q), chunks(k), chunks(v), chunks(g), chunks(beta)
    gc = jnp.cumsum(g, axis=-1)
    causal = jnp.tril(jnp.ones((C, C), dtype=bool))
    strict = jnp.tril(jnp.ones((C, C), dtype=bool), -1)
    decay = jnp.exp(jnp.where(causal, gc[..., :, None] - gc[..., None, :], NEG_INF))
    kb = k * beta[..., None]
    lmat = jnp.where(strict, jnp.einsum('bhnid,bhnjd->bhnij', kb, k) * decay, 0.0)
    eye = jnp.eye(C, dtype=jnp.float32)
    rhs = jnp.concatenate([kb * jnp.exp(gc)[..., None], v * beta[..., None]], axis=-1)
    wu = lax.linalg.triangular_solve(eye + lmat, rhs, left_side=True, lower=True, unit_diagonal=True)
    w_c, u_c = wu[..., :DK], wu[..., DK:]
    a_qk = jnp.where(causal, jnp.einsum('bhnid,bhnjd->bhnij', q, k) * decay, 0.0)
    q_dec = q * jnp.exp(gc)[..., None]
    g_last = gc[..., -1]
    k_dec = k * jnp.exp(g_last[..., None] - gc)[..., None]
    xs = tuple(jnp.moveaxis(t, 2, 0) for t in (w_c, u_c, q_dec, a_qk, k_dec, jnp.exp(g_last)))

    def step(state, inp):
        w_i, u_i, q_i, a_i, k_i, d_i = inp
        v_new = u_i - jnp.einsum('bhck,bhkv->bhcv', w_i, state)
        o_i = jnp.einsum('bhck,bhkv->bhcv', q_i, state) + jnp.einsum('bhij,bhjv->bhiv', a_i, v_new)
        state = state * d_i[..., None, None] + jnp.einsum('bhck,bhcv->bhkv', k_i, v_new)
        return state, o_i

    s0 = jnp.zeros((B, H, DK, DV), jnp.float32)
    _, o = lax.scan(step, s0, xs)
    o = jnp.moveaxis(o, 0, 2)
    return jnp.moveaxis(o, 1, 3).reshape(B, S, H, DV)


def gated_delta_mixer(qkv, a, b, gate, conv_w, a_log, dt_bias, o_norm):
    B, S, _ = qkv.shape
    H = DN_HEADS
    f32 = jnp.float32
    qkv_c = jax.nn.silu(causal_depthwise_conv(qkv, conv_w))
    q, k, v = jnp.split(qkv_c, 3, axis=-1)
    q = l2_norm(q.reshape(B, S, H, DN_DK).astype(f32)) * (DN_DK ** -0.5)
    k = l2_norm(k.reshape(B, S, H, DN_DK).astype(f32))
    v = v.reshape(B, S, H, DN_DV).astype(f32)
    beta = jax.nn.sigmoid(b.astype(f32))
    g = -jnp.exp(a_log.astype(f32)) * jax.nn.softplus(a.astype(f32) + dt_bias.astype(f32))
    o = chunked_gated_delta(q, k, v, g, beta)
    o = rms_norm(o, o_norm) * jax.nn.silu(gate.astype(f32).reshape(B, S, H, DN_DV))
    return o.reshape(B, S, H * DN_DV).astype(qkv.dtype)


def setup_inputs(seed: int = 0) -> dict:
    key = jax.random.key(seed)
    ks = iter(jax.random.split(key, 32))
    f32 = jnp.float32
    L = DEPTH

    def nrm(shape, scale):
        return jax.random.normal(next(ks), shape, f32) * scale

    def gain(shape):
        return 1.0 + nrm(shape, 0.02)

    def unif(shape, lo, hi):
        return jax.random.uniform(next(ks), shape, f32, lo, hi)

    G, P, CG = S5_GROUPS, S5_STATE, S5_GROUP_CH
    x = nrm((BATCH, SEQ, D_MODEL), 1.0)
    attn_norm = gain((L, D_MODEL))
    w_in = nrm((L, D_MODEL, IN_COLS), D_MODEL ** -0.5)
    w_out = nrm((L, D_MIX, D_MODEL), D_MIX ** -0.5)
    mla_q_norm = gain((L, MLA_Q_RANK))
    mla_kv_norm = gain((L, MLA_KV_RANK))
    mla_w_uq = nrm((L, MLA_Q_RANK, MLA_HEADS * (MLA_NOPE + MLA_ROPE)), MLA_Q_RANK ** -0.5)
    mla_w_ukv = nrm((L, MLA_KV_RANK, MLA_HEADS * (MLA_NOPE + MLA_V)), MLA_KV_RANK ** -0.5)
    mla_qk_q = gain((L, MLA_NOPE + MLA_ROPE))
    mla_qk_k = gain((L, MLA_NOPE + MLA_ROPE))
    s5_lambda_re = -0.5 * (1.0 + nrm((L, G, P), 0.02))
    s5_lambda_im = jnp.pi * jnp.arange(P, dtype=f32)[None, None, :] + nrm((L, G, P), 0.01)
    s5_log_dt = unif((L, G), math.log(S5_DT_MIN), math.log(S5_DT_MAX))
    s5_b_re = nrm((L, G, P, CG), (2 * CG) ** -0.5)
    s5_b_im = nrm((L, G, P, CG), (2 * CG) ** -0.5)
    s5_c_re = nrm((L, G, CG, P), P ** -0.5)
    s5_c_im = nrm((L, G, CG, P), P ** -0.5)
    s5_d = nrm((L, GROUP_W), 1.0)
    s5_w_glu = nrm((L, GROUP_W, 2 * GROUP_W), GROUP_W ** -0.5)
    dil_q_norm = gain((L, HEAD_DIM))
    dil_k_norm = gain((L, HEAD_DIM))
    t5_bias = nrm((T5_BUCKETS, DIL_HEADS), 0.2)
    dn_conv = nrm((L, DN_CONV, 3 * GROUP_W), DN_CONV ** -0.5)
    dn_a_log = jnp.log(unif((L, DN_HEADS), 1.0, 16.0))
    dt = jnp.exp(unif((L, DN_HEADS), math.log(DN_DT_MIN), math.log(DN_DT_MAX)))
    dn_dt_bias = dt + jnp.log(-jnp.expm1(-dt))
    dn_o_norm = gain((L, DN_DV))
    ffn_norm = gain((L, D_MODEL))
    ffn_w1 = nrm((L, D_MODEL, FFN_HIDDEN), D_MODEL ** -0.5)
    ffn_w3 = nrm((L, D_MODEL, FFN_HIDDEN), D_MODEL ** -0.5)
    ffn_w2 = nrm((L, FFN_HIDDEN, D_MODEL), FFN_HIDDEN ** -0.5)
    return {'x': x, 'attn_norm': attn_norm, 'w_in': w_in, 'w_out': w_out,
            'mla_q_norm': mla_q_norm, 'mla_kv_norm': mla_kv_norm, 'mla_w_uq': mla_w_uq,
            'mla_w_ukv': mla_w_ukv, 'mla_qk_q': mla_qk_q, 'mla_qk_k': mla_qk_k,
            's5_lambda_re': s5_lambda_re, 's5_lambda_im': s5_lambda_im, 's5_log_dt': s5_log_dt,
            's5_b_re': s5_b_re, 's5_b_im': s5_b_im, 's5_c_re': s5_c_re, 's5_c_im': s5_c_im,
            's5_d': s5_d, 's5_w_glu': s5_w_glu, 'dil_q_norm': dil_q_norm, 'dil_k_norm': dil_k_norm,
            't5_bias': t5_bias, 'dn_conv': dn_conv, 'dn_a_log': dn_a_log, 'dn_dt_bias': dn_dt_bias,
            'dn_o_norm': dn_o_norm, 'ffn_norm': ffn_norm, 'ffn_w1': ffn_w1, 'ffn_w3': ffn_w3,
            'ffn_w2': ffn_w2}


def reference(x, attn_norm, w_in, w_out, mla_q_norm, mla_kv_norm, mla_w_uq, mla_w_ukv, mla_qk_q, mla_qk_k,
              s5_lambda_re, s5_lambda_im, s5_log_dt, s5_b_re, s5_b_im, s5_c_re, s5_c_im, s5_d, s5_w_glu,
              dil_q_norm, dil_k_norm, t5_bias, dn_conv, dn_a_log, dn_dt_bias, dn_o_norm,
              ffn_norm, ffn_w1, ffn_w3, ffn_w2):
    h = x
    for l in range(DEPTH):
        n = rms_norm(h, attn_norm[l])
        proj = n @ w_in[l]
        c_q, c_kv, k_rope, u_s5, qkv_dil, qkv_dn, a_dn, b_dn, gate_dn = split_cols(proj, IN_SPLITS)
        y_mla = mla_mixer(c_q, c_kv, k_rope, mla_q_norm[l], mla_kv_norm[l], mla_w_uq[l], mla_w_ukv[l],
                          mla_qk_q[l], mla_qk_k[l])
        y_s5 = s5_mixer(u_s5, s5_lambda_re[l], s5_lambda_im[l], s5_log_dt[l], s5_b_re[l], s5_b_im[l],
                        s5_c_re[l], s5_c_im[l], s5_d[l], s5_w_glu[l])
        y_dil = dilated_mixer(qkv_dil, dil_q_norm[l], dil_k_norm[l], t5_bias)
        y_dn = gated_delta_mixer(qkv_dn, a_dn, b_dn, gate_dn, dn_conv[l], dn_a_log[l], dn_dt_bias[l],
                                 dn_o_norm[l])
        mixed = jnp.concatenate([y_mla, y_s5, y_dil, y_dn], axis=-1)
        h = h + mixed @ w_out[l]
        n = rms_norm(h, ffn_norm[l])
        h = h + (jax.nn.silu(n @ ffn_w1[l]) * (n @ ffn_w3[l])) @ ffn_w2[l]
    return h
```

```python
import functools
import math

import jax
import jax.numpy as jnp
import numpy as np
from jax import lax
from jax.experimental import pallas as pl
from jax.experimental.pallas import tpu as pltpu

F32 = jnp.float32
BF16 = jnp.bfloat16
HIGHEST = lax.Precision.HIGHEST

D_MODEL = 1024
GROUP_W = 256
HEAD_DIM = 64
N_HEADS = 4
EPS = 1e-6
NEG_INF = -1e30

MLA_NOPE = 64
MLA_ROPE = 32
MLA_DQK = MLA_NOPE + MLA_ROPE
MLA_KV_RANK = 128
ROPE_THETA = 10000.0

S5_GROUP_CH = 16
S5_GROUPS = 16
S5_STATE = 64
S5_WIDTH = S5_GROUPS * S5_STATE

DIL_PAIRS = ((128, 1), (512, 4), (2048, 16))
DIL_SPAN = 128
T5_BUCKETS = 32
T5_MAX_DIST = 2048

DN_CONV = 4
DN_CHUNK = 64

FFN_HIDDEN = 2816
FFN_CHUNK = 256

VMEM_LIMIT_BYTES = 56 * 1024 * 1024
LANES = 128

MLA_SLAB = 512
S5_SLAB = 256
DIL_SLAB = 768
DN_SLAB = 1152
PROJ_COLS = MLA_SLAB + S5_SLAB + DIL_SLAB + DN_SLAB


def _dot(a, b, precision=None):
    return jnp.dot(a, b, preferred_element_type=F32, precision=precision)


def _dot_nt(a, b, precision=None):
    return lax.dot_general(a, b, (((1,), (1,)), ((), ())), preferred_element_type=F32, precision=precision)


def _const_spec(shape):
    nd = len(shape)
    return pl.BlockSpec(shape, lambda *_: (0,) * nd)


def _params(*sem):
    return pltpu.CompilerParams(dimension_semantics=sem, vmem_limit_bytes=VMEM_LIMIT_BYTES)


def _proj_body(x_ref, g_ref, w_ref, mla_ref, s5_ref, dil_ref, dn_ref):
    x = x_ref[...]
    n = x * lax.rsqrt(jnp.mean(x * x, axis=-1, keepdims=True) + EPS) * g_ref[...]
    nb = n.astype(BF16)
    start = 0
    for ref in (mla_ref, s5_ref, dil_ref, dn_ref):
        width = ref.shape[-1]
        ref[...] = _dot(nb, w_ref[:, start:start + width])
        start += width


def _proj(xt, gain, w_big, tm=512):
    t = xt.shape[0]
    widths = (MLA_SLAB, S5_SLAB, DIL_SLAB, DN_SLAB)
    return pl.pallas_call(
        _proj_body,
        grid=(t // tm,),
        in_specs=[pl.BlockSpec((tm, D_MODEL), lambda i: (i, 0)),
                  _const_spec((1, D_MODEL)),
                  _const_spec((D_MODEL, PROJ_COLS))],
        out_specs=[pl.BlockSpec((tm, w), lambda i: (i, 0)) for w in widths],
        out_shape=[jax.ShapeDtypeStruct((t, w), F32) for w in widths],
        compiler_params=_params("parallel"),
        name="proj",
    )(xt, gain, w_big)


def _out_ffn_body(h_ref, y0_ref, y1_ref, y2_ref, y3_ref, wo_ref, g_ref, w1_ref, w3_ref, w2_ref, o_ref, acc_ref):
    h = h_ref[...]
    for i, y_ref in enumerate((y0_ref, y1_ref, y2_ref, y3_ref)):
        h = h + _dot(y_ref[...].astype(BF16), wo_ref[i * GROUP_W:(i + 1) * GROUP_W, :])
    n = h * lax.rsqrt(jnp.mean(h * h, axis=-1, keepdims=True) + EPS) * g_ref[...]
    nb = n.astype(BF16)
    acc_ref[...] = h

    def hidden_chunk(c, carry):
        cols = pl.ds(pl.multiple_of(c * FFN_CHUNK, FFN_CHUNK), FFN_CHUNK)
        a = _dot(nb, w1_ref[:, cols])
        b = _dot(nb, w3_ref[:, cols])
        z = (a * jax.nn.sigmoid(a) * b).astype(BF16)
        acc_ref[...] += _dot(z, w2_ref[cols, :])
        return carry
    lax.fori_loop(0, FFN_HIDDEN // FFN_CHUNK, hidden_chunk, 0)
    o_ref[...] = acc_ref[...]


def _out_ffn(h, ys, w_out, gain, w1, w3, w2, tm=256):
    t = h.shape[0]
    row = lambda w: pl.BlockSpec((tm, w), lambda i: (i, 0))
    return pl.pallas_call(
        _out_ffn_body,
        grid=(t // tm,),
        in_specs=[row(D_MODEL)] + [row(GROUP_W)] * 4 + [
            _const_spec((D_MODEL, D_MODEL)), _const_spec((1, D_MODEL)),
            _const_spec((D_MODEL, FFN_HIDDEN)), _const_spec((D_MODEL, FFN_HIDDEN)),
            _const_spec((FFN_HIDDEN, D_MODEL))],
        out_specs=row(D_MODEL),
        out_shape=jax.ShapeDtypeStruct((t, D_MODEL), F32),
        scratch_shapes=[pltpu.VMEM((tm, D_MODEL), F32)],
        compiler_params=_params("parallel"),
        name="out_ffn",
    )(h, *ys, w_out, gain, w1, w3, w2)


MLA_BLOCK = 256


def _mla_body(x_ref, qn_ref, kvn_ref, wuq_ref, wuk_ref, wuv_ref, gq_ref, gk_ref, cos_ref, sin_ref,
              o_ref, k_scr, v_scr, *, seq):
    blk = MLA_BLOCK
    qi = pl.program_id(1)
    lane = lax.broadcasted_iota(jnp.int32, (blk, LANES), 1)

    def rope(x, c, s):
        rot = jnp.where(lane < MLA_NOPE + MLA_ROPE // 2, pltpu.roll(x, LANES - MLA_ROPE // 2, 1),
                        pltpu.roll(x, MLA_ROPE // 2, 1))
        return x * c + rot * s

    def norm_head(x, g):
        ssq = jnp.sum(x * x, axis=-1, keepdims=True)
        return x * lax.rsqrt(ssq * (1.0 / MLA_DQK) + EPS) * g

    @pl.when(qi == 0)
    def _prepare_keys_values():
        def tile(i, carry):
            r0 = pl.multiple_of(i * blk, blk)
            rows = pl.ds(r0, blk)
            ckv = x_ref[rows, 256:384]
            k_rope = x_ref[rows, 384:512]
            kvn = ckv * lax.rsqrt(jnp.mean(ckv * ckv, axis=-1, keepdims=True) + EPS) * kvn_ref[...]
            kvn = kvn.astype(BF16)
            k_nope = _dot(kvn, wuk_ref[...])
            v = _dot(kvn, wuv_ref[...])
            k_rope = pltpu.roll(k_rope, MLA_NOPE, 1)
            c = cos_ref[rows, :]
            s = sin_ref[rows, :]
            for h in range(N_HEADS):
                cols = slice(h * LANES, (h + 1) * LANES)
                k = rope(norm_head(k_nope[:, cols] + k_rope, gk_ref[...]), c, s)
                k_scr[h, rows, :] = k.astype(BF16)
                v_scr[h, rows, :] = v[:, cols].astype(BF16)
            return carry
        lax.fori_loop(0, seq // blk, tile, 0)

    r0 = pl.multiple_of(qi * blk, blk)
    rows = pl.ds(r0, blk)
    cq = x_ref[rows, 0:256]
    qn = cq * lax.rsqrt(jnp.mean(cq * cq, axis=-1, keepdims=True) + EPS) * qn_ref[...]
    q_all = _dot(qn.astype(BF16), wuq_ref[...])
    c = cos_ref[rows, :]
    s = sin_ref[rows, :]
    row_pos = r0 + lax.broadcasted_iota(jnp.int32, (blk, blk), 0)
    col_pos = lax.broadcasted_iota(jnp.int32, (blk, blk), 1)
    outs = []
    for h in range(N_HEADS):
        q = rope(norm_head(q_all[:, h * LANES:(h + 1) * LANES], gq_ref[...]), c, s)
        q = (q * (MLA_DQK ** -0.5)).astype(BF16)

        def kv_step(j, carry, q=q, h=h):
            m, l, acc = carry
            c0 = pl.multiple_of(j * blk, blk)
            keys = pl.ds(c0, blk)
            logits = _dot_nt(q, k_scr[h, keys, :])
            logits = jnp.where(col_pos + c0 <= row_pos, logits, NEG_INF)
            m_new = jnp.maximum(m, jnp.max(logits, axis=-1, keepdims=True))
            alpha = jnp.exp(m - m_new)
            p = jnp.exp(logits - m_new)
            l = alpha * l + jnp.sum(p, axis=-1, keepdims=True)
            acc = alpha * acc + _dot(p.astype(BF16), v_scr[h, keys, :])
            return m_new, l, acc

        init = (jnp.full((blk, 1), NEG_INF, F32), jnp.zeros((blk, 1), F32), jnp.zeros((blk, LANES), F32))
        _, l, acc = lax.fori_loop(0, qi + 1, kv_step, init)
        outs.append(acc / l)
    o_ref[rows, 0:LANES] = outs[0] + pltpu.roll(outs[1], HEAD_DIM, 1)
    o_ref[rows, LANES:2 * LANES] = outs[2] + pltpu.roll(outs[3], HEAD_DIM, 1)


def _mla(slab, qn, kvn, wuq, wuk, wuv, gq, gk, cos, sin, *, batch, seq):
    nq = seq // MLA_BLOCK
    return pl.pallas_call(
        functools.partial(_mla_body, seq=seq),
        grid=(batch, nq),
        in_specs=[pl.BlockSpec((seq, MLA_SLAB), lambda b, i: (0, b)),
                  _const_spec((1, 256)), _const_spec((1, MLA_KV_RANK)),
                  _const_spec((256, N_HEADS * LANES)), _const_spec((MLA_KV_RANK, N_HEADS * LANES)),
                  _const_spec((MLA_KV_RANK, N_HEADS * LANES)),
                  _const_spec((1, LANES)), _const_spec((1, LANES)),
                  _const_spec((seq, LANES)), _const_spec((seq, LANES))],
        out_specs=pl.BlockSpec((seq, GROUP_W), lambda b, i: (0, b)),
        out_shape=jax.ShapeDtypeStruct((seq, batch * GROUP_W), F32),
        scratch_shapes=[pltpu.VMEM((N_HEADS, seq, LANES), BF16), pltpu.VMEM((N_HEADS, seq, LANES), BF16)],
        compiler_params=_params("parallel", "arbitrary"),
        name="mla",
    )(slab, qn, kvn, wuq, wuk, wuv, gq, gk, cos, sin)


def _s5_body(u_ref, lre_ref, lim_ref, ldt_ref, bre_ref, bim_ref, cre_ref, cim_ref, d_ref, wglu_ref, o_ref,
             a_scr, bbar_scr, h_scr, x_scr, *, batch, steps):
    @pl.when(pl.program_id(0) == 0)
    def _discretise():
        lr = lre_ref[...]
        li = lim_ref[...]
        dt = jnp.exp(ldt_ref[...])
        mag = jnp.exp(lr * dt)
        ar = mag * jnp.cos(li * dt)
        ai = mag * jnp.sin(li * dt)
        den = lr * lr + li * li
        nr = ar - 1.0
        zr = (nr * lr + ai * li) / den
        zi = (ai * lr - nr * li) / den
        a_scr[0] = jnp.broadcast_to(ar, (batch, S5_WIDTH))
        a_scr[1] = jnp.broadcast_to(ai, (batch, S5_WIDTH))
        bre = bre_ref[...]
        bim = bim_ref[...]
        bbar_scr[0] = (zr * bre - zi * bim).astype(BF16)
        bbar_scr[1] = (zr * bim + zi * bre).astype(BF16)
        h_scr[...] = jnp.zeros_like(h_scr)

    u = u_ref[...]
    ub = u.astype(BF16)
    x_scr[0] = _dot(ub, bbar_scr[0])
    x_scr[1] = _dot(ub, bbar_scr[1])
    ar = a_scr[0]
    ai = a_scr[1]

    def step(t, carry):
        hr, hi = carry
        rows = pl.ds(pl.multiple_of(t * batch, batch), batch)
        nhr = ar * hr - ai * hi + x_scr[0, rows, :]
        nhi = ar * hi + ai * hr + x_scr[1, rows, :]
        x_scr[0, rows, :] = nhr
        x_scr[1, rows, :] = nhi
        return nhr, nhi

    hr, hi = lax.fori_loop(0, steps, step, (h_scr[0], h_scr[1]), unroll=8)
    h_scr[0] = hr
    h_scr[1] = hi
    y = _dot(x_scr[0].astype(BF16), cre_ref[...]) - _dot(x_scr[1].astype(BF16), cim_ref[...]) + d_ref[...] * u
    z = _dot(y.astype(BF16), wglu_ref[...])
    o_ref[...] = z[:, :GROUP_W] * jax.nn.sigmoid(z[:, GROUP_W:])


def _s5(u, lre, lim, ldt, bre, bim, cre, cim, d, wglu, *, batch, seq, steps=128):
    rows = steps * batch
    return pl.pallas_call(
        functools.partial(_s5_body, batch=batch, steps=steps),
        grid=(seq // steps,),
        in_specs=[pl.BlockSpec((rows, S5_SLAB), lambda i: (i, 0)),
                  _const_spec((1, S5_WIDTH)), _const_spec((1, S5_WIDTH)), _const_spec((1, S5_WIDTH)),
                  _const_spec((GROUP_W, S5_WIDTH)), _const_spec((GROUP_W, S5_WIDTH)),
                  _const_spec((S5_WIDTH, GROUP_W)), _const_spec((S5_WIDTH, GROUP_W)),
                  _const_spec((1, GROUP_W)), _const_spec((GROUP_W, 2 * GROUP_W))],
        out_specs=pl.BlockSpec((rows, GROUP_W), lambda i: (i, 0)),
        out_shape=jax.ShapeDtypeStruct((seq * batch, GROUP_W), F32),
        scratch_shapes=[pltpu.VMEM((2, batch, S5_WIDTH), F32), pltpu.VMEM((2, GROUP_W, S5_WIDTH), BF16),
                        pltpu.VMEM((2, batch, S5_WIDTH), F32), pltpu.VMEM((2, rows, S5_WIDTH), F32)],
        compiler_params=_params("arbitrary"),
        name="s5",
    )(u, lre, lim, ldt, bre, bim, cre, cim, d, wglu)


DIL_TILE = 256


def _pair_norm(x, gain2, lo_half):
    sq = x * x
    tot = jnp.sum(sq, axis=-1, keepdims=True)
    lo = jnp.sum(jnp.where(lo_half, sq, 0.0), axis=-1, keepdims=True)
    ms = jnp.where(lo_half, lo, tot - lo) * (1.0 / HEAD_DIM)
    return x * lax.rsqrt(ms + EPS) * gain2


def _dil_body(x_ref, gq_ref, gk_ref, bias_ref, o_ref, q_scr, k_scr, v_scr, qd, kd, vd, ob, mb, lb, oa, ma, la,
              *, seq):
    span = DIL_SPAN
    n_blocks = seq // span
    n_tiles = GROUP_W // LANES
    lo_tile = lax.broadcasted_iota(jnp.int32, (DIL_TILE, LANES), 1) < HEAD_DIM
    lo_blk = lax.broadcasted_iota(jnp.int32, (span, LANES), 1) < HEAD_DIM

    def norm_tile(i, carry):
        rows = pl.ds(pl.multiple_of(i * DIL_TILE, DIL_TILE), DIL_TILE)
        for j in range(n_tiles):
            cols = lambda part: slice(part * GROUP_W + j * LANES, part * GROUP_W + (j + 1) * LANES)
            q = _pair_norm(x_ref[rows, cols(0)], gq_ref[...], lo_tile)
            q_scr[j, rows, :] = q * (HEAD_DIM ** -0.5)
            k_scr[j, rows, :] = _pair_norm(x_ref[rows, cols(1)], gk_ref[...], lo_tile)
            v_scr[j, rows, :] = x_ref[rows, cols(2)]
        return carry
    lax.fori_loop(0, seq // DIL_TILE, norm_tile, 0)

    for bi, (window, dil) in enumerate(DIL_PAIRS):
        sub_len = seq // dil
        nb = sub_len // span
        chunks = [(r, c) for r in range(dil) for c in range(nb)]

        def natural(r, c, dil=dil):
            return pl.ds(r + dil * c * span, span, stride=dil) if dil > 1 else pl.ds(c * span, span)

        if dil > 1:
            for r, c in chunks:
                dst = pl.ds((r * nb + c) * span, span)
                for j in range(n_tiles):
                    qd[j, dst, :] = q_scr[j, natural(r, c), :]
                    kd[j, dst, :] = k_scr[j, natural(r, c), :]
                    vd[j, dst, :] = v_scr[j, natural(r, c), :]
            q_src, k_src, v_src = qd, kd, vd
        else:
            q_src, k_src, v_src = q_scr, k_scr, v_scr
        o_dst, m_dst, l_dst = (oa, ma, la) if bi == 0 else (ob, mb, lb)

        def block(t, carry, bi=bi, nb=nb, q_src=q_src, k_src=k_src, v_src=v_src,
                  o_dst=o_dst, m_dst=m_dst, l_dst=l_dst):
            rows = pl.ds(pl.multiple_of(t * span, span), span)
            prev = pl.ds(pl.multiple_of(jnp.maximum(t - 1, 0) * span, span), span)
            has_prev = (t & (nb - 1)) != 0
            for j in range(n_tiles):
                q2 = q_src[j, rows, :]
                kc = k_src[j, rows, :].astype(BF16)
                vc = v_src[j, rows, :].astype(BF16)
                if nb > 1:
                    kp = k_src[j, prev, :].astype(BF16)
                    vp = v_src[j, prev, :].astype(BF16)
                res = []
                for a in range(2):
                    h = 2 * j + a
                    qa = jnp.where(lo_blk if a == 0 else jnp.logical_not(lo_blk), q2, 0.0).astype(BF16)
                    lc = _dot_nt(qa, kc) + bias_ref[(bi * N_HEADS + h) * 2 + 1]
                    m = jnp.max(lc, axis=-1, keepdims=True)
                    if nb > 1:
                        lp = _dot_nt(qa, kp) + bias_ref[(bi * N_HEADS + h) * 2]
                        lp = jnp.where(has_prev, lp, NEG_INF)
                        m = jnp.maximum(m, jnp.max(lp, axis=-1, keepdims=True))
                    pc = jnp.exp(lc - m)
                    l = jnp.sum(pc, axis=-1, keepdims=True)
                    o = _dot(pc.astype(BF16), vc)
                    if nb > 1:
                        pp = jnp.exp(lp - m)
                        l = l + jnp.sum(pp, axis=-1, keepdims=True)
                        o = o + _dot(pp.astype(BF16), vp)
                    res.append((o, m, l))
                (o0, m0, l0), (o1, m1, l1) = res
                o_dst[j, rows, :] = jnp.where(lo_blk, o0, o1)
                m_dst[j, rows, :] = jnp.where(lo_blk, m0, m1)
                l_dst[j, rows, :] = jnp.where(lo_blk, l0, l1)
            return carry
        lax.fori_loop(0, n_blocks, block, 0)

        if bi > 0:
            for r, c in chunks:
                src = pl.ds((r * nb + c) * span, span)
                nat = natural(r, c)
                for j in range(n_tiles):
                    m_old, m_in = ma[j, nat, :], mb[j, src, :]
                    m_new = jnp.maximum(m_old, m_in)
                    w_old, w_in = jnp.exp(m_old - m_new), jnp.exp(m_in - m_new)
                    oa[j, nat, :] = w_old * oa[j, nat, :] + w_in * ob[j, src, :]
                    la[j, nat, :] = w_old * la[j, nat, :] + w_in * lb[j, src, :]
                    ma[j, nat, :] = m_new

    def finish(i, carry):
        rows = pl.ds(pl.multiple_of(i * DIL_TILE, DIL_TILE), DIL_TILE)
        for j in range(n_tiles):
            o_ref[rows, j * LANES:(j + 1) * LANES] = oa[j, rows, :] / la[j, rows, :]
        return carry
    lax.fori_loop(0, seq // DIL_TILE, finish, 0)


def _dil(slab, gq2, gk2, bias, *, batch, seq):
    big = lambda: pltpu.VMEM((GROUP_W // LANES, seq, LANES), F32)
    return pl.pallas_call(
        functools.partial(_dil_body, seq=seq),
        grid=(batch,),
        in_specs=[pl.BlockSpec((seq, DIL_SLAB), lambda b: (0, b)),
                  _const_spec((1, LANES)), _const_spec((1, LANES)),
                  _const_spec(bias.shape)],
        out_specs=pl.BlockSpec((seq, GROUP_W), lambda b: (0, b)),
        out_shape=jax.ShapeDtypeStruct((seq, batch * GROUP_W), F32),
        scratch_shapes=[big() for _ in range(12)],
        compiler_params=_params("parallel"),
        name="dilated",
    )(slab, gq2, gk2, bias)


def _t5_bucket(dist):
    exact = T5_BUCKETS // 2
    df = jnp.maximum(dist, 1).astype(F32)
    large = exact + (jnp.log(df / exact) / math.log(T5_MAX_DIST / exact) * (T5_BUCKETS - exact)).astype(jnp.int32)
    large = jnp.minimum(large, T5_BUCKETS - 1)
    return jnp.where(dist < exact, dist, large)


def _dil_bias(table):
    span = DIL_SPAN
    i = jnp.arange(span, dtype=jnp.int32)[:, None]
    j = jnp.arange(span, dtype=jnp.int32)[None, :]
    out = []
    for _, dil in DIL_PAIRS:
        for h in range(N_HEADS):
            for delta, valid in ((span + i - j, j >= i), (i - j, j <= i)):
                vals = table[_t5_bucket(jnp.clip(delta, 0, span) * dil), h].astype(F32)
                out.append(jnp.where(valid, vals, NEG_INF))
    return jnp.stack(out)


DN_TILE = 256
DN_SOLVE_PRECISION = HIGHEST


def _softplus(x):
    return jnp.maximum(x, 0.0) + jnp.log1p(jnp.exp(-jnp.abs(x)))


def _pair_l2(x, lo_half):
    sq = x * x
    tot = jnp.sum(sq, axis=-1, keepdims=True)
    lo = jnp.sum(jnp.where(lo_half, sq, 0.0), axis=-1, keepdims=True)
    return x * lax.rsqrt(jnp.where(lo_half, lo, tot - lo) + EPS)


def _dn_body(x_ref, cw_ref, alog_ref, dtb_ref, on_ref, o_ref, q_scr, k_scr, v_scr, g_scr, b_scr, s_scr, *, seq):
    c = DN_CHUNK
    w = GROUP_W
    lo_tile = lax.broadcasted_iota(jnp.int32, (DN_TILE, LANES), 1) < HEAD_DIM

    def prep_tile(i, carry):
        r0 = pl.multiple_of(i * DN_TILE, DN_TILE)
        rows = pl.ds(r0, DN_TILE)
        cur = x_ref[rows, 0:3 * w]
        halo_rows = pl.ds(pl.multiple_of(jnp.maximum(r0 - 8, 0), 8), 8)
        halo = jnp.where(i > 0, x_ref[halo_rows, 0:3 * w], 0.0)
        ext = jnp.concatenate([halo, cur], axis=0)
        acc = cw_ref[DN_CONV - 1:DN_CONV, :] * cur
        for j in range(DN_CONV - 1):
            acc = acc + cw_ref[j:j + 1, :] * pltpu.roll(ext, DN_CONV - 1 - j, 0)[8:, :]
        y = acc * jax.nn.sigmoid(acc)
        for j in range(w // LANES):
            cols = lambda part: slice(part * w + j * LANES, part * w + (j + 1) * LANES)
            q_scr[rows, j * LANES:(j + 1) * LANES] = _pair_l2(y[:, cols(0)], lo_tile) * (HEAD_DIM ** -0.5)
            k_scr[rows, j * LANES:(j + 1) * LANES] = _pair_l2(y[:, cols(1)], lo_tile)
            v_scr[rows, j * LANES:(j + 1) * LANES] = y[:, cols(2)]
        ab = x_ref[rows, 3 * w:3 * w + LANES]
        g_scr[rows, :] = -jnp.exp(alog_ref[...]) * _softplus(ab + dtb_ref[...])
        b_scr[rows, :] = pltpu.roll(jax.nn.sigmoid(ab), LANES - N_HEADS, 1)
        return carry
    lax.fori_loop(0, seq // DN_TILE, prep_tile, 0)

    ri = lax.broadcasted_iota(jnp.int32, (c, w), 0)
    ci = lax.broadcasted_iota(jnp.int32, (c, w), 1)
    cj = jnp.bitwise_and(ci, HEAD_DIM - 1)
    causal = ri >= cj
    strict = ri > cj
    eye4 = (ri == cj).astype(F32)
    bi_r = lax.broadcasted_iota(jnp.int32, (w, w), 0)
    bi_c = lax.broadcasted_iota(jnp.int32, (w, w), 1)
    same_head = lax.shift_right_logical(bi_r, 6) == lax.shift_right_logical(bi_c, 6)
    block_mask = same_head.astype(F32)
    expand = (lax.broadcasted_iota(jnp.int32, (LANES, w), 0)
              == lax.shift_right_logical(lax.broadcasted_iota(jnp.int32, (LANES, w), 1), 6)).astype(F32)
    tril = (lax.broadcasted_iota(jnp.int32, (c, c), 0) >= lax.broadcasted_iota(jnp.int32, (c, c), 1)).astype(F32)

    def head_blocks(b):
        return jnp.concatenate([b] * N_HEADS, axis=0) * block_mask

    def per_head(a, b, precision=None):
        bd = head_blocks(b)
        if precision is None:
            return _dot(a.astype(BF16), bd.astype(BF16))
        return _dot(a, bd, precision)

    s_scr[...] = jnp.zeros_like(s_scr)

    def chunk(n, carry):
        rows = pl.ds(pl.multiple_of(n * c, c), c)
        q = q_scr[rows, :]
        k = k_scr[rows, :]
        v = v_scr[rows, :]
        gc = _dot(_dot(tril, g_scr[rows, :], HIGHEST), expand, HIGHEST)
        beta = _dot(b_scr[rows, :], expand, HIGHEST)
        g_row = jnp.sum(gc * eye4, axis=0, keepdims=True)
        g_last = gc[c - 1:c, :]
        decay = jnp.exp(jnp.where(causal, gc - g_row, NEG_INF))
        kb = k * beta
        k_bd = head_blocks(k).astype(BF16)
        lmat = jnp.where(strict, _dot_nt(kb.astype(BF16), k_bd) * decay, 0.0)
        a_qk = jnp.where(causal, _dot_nt(q.astype(BF16), k_bd) * decay, 0.0)
        p = -lmat
        t_inv = eye4 + p
        for _ in range(5):
            p = per_head(p, p, DN_SOLVE_PRECISION)
            t_inv = t_inv + per_head(t_inv, p, DN_SOLVE_PRECISION)
        eg = jnp.exp(gc)
        w_c = per_head(t_inv, kb * eg, DN_SOLVE_PRECISION)
        u_c = per_head(t_inv, v * beta, DN_SOLVE_PRECISION)
        state = s_scr[...]
        state_b = state.astype(BF16)
        v_new = u_c - _dot(w_c.astype(BF16), state_b)
        o = _dot((q * eg).astype(BF16), state_b) + per_head(a_qk, v_new)
        k_dec = (k * jnp.exp(g_last - gc)).astype(BF16)
        upd = lax.dot_general(k_dec, v_new.astype(BF16), (((0,), (0,)), ((), ())), preferred_element_type=F32)
        s_scr[...] = state * jnp.exp(g_last) + upd * block_mask
        ms = _dot(o * o, block_mask, HIGHEST) * (1.0 / HEAD_DIM)
        gate = x_ref[rows, 3 * w + LANES:4 * w + LANES]
        o_ref[rows, :] = o * lax.rsqrt(ms + EPS) * on_ref[...] * (gate * jax.nn.sigmoid(gate))
        return carry
    lax.fori_loop(0, seq // c, chunk, 0)


def _dn(slab, conv_w, a_log, dt_bias, o_norm, *, batch, seq):
    wide = lambda: pltpu.VMEM((seq, GROUP_W), F32)
    narrow = lambda: pltpu.VMEM((seq, LANES), F32)
    return pl.pallas_call(
        functools.partial(_dn_body, seq=seq),
        grid=(batch,),
        in_specs=[pl.BlockSpec((seq, DN_SLAB), lambda b: (0, b)),
                  _const_spec((DN_CONV, 3 * GROUP_W)), _const_spec((1, LANES)), _const_spec((1, LANES)),
                  _const_spec((1, GROUP_W))],
        out_specs=pl.BlockSpec((seq, GROUP_W), lambda b: (0, b)),
        out_shape=jax.ShapeDtypeStruct((seq, batch * GROUP_W), F32),
        scratch_shapes=[wide(), wide(), wide(), narrow(), narrow(), pltpu.VMEM((GROUP_W, GROUP_W), F32)],
        compiler_params=_params("parallel"),
        name="deltanet",
    )(slab, conv_w, a_log, dt_bias, o_norm)


def _row(v, width=None):
    v = v.astype(F32).reshape(1, -1)
    if width is not None and v.shape[1] < width:
        v = jnp.pad(v, ((0, 0), (0, width - v.shape[1])))
    return v


def _prep_w_in(w):
    z = lambda n: jnp.zeros((D_MODEL, n), w.dtype)
    cols = [w[:, 0:416], z(96), w[:, 416:672], w[:, 672:1440], w[:, 1440:2208], w[:, 2208:2216], z(120),
            w[:, 2216:2472]]
    return jnp.concatenate(cols, axis=1).astype(BF16)


def _pad_heads(w, per_head, lo, hi):
    k = w.shape[0]
    w = w.reshape(k, N_HEADS, per_head)[:, :, lo:hi]
    w = jnp.pad(w, ((0, 0), (0, 0), (0, LANES - (hi - lo))))
    return w.reshape(k, N_HEADS * LANES).astype(BF16)


def _rope_tables(seq):
    half = MLA_ROPE // 2
    pos = jnp.arange(seq, dtype=F32)
    freqs = ROPE_THETA ** (-jnp.arange(half, dtype=F32) / half)
    ang = pos[:, None] * freqs[None, :]
    cos, sin = jnp.cos(ang), jnp.sin(ang)
    ones = jnp.ones((seq, MLA_NOPE), F32)
    zeros = jnp.zeros((seq, MLA_NOPE), F32)
    tail1 = jnp.ones((seq, LANES - MLA_DQK), F32)
    tail0 = jnp.zeros((seq, LANES - MLA_DQK), F32)
    return (jnp.concatenate([ones, cos, cos, tail1], axis=1),
            jnp.concatenate([zeros, -sin, sin, tail0], axis=1))


def _block_diag(blocks):
    g, r, c = blocks.shape
    eye = jnp.eye(g, dtype=blocks.dtype)
    return (blocks[:, :, None, :] * eye[:, None, :, None]).reshape(g * r, g * c)


def _mla_layer(slab, p, l, *, batch, seq):
    w_ukv = p["mla_w_ukv"][l]
    cos, sin = _rope_tables(seq)
    return _mla(slab, _row(p["mla_q_norm"][l]), _row(p["mla_kv_norm"][l]),
                _pad_heads(p["mla_w_uq"][l], MLA_DQK, 0, MLA_DQK),
                _pad_heads(w_ukv, MLA_NOPE + HEAD_DIM, 0, MLA_NOPE),
                _pad_heads(w_ukv, MLA_NOPE + HEAD_DIM, MLA_NOPE, MLA_NOPE + HEAD_DIM),
                _row(p["mla_qk_q"][l], LANES), _row(p["mla_qk_k"][l], LANES), cos, sin, batch=batch, seq=seq)


def _dil_layer(slab, p, l, *, batch, seq):
    pair = lambda g: jnp.tile(g.astype(F32).reshape(1, HEAD_DIM), (1, LANES // HEAD_DIM))
    return _dil(slab, pair(p["dil_q_norm"][l]), pair(p["dil_k_norm"][l]), _dil_bias(p["t5_bias"]),
                batch=batch, seq=seq)


def _dn_layer(slab, p, l, *, batch, seq):
    return _dn(slab, p["dn_conv"][l].astype(F32), _row(p["dn_a_log"][l], LANES), _row(p["dn_dt_bias"][l], LANES),
               jnp.tile(p["dn_o_norm"][l].astype(F32).reshape(1, HEAD_DIM), (1, N_HEADS)), batch=batch, seq=seq)


def _s5_layer(u, p, l, *, batch, seq):
    state_row = lambda v: v.astype(F32).reshape(1, S5_WIDTH)
    ldt = jnp.broadcast_to(p["s5_log_dt"][l][:, None], (S5_GROUPS, S5_STATE))
    bre = _block_diag(jnp.swapaxes(p["s5_b_re"][l], 1, 2).astype(F32))
    bim = _block_diag(jnp.swapaxes(p["s5_b_im"][l], 1, 2).astype(F32))
    cre = _block_diag(jnp.swapaxes(p["s5_c_re"][l], 1, 2)).astype(BF16)
    cim = _block_diag(jnp.swapaxes(p["s5_c_im"][l], 1, 2)).astype(BF16)
    return _s5(u, state_row(p["s5_lambda_re"][l]), state_row(p["s5_lambda_im"][l]), state_row(ldt),
               bre, bim, cre, cim, _row(p["s5_d"][l]), p["s5_w_glu"][l].astype(BF16), batch=batch, seq=seq,
               steps=min(128, seq))


def kernel(x, attn_norm, w_in, w_out, mla_q_norm, mla_kv_norm, mla_w_uq, mla_w_ukv, mla_qk_q, mla_qk_k,
           s5_lambda_re, s5_lambda_im, s5_log_dt, s5_b_re, s5_b_im, s5_c_re, s5_c_im, s5_d, s5_w_glu,
           dil_q_norm, dil_k_norm, t5_bias, dn_conv, dn_a_log, dn_dt_bias, dn_o_norm,
           ffn_norm, ffn_w1, ffn_w3, ffn_w2):
    p = dict(mla_q_norm=mla_q_norm, mla_kv_norm=mla_kv_norm, mla_w_uq=mla_w_uq, mla_w_ukv=mla_w_ukv,
             mla_qk_q=mla_qk_q, mla_qk_k=mla_qk_k, s5_lambda_re=s5_lambda_re, s5_lambda_im=s5_lambda_im,
             s5_log_dt=s5_log_dt, s5_b_re=s5_b_re, s5_b_im=s5_b_im, s5_c_re=s5_c_re, s5_c_im=s5_c_im, s5_d=s5_d,
             s5_w_glu=s5_w_glu, dil_q_norm=dil_q_norm, dil_k_norm=dil_k_norm, t5_bias=t5_bias, dn_conv=dn_conv,
             dn_a_log=dn_a_log, dn_dt_bias=dn_dt_bias, dn_o_norm=dn_o_norm)
    batch, seq, _ = x.shape
    tokens = batch * seq
    per_batch = lambda a: a.reshape(seq, batch * a.shape[1])
    per_token = lambda a: a.reshape(tokens, a.shape[1] // batch)
    h = x.transpose(1, 0, 2).reshape(tokens, D_MODEL)
    for l in range(attn_norm.shape[0]):
        mla_in, s5_in, dil_in, dn_in = _proj(h, _row(attn_norm[l]), _prep_w_in(w_in[l]))
        y_mla = per_token(_mla_layer(per_batch(mla_in), p, l, batch=batch, seq=seq))
        y_s5 = _s5_layer(s5_in, p, l, batch=batch, seq=seq)
        y_dil = per_token(_dil_layer(per_batch(dil_in), p, l, batch=batch, seq=seq))
        y_dn = per_token(_dn_layer(per_batch(dn_in), p, l, batch=batch, seq=seq))
        h = _out_ffn(h, [y_mla, y_s5, y_dil, y_dn], w_out[l].astype(BF16), _row(ffn_norm[l]),
                     ffn_w1[l].astype(BF16), ffn_w3[l].astype(BF16), ffn_w2[l].astype(BF16))
    return h.reshape(seq, batch, D_MODEL).transpose(1, 0, 2)
```

```python
import functools
import math

import jax
import jax.numpy as jnp
import numpy as np
from jax import lax
from jax.experimental import pallas as pl
from jax.experimental.pallas import tpu as pltpu

F32 = jnp.float32
BF16 = jnp.bfloat16
HIGHEST = lax.Precision.HIGHEST

D_MODEL = 1024
GROUP_W = 256
HEAD_DIM = 64
N_HEADS = 4
EPS = 1e-6
NEG_INF = -1e30

MLA_NOPE = 64
MLA_ROPE = 32
MLA_DQK = MLA_NOPE + MLA_ROPE
MLA_KV_RANK = 128
ROPE_THETA = 10000.0

S5_GROUP_CH = 16
S5_GROUPS = 16
S5_STATE = 64
S5_WIDTH = S5_GROUPS * S5_STATE

DIL_PAIRS = ((128, 1), (512, 4), (2048, 16))
DIL_SPAN = 128
T5_BUCKETS = 32
T5_MAX_DIST = 2048

DN_CONV = 4
DN_CHUNK = 64

FFN_HIDDEN = 2816
FFN_CHUNK = 256

VMEM_LIMIT_BYTES = 56 * 1024 * 1024
LANES = 128

MLA_SLAB = 512
S5_SLAB = 256
DIL_SLAB = 768
DN_SLAB = 1152
PROJ_COLS = MLA_SLAB + S5_SLAB + DIL_SLAB + DN_SLAB


def _dot(a, b, precision=None):
    return jnp.dot(a, b, preferred_element_type=F32, precision=precision)


def _dot_nt(a, b, precision=None):
    return lax.dot_general(a, b, (((1,), (1,)), ((), ())), preferred_element_type=F32, precision=precision)


def _const_spec(shape):
    nd = len(shape)
    return pl.BlockSpec(shape, lambda *_: (0,) * nd)


def _params(*sem):
    return pltpu.CompilerParams(dimension_semantics=sem, vmem_limit_bytes=VMEM_LIMIT_BYTES)


def _proj_body(x_ref, g_ref, w_ref, mla_ref, s5_ref, dil_ref, dn_ref):
    x = x_ref[...]
    n = x * lax.rsqrt(jnp.mean(x * x, axis=-1, keepdims=True) + EPS) * g_ref[...]
    nb = n.astype(BF16)
    start = 0
    for ref in (mla_ref, s5_ref, dil_ref, dn_ref):
        width = ref.shape[-1]
        ref[...] = _dot(nb, w_ref[:, start:start + width])
        start += width


def _proj(h, gain, w_big, *, batch, seq, tm=512):
    nt = seq // tm
    widths = (MLA_SLAB, S5_SLAB, DIL_SLAB, DN_SLAB)
    return pl.pallas_call(
        _proj_body,
        grid=(batch, nt),
        in_specs=[pl.BlockSpec((tm, D_MODEL), lambda b, i: (b * nt + i, 0)),
                  _const_spec((1, D_MODEL)),
                  _const_spec((D_MODEL, PROJ_COLS))],
        out_specs=[pl.BlockSpec((tm, w), lambda b, i: (i, b)) for w in widths],
        out_shape=[jax.ShapeDtypeStruct((seq, batch * w), F32) for w in widths],
        compiler_params=_params("parallel", "parallel"),
        name="proj",
    )(h, gain, w_big)


def _out_ffn_body(h_ref, y0_ref, y1_ref, y2_ref, y3_ref, wo_ref, g_ref, w1_ref, w3_ref, w2_ref, o_ref, acc_ref):
    h = h_ref[...]
    for i, y_ref in enumerate((y0_ref, y1_ref, y2_ref, y3_ref)):
        h = h + _dot(y_ref[...].astype(BF16), wo_ref[i * GROUP_W:(i + 1) * GROUP_W, :])
    n = h * lax.rsqrt(jnp.mean(h * h, axis=-1, keepdims=True) + EPS) * g_ref[...]
    nb = n.astype(BF16)
    acc_ref[...] = h

    def hidden_chunk(c, carry):
        cols = pl.ds(pl.multiple_of(c * FFN_CHUNK, FFN_CHUNK), FFN_CHUNK)
        a = _dot(nb, w1_ref[:, cols])
        b = _dot(nb, w3_ref[:, cols])
        z = (a * jax.nn.sigmoid(a) * b).astype(BF16)
        acc_ref[...] += _dot(z, w2_ref[cols, :])
        return carry
    lax.fori_loop(0, FFN_HIDDEN // FFN_CHUNK, hidden_chunk, 0)
    o_ref[...] = acc_ref[...]


def _out_ffn(h, ys, w_out, gain, w1, w3, w2, *, batch, seq, tm=256):
    nt = seq // tm
    row = pl.BlockSpec((tm, D_MODEL), lambda b, i: (b * nt + i, 0))
    slab = pl.BlockSpec((tm, GROUP_W), lambda b, i: (i, b))
    return pl.pallas_call(
        _out_ffn_body,
        grid=(batch, nt),
        in_specs=[row] + [slab] * 4 + [
            _const_spec((D_MODEL, D_MODEL)), _const_spec((1, D_MODEL)),
            _const_spec((D_MODEL, FFN_HIDDEN)), _const_spec((D_MODEL, FFN_HIDDEN)),
            _const_spec((FFN_HIDDEN, D_MODEL))],
        out_specs=row,
        out_shape=jax.ShapeDtypeStruct((batch * seq, D_MODEL), F32),
        scratch_shapes=[pltpu.VMEM((tm, D_MODEL), F32)],
        compiler_params=_params("parallel", "parallel"),
        name="out_ffn",
    )(h, *ys, w_out, gain, w1, w3, w2)


MLA_BLOCK = 256


def _mla_body(x_ref, qn_ref, kvn_ref, wuq_ref, wuk_ref, wuv_ref, gq_ref, gk_ref, cos_ref, sin_ref,
              o_ref, k_scr, v_scr, *, seq):
    blk = MLA_BLOCK
    qi = pl.program_id(1)
    lane = lax.broadcasted_iota(jnp.int32, (blk, LANES), 1)

    def rope(x, c, s):
        rot = jnp.where(lane < MLA_NOPE + MLA_ROPE // 2, pltpu.roll(x, LANES - MLA_ROPE // 2, 1),
                        pltpu.roll(x, MLA_ROPE // 2, 1))
        return x * c + rot * s

    def norm_head(x, g):
        ssq = jnp.sum(x * x, axis=-1, keepdims=True)
        return x * lax.rsqrt(ssq * (1.0 / MLA_DQK) + EPS) * g

    @pl.when(qi == 0)
    def _prepare_keys_values():
        def tile(i, carry):
            r0 = pl.multiple_of(i * blk, blk)
            rows = pl.ds(r0, blk)
            ckv = x_ref[rows, 256:384]
            k_rope = x_ref[rows, 384:512]
            kvn = ckv * lax.rsqrt(jnp.mean(ckv * ckv, axis=-1, keepdims=True) + EPS) * kvn_ref[...]
            kvn = kvn.astype(BF16)
            k_nope = _dot(kvn, wuk_ref[...])
            v = _dot(kvn, wuv_ref[...])
            k_rope = pltpu.roll(k_rope, MLA_NOPE, 1)
            c = cos_ref[rows, :]
            s = sin_ref[rows, :]
            for h in range(N_HEADS):
                cols = slice(h * LANES, (h + 1) * LANES)
                k = rope(norm_head(k_nope[:, cols] + k_rope, gk_ref[...]), c, s)
                k_scr[h, rows, :] = k.astype(BF16)
                v_scr[h, rows, :] = v[:, cols].astype(BF16)
            return carry
        lax.fori_loop(0, seq // blk, tile, 0)

    r0 = pl.multiple_of(qi * blk, blk)
    rows = pl.ds(r0, blk)
    cq = x_ref[rows, 0:256]
    qn = cq * lax.rsqrt(jnp.mean(cq * cq, axis=-1, keepdims=True) + EPS) * qn_ref[...]
    q_all = _dot(qn.astype(BF16), wuq_ref[...])
    c = cos_ref[rows, :]
    s = sin_ref[rows, :]
    row_pos = r0 + lax.broadcasted_iota(jnp.int32, (blk, blk), 0)
    col_pos = lax.broadcasted_iota(jnp.int32, (blk, blk), 1)
    outs = []
    for h in range(N_HEADS):
        q = rope(norm_head(q_all[:, h * LANES:(h + 1) * LANES], gq_ref[...]), c, s)
        q = (q * (MLA_DQK ** -0.5)).astype(BF16)

        def kv_step(j, carry, q=q, h=h):
            m, l, acc = carry
            c0 = pl.multiple_of(j * blk, blk)
            keys = pl.ds(c0, blk)
            logits = _dot_nt(q, k_scr[h, keys, :])
            logits = jnp.where(col_pos + c0 <= row_pos, logits, NEG_INF)
            m_new = jnp.maximum(m, jnp.max(logits, axis=-1, keepdims=True))
            alpha = jnp.exp(m - m_new)
            p = jnp.exp(logits - m_new)
            l = alpha * l + jnp.sum(p, axis=-1, keepdims=True)
            acc = alpha * acc + _dot(p.astype(BF16), v_scr[h, keys, :])
            return m_new, l, acc

        init = (jnp.full((blk, 1), NEG_INF, F32), jnp.zeros((blk, 1), F32), jnp.zeros((blk, LANES), F32))
        _, l, acc = lax.fori_loop(0, qi + 1, kv_step, init)
        outs.append(acc / l)
    o_ref[rows, 0:LANES] = outs[0] + pltpu.roll(outs[1], HEAD_DIM, 1)
    o_ref[rows, LANES:2 * LANES] = outs[2] + pltpu.roll(outs[3], HEAD_DIM, 1)


def _mla(slab, qn, kvn, wuq, wuk, wuv, gq, gk, cos, sin, *, batch, seq):
    nq = seq // MLA_BLOCK
    return pl.pallas_call(
        functools.partial(_mla_body, seq=seq),
        grid=(batch, nq),
        in_specs=[pl.BlockSpec((seq, MLA_SLAB), lambda b, i: (0, b)),
                  _const_spec((1, 256)), _const_spec((1, MLA_KV_RANK)),
                  _const_spec((256, N_HEADS * LANES)), _const_spec((MLA_KV_RANK, N_HEADS * LANES)),
                  _const_spec((MLA_KV_RANK, N_HEADS * LANES)),
                  _const_spec((1, LANES)), _const_spec((1, LANES)),
                  _const_spec((seq, LANES)), _const_spec((seq, LANES))],
        out_specs=pl.BlockSpec((seq, GROUP_W), lambda b, i: (0, b)),
        out_shape=jax.ShapeDtypeStruct((seq, batch * GROUP_W), F32),
        scratch_shapes=[pltpu.VMEM((N_HEADS, seq, LANES), BF16), pltpu.VMEM((N_HEADS, seq, LANES), BF16)],
        compiler_params=_params("parallel", "arbitrary"),
        name="mla",
    )(slab, qn, kvn, wuq, wuk, wuv, gq, gk, cos, sin)


def _s5_body(u_ref, lre_ref, lim_ref, ldt_ref, bre_ref, bim_ref, cre_ref, cim_ref, d_ref, wglu_ref, o_ref,
             a_scr, bbar_scr, h_scr, x_scr, *, batch, steps):
    n_tiles = S5_WIDTH // LANES

    @pl.when(pl.program_id(0) == 0)
    def _discretise():
        lr = lre_ref[...]
        li = lim_ref[...]
        dt = jnp.exp(ldt_ref[...])
        mag = jnp.exp(lr * dt)
        ar = mag * jnp.cos(li * dt)
        ai = mag * jnp.sin(li * dt)
        den = lr * lr + li * li
        nr = ar - 1.0
        zr = (nr * lr + ai * li) / den
        zi = (ai * lr - nr * li) / den
        for j in range(n_tiles):
            lanes = slice(j * LANES, (j + 1) * LANES)
            a_scr[0, j] = jnp.broadcast_to(ar[:, lanes], (batch, LANES))
            a_scr[1, j] = jnp.broadcast_to(ai[:, lanes], (batch, LANES))
        bre = bre_ref[...]
        bim = bim_ref[...]
        bbar_scr[0] = (zr * bre - zi * bim).astype(BF16)
        bbar_scr[1] = (zr * bim + zi * bre).astype(BF16)
        h_scr[...] = jnp.zeros_like(h_scr)

    for b in range(batch):
        ub = u_ref[:, b * GROUP_W:(b + 1) * GROUP_W].astype(BF16)
        rows_b = pl.ds(b, steps, stride=batch)
        for part in range(2):
            x = _dot(ub, bbar_scr[part])
            for j in range(n_tiles):
                x_scr[part, j, rows_b, :] = x[:, j * LANES:(j + 1) * LANES]
    ar = a_scr[0]
    ai = a_scr[1]

    def step(t, carry):
        hr, hi = carry
        rows = pl.ds(pl.multiple_of(t * batch, batch), batch)
        nhr = ar * hr - ai * hi + x_scr[0, :, rows, :]
        nhi = ar * hi + ai * hr + x_scr[1, :, rows, :]
        x_scr[0, :, rows, :] = nhr
        x_scr[1, :, rows, :] = nhi
        return nhr, nhi

    hr, hi = lax.fori_loop(0, steps, step, (h_scr[0], h_scr[1]), unroll=8)
    h_scr[0] = hr
    h_scr[1] = hi
    for b in range(batch):
        rows_b = pl.ds(b, steps, stride=batch)
        state = [jnp.concatenate([x_scr[part, j, rows_b, :] for j in range(n_tiles)], axis=1).astype(BF16)
                 for part in range(2)]
        u = u_ref[:, b * GROUP_W:(b + 1) * GROUP_W]
        y = _dot(state[0], cre_ref[...]) - _dot(state[1], cim_ref[...]) + d_ref[...] * u
        z = _dot(y.astype(BF16), wglu_ref[...])
        o_ref[:, b * GROUP_W:(b + 1) * GROUP_W] = z[:, :GROUP_W] * jax.nn.sigmoid(z[:, GROUP_W:])


def _s5(u, lre, lim, ldt, bre, bim, cre, cim, d, wglu, *, batch, seq, steps=256):
    n_tiles = S5_WIDTH // LANES
    state = lambda rows: pltpu.VMEM((2, n_tiles, rows, LANES), F32)
    return pl.pallas_call(
        functools.partial(_s5_body, batch=batch, steps=steps),
        grid=(seq // steps,),
        in_specs=[pl.BlockSpec((steps, batch * S5_SLAB), lambda i: (i, 0)),
                  _const_spec((1, S5_WIDTH)), _const_spec((1, S5_WIDTH)), _const_spec((1, S5_WIDTH)),
                  _const_spec((GROUP_W, S5_WIDTH)), _const_spec((GROUP_W, S5_WIDTH)),
                  _const_spec((S5_WIDTH, GROUP_W)), _const_spec((S5_WIDTH, GROUP_W)),
                  _const_spec((1, GROUP_W)), _const_spec((GROUP_W, 2 * GROUP_W))],
        out_specs=pl.BlockSpec((steps, batch * GROUP_W), lambda i: (i, 0)),
        out_shape=jax.ShapeDtypeStruct((seq, batch * GROUP_W), F32),
        scratch_shapes=[state(batch), pltpu.VMEM((2, GROUP_W, S5_WIDTH), BF16), state(batch),
                        state(steps * batch)],
        compiler_params=_params("arbitrary"),
        name="s5",
    )(u, lre, lim, ldt, bre, bim, cre, cim, d, wglu)


DIL_TILE = 256


def _pair_norm(x, gain2, lo_half):
    sq = x * x
    tot = jnp.sum(sq, axis=-1, keepdims=True)
    lo = jnp.sum(jnp.where(lo_half, sq, 0.0), axis=-1, keepdims=True)
    ms = jnp.where(lo_half, lo, tot - lo) * (1.0 / HEAD_DIM)
    return x * lax.rsqrt(ms + EPS) * gain2


def _dil_body(x_ref, gq_ref, gk_ref, band_ref, o_ref, bias_ref, q_scr, k_scr, v_scr, qd, kd, vd, ob, mb, lb,
              oa, ma, la, *, seq):
    span = DIL_SPAN
    n_blocks = seq // span
    n_tiles = GROUP_W // LANES
    lo_tile = lax.broadcasted_iota(jnp.int32, (DIL_TILE, LANES), 1) < HEAD_DIM
    lo_blk = lax.broadcasted_iota(jnp.int32, (span, LANES), 1) < HEAD_DIM

    for bh in range(len(DIL_PAIRS) * N_HEADS):
        profile = jnp.broadcast_to(band_ref[bh:bh + 1, :], (span, 2 * span))
        full = pltpu.roll(profile, 0, 1, stride=1, stride_axis=0)
        bias_ref[2 * bh] = full[:, :span]
        bias_ref[2 * bh + 1] = full[:, span:]

    def norm_tile(i, carry):
        rows = pl.ds(pl.multiple_of(i * DIL_TILE, DIL_TILE), DIL_TILE)
        for j in range(n_tiles):
            cols = lambda part: slice(part * GROUP_W + j * LANES, part * GROUP_W + (j + 1) * LANES)
            q = _pair_norm(x_ref[rows, cols(0)], gq_ref[...], lo_tile)
            q_scr[j, rows, :] = q * (HEAD_DIM ** -0.5)
            k_scr[j, rows, :] = _pair_norm(x_ref[rows, cols(1)], gk_ref[...], lo_tile)
            v_scr[j, rows, :] = x_ref[rows, cols(2)]
        return carry
    lax.fori_loop(0, seq // DIL_TILE, norm_tile, 0)

    for bi, (window, dil) in enumerate(DIL_PAIRS):
        sub_len = seq // dil
        nb = sub_len // span
        chunks = [(r, c) for r in range(dil) for c in range(nb)]

        def natural(r, c, dil=dil):
            return pl.ds(r + dil * c * span, span, stride=dil) if dil > 1 else pl.ds(c * span, span)

        if dil > 1:
            for r, c in chunks:
                dst = pl.ds((r * nb + c) * span, span)
                for j in range(n_tiles):
                    qd[j, dst, :] = q_scr[j, natural(r, c), :]
                    kd[j, dst, :] = k_scr[j, natural(r, c), :]
                    vd[j, dst, :] = v_scr[j, natural(r, c), :]
            q_src, k_src, v_src = qd, kd, vd
        else:
            q_src, k_src, v_src = q_scr, k_scr, v_scr
        o_dst, m_dst, l_dst = (oa, ma, la) if bi == 0 else (ob, mb, lb)

        def block(t, carry, bi=bi, nb=nb, q_src=q_src, k_src=k_src, v_src=v_src,
                  o_dst=o_dst, m_dst=m_dst, l_dst=l_dst):
            rows = pl.ds(pl.multiple_of(t * span, span), span)
            prev = pl.ds(pl.multiple_of(jnp.maximum(t - 1, 0) * span, span), span)
            has_prev = (t & (nb - 1)) != 0
            for j in range(n_tiles):
                q2 = q_src[j, rows, :]
                kc = k_src[j, rows, :].astype(BF16)
                vc = v_src[j, rows, :].astype(BF16)
                if nb > 1:
                    kp = k_src[j, prev, :].astype(BF16)
                    vp = v_src[j, prev, :].astype(BF16)
                res = []
                for a in range(2):
                    h = 2 * j + a
                    qa = jnp.where(lo_blk if a == 0 else jnp.logical_not(lo_blk), q2, 0.0).astype(BF16)
                    lc = _dot_nt(qa, kc) + bias_ref[(bi * N_HEADS + h) * 2 + 1]
                    m = jnp.max(lc, axis=-1, keepdims=True)
                    if nb > 1:
                        lp = _dot_nt(qa, kp) + bias_ref[(bi * N_HEADS + h) * 2]
                        lp = jnp.where(has_prev, lp, NEG_INF)
                        m = jnp.maximum(m, jnp.max(lp, axis=-1, keepdims=True))
                    pc = jnp.exp(lc - m)
                    l = jnp.sum(pc, axis=-1, keepdims=True)
                    o = _dot(pc.astype(BF16), vc)
                    if nb > 1:
                        pp = jnp.exp(lp - m)
                        l = l + jnp.sum(pp, axis=-1, keepdims=True)
                        o = o + _dot(pp.astype(BF16), vp)
                    res.append((o, m, l))
                (o0, m0, l0), (o1, m1, l1) = res
                o_dst[j, rows, :] = jnp.where(lo_blk, o0, o1)
                m_dst[j, rows, :] = jnp.where(lo_blk, m0, m1)
                l_dst[j, rows, :] = jnp.where(lo_blk, l0, l1)
            return carry
        lax.fori_loop(0, n_blocks, block, 0)

        if bi > 0:
            for r, c in chunks:
                src = pl.ds((r * nb + c) * span, span)
                nat = natural(r, c)
                for j in range(n_tiles):
                    m_old, m_in = ma[j, nat, :], mb[j, src, :]
                    m_new = jnp.maximum(m_old, m_in)
                    w_old, w_in = jnp.exp(m_old - m_new), jnp.exp(m_in - m_new)
                    oa[j, nat, :] = w_old * oa[j, nat, :] + w_in * ob[j, src, :]
                    la[j, nat, :] = w_old * la[j, nat, :] + w_in * lb[j, src, :]
                    ma[j, nat, :] = m_new

    def finish(i, carry):
        rows = pl.ds(pl.multiple_of(i * DIL_TILE, DIL_TILE), DIL_TILE)
        for j in range(n_tiles):
            o_ref[rows, j * LANES:(j + 1) * LANES] = oa[j, rows, :] / la[j, rows, :]
        return carry
    lax.fori_loop(0, seq // DIL_TILE, finish, 0)


def _dil(slab, gq2, gk2, band, *, batch, seq):
    big = lambda: pltpu.VMEM((GROUP_W // LANES, seq, LANES), F32)
    n_bias = 2 * len(DIL_PAIRS) * N_HEADS
    return pl.pallas_call(
        functools.partial(_dil_body, seq=seq),
        grid=(batch,),
        in_specs=[pl.BlockSpec((seq, DIL_SLAB), lambda b: (0, b)),
                  _const_spec((1, LANES)), _const_spec((1, LANES)),
                  _const_spec(band.shape)],
        out_specs=pl.BlockSpec((seq, GROUP_W), lambda b: (0, b)),
        out_shape=jax.ShapeDtypeStruct((seq, batch * GROUP_W), F32),
        scratch_shapes=[pltpu.VMEM((n_bias, DIL_SPAN, DIL_SPAN), F32)] + [big() for _ in range(12)],
        compiler_params=_params("parallel"),
        name="dilated",
    )(slab, gq2, gk2, band)


def _t5_bucket(dist):
    exact = T5_BUCKETS // 2
    df = jnp.maximum(dist, 1).astype(F32)
    large = exact + (jnp.log(df / exact) / math.log(T5_MAX_DIST / exact) * (T5_BUCKETS - exact)).astype(jnp.int32)
    large = jnp.minimum(large, T5_BUCKETS - 1)
    return jnp.where(dist < exact, dist, large)


def _dil_band(table):
    span = DIL_SPAN
    delta = span - jnp.arange(2 * span, dtype=jnp.int32)
    rows = []
    for _, dil in DIL_PAIRS:
        bucket = _t5_bucket(jnp.clip(delta, 0, span) * dil)
        onehot = (bucket[:, None] == jnp.arange(T5_BUCKETS, dtype=jnp.int32)[None, :]).astype(F32)
        vals = jnp.dot(onehot, table.astype(F32), precision=HIGHEST)
        rows.append(jnp.where((delta >= 0)[:, None], vals, NEG_INF).T)
    return jnp.concatenate(rows, axis=0)


DN_TILE = 256
DN_SOLVE_PRECISION = HIGHEST


def _softplus(x):
    return jnp.maximum(x, 0.0) + jnp.log1p(jnp.exp(-jnp.abs(x)))


def _pair_l2(x, lo_half):
    sq = x * x
    tot = jnp.sum(sq, axis=-1, keepdims=True)
    lo = jnp.sum(jnp.where(lo_half, sq, 0.0), axis=-1, keepdims=True)
    return x * lax.rsqrt(jnp.where(lo_half, lo, tot - lo) + EPS)


def _dn_body(x_ref, cw_ref, alog_ref, dtb_ref, on_ref, o_ref, q_scr, k_scr, v_scr, g_scr, b_scr, s_scr, *, seq):
    c = DN_CHUNK
    w = GROUP_W
    lo_tile = lax.broadcasted_iota(jnp.int32, (DN_TILE, LANES), 1) < HEAD_DIM

    def prep_tile(i, carry):
        r0 = pl.multiple_of(i * DN_TILE, DN_TILE)
        rows = pl.ds(r0, DN_TILE)
        cur = x_ref[rows, 0:3 * w]
        halo_rows = pl.ds(pl.multiple_of(jnp.maximum(r0 - 8, 0), 8), 8)
        halo = jnp.where(i > 0, x_ref[halo_rows, 0:3 * w], 0.0)
        ext = jnp.concatenate([halo, cur], axis=0)
        acc = cw_ref[DN_CONV - 1:DN_CONV, :] * cur
        for j in range(DN_CONV - 1):
            acc = acc + cw_ref[j:j + 1, :] * pltpu.roll(ext, DN_CONV - 1 - j, 0)[8:, :]
        y = acc * jax.nn.sigmoid(acc)
        for j in range(w // LANES):
            cols = lambda part: slice(part * w + j * LANES, part * w + (j + 1) * LANES)
            q_scr[rows, j * LANES:(j + 1) * LANES] = _pair_l2(y[:, cols(0)], lo_tile) * (HEAD_DIM ** -0.5)
            k_scr[rows, j * LANES:(j + 1) * LANES] = _pair_l2(y[:, cols(1)], lo_tile)
            v_scr[rows, j * LANES:(j + 1) * LANES] = y[:, cols(2)]
        ab = x_ref[rows, 3 * w:3 * w + LANES]
        g_scr[rows, :] = -jnp.exp(alog_ref[...]) * _softplus(ab + dtb_ref[...])
        b_scr[rows, :] = pltpu.roll(jax.nn.sigmoid(ab), LANES - N_HEADS, 1)
        return carry
    lax.fori_loop(0, seq // DN_TILE, prep_tile, 0)

    ri = lax.broadcasted_iota(jnp.int32, (c, w), 0)
    ci = lax.broadcasted_iota(jnp.int32, (c, w), 1)
    cj = jnp.bitwise_and(ci, HEAD_DIM - 1)
    causal = ri >= cj
    strict = ri > cj
    eye4 = (ri == cj).astype(F32)
    bi_r = lax.broadcasted_iota(jnp.int32, (w, w), 0)
    bi_c = lax.broadcasted_iota(jnp.int32, (w, w), 1)
    same_head = lax.shift_right_logical(bi_r, 6) == lax.shift_right_logical(bi_c, 6)
    block_mask = same_head.astype(F32)
    expand = (lax.broadcasted_iota(jnp.int32, (LANES, w), 0)
              == lax.shift_right_logical(lax.broadcasted_iota(jnp.int32, (LANES, w), 1), 6)).astype(F32)
    tril = (lax.broadcasted_iota(jnp.int32, (c, c), 0) >= lax.broadcasted_iota(jnp.int32, (c, c), 1)).astype(F32)

    def head_blocks(b):
        return jnp.concatenate([b] * N_HEADS, axis=0) * block_mask

    def per_head(a, b, precision=None):
        bd = head_blocks(b)
        if precision is None:
            return _dot(a.astype(BF16), bd.astype(BF16))
        return _dot(a, bd, precision)

    s_scr[...] = jnp.zeros_like(s_scr)

    def chunk(n, carry):
        rows = pl.ds(pl.multiple_of(n * c, c), c)
        q = q_scr[rows, :]
        k = k_scr[rows, :]
        v = v_scr[rows, :]
        gc = _dot(_dot(tril, g_scr[rows, :], HIGHEST), expand, HIGHEST)
        beta = _dot(b_scr[rows, :], expand, HIGHEST)
        g_row = jnp.sum(gc * eye4, axis=0, keepdims=True)
        g_last = gc[c - 1:c, :]
        decay = jnp.exp(jnp.where(causal, gc - g_row, NEG_INF))
        kb = k * beta
        k_bd = head_blocks(k).astype(BF16)
        lmat = jnp.where(strict, _dot_nt(kb.astype(BF16), k_bd) * decay, 0.0)
        a_qk = jnp.where(causal, _dot_nt(q.astype(BF16), k_bd) * decay, 0.0)
        p = -lmat
        t_inv = eye4 + p
        for _ in range(5):
            p = per_head(p, p, DN_SOLVE_PRECISION)
            t_inv = t_inv + per_head(t_inv, p, DN_SOLVE_PRECISION)
        eg = jnp.exp(gc)
        w_c = per_head(t_inv, kb * eg, DN_SOLVE_PRECISION)
        u_c = per_head(t_inv, v * beta, DN_SOLVE_PRECISION)
        state = s_scr[...]
        state_b = state.astype(BF16)
        v_new = u_c - _dot(w_c.astype(BF16), state_b)
        o = _dot((q * eg).astype(BF16), state_b) + per_head(a_qk, v_new)
        k_dec = (k * jnp.exp(g_last - gc)).astype(BF16)
        upd = lax.dot_general(k_dec, v_new.astype(BF16), (((0,), (0,)), ((), ())), preferred_element_type=F32)
        s_scr[...] = state * jnp.exp(g_last) + upd * block_mask
        ms = _dot(o * o, block_mask, HIGHEST) * (1.0 / HEAD_DIM)
        gate = x_ref[rows, 3 * w + LANES:4 * w + LANES]
        o_ref[rows, :] = o * lax.rsqrt(ms + EPS) * on_ref[...] * (gate * jax.nn.sigmoid(gate))
        return carry
    lax.fori_loop(0, seq // c, chunk, 0)


def _dn(slab, conv_w, a_log, dt_bias, o_norm, *, batch, seq):
    wide = lambda: pltpu.VMEM((seq, GROUP_W), F32)
    narrow = lambda: pltpu.VMEM((seq, LANES), F32)
    return pl.pallas_call(
        functools.partial(_dn_body, seq=seq),
        grid=(batch,),
        in_specs=[pl.BlockSpec((seq, DN_SLAB), lambda b: (0, b)),
                  _const_spec((DN_CONV, 3 * GROUP_W)), _const_spec((1, LANES)), _const_spec((1, LANES)),
                  _const_spec((1, GROUP_W))],
        out_specs=pl.BlockSpec((seq, GROUP_W), lambda b: (0, b)),
        out_shape=jax.ShapeDtypeStruct((seq, batch * GROUP_W), F32),
        scratch_shapes=[wide(), wide(), wide(), narrow(), narrow(), pltpu.VMEM((GROUP_W, GROUP_W), F32)],
        compiler_params=_params("parallel"),
        name="deltanet",
    )(slab, conv_w, a_log, dt_bias, o_norm)


def _row(v, width=None):
    v = v.astype(F32).reshape(1, -1)
    if width is not None and v.shape[1] < width:
        v = jnp.pad(v, ((0, 0), (0, width - v.shape[1])))
    return v


def _prep_w_in(w):
    z = lambda n: jnp.zeros((D_MODEL, n), w.dtype)
    cols = [w[:, 0:416], z(96), w[:, 416:672], w[:, 672:1440], w[:, 1440:2208], w[:, 2208:2216], z(120),
            w[:, 2216:2472]]
    return jnp.concatenate(cols, axis=1).astype(BF16)


def _pad_heads(w, per_head, lo, hi):
    k = w.shape[0]
    w = w.reshape(k, N_HEADS, per_head)[:, :, lo:hi]
    w = jnp.pad(w, ((0, 0), (0, 0), (0, LANES - (hi - lo))))
    return w.reshape(k, N_HEADS * LANES).astype(BF16)


def _rope_tables(seq):
    half = MLA_ROPE // 2
    pos = jnp.arange(seq, dtype=F32)
    freqs = ROPE_THETA ** (-jnp.arange(half, dtype=F32) / half)
    ang = pos[:, None] * freqs[None, :]
    cos, sin = jnp.cos(ang), jnp.sin(ang)
    ones = jnp.ones((seq, MLA_NOPE), F32)
    zeros = jnp.zeros((seq, MLA_NOPE), F32)
    tail1 = jnp.ones((seq, LANES - MLA_DQK), F32)
    tail0 = jnp.zeros((seq, LANES - MLA_DQK), F32)
    return (jnp.concatenate([ones, cos, cos, tail1], axis=1),
            jnp.concatenate([zeros, -sin, sin, tail0], axis=1))


def _block_diag(blocks):
    g, r, c = blocks.shape
    eye = jnp.eye(g, dtype=blocks.dtype)
    return (blocks[:, :, None, :] * eye[:, None, :, None]).reshape(g * r, g * c)


def _mla_layer(slab, p, l, *, batch, seq):
    w_ukv = p["mla_w_ukv"][l]
    cos, sin = _rope_tables(seq)
    return _mla(slab, _row(p["mla_q_norm"][l]), _row(p["mla_kv_norm"][l]),
                _pad_heads(p["mla_w_uq"][l], MLA_DQK, 0, MLA_DQK),
                _pad_heads(w_ukv, MLA_NOPE + HEAD_DIM, 0, MLA_NOPE),
                _pad_heads(w_ukv, MLA_NOPE + HEAD_DIM, MLA_NOPE, MLA_NOPE + HEAD_DIM),
                _row(p["mla_qk_q"][l], LANES), _row(p["mla_qk_k"][l], LANES), cos, sin, batch=batch, seq=seq)


def _dil_layer(slab, p, l, *, batch, seq):
    pair = lambda g: jnp.tile(g.astype(F32).reshape(1, HEAD_DIM), (1, LANES // HEAD_DIM))
    return _dil(slab, pair(p["dil_q_norm"][l]), pair(p["dil_k_norm"][l]), _dil_band(p["t5_bias"]),
                batch=batch, seq=seq)


def _dn_layer(slab, p, l, *, batch, seq):
    return _dn(slab, p["dn_conv"][l].astype(F32), _row(p["dn_a_log"][l], LANES), _row(p["dn_dt_bias"][l], LANES),
               jnp.tile(p["dn_o_norm"][l].astype(F32).reshape(1, HEAD_DIM), (1, N_HEADS)), batch=batch, seq=seq)


def _s5_layer(u, p, l, *, batch, seq):
    state_row = lambda v: v.astype(F32).reshape(1, S5_WIDTH)
    ldt = jnp.broadcast_to(p["s5_log_dt"][l][:, None], (S5_GROUPS, S5_STATE))
    bre = _block_diag(jnp.swapaxes(p["s5_b_re"][l], 1, 2).astype(F32))
    bim = _block_diag(jnp.swapaxes(p["s5_b_im"][l], 1, 2).astype(F32))
    cre = _block_diag(jnp.swapaxes(p["s5_c_re"][l], 1, 2)).astype(BF16)
    cim = _block_diag(jnp.swapaxes(p["s5_c_im"][l], 1, 2)).astype(BF16)
    return _s5(u, state_row(p["s5_lambda_re"][l]), state_row(p["s5_lambda_im"][l]), state_row(ldt),
               bre, bim, cre, cim, _row(p["s5_d"][l]), p["s5_w_glu"][l].astype(BF16), batch=batch, seq=seq,
               steps=min(256, seq))


def kernel(x, attn_norm, w_in, w_out, mla_q_norm, mla_kv_norm, mla_w_uq, mla_w_ukv, mla_qk_q, mla_qk_k,
           s5_lambda_re, s5_lambda_im, s5_log_dt, s5_b_re, s5_b_im, s5_c_re, s5_c_im, s5_d, s5_w_glu,
           dil_q_norm, dil_k_norm, t5_bias, dn_conv, dn_a_log, dn_dt_bias, dn_o_norm,
           ffn_norm, ffn_w1, ffn_w3, ffn_w2):
    p = dict(mla_q_norm=mla_q_norm, mla_kv_norm=mla_kv_norm, mla_w_uq=mla_w_uq, mla_w_ukv=mla_w_ukv,
             mla_qk_q=mla_qk_q, mla_qk_k=mla_qk_k, s5_lambda_re=s5_lambda_re, s5_lambda_im=s5_lambda_im,
             s5_log_dt=s5_log_dt, s5_b_re=s5_b_re, s5_b_im=s5_b_im, s5_c_re=s5_c_re, s5_c_im=s5_c_im, s5_d=s5_d,
             s5_w_glu=s5_w_glu, dil_q_norm=dil_q_norm, dil_k_norm=dil_k_norm, t5_bias=t5_bias, dn_conv=dn_conv,
             dn_a_log=dn_a_log, dn_dt_bias=dn_dt_bias, dn_o_norm=dn_o_norm)
    batch, seq, _ = x.shape
    h = x.reshape(batch * seq, D_MODEL)
    for l in range(attn_norm.shape[0]):
        mla_in, s5_in, dil_in, dn_in = _proj(h, _row(attn_norm[l]), _prep_w_in(w_in[l]), batch=batch, seq=seq)
        ys = [_mla_layer(mla_in, p, l, batch=batch, seq=seq),
              _s5_layer(s5_in, p, l, batch=batch, seq=seq),
              _dil_layer(dil_in, p, l, batch=batch, seq=seq),
              _dn_layer(dn_in, p, l, batch=batch, seq=seq)]
        h = _out_ffn(h, ys, w_out[l].astype(BF16), _row(ffn_norm[l]),
                     ffn_w1[l].astype(BF16), ffn_w3[l].astype(BF16), ffn_w2[l].astype(BF16), batch=batch, seq=seq)
    return h.reshape(batch, seq, D_MODEL)
```

```python
import functools
import math

import jax
import jax.numpy as jnp
import numpy as np
from jax import lax
from jax.experimental import pallas as pl
from jax.experimental.pallas import tpu as pltpu

F32 = jnp.float32
BF16 = jnp.bfloat16
HIGHEST = lax.Precision.HIGHEST

D_MODEL = 1024
GROUP_W = 256
HEAD_DIM = 64
N_HEADS = 4
EPS = 1e-6
NEG_INF = -1e30

MLA_NOPE = 64
MLA_ROPE = 32
MLA_DQK = MLA_NOPE + MLA_ROPE
MLA_KV_RANK = 128
ROPE_THETA = 10000.0

S5_GROUP_CH = 16
S5_GROUPS = 16
S5_STATE = 64
S5_WIDTH = S5_GROUPS * S5_STATE

DIL_PAIRS = ((128, 1), (512, 4), (2048, 16))
DIL_SPAN = 128
T5_BUCKETS = 32
T5_MAX_DIST = 2048

DN_CONV = 4
DN_CHUNK = 64

FFN_HIDDEN = 2816
FFN_CHUNK = 256

VMEM_LIMIT_BYTES = 56 * 1024 * 1024
LANES = 128

MLA_SLAB = 512
S5_SLAB = 256
DIL_SLAB = 768
DN_SLAB = 1152
PROJ_COLS = MLA_SLAB + S5_SLAB + DIL_SLAB + DN_SLAB


def _dot(a, b, precision=None):
    return jnp.dot(a, b, preferred_element_type=F32, precision=precision)


def _dot_nt(a, b, precision=None):
    return lax.dot_general(a, b, (((1,), (1,)), ((), ())), preferred_element_type=F32, precision=precision)


def _const_spec(shape):
    nd = len(shape)
    return pl.BlockSpec(shape, lambda *_: (0,) * nd, pipeline_mode=pl.Buffered(1))


def _params(*sem):
    return pltpu.CompilerParams(dimension_semantics=sem, vmem_limit_bytes=VMEM_LIMIT_BYTES)


def _proj_body(x_ref, g_ref, w_ref, mla_ref, s5_ref, dil_ref, dn_ref):
    x = x_ref[...]
    n = x * lax.rsqrt(jnp.mean(x * x, axis=-1, keepdims=True) + EPS) * g_ref[...]
    nb = n.astype(BF16)
    start = 0
    for ref in (mla_ref, s5_ref, dil_ref, dn_ref):
        width = ref.shape[-1]
        ref[...] = _dot(nb, w_ref[:, start:start + width])
        start += width


def _proj(h, gain, w_big, *, batch, seq, tm=512):
    nt = seq // tm
    widths = (MLA_SLAB, S5_SLAB, DIL_SLAB, DN_SLAB)
    return pl.pallas_call(
        _proj_body,
        grid=(batch, nt),
        in_specs=[pl.BlockSpec((tm, D_MODEL), lambda b, i: (b * nt + i, 0)),
                  _const_spec((1, D_MODEL)),
                  _const_spec((D_MODEL, PROJ_COLS))],
        out_specs=[pl.BlockSpec((tm, w), lambda b, i: (i, b)) for w in widths],
        out_shape=[jax.ShapeDtypeStruct((seq, batch * w), F32) for w in widths],
        compiler_params=_params("parallel", "parallel"),
        name="proj",
    )(h, gain, w_big)


def _out_ffn_body(h_ref, y0_ref, y1_ref, y2_ref, y3_ref, wo_ref, g_ref, w1_ref, w3_ref, w2_ref, o_ref, acc_ref):
    h = h_ref[...]
    for i, y_ref in enumerate((y0_ref, y1_ref, y2_ref, y3_ref)):
        h = h + _dot(y_ref[...].astype(BF16), wo_ref[i * GROUP_W:(i + 1) * GROUP_W, :])
    n = h * lax.rsqrt(jnp.mean(h * h, axis=-1, keepdims=True) + EPS) * g_ref[...]
    nb = n.astype(BF16)
    acc_ref[...] = h

    def hidden_chunk(c, carry):
        cols = pl.ds(pl.multiple_of(c * FFN_CHUNK, FFN_CHUNK), FFN_CHUNK)
        a = _dot(nb, w1_ref[:, cols])
        b = _dot(nb, w3_ref[:, cols])
        z = (a * jax.nn.sigmoid(a) * b).astype(BF16)
        acc_ref[...] += _dot(z, w2_ref[cols, :])
        return carry
    lax.fori_loop(0, FFN_HIDDEN // FFN_CHUNK, hidden_chunk, 0)
    o_ref[...] = acc_ref[...]


def _out_ffn(h, ys, w_out, gain, w1, w3, w2, *, batch, seq, tm=512):
    nt = seq // tm
    row = pl.BlockSpec((tm, D_MODEL), lambda b, i: (b * nt + i, 0))
    slab = pl.BlockSpec((tm, GROUP_W), lambda b, i: (i, b))
    return pl.pallas_call(
        _out_ffn_body,
        grid=(batch, nt),
        in_specs=[row] + [slab] * 4 + [
            _const_spec((D_MODEL, D_MODEL)), _const_spec((1, D_MODEL)),
            _const_spec((D_MODEL, FFN_HIDDEN)), _const_spec((D_MODEL, FFN_HIDDEN)),
            _const_spec((FFN_HIDDEN, D_MODEL))],
        out_specs=row,
        out_shape=jax.ShapeDtypeStruct((batch * seq, D_MODEL), F32),
        scratch_shapes=[pltpu.VMEM((tm, D_MODEL), F32)],
        compiler_params=_params("parallel", "parallel"),
        name="out_ffn",
    )(h, *ys, w_out, gain, w1, w3, w2)


MLA_BLOCK = 256


def _mla_body(x_ref, qn_ref, kvn_ref, wuq_ref, wuk_ref, wuv_ref, gq_ref, gk_ref, cos_ref, sin_ref,
              o_ref, k_scr, v_scr, *, seq):
    blk = MLA_BLOCK
    qi = pl.program_id(1)
    lane = lax.broadcasted_iota(jnp.int32, (blk, LANES), 1)

    def rope(x, c, s):
        rot = jnp.where(lane < MLA_NOPE + MLA_ROPE // 2, pltpu.roll(x, LANES - MLA_ROPE // 2, 1),
                        pltpu.roll(x, MLA_ROPE // 2, 1))
        return x * c + rot * s

    def norm_head(x, g):
        ssq = jnp.sum(x * x, axis=-1, keepdims=True)
        return x * lax.rsqrt(ssq * (1.0 / MLA_DQK) + EPS) * g

    @pl.when(qi == 0)
    def _prepare_keys_values():
        def tile(i, carry):
            r0 = pl.multiple_of(i * blk, blk)
            rows = pl.ds(r0, blk)
            ckv = x_ref[rows, 256:384]
            k_rope = x_ref[rows, 384:512]
            kvn = ckv * lax.rsqrt(jnp.mean(ckv * ckv, axis=-1, keepdims=True) + EPS) * kvn_ref[...]
            kvn = kvn.astype(BF16)
            k_nope = _dot(kvn, wuk_ref[...])
            v = _dot(kvn, wuv_ref[...])
            k_rope = pltpu.roll(k_rope, MLA_NOPE, 1)
            c = cos_ref[rows, :]
            s = sin_ref[rows, :]
            for h in range(N_HEADS):
                cols = slice(h * LANES, (h + 1) * LANES)
                k = rope(norm_head(k_nope[:, cols] + k_rope, gk_ref[...]), c, s)
                k_scr[h, rows, :] = k.astype(BF16)
                v_scr[h, rows, :] = v[:, cols].astype(BF16)
            return carry
        lax.fori_loop(0, seq // blk, tile, 0)

    r0 = pl.multiple_of(qi * blk, blk)
    rows = pl.ds(r0, blk)
    cq = x_ref[rows, 0:256]
    qn = cq * lax.rsqrt(jnp.mean(cq * cq, axis=-1, keepdims=True) + EPS) * qn_ref[...]
    q_all = _dot(qn.astype(BF16), wuq_ref[...])
    c = cos_ref[rows, :]
    s = sin_ref[rows, :]
    row_pos = r0 + lax.broadcasted_iota(jnp.int32, (blk, blk), 0)
    col_pos = lax.broadcasted_iota(jnp.int32, (blk, blk), 1)
    outs = []
    for h in range(N_HEADS):
        q = rope(norm_head(q_all[:, h * LANES:(h + 1) * LANES], gq_ref[...]), c, s)
        q = (q * (MLA_DQK ** -0.5)).astype(BF16)

        def kv_step(j, carry, q=q, h=h):
            m, l, acc = carry
            c0 = pl.multiple_of(j * blk, blk)
            keys = pl.ds(c0, blk)
            logits = _dot_nt(q, k_scr[h, keys, :])
            logits = jnp.where(col_pos + c0 <= row_pos, logits, NEG_INF)
            m_new = jnp.maximum(m, jnp.max(logits, axis=-1, keepdims=True))
            alpha = jnp.exp(m - m_new)
            p = jnp.exp(logits - m_new)
            l = alpha * l + jnp.sum(p, axis=-1, keepdims=True)
            acc = alpha * acc + _dot(p.astype(BF16), v_scr[h, keys, :])
            return m_new, l, acc

        init = (jnp.full((blk, 1), NEG_INF, F32), jnp.zeros((blk, 1), F32), jnp.zeros((blk, LANES), F32))
        _, l, acc = lax.fori_loop(0, qi + 1, kv_step, init)
        outs.append(acc / l)
    o_ref[rows, 0:LANES] = outs[0] + pltpu.roll(outs[1], HEAD_DIM, 1)
    o_ref[rows, LANES:2 * LANES] = outs[2] + pltpu.roll(outs[3], HEAD_DIM, 1)


def _mla(slab, qn, kvn, wuq, wuk, wuv, gq, gk, cos, sin, *, batch, seq):
    nq = seq // MLA_BLOCK
    return pl.pallas_call(
        functools.partial(_mla_body, seq=seq),
        grid=(batch, nq),
        in_specs=[pl.BlockSpec((seq, MLA_SLAB), lambda b, i: (0, b)),
                  _const_spec((1, 256)), _const_spec((1, MLA_KV_RANK)),
                  _const_spec((256, N_HEADS * LANES)), _const_spec((MLA_KV_RANK, N_HEADS * LANES)),
                  _const_spec((MLA_KV_RANK, N_HEADS * LANES)),
                  _const_spec((1, LANES)), _const_spec((1, LANES)),
                  _const_spec((seq, LANES)), _const_spec((seq, LANES))],
        out_specs=pl.BlockSpec((seq, GROUP_W), lambda b, i: (0, b)),
        out_shape=jax.ShapeDtypeStruct((seq, batch * GROUP_W), F32),
        scratch_shapes=[pltpu.VMEM((N_HEADS, seq, LANES), BF16), pltpu.VMEM((N_HEADS, seq, LANES), BF16)],
        compiler_params=_params("parallel", "arbitrary"),
        name="mla",
    )(slab, qn, kvn, wuq, wuk, wuv, gq, gk, cos, sin)


def _s5_body(u_ref, lre_ref, lim_ref, ldt_ref, bre_ref, bim_ref, cre_ref, cim_ref, d_ref, wglu_ref, o_ref,
             a_scr, bbar_scr, h_scr, x_scr, *, batch, steps):
    n_tiles = S5_WIDTH // LANES

    @pl.when(pl.program_id(0) == 0)
    def _discretise():
        lr = lre_ref[...]
        li = lim_ref[...]
        dt = jnp.exp(ldt_ref[...])
        mag = jnp.exp(lr * dt)
        ar = mag * jnp.cos(li * dt)
        ai = mag * jnp.sin(li * dt)
        den = lr * lr + li * li
        nr = ar - 1.0
        zr = (nr * lr + ai * li) / den
        zi = (ai * lr - nr * li) / den
        for j in range(n_tiles):
            lanes = slice(j * LANES, (j + 1) * LANES)
            a_scr[0, j] = jnp.broadcast_to(ar[:, lanes], (batch, LANES))
            a_scr[1, j] = jnp.broadcast_to(ai[:, lanes], (batch, LANES))
        bre = bre_ref[...]
        bim = bim_ref[...]
        bbar_scr[0] = (zr * bre - zi * bim).astype(BF16)
        bbar_scr[1] = (zr * bim + zi * bre).astype(BF16)
        h_scr[...] = jnp.zeros_like(h_scr)

    for b in range(batch):
        ub = u_ref[:, b * GROUP_W:(b + 1) * GROUP_W].astype(BF16)
        rows_b = pl.ds(b, steps, stride=batch)
        for part in range(2):
            x = _dot(ub, bbar_scr[part])
            for j in range(n_tiles):
                x_scr[part, j, rows_b, :] = x[:, j * LANES:(j + 1) * LANES]
    ar = a_scr[0]
    ai = a_scr[1]

    def step(t, carry):
        hr, hi = carry
        rows = pl.ds(pl.multiple_of(t * batch, batch), batch)
        nhr = ar * hr - ai * hi + x_scr[0, :, rows, :]
        nhi = ar * hi + ai * hr + x_scr[1, :, rows, :]
        x_scr[0, :, rows, :] = nhr
        x_scr[1, :, rows, :] = nhi
        return nhr, nhi

    hr, hi = lax.fori_loop(0, steps, step, (h_scr[0], h_scr[1]), unroll=8)
    h_scr[0] = hr
    h_scr[1] = hi
    for b in range(batch):
        rows_b = pl.ds(b, steps, stride=batch)
        state = [jnp.concatenate([x_scr[part, j, rows_b, :] for j in range(n_tiles)], axis=1).astype(BF16)
                 for part in range(2)]
        u = u_ref[:, b * GROUP_W:(b + 1) * GROUP_W]
        y = _dot(state[0], cre_ref[...]) - _dot(state[1], cim_ref[...]) + d_ref[...] * u
        z = _dot(y.astype(BF16), wglu_ref[...])
        o_ref[:, b * GROUP_W:(b + 1) * GROUP_W] = z[:, :GROUP_W] * jax.nn.sigmoid(z[:, GROUP_W:])


def _s5(u, lre, lim, ldt, bre, bim, cre, cim, d, wglu, *, batch, seq, steps=256):
    n_tiles = S5_WIDTH // LANES
    state = lambda rows: pltpu.VMEM((2, n_tiles, rows, LANES), F32)
    return pl.pallas_call(
        functools.partial(_s5_body, batch=batch, steps=steps),
        grid=(seq // steps,),
        in_specs=[pl.BlockSpec((steps, batch * S5_SLAB), lambda i: (i, 0)),
                  _const_spec((1, S5_WIDTH)), _const_spec((1, S5_WIDTH)), _const_spec((1, S5_WIDTH)),
                  _const_spec((GROUP_W, S5_WIDTH)), _const_spec((GROUP_W, S5_WIDTH)),
                  _const_spec((S5_WIDTH, GROUP_W)), _const_spec((S5_WIDTH, GROUP_W)),
                  _const_spec((1, GROUP_W)), _const_spec((GROUP_W, 2 * GROUP_W))],
        out_specs=pl.BlockSpec((steps, batch * GROUP_W), lambda i: (i, 0)),
        out_shape=jax.ShapeDtypeStruct((seq, batch * GROUP_W), F32),
        scratch_shapes=[state(batch), pltpu.VMEM((2, GROUP_W, S5_WIDTH), BF16), state(batch),
                        state(steps * batch)],
        compiler_params=_params("arbitrary"),
        name="s5",
    )(u, lre, lim, ldt, bre, bim, cre, cim, d, wglu)


DIL_TILE = 256


def _pair_norm(x, gain2, lo_half):
    sq = x * x
    tot = jnp.sum(sq, axis=-1, keepdims=True)
    lo = jnp.sum(jnp.where(lo_half, sq, 0.0), axis=-1, keepdims=True)
    ms = jnp.where(lo_half, lo, tot - lo) * (1.0 / HEAD_DIM)
    return x * lax.rsqrt(ms + EPS) * gain2


def _dil_body(x_ref, gq_ref, gk_ref, band_ref, o_ref, bias_ref, q_scr, k_scr, v_scr, qd, kd, vd, ob, mb, lb,
              oa, ma, la, *, seq):
    span = DIL_SPAN
    n_blocks = seq // span
    n_tiles = GROUP_W // LANES
    lo_tile = lax.broadcasted_iota(jnp.int32, (DIL_TILE, LANES), 1) < HEAD_DIM
    lo_blk = lax.broadcasted_iota(jnp.int32, (span, LANES), 1) < HEAD_DIM

    for bh in range(len(DIL_PAIRS) * N_HEADS):
        profile = jnp.broadcast_to(band_ref[bh:bh + 1, :], (span, 2 * span))
        full = pltpu.roll(profile, 0, 1, stride=1, stride_axis=0)
        bias_ref[2 * bh] = full[:, :span]
        bias_ref[2 * bh + 1] = full[:, span:]

    def norm_tile(i, carry):
        rows = pl.ds(pl.multiple_of(i * DIL_TILE, DIL_TILE), DIL_TILE)
        for j in range(n_tiles):
            cols = lambda part: slice(part * GROUP_W + j * LANES, part * GROUP_W + (j + 1) * LANES)
            q = _pair_norm(x_ref[rows, cols(0)], gq_ref[...], lo_tile)
            q_scr[j, rows, :] = q * (HEAD_DIM ** -0.5)
            k_scr[j, rows, :] = _pair_norm(x_ref[rows, cols(1)], gk_ref[...], lo_tile)
            v_scr[j, rows, :] = x_ref[rows, cols(2)]
        return carry
    lax.fori_loop(0, seq // DIL_TILE, norm_tile, 0)

    for bi, (window, dil) in enumerate(DIL_PAIRS):
        sub_len = seq // dil
        nb = sub_len // span
        chunks = [(r, c) for r in range(dil) for c in range(nb)]

        def natural(r, c, dil=dil):
            return pl.ds(r + dil * c * span, span, stride=dil) if dil > 1 else pl.ds(c * span, span)

        if dil > 1:
            for r, c in chunks:
                dst = pl.ds((r * nb + c) * span, span)
                for j in range(n_tiles):
                    qd[j, dst, :] = q_scr[j, natural(r, c), :]
                    kd[j, dst, :] = k_scr[j, natural(r, c), :]
                    vd[j, dst, :] = v_scr[j, natural(r, c), :]
            q_src, k_src, v_src = qd, kd, vd
        else:
            q_src, k_src, v_src = q_scr, k_scr, v_scr
        o_dst, m_dst, l_dst = (oa, ma, la) if bi == 0 else (ob, mb, lb)

        def block(t, carry, bi=bi, nb=nb, q_src=q_src, k_src=k_src, v_src=v_src,
                  o_dst=o_dst, m_dst=m_dst, l_dst=l_dst):
            rows = pl.ds(pl.multiple_of(t * span, span), span)
            prev = pl.ds(pl.multiple_of(jnp.maximum(t - 1, 0) * span, span), span)
            has_prev = (t & (nb - 1)) != 0
            for j in range(n_tiles):
                q2 = q_src[j, rows, :]
                kc = k_src[j, rows, :].astype(BF16)
                vc = v_src[j, rows, :].astype(BF16)
                if nb > 1:
                    kp = k_src[j, prev, :].astype(BF16)
                    vp = v_src[j, prev, :].astype(BF16)
                res = []
                for a in range(2):
                    h = 2 * j + a
                    qa = jnp.where(lo_blk if a == 0 else jnp.logical_not(lo_blk), q2, 0.0).astype(BF16)
                    lc = _dot_nt(qa, kc) + bias_ref[(bi * N_HEADS + h) * 2 + 1]
                    m = jnp.max(lc, axis=-1, keepdims=True)
                    if nb > 1:
                        lp = _dot_nt(qa, kp) + bias_ref[(bi * N_HEADS + h) * 2]
                        lp = jnp.where(has_prev, lp, NEG_INF)
                        m = jnp.maximum(m, jnp.max(lp, axis=-1, keepdims=True))
                    pc = jnp.exp(lc - m)
                    l = jnp.sum(pc, axis=-1, keepdims=True)
                    o = _dot(pc.astype(BF16), vc)
                    if nb > 1:
                        pp = jnp.exp(lp - m)
                        l = l + jnp.sum(pp, axis=-1, keepdims=True)
                        o = o + _dot(pp.astype(BF16), vp)
                    res.append((o, m, l))
                (o0, m0, l0), (o1, m1, l1) = res
                o_dst[j, rows, :] = jnp.where(lo_blk, o0, o1)
                m_dst[j, rows, :] = jnp.where(lo_blk, m0, m1)
                l_dst[j, rows, :] = jnp.where(lo_blk, l0, l1)
            return carry
        lax.fori_loop(0, n_blocks, block, 0)

        if bi > 0:
            for r, c in chunks:
                src = pl.ds((r * nb + c) * span, span)
                nat = natural(r, c)
                for j in range(n_tiles):
                    m_old, m_in = ma[j, nat, :], mb[j, src, :]
                    m_new = jnp.maximum(m_old, m_in)
                    w_old, w_in = jnp.exp(m_old - m_new), jnp.exp(m_in - m_new)
                    oa[j, nat, :] = w_old * oa[j, nat, :] + w_in * ob[j, src, :]
                    la[j, nat, :] = w_old * la[j, nat, :] + w_in * lb[j, src, :]
                    ma[j, nat, :] = m_new

    def finish(i, carry):
        rows = pl.ds(pl.multiple_of(i * DIL_TILE, DIL_TILE), DIL_TILE)
        for j in range(n_tiles):
            o_ref[rows, j * LANES:(j + 1) * LANES] = oa[j, rows, :] / la[j, rows, :]
        return carry
    lax.fori_loop(0, seq // DIL_TILE, finish, 0)


def _dil(slab, gq2, gk2, band, *, batch, seq):
    big = lambda: pltpu.VMEM((GROUP_W // LANES, seq, LANES), F32)
    n_bias = 2 * len(DIL_PAIRS) * N_HEADS
    return pl.pallas_call(
        functools.partial(_dil_body, seq=seq),
        grid=(batch,),
        in_specs=[pl.BlockSpec((seq, DIL_SLAB), lambda b: (0, b)),
                  _const_spec((1, LANES)), _const_spec((1, LANES)),
                  _const_spec(band.shape)],
        out_specs=pl.BlockSpec((seq, GROUP_W), lambda b: (0, b)),
        out_shape=jax.ShapeDtypeStruct((seq, batch * GROUP_W), F32),
        scratch_shapes=[pltpu.VMEM((n_bias, DIL_SPAN, DIL_SPAN), F32)] + [big() for _ in range(12)],
        compiler_params=_params("parallel"),
        name="dilated",
    )(slab, gq2, gk2, band)


def _t5_bucket(dist):
    exact = T5_BUCKETS // 2
    df = jnp.maximum(dist, 1).astype(F32)
    large = exact + (jnp.log(df / exact) / math.log(T5_MAX_DIST / exact) * (T5_BUCKETS - exact)).astype(jnp.int32)
    large = jnp.minimum(large, T5_BUCKETS - 1)
    return jnp.where(dist < exact, dist, large)


def _dil_band(table):
    span = DIL_SPAN
    delta = span - jnp.arange(2 * span, dtype=jnp.int32)
    rows = []
    for _, dil in DIL_PAIRS:
        bucket = _t5_bucket(jnp.clip(delta, 0, span) * dil)
        onehot = (bucket[:, None] == jnp.arange(T5_BUCKETS, dtype=jnp.int32)[None, :]).astype(F32)
        vals = jnp.dot(onehot, table.astype(F32), precision=HIGHEST)
        rows.append(jnp.where((delta >= 0)[:, None], vals, NEG_INF).T)
    return jnp.concatenate(rows, axis=0)


DN_TILE = 256
DN_GROUP = 4


def _softplus(x):
    return jnp.maximum(x, 0.0) + jnp.log1p(jnp.exp(-jnp.abs(x)))


def _pair_l2(x, lo_half):
    sq = x * x
    tot = jnp.sum(sq, axis=-1, keepdims=True)
    lo = jnp.sum(jnp.where(lo_half, sq, 0.0), axis=-1, keepdims=True)
    return x * lax.rsqrt(jnp.where(lo_half, lo, tot - lo) + EPS)


def _dn_body(x_ref, cw_ref, alog_ref, dtb_ref, on_ref, o_ref, q_scr, k_scr, v_scr, g_scr, b_scr, w_scr, a_scr,
             s_scr, *, seq):
    c = DN_CHUNK
    w = GROUP_W
    lo_tile = lax.broadcasted_iota(jnp.int32, (DN_TILE, LANES), 1) < HEAD_DIM

    def prep_tile(i, carry):
        r0 = pl.multiple_of(i * DN_TILE, DN_TILE)
        rows = pl.ds(r0, DN_TILE)
        cur = x_ref[rows, 0:3 * w]
        halo_rows = pl.ds(pl.multiple_of(jnp.maximum(r0 - 8, 0), 8), 8)
        halo = jnp.where(i > 0, x_ref[halo_rows, 0:3 * w], 0.0)
        ext = jnp.concatenate([halo, cur], axis=0)
        acc = cw_ref[DN_CONV - 1:DN_CONV, :] * cur
        for j in range(DN_CONV - 1):
            acc = acc + cw_ref[j:j + 1, :] * pltpu.roll(ext, DN_CONV - 1 - j, 0)[8:, :]
        y = acc * jax.nn.sigmoid(acc)
        for j in range(w // LANES):
            cols = lambda part: slice(part * w + j * LANES, part * w + (j + 1) * LANES)
            q_scr[rows, j * LANES:(j + 1) * LANES] = _pair_l2(y[:, cols(0)], lo_tile) * (HEAD_DIM ** -0.5)
            k_scr[rows, j * LANES:(j + 1) * LANES] = _pair_l2(y[:, cols(1)], lo_tile)
            v_scr[rows, j * LANES:(j + 1) * LANES] = y[:, cols(2)]
        ab = x_ref[rows, 3 * w:3 * w + LANES]
        g = -jnp.exp(alog_ref[...]) * _softplus(ab + dtb_ref[...])
        g_scr[rows, :] = _dot(chunk_tril, _dot(g, expand, HIGHEST), HIGHEST)
        beta = pltpu.roll(jax.nn.sigmoid(ab), LANES - N_HEADS, 1)
        b_scr[rows, :] = _dot(beta, expand, HIGHEST)
        return carry

    ri = lax.broadcasted_iota(jnp.int32, (c, w), 0)
    ci = lax.broadcasted_iota(jnp.int32, (c, w), 1)
    cj = jnp.bitwise_and(ci, HEAD_DIM - 1)
    causal = ri >= cj
    strict = ri > cj
    eye4 = (ri == cj).astype(F32)
    same_sub = lax.shift_right_logical(ri, 4) == lax.shift_right_logical(cj, 4)
    bi_r = lax.broadcasted_iota(jnp.int32, (w, w), 0)
    bi_c = lax.broadcasted_iota(jnp.int32, (w, w), 1)
    same_head = lax.shift_right_logical(bi_r, 6) == lax.shift_right_logical(bi_c, 6)
    block_mask = same_head.astype(F32)
    chunk_tril = (same_head & (jnp.bitwise_and(bi_r, c - 1) >= jnp.bitwise_and(bi_c, c - 1))).astype(F32)
    expand = (lax.broadcasted_iota(jnp.int32, (LANES, w), 0)
              == lax.shift_right_logical(lax.broadcasted_iota(jnp.int32, (LANES, w), 1), 6)).astype(F32)
    lax.fori_loop(0, seq // DN_TILE, prep_tile, 0)

    def per_head(a, b):
        bd = jnp.where(same_head, jnp.concatenate([b.astype(BF16)] * N_HEADS, axis=0), 0.0)
        return _dot(a.astype(BF16), bd)

    def solve_chunk(q, k, v, gc, beta):
        g_row = jnp.sum(gc * eye4, axis=0, keepdims=True)
        g_last = gc[c - 1:c, :]
        decay = jnp.exp(jnp.where(causal, gc - g_row, NEG_INF))
        kb = k * beta
        k_bd = jnp.where(same_head, jnp.concatenate([k.astype(BF16)] * N_HEADS, axis=0), 0.0)
        lmat = jnp.where(strict, _dot_nt(kb.astype(BF16), k_bd) * decay, 0.0)
        a_qk = jnp.where(causal, _dot_nt(q.astype(BF16), k_bd) * decay, 0.0)
        p = jnp.where(same_sub, -lmat, 0.0)
        t_diag = eye4 + p
        for _ in range(3):
            p = per_head(p, p)
            t_diag = t_diag + per_head(t_diag, p)
        nil = per_head(t_diag, jnp.where(same_sub, 0.0, lmat))
        nil2 = per_head(nil, nil)
        t_inv = per_head(eye4 - nil + nil2 - per_head(nil, nil2), t_diag)
        eg = jnp.exp(gc)
        return (per_head(t_inv, kb * eg), per_head(t_inv, v * beta), a_qk, q * eg, k * jnp.exp(g_last - gc))

    def solve_group(i, carry):
        base = pl.multiple_of(i * (DN_GROUP * c), DN_GROUP * c)
        rows = [pl.ds(base + u * c, c) for u in range(DN_GROUP)]
        loaded = [tuple(ref[r, :] for ref in (q_scr, k_scr, v_scr, g_scr, b_scr)) for r in rows]
        solved = [solve_chunk(*args) for args in loaded]
        for r, outs in zip(rows, solved):
            for ref, val in zip((w_scr, v_scr, a_scr, q_scr, k_scr), outs):
                ref[r, :] = val
        return carry
    lax.fori_loop(0, seq // (DN_GROUP * c), solve_group, 0)

    s_scr[...] = jnp.zeros_like(s_scr)

    def state_chunk(n, carry):
        rows = pl.ds(pl.multiple_of(n * c, c), c)
        state = s_scr[...]
        state_b = state.astype(BF16)
        v_new = v_scr[rows, :] - _dot(w_scr[rows, :].astype(BF16), state_b)
        o_ref[rows, :] = _dot(q_scr[rows, :].astype(BF16), state_b) + per_head(a_scr[rows, :], v_new)
        upd = lax.dot_general(k_scr[rows, :].astype(BF16), v_new.astype(BF16), (((0,), (0,)), ((), ())),
                              preferred_element_type=F32)
        d_last = jnp.exp(g_scr[pl.ds(pl.multiple_of(n * c, c) + (c - 1), 1), :])
        s_scr[...] = state * d_last + upd * block_mask
        return carry
    lax.fori_loop(0, seq // c, state_chunk, 0, unroll=2)

    def finish_tile(i, carry):
        rows = pl.ds(pl.multiple_of(i * DN_TILE, DN_TILE), DN_TILE)
        o = o_ref[rows, :]
        ms = _dot(o * o, block_mask, HIGHEST) * (1.0 / HEAD_DIM)
        gate = x_ref[rows, 3 * w + LANES:4 * w + LANES]
        o_ref[rows, :] = o * lax.rsqrt(ms + EPS) * on_ref[...] * (gate * jax.nn.sigmoid(gate))
        return carry
    lax.fori_loop(0, seq // DN_TILE, finish_tile, 0)


def _dn(slab, conv_w, a_log, dt_bias, o_norm, *, batch, seq):
    wide = lambda: pltpu.VMEM((seq, GROUP_W), F32)
    return pl.pallas_call(
        functools.partial(_dn_body, seq=seq),
        grid=(batch,),
        in_specs=[pl.BlockSpec((seq, DN_SLAB), lambda b: (0, b)),
                  _const_spec((DN_CONV, 3 * GROUP_W)), _const_spec((1, LANES)), _const_spec((1, LANES)),
                  _const_spec((1, GROUP_W))],
        out_specs=pl.BlockSpec((seq, GROUP_W), lambda b: (0, b)),
        out_shape=jax.ShapeDtypeStruct((seq, batch * GROUP_W), F32),
        scratch_shapes=[wide() for _ in range(7)] + [pltpu.VMEM((GROUP_W, GROUP_W), F32)],
        compiler_params=_params("parallel"),
        name="deltanet",
    )(slab, conv_w, a_log, dt_bias, o_norm)


def _row(v, width=None):
    v = v.astype(F32).reshape(1, -1)
    if width is not None and v.shape[1] < width:
        v = jnp.pad(v, ((0, 0), (0, width - v.shape[1])))
    return v


def _prep_w_in(w):
    z = lambda n: jnp.zeros((D_MODEL, n), w.dtype)
    cols = [w[:, 0:416], z(96), w[:, 416:672], w[:, 672:1440], w[:, 1440:2208], w[:, 2208:2216], z(120),
            w[:, 2216:2472]]
    return jnp.concatenate(cols, axis=1).astype(BF16)


def _pad_heads(w, per_head, lo, hi):
    k = w.shape[0]
    w = w.reshape(k, N_HEADS, per_head)[:, :, lo:hi]
    w = jnp.pad(w, ((0, 0), (0, 0), (0, LANES - (hi - lo))))
    return w.reshape(k, N_HEADS * LANES).astype(BF16)


def _rope_tables(seq):
    half = MLA_ROPE // 2
    pos = jnp.arange(seq, dtype=F32)
    freqs = ROPE_THETA ** (-jnp.arange(half, dtype=F32) / half)
    ang = pos[:, None] * freqs[None, :]
    cos, sin = jnp.cos(ang), jnp.sin(ang)
    ones = jnp.ones((seq, MLA_NOPE), F32)
    zeros = jnp.zeros((seq, MLA_NOPE), F32)
    tail1 = jnp.ones((seq, LANES - MLA_DQK), F32)
    tail0 = jnp.zeros((seq, LANES - MLA_DQK), F32)
    return (jnp.concatenate([ones, cos, cos, tail1], axis=1),
            jnp.concatenate([zeros, -sin, sin, tail0], axis=1))


def _block_diag(blocks):
    g, r, c = blocks.shape
    eye = jnp.eye(g, dtype=blocks.dtype)
    return (blocks[:, :, None, :] * eye[:, None, :, None]).reshape(g * r, g * c)


def _mla_layer(slab, p, l, *, batch, seq):
    w_ukv = p["mla_w_ukv"][l]
    cos, sin = _rope_tables(seq)
    return _mla(slab, _row(p["mla_q_norm"][l]), _row(p["mla_kv_norm"][l]),
                _pad_heads(p["mla_w_uq"][l], MLA_DQK, 0, MLA_DQK),
                _pad_heads(w_ukv, MLA_NOPE + HEAD_DIM, 0, MLA_NOPE),
                _pad_heads(w_ukv, MLA_NOPE + HEAD_DIM, MLA_NOPE, MLA_NOPE + HEAD_DIM),
                _row(p["mla_qk_q"][l], LANES), _row(p["mla_qk_k"][l], LANES), cos, sin, batch=batch, seq=seq)


def _dil_layer(slab, p, l, *, batch, seq):
    pair = lambda g: jnp.tile(g.astype(F32).reshape(1, HEAD_DIM), (1, LANES // HEAD_DIM))
    return _dil(slab, pair(p["dil_q_norm"][l]), pair(p["dil_k_norm"][l]), _dil_band(p["t5_bias"]),
                batch=batch, seq=seq)


def _dn_layer(slab, p, l, *, batch, seq):
    return _dn(slab, p["dn_conv"][l].astype(F32), _row(p["dn_a_log"][l], LANES), _row(p["dn_dt_bias"][l], LANES),
               jnp.tile(p["dn_o_norm"][l].astype(F32).reshape(1, HEAD_DIM), (1, N_HEADS)), batch=batch, seq=seq)


def _s5_layer(u, p, l, *, batch, seq):
    state_row = lambda v: v.astype(F32).reshape(1, S5_WIDTH)
    ldt = jnp.broadcast_to(p["s5_log_dt"][l][:, None], (S5_GROUPS, S5_STATE))
    bre = _block_diag(jnp.swapaxes(p["s5_b_re"][l], 1, 2).astype(F32))
    bim = _block_diag(jnp.swapaxes(p["s5_b_im"][l], 1, 2).astype(F32))
    cre = _block_diag(jnp.swapaxes(p["s5_c_re"][l], 1, 2)).astype(BF16)
    cim = _block_diag(jnp.swapaxes(p["s5_c_im"][l], 1, 2)).astype(BF16)
    return _s5(u, state_row(p["s5_lambda_re"][l]), state_row(p["s5_lambda_im"][l]), state_row(ldt),
               bre, bim, cre, cim, _row(p["s5_d"][l]), p["s5_w_glu"][l].astype(BF16), batch=batch, seq=seq,
               steps=min(256, seq))


def kernel(x, attn_norm, w_in, w_out, mla_q_norm, mla_kv_norm, mla_w_uq, mla_w_ukv, mla_qk_q, mla_qk_k,
           s5_lambda_re, s5_lambda_im, s5_log_dt, s5_b_re, s5_b_im, s5_c_re, s5_c_im, s5_d, s5_w_glu,
           dil_q_norm, dil_k_norm, t5_bias, dn_conv, dn_a_log, dn_dt_bias, dn_o_norm,
           ffn_norm, ffn_w1, ffn_w3, ffn_w2):
    p = dict(mla_q_norm=mla_q_norm, mla_kv_norm=mla_kv_norm, mla_w_uq=mla_w_uq, mla_w_ukv=mla_w_ukv,
             mla_qk_q=mla_qk_q, mla_qk_k=mla_qk_k, s5_lambda_re=s5_lambda_re, s5_lambda_im=s5_lambda_im,
             s5_log_dt=s5_log_dt, s5_b_re=s5_b_re, s5_b_im=s5_b_im, s5_c_re=s5_c_re, s5_c_im=s5_c_im, s5_d=s5_d,
             s5_w_glu=s5_w_glu, dil_q_norm=dil_q_norm, dil_k_norm=dil_k_norm, t5_bias=t5_bias, dn_conv=dn_conv,
             dn_a_log=dn_a_log, dn_dt_bias=dn_dt_bias, dn_o_norm=dn_o_norm)
    batch, seq, _ = x.shape
    h = x.reshape(batch * seq, D_MODEL)
    for l in range(attn_norm.shape[0]):
        mla_in, s5_in, dil_in, dn_in = _proj(h, _row(attn_norm[l]), _prep_w_in(w_in[l]), batch=batch, seq=seq)
        ys = [_mla_layer(mla_in, p, l, batch=batch, seq=seq),
              _s5_layer(s5_in, p, l, batch=batch, seq=seq),
              _dil_layer(dil_in, p, l, batch=batch, seq=seq),
              _dn_layer(dn_in, p, l, batch=batch, seq=seq)]
        h = _out_ffn(h, ys, w_out[l].astype(BF16), _row(ffn_norm[l]),
                     ffn_w1[l].astype(BF16), ffn_w3[l].astype(BF16), ffn_w2[l].astype(BF16), batch=batch, seq=seq)
    return h.reshape(batch, seq, D_MODEL)
```

```python
import functools
import math

import jax
import jax.numpy as jnp
import numpy as np
from jax import lax
from jax.experimental import pallas as pl
from jax.experimental.pallas import tpu as pltpu

F32 = jnp.float32
BF16 = jnp.bfloat16
HIGHEST = lax.Precision.HIGHEST

D_MODEL = 1024
GROUP_W = 256
HEAD_DIM = 64
N_HEADS = 4
EPS = 1e-6
NEG_INF = -1e30

MLA_NOPE = 64
MLA_ROPE = 32
MLA_DQK = MLA_NOPE + MLA_ROPE
MLA_KV_RANK = 128
ROPE_THETA = 10000.0

S5_GROUP_CH = 16
S5_GROUPS = 16
S5_STATE = 64
S5_WIDTH = S5_GROUPS * S5_STATE

DIL_PAIRS = ((128, 1), (512, 4), (2048, 16))
DIL_SPAN = 128
T5_BUCKETS = 32
T5_MAX_DIST = 2048

DN_CONV = 4
DN_CHUNK = 64

FFN_HIDDEN = 2816
FFN_CHUNK = 256

VMEM_LIMIT_BYTES = 56 * 1024 * 1024
LANES = 128

MLA_SLAB = 512
S5_SLAB = 256
DIL_SLAB = 768
DN_SLAB = 1152
PROJ_COLS = MLA_SLAB + S5_SLAB + DIL_SLAB + DN_SLAB


def _dot(a, b, precision=None):
    return jnp.dot(a, b, preferred_element_type=F32, precision=precision)


def _dot_nt(a, b, precision=None):
    return lax.dot_general(a, b, (((1,), (1,)), ((), ())), preferred_element_type=F32, precision=precision)


def _const_spec(shape):
    nd = len(shape)
    return pl.BlockSpec(shape, lambda *_: (0,) * nd, pipeline_mode=pl.Buffered(1))


def _params(*sem):
    return pltpu.CompilerParams(dimension_semantics=sem, vmem_limit_bytes=VMEM_LIMIT_BYTES)


def _proj_body(x_ref, g_ref, w_ref, mla_ref, s5_ref, dil_ref, dn_ref):
    x = x_ref[...]
    n = x * lax.rsqrt(jnp.mean(x * x, axis=-1, keepdims=True) + EPS) * g_ref[...]
    nb = n.astype(BF16)
    start = 0
    for ref in (mla_ref, s5_ref, dil_ref, dn_ref):
        width = ref.shape[-1]
        ref[...] = _dot(nb, w_ref[:, start:start + width])
        start += width


def _proj(h, gain, w_big, *, batch, seq, tm=512):
    nt = seq // tm
    widths = (MLA_SLAB, S5_SLAB, DIL_SLAB, DN_SLAB)
    return pl.pallas_call(
        _proj_body,
        grid=(batch, nt),
        in_specs=[pl.BlockSpec((tm, D_MODEL), lambda b, i: (b * nt + i, 0)),
                  _const_spec((1, D_MODEL)),
                  _const_spec((D_MODEL, PROJ_COLS))],
        out_specs=[pl.BlockSpec((tm, w), lambda b, i: (i, b)) for w in widths],
        out_shape=[jax.ShapeDtypeStruct((seq, batch * w), F32) for w in widths],
        compiler_params=_params("parallel", "parallel"),
        name="proj",
    )(h, gain, w_big)


def _out_ffn_body(h_ref, y0_ref, y1_ref, y2_ref, y3_ref, wo_ref, g_ref, w1_ref, w3_ref, w2_ref, o_ref, acc_ref):
    h = h_ref[...]
    for i, y_ref in enumerate((y0_ref, y1_ref, y2_ref, y3_ref)):
        h = h + _dot(y_ref[...].astype(BF16), wo_ref[i * GROUP_W:(i + 1) * GROUP_W, :])
    n = h * lax.rsqrt(jnp.mean(h * h, axis=-1, keepdims=True) + EPS) * g_ref[...]
    nb = n.astype(BF16)
    acc_ref[...] = h

    def hidden_chunk(c, carry):
        cols = pl.ds(pl.multiple_of(c * FFN_CHUNK, FFN_CHUNK), FFN_CHUNK)
        a = _dot(nb, w1_ref[:, cols])
        b = _dot(nb, w3_ref[:, cols])
        z = (a * jax.nn.sigmoid(a) * b).astype(BF16)
        acc_ref[...] += _dot(z, w2_ref[cols, :])
        return carry
    lax.fori_loop(0, FFN_HIDDEN // FFN_CHUNK, hidden_chunk, 0)
    o_ref[...] = acc_ref[...]


def _out_ffn(h, ys, w_out, gain, w1, w3, w2, *, batch, seq, tm=512):
    nt = seq // tm
    row = pl.BlockSpec((tm, D_MODEL), lambda b, i: (b * nt + i, 0))
    slab = pl.BlockSpec((tm, GROUP_W), lambda b, i: (i, b))
    return pl.pallas_call(
        _out_ffn_body,
        grid=(batch, nt),
        in_specs=[row] + [slab] * 4 + [
            _const_spec((D_MODEL, D_MODEL)), _const_spec((1, D_MODEL)),
            _const_spec((D_MODEL, FFN_HIDDEN)), _const_spec((D_MODEL, FFN_HIDDEN)),
            _const_spec((FFN_HIDDEN, D_MODEL))],
        out_specs=row,
        out_shape=jax.ShapeDtypeStruct((batch * seq, D_MODEL), F32),
        scratch_shapes=[pltpu.VMEM((tm, D_MODEL), F32)],
        compiler_params=_params("parallel", "parallel"),
        name="out_ffn",
    )(h, *ys, w_out, gain, w1, w3, w2)


MLA_BLOCK = 256


def _mla_body(x_ref, qn_ref, kvn_ref, wuq_ref, wuk_ref, wuv_ref, gq_ref, gk_ref, cos_ref, sin_ref,
              o_ref, k_scr, v_scr, *, seq):
    blk = MLA_BLOCK
    qi = pl.program_id(1)
    lane = lax.broadcasted_iota(jnp.int32, (blk, LANES), 1)

    def rope(x, c, s):
        rot = jnp.where(lane < MLA_NOPE + MLA_ROPE // 2, pltpu.roll(x, LANES - MLA_ROPE // 2, 1),
                        pltpu.roll(x, MLA_ROPE // 2, 1))
        return x * c + rot * s

    def norm_head(x, g):
        ssq = jnp.sum(x * x, axis=-1, keepdims=True)
        return x * lax.rsqrt(ssq * (1.0 / MLA_DQK) + EPS) * g

    @pl.when(qi == 0)
    def _prepare_keys_values():
        def tile(i, carry):
            r0 = pl.multiple_of(i * blk, blk)
            rows = pl.ds(r0, blk)
            ckv = x_ref[rows, 256:384]
            k_rope = x_ref[rows, 384:512]
            kvn = ckv * lax.rsqrt(jnp.mean(ckv * ckv, axis=-1, keepdims=True) + EPS) * kvn_ref[...]
            kvn = kvn.astype(BF16)
            k_nope = _dot(kvn, wuk_ref[...])
            v = _dot(kvn, wuv_ref[...])
            k_rope = pltpu.roll(k_rope, MLA_NOPE, 1)
            c = cos_ref[rows, :]
            s = sin_ref[rows, :]
            for h in range(N_HEADS):
                cols = slice(h * LANES, (h + 1) * LANES)
                k = rope(norm_head(k_nope[:, cols] + k_rope, gk_ref[...]), c, s)
                k_scr[h, rows, :] = k.astype(BF16)
                v_scr[h, rows, :] = v[:, cols].astype(BF16)
            return carry
        lax.fori_loop(0, seq // blk, tile, 0)

    r0 = pl.multiple_of(qi * blk, blk)
    rows = pl.ds(r0, blk)
    cq = x_ref[rows, 0:256]
    qn = cq * lax.rsqrt(jnp.mean(cq * cq, axis=-1, keepdims=True) + EPS) * qn_ref[...]
    q_all = _dot(qn.astype(BF16), wuq_ref[...])
    c = cos_ref[rows, :]
    s = sin_ref[rows, :]
    row_pos = lax.broadcasted_iota(jnp.int32, (blk, blk), 0)
    col_pos = lax.broadcasted_iota(jnp.int32, (blk, blk), 1)
    qs = []
    for h in range(N_HEADS):
        q = rope(norm_head(q_all[:, h * LANES:(h + 1) * LANES], gq_ref[...]), c, s)
        qs.append((q * (MLA_DQK ** -0.5)).astype(BF16))

    def kv_block(j, carry, diagonal):
        keys = pl.ds(pl.multiple_of(j * blk, blk), blk)
        logits = [_dot_nt(qs[h], k_scr[h, keys, :]) for h in range(N_HEADS)]
        if diagonal:
            logits = [jnp.where(col_pos <= row_pos, x, NEG_INF) for x in logits]
        m_new = [jnp.maximum(carry[h][0], jnp.max(logits[h], axis=-1, keepdims=True)) for h in range(N_HEADS)]
        alpha = [jnp.exp(carry[h][0] - m_new[h]) for h in range(N_HEADS)]
        p = [jnp.exp(logits[h] - m_new[h]) for h in range(N_HEADS)]
        l_new = [alpha[h] * carry[h][1] + jnp.sum(p[h], axis=-1, keepdims=True) for h in range(N_HEADS)]
        pv = [_dot(p[h].astype(BF16), v_scr[h, keys, :]) for h in range(N_HEADS)]
        return tuple((m_new[h], l_new[h], alpha[h] * carry[h][2] + pv[h]) for h in range(N_HEADS))

    init = tuple((jnp.full((blk, 1), NEG_INF, F32), jnp.zeros((blk, 1), F32), jnp.zeros((blk, LANES), F32))
                 for _ in range(N_HEADS))
    carry = lax.fori_loop(0, qi, functools.partial(kv_block, diagonal=False), init)
    carry = kv_block(qi, carry, diagonal=True)
    outs = [acc / l for _, l, acc in carry]
    o_ref[rows, 0:LANES] = outs[0] + pltpu.roll(outs[1], HEAD_DIM, 1)
    o_ref[rows, LANES:2 * LANES] = outs[2] + pltpu.roll(outs[3], HEAD_DIM, 1)


def _mla(slab, qn, kvn, wuq, wuk, wuv, gq, gk, cos, sin, *, batch, seq):
    nq = seq // MLA_BLOCK
    return pl.pallas_call(
        functools.partial(_mla_body, seq=seq),
        grid=(batch, nq),
        in_specs=[pl.BlockSpec((seq, MLA_SLAB), lambda b, i: (0, b)),
                  _const_spec((1, 256)), _const_spec((1, MLA_KV_RANK)),
                  _const_spec((256, N_HEADS * LANES)), _const_spec((MLA_KV_RANK, N_HEADS * LANES)),
                  _const_spec((MLA_KV_RANK, N_HEADS * LANES)),
                  _const_spec((1, LANES)), _const_spec((1, LANES)),
                  _const_spec((seq, LANES)), _const_spec((seq, LANES))],
        out_specs=pl.BlockSpec((seq, GROUP_W), lambda b, i: (0, b)),
        out_shape=jax.ShapeDtypeStruct((seq, batch * GROUP_W), F32),
        scratch_shapes=[pltpu.VMEM((N_HEADS, seq, LANES), BF16), pltpu.VMEM((N_HEADS, seq, LANES), BF16)],
        compiler_params=_params("parallel", "arbitrary"),
        name="mla",
    )(slab, qn, kvn, wuq, wuk, wuv, gq, gk, cos, sin)


def _s5_body(u_ref, lre_ref, lim_ref, ldt_ref, bre_ref, bim_ref, cre_ref, cim_ref, d_ref, wglu_ref, o_ref,
             a_scr, bbar_scr, h_scr, x_scr, *, batch, steps):
    n_tiles = S5_WIDTH // LANES

    @pl.when(pl.program_id(0) == 0)
    def _discretise():
        lr = lre_ref[...]
        li = lim_ref[...]
        dt = jnp.exp(ldt_ref[...])
        mag = jnp.exp(lr * dt)
        ar = mag * jnp.cos(li * dt)
        ai = mag * jnp.sin(li * dt)
        den = lr * lr + li * li
        nr = ar - 1.0
        zr = (nr * lr + ai * li) / den
        zi = (ai * lr - nr * li) / den
        for j in range(n_tiles):
            lanes = slice(j * LANES, (j + 1) * LANES)
            a_scr[0, j] = jnp.broadcast_to(ar[:, lanes], (batch, LANES))
            a_scr[1, j] = jnp.broadcast_to(ai[:, lanes], (batch, LANES))
        bre = bre_ref[...]
        bim = bim_ref[...]
        bbar_scr[0] = (zr * bre - zi * bim).astype(BF16)
        bbar_scr[1] = (zr * bim + zi * bre).astype(BF16)
        h_scr[...] = jnp.zeros_like(h_scr)

    for b in range(batch):
        ub = u_ref[:, b * GROUP_W:(b + 1) * GROUP_W].astype(BF16)
        rows_b = pl.ds(b, steps, stride=batch)
        for part in range(2):
            x = _dot(ub, bbar_scr[part])
            for j in range(n_tiles):
                x_scr[part, j, rows_b, :] = x[:, j * LANES:(j + 1) * LANES]
    ar = a_scr[0]
    ai = a_scr[1]

    def step(t, carry):
        hr, hi = carry
        rows = pl.ds(pl.multiple_of(t * batch, batch), batch)
        nhr = ar * hr - ai * hi + x_scr[0, :, rows, :]
        nhi = ar * hi + ai * hr + x_scr[1, :, rows, :]
        x_scr[0, :, rows, :] = nhr
        x_scr[1, :, rows, :] = nhi
        return nhr, nhi

    hr, hi = lax.fori_loop(0, steps, step, (h_scr[0], h_scr[1]), unroll=8)
    h_scr[0] = hr
    h_scr[1] = hi
    for b in range(batch):
        rows_b = pl.ds(b, steps, stride=batch)
        state = [jnp.concatenate([x_scr[part, j, rows_b, :] for j in range(n_tiles)], axis=1).astype(BF16)
                 for part in range(2)]
        u = u_ref[:, b * GROUP_W:(b + 1) * GROUP_W]
        y = _dot(state[0], cre_ref[...]) - _dot(state[1], cim_ref[...]) + d_ref[...] * u
        z = _dot(y.astype(BF16), wglu_ref[...])
        o_ref[:, b * GROUP_W:(b + 1) * GROUP_W] = z[:, :GROUP_W] * jax.nn.sigmoid(z[:, GROUP_W:])


def _s5(u, lre, lim, ldt, bre, bim, cre, cim, d, wglu, *, batch, seq, steps=256):
    n_tiles = S5_WIDTH // LANES
    state = lambda rows: pltpu.VMEM((2, n_tiles, rows, LANES), F32)
    return pl.pallas_call(
        functools.partial(_s5_body, batch=batch, steps=steps),
        grid=(seq // steps,),
        in_specs=[pl.BlockSpec((steps, batch * S5_SLAB), lambda i: (i, 0)),
                  _const_spec((1, S5_WIDTH)), _const_spec((1, S5_WIDTH)), _const_spec((1, S5_WIDTH)),
                  _const_spec((GROUP_W, S5_WIDTH)), _const_spec((GROUP_W, S5_WIDTH)),
                  _const_spec((S5_WIDTH, GROUP_W)), _const_spec((S5_WIDTH, GROUP_W)),
                  _const_spec((1, GROUP_W)), _const_spec((GROUP_W, 2 * GROUP_W))],
        out_specs=pl.BlockSpec((steps, batch * GROUP_W), lambda i: (i, 0)),
        out_shape=jax.ShapeDtypeStruct((seq, batch * GROUP_W), F32),
        scratch_shapes=[state(batch), pltpu.VMEM((2, GROUP_W, S5_WIDTH), BF16), state(batch),
                        state(steps * batch)],
        compiler_params=_params("arbitrary"),
        name="s5",
    )(u, lre, lim, ldt, bre, bim, cre, cim, d, wglu)


DIL_TILE = 256
DIL_GROUP = 2


def _pair_norm(x, gain2, lo_half):
    sq = x * x
    tot = jnp.sum(sq, axis=-1, keepdims=True)
    lo = jnp.sum(jnp.where(lo_half, sq, 0.0), axis=-1, keepdims=True)
    ms = jnp.where(lo_half, lo, tot - lo) * (1.0 / HEAD_DIM)
    return x * lax.rsqrt(ms + EPS) * gain2


def _dil_body(x_ref, gq_ref, gk_ref, band_ref, o_ref, bias_ref, q_scr, k_scr, v_scr, qd, kd, vd, ob, mb, lb,
              oa, ma, la, *, seq):
    span = DIL_SPAN
    n_blocks = seq // span
    n_tiles = GROUP_W // LANES
    lo_tile = lax.broadcasted_iota(jnp.int32, (DIL_TILE, LANES), 1) < HEAD_DIM
    lo_blk = lax.broadcasted_iota(jnp.int32, (span, LANES), 1) < HEAD_DIM

    for bh in range(len(DIL_PAIRS) * N_HEADS):
        profile = jnp.broadcast_to(band_ref[bh:bh + 1, :], (span, 2 * span))
        bias_ref[bh] = pltpu.roll(profile, 0, 1, stride=1, stride_axis=0)
    in_current = lax.broadcasted_iota(jnp.int32, (span, 2 * span), 1) >= span

    def norm_tile(i, carry):
        rows = pl.ds(pl.multiple_of(i * DIL_TILE, DIL_TILE), DIL_TILE)
        for j in range(n_tiles):
            cols = lambda part: slice(part * GROUP_W + j * LANES, part * GROUP_W + (j + 1) * LANES)
            q = _pair_norm(x_ref[rows, cols(0)], gq_ref[...], lo_tile)
            q_scr[j, rows, :] = q * (HEAD_DIM ** -0.5)
            k_scr[j, rows, :] = _pair_norm(x_ref[rows, cols(1)], gk_ref[...], lo_tile)
            v_scr[j, rows, :] = x_ref[rows, cols(2)]
        return carry
    lax.fori_loop(0, seq // DIL_TILE, norm_tile, 0)

    for bi, (window, dil) in enumerate(DIL_PAIRS):
        sub_len = seq // dil
        nb = sub_len // span
        chunks = [(r, c) for r in range(dil) for c in range(nb)]

        def natural(r, c, dil=dil):
            return pl.ds(r + dil * c * span, span, stride=dil) if dil > 1 else pl.ds(c * span, span)

        if dil > 1:
            for r, c in chunks:
                dst = pl.ds((r * nb + c) * span, span)
                for j in range(n_tiles):
                    qd[j, dst, :] = q_scr[j, natural(r, c), :]
                    kd[j, dst, :] = k_scr[j, natural(r, c), :]
                    vd[j, dst, :] = v_scr[j, natural(r, c), :]
            q_src, k_src, v_src = qd, kd, vd
        else:
            q_src, k_src, v_src = q_scr, k_scr, v_scr
        o_dst, m_dst, l_dst = (oa, ma, la) if bi == 0 else (ob, mb, lb)

        def block_group(i, carry, bi=bi, nb=nb, q_src=q_src, k_src=k_src, v_src=v_src,
                        o_dst=o_dst, m_dst=m_dst, l_dst=l_dst):
            todo = []
            for u in range(DIL_GROUP):
                t = i * DIL_GROUP + u
                rows = pl.ds(pl.multiple_of(t * span, span), span)
                prev = pl.ds(pl.multiple_of(jnp.maximum(t - 1, 0) * span, span), span)
                keep = jnp.logical_or(in_current, (t & (nb - 1)) != 0) if nb > 1 else None
                for j in range(n_tiles):
                    q2 = q_src[j, rows, :]
                    if nb > 1:
                        k_cat = jnp.concatenate([k_src[j, prev, :], k_src[j, rows, :]], axis=0).astype(BF16)
                        v_cat = jnp.concatenate([v_src[j, prev, :], v_src[j, rows, :]], axis=0).astype(BF16)
                    else:
                        k_cat = k_src[j, rows, :].astype(BF16)
                        v_cat = v_src[j, rows, :].astype(BF16)
                    todo.append((rows, j, q2, k_cat, v_cat, keep))
            logits = []
            for rows, j, q2, k_cat, v_cat, keep in todo:
                for a in range(2):
                    qa = jnp.where(lo_blk if a == 0 else jnp.logical_not(lo_blk), q2, 0.0).astype(BF16)
                    bias = bias_ref[bi * N_HEADS + 2 * j + a]
                    lg = _dot_nt(qa, k_cat) + (bias if nb > 1 else bias[:, span:])
                    logits.append(jnp.where(keep, lg, NEG_INF) if nb > 1 else lg)
            m = [jnp.max(lg, axis=-1, keepdims=True) for lg in logits]
            p = [jnp.exp(lg - mx) for lg, mx in zip(logits, m)]
            l = [jnp.sum(px, axis=-1, keepdims=True) for px in p]
            o = [_dot(p[2 * n + a].astype(BF16), todo[n][4]) for n in range(len(todo)) for a in range(2)]
            for n, (rows, j, *_) in enumerate(todo):
                o_dst[j, rows, :] = jnp.where(lo_blk, o[2 * n], o[2 * n + 1])
                m_dst[j, rows, :] = jnp.where(lo_blk, m[2 * n], m[2 * n + 1])
                l_dst[j, rows, :] = jnp.where(lo_blk, l[2 * n], l[2 * n + 1])
            return carry
        lax.fori_loop(0, n_blocks // DIL_GROUP, block_group, 0)

        if bi > 0:
            for r, c in chunks:
                src = pl.ds((r * nb + c) * span, span)
                nat = natural(r, c)
                for j in range(n_tiles):
                    m_old, m_in = ma[j, nat, :], mb[j, src, :]
                    m_new = jnp.maximum(m_old, m_in)
                    w_old, w_in = jnp.exp(m_old - m_new), jnp.exp(m_in - m_new)
                    oa[j, nat, :] = w_old * oa[j, nat, :] + w_in * ob[j, src, :]
                    la[j, nat, :] = w_old * la[j, nat, :] + w_in * lb[j, src, :]
                    ma[j, nat, :] = m_new

    def finish(i, carry):
        rows = pl.ds(pl.multiple_of(i * DIL_TILE, DIL_TILE), DIL_TILE)
        for j in range(n_tiles):
            o_ref[rows, j * LANES:(j + 1) * LANES] = oa[j, rows, :] / la[j, rows, :]
        return carry
    lax.fori_loop(0, seq // DIL_TILE, finish, 0)


def _dil(slab, gq2, gk2, band, *, batch, seq):
    big = lambda: pltpu.VMEM((GROUP_W // LANES, seq, LANES), F32)
    n_bias = len(DIL_PAIRS) * N_HEADS
    return pl.pallas_call(
        functools.partial(_dil_body, seq=seq),
        grid=(batch,),
        in_specs=[pl.BlockSpec((seq, DIL_SLAB), lambda b: (0, b)),
                  _const_spec((1, LANES)), _const_spec((1, LANES)),
                  _const_spec(band.shape)],
        out_specs=pl.BlockSpec((seq, GROUP_W), lambda b: (0, b)),
        out_shape=jax.ShapeDtypeStruct((seq, batch * GROUP_W), F32),
        scratch_shapes=[pltpu.VMEM((n_bias, DIL_SPAN, 2 * DIL_SPAN), F32)] + [big() for _ in range(12)],
        compiler_params=_params("parallel"),
        name="dilated",
    )(slab, gq2, gk2, band)


def _t5_bucket(dist):
    exact = T5_BUCKETS // 2
    df = jnp.maximum(dist, 1).astype(F32)
    large = exact + (jnp.log(df / exact) / math.log(T5_MAX_DIST / exact) * (T5_BUCKETS - exact)).astype(jnp.int32)
    large = jnp.minimum(large, T5_BUCKETS - 1)
    return jnp.where(dist < exact, dist, large)


def _dil_band(table):
    span = DIL_SPAN
    delta = span - jnp.arange(2 * span, dtype=jnp.int32)
    rows = []
    for _, dil in DIL_PAIRS:
        bucket = _t5_bucket(jnp.clip(delta, 0, span) * dil)
        onehot = (bucket[:, None] == jnp.arange(T5_BUCKETS, dtype=jnp.int32)[None, :]).astype(F32)
        vals = jnp.dot(onehot, table.astype(F32), precision=HIGHEST)
        rows.append(jnp.where((delta >= 0)[:, None], vals, NEG_INF).T)
    return jnp.concatenate(rows, axis=0)


DN_TILE = 256
DN_GROUP = 4


def _softplus(x):
    return jnp.maximum(x, 0.0) + jnp.log1p(jnp.exp(-jnp.abs(x)))


def _pair_l2(x, lo_half):
    sq = x * x
    tot = jnp.sum(sq, axis=-1, keepdims=True)
    lo = jnp.sum(jnp.where(lo_half, sq, 0.0), axis=-1, keepdims=True)
    return x * lax.rsqrt(jnp.where(lo_half, lo, tot - lo) + EPS)


def _dn_body(x_ref, cw_ref, alog_ref, dtb_ref, on_ref, o_ref, q_scr, k_scr, v_scr, g_scr, b_scr, w_scr, a_scr,
             s_scr, *, seq):
    c = DN_CHUNK
    w = GROUP_W
    lo_tile = lax.broadcasted_iota(jnp.int32, (DN_TILE, LANES), 1) < HEAD_DIM

    def prep_tile(i, carry):
        r0 = pl.multiple_of(i * DN_TILE, DN_TILE)
        rows = pl.ds(r0, DN_TILE)
        cur = x_ref[rows, 0:3 * w]
        halo_rows = pl.ds(pl.multiple_of(jnp.maximum(r0 - 8, 0), 8), 8)
        halo = jnp.where(i > 0, x_ref[halo_rows, 0:3 * w], 0.0)
        ext = jnp.concatenate([halo, cur], axis=0)
        acc = cw_ref[DN_CONV - 1:DN_CONV, :] * cur
        for j in range(DN_CONV - 1):
            acc = acc + cw_ref[j:j + 1, :] * pltpu.roll(ext, DN_CONV - 1 - j, 0)[8:, :]
        y = acc * jax.nn.sigmoid(acc)
        for j in range(w // LANES):
            cols = lambda part: slice(part * w + j * LANES, part * w + (j + 1) * LANES)
            q_scr[rows, j * LANES:(j + 1) * LANES] = _pair_l2(y[:, cols(0)], lo_tile) * (HEAD_DIM ** -0.5)
            k_scr[rows, j * LANES:(j + 1) * LANES] = _pair_l2(y[:, cols(1)], lo_tile)
            v_scr[rows, j * LANES:(j + 1) * LANES] = y[:, cols(2)]
        ab = x_ref[rows, 3 * w:3 * w + LANES]
        g = -jnp.exp(alog_ref[...]) * _softplus(ab + dtb_ref[...])
        g_scr[rows, :] = _dot(chunk_tril, _dot(g, expand, HIGHEST), HIGHEST)
        beta = pltpu.roll(jax.nn.sigmoid(ab), LANES - N_HEADS, 1)
        b_scr[rows, :] = _dot(beta, expand, HIGHEST)
        return carry

    ri = lax.broadcasted_iota(jnp.int32, (c, w), 0)
    ci = lax.broadcasted_iota(jnp.int32, (c, w), 1)
    cj = jnp.bitwise_and(ci, HEAD_DIM - 1)
    causal = ri >= cj
    strict = ri > cj
    eye4 = (ri == cj).astype(F32)
    same_sub = lax.shift_right_logical(ri, 4) == lax.shift_right_logical(cj, 4)
    bi_r = lax.broadcasted_iota(jnp.int32, (w, w), 0)
    bi_c = lax.broadcasted_iota(jnp.int32, (w, w), 1)
    same_head = lax.shift_right_logical(bi_r, 6) == lax.shift_right_logical(bi_c, 6)
    block_mask = same_head.astype(F32)
    chunk_tril = (same_head & (jnp.bitwise_and(bi_r, c - 1) >= jnp.bitwise_and(bi_c, c - 1))).astype(F32)
    expand = (lax.broadcasted_iota(jnp.int32, (LANES, w), 0)
              == lax.shift_right_logical(lax.broadcasted_iota(jnp.int32, (LANES, w), 1), 6)).astype(F32)
    lax.fori_loop(0, seq // DN_TILE, prep_tile, 0)

    def per_head(a, b):
        bd = jnp.where(same_head, jnp.concatenate([b.astype(BF16)] * N_HEADS, axis=0), 0.0)
        return _dot(a.astype(BF16), bd)

    def solve_group(i, carry):
        base = pl.multiple_of(i * (DN_GROUP * c), DN_GROUP * c)
        rows = [pl.ds(base + u * c, c) for u in range(DN_GROUP)]
        each = lambda f, *lists: [f(*args) for args in zip(*lists)]
        q, k, v, gc, beta = ([ref[r, :] for r in rows] for ref in (q_scr, k_scr, v_scr, g_scr, b_scr))
        g_row = each(lambda g: jnp.sum(g * eye4, axis=0, keepdims=True), gc)
        decay = each(lambda g, gr: jnp.exp(jnp.where(causal, g - gr, NEG_INF)), gc, g_row)
        kb = each(jnp.multiply, k, beta)
        k_bd = each(lambda x: jnp.where(same_head, jnp.concatenate([x.astype(BF16)] * N_HEADS, axis=0), 0.0), k)
        lmat = each(lambda a, b, d: jnp.where(strict, _dot_nt(a.astype(BF16), b) * d, 0.0), kb, k_bd, decay)
        a_qk = each(lambda a, b, d: jnp.where(causal, _dot_nt(a.astype(BF16), b) * d, 0.0), q, k_bd, decay)
        p = each(lambda l_: jnp.where(same_sub, -l_, 0.0), lmat)
        t_diag = each(lambda x: eye4 + x, p)
        for _ in range(3):
            p = each(per_head, p, p)
            t_diag = each(lambda t, x: t + per_head(t, x), t_diag, p)
        nil = each(lambda t, l_: per_head(t, jnp.where(same_sub, 0.0, l_)), t_diag, lmat)
        nil2 = each(per_head, nil, nil)
        nil3 = each(per_head, nil, nil2)
        t_inv = each(lambda n1, n2, n3, t: per_head(eye4 - n1 + n2 - n3, t), nil, nil2, nil3, t_diag)
        eg = each(jnp.exp, gc)
        w_c = each(lambda t, a, e: per_head(t, a * e), t_inv, kb, eg)
        u_c = each(lambda t, a, b: per_head(t, a * b), t_inv, v, beta)
        q_dec = each(jnp.multiply, q, eg)
        k_dec = each(lambda x, g: x * jnp.exp(g[c - 1:c, :] - g), k, gc)
        for ref, vals in zip((w_scr, v_scr, a_scr, q_scr, k_scr), (w_c, u_c, a_qk, q_dec, k_dec)):
            for r, val in zip(rows, vals):
                ref[r, :] = val
        return carry
    lax.fori_loop(0, seq // (DN_GROUP * c), solve_group, 0)

    s_scr[...] = jnp.zeros_like(s_scr)

    def state_chunk(n, carry):
        rows = pl.ds(pl.multiple_of(n * c, c), c)
        state = s_scr[...]
        state_b = state.astype(BF16)
        v_new = v_scr[rows, :] - _dot(w_scr[rows, :].astype(BF16), state_b)
        o_ref[rows, :] = _dot(q_scr[rows, :].astype(BF16), state_b) + per_head(a_scr[rows, :], v_new)
        upd = lax.dot_general(k_scr[rows, :].astype(BF16), v_new.astype(BF16), (((0,), (0,)), ((), ())),
                              preferred_element_type=F32)
        d_last = jnp.exp(g_scr[pl.ds(pl.multiple_of(n * c, c) + (c - 1), 1), :])
        s_scr[...] = state * d_last + upd * block_mask
        return carry
    lax.fori_loop(0, seq // c, state_chunk, 0, unroll=2)

    def finish_tile(i, carry):
        rows = pl.ds(pl.multiple_of(i * DN_TILE, DN_TILE), DN_TILE)
        o = o_ref[rows, :]
        ms = _dot(o * o, block_mask, HIGHEST) * (1.0 / HEAD_DIM)
        gate = x_ref[rows, 3 * w + LANES:4 * w + LANES]
        o_ref[rows, :] = o * lax.rsqrt(ms + EPS) * on_ref[...] * (gate * jax.nn.sigmoid(gate))
        return carry
    lax.fori_loop(0, seq // DN_TILE, finish_tile, 0)


def _dn(slab, conv_w, a_log, dt_bias, o_norm, *, batch, seq):
    wide = lambda: pltpu.VMEM((seq, GROUP_W), F32)
    return pl.pallas_call(
        functools.partial(_dn_body, seq=seq),
        grid=(batch,),
        in_specs=[pl.BlockSpec((seq, DN_SLAB), lambda b: (0, b)),
                  _const_spec((DN_CONV, 3 * GROUP_W)), _const_spec((1, LANES)), _const_spec((1, LANES)),
                  _const_spec((1, GROUP_W))],
        out_specs=pl.BlockSpec((seq, GROUP_W), lambda b: (0, b)),
        out_shape=jax.ShapeDtypeStruct((seq, batch * GROUP_W), F32),
        scratch_shapes=[wide() for _ in range(7)] + [pltpu.VMEM((GROUP_W, GROUP_W), F32)],
        compiler_params=_params("parallel"),
        name="deltanet",
    )(slab, conv_w, a_log, dt_bias, o_norm)


def _row(v, width=None):
    v = v.astype(F32).reshape(1, -1)
    if width is not None and v.shape[1] < width:
        v = jnp.pad(v, ((0, 0), (0, width - v.shape[1])))
    return v


def _prep_w_in(w):
    z = lambda n: jnp.zeros((D_MODEL, n), w.dtype)
    cols = [w[:, 0:416], z(96), w[:, 416:672], w[:, 672:1440], w[:, 1440:2208], w[:, 2208:2216], z(120),
            w[:, 2216:2472]]
    return jnp.concatenate(cols, axis=1).astype(BF16)


def _pad_heads(w, per_head, lo, hi):
    k = w.shape[0]
    w = w.reshape(k, N_HEADS, per_head)[:, :, lo:hi]
    w = jnp.pad(w, ((0, 0), (0, 0), (0, LANES - (hi - lo))))
    return w.reshape(k, N_HEADS * LANES).astype(BF16)


def _rope_tables(seq):
    half = MLA_ROPE // 2
    pos = jnp.arange(seq, dtype=F32)
    freqs = ROPE_THETA ** (-jnp.arange(half, dtype=F32) / half)
    ang = pos[:, None] * freqs[None, :]
    cos, sin = jnp.cos(ang), jnp.sin(ang)
    ones = jnp.ones((seq, MLA_NOPE), F32)
    zeros = jnp.zeros((seq, MLA_NOPE), F32)
    tail1 = jnp.ones((seq, LANES - MLA_DQK), F32)
    tail0 = jnp.zeros((seq, LANES - MLA_DQK), F32)
    return (jnp.concatenate([ones, cos, cos, tail1], axis=1),
            jnp.concatenate([zeros, -sin, sin, tail0], axis=1))


def _block_diag(blocks):
    g, r, c = blocks.shape
    eye = jnp.eye(g, dtype=blocks.dtype)
    return (blocks[:, :, None, :] * eye[:, None, :, None]).reshape(g * r, g * c)


def _mla_layer(slab, p, l, *, batch, seq):
    w_ukv = p["mla_w_ukv"][l]
    cos, sin = _rope_tables(seq)
    return _mla(slab, _row(p["mla_q_norm"][l]), _row(p["mla_kv_norm"][l]),
                _pad_heads(p["mla_w_uq"][l], MLA_DQK, 0, MLA_DQK),
                _pad_heads(w_ukv, MLA_NOPE + HEAD_DIM, 0, MLA_NOPE),
                _pad_heads(w_ukv, MLA_NOPE + HEAD_DIM, MLA_NOPE, MLA_NOPE + HEAD_DIM),
                _row(p["mla_qk_q"][l], LANES), _row(p["mla_qk_k"][l], LANES), cos, sin, batch=batch, seq=seq)


def _dil_layer(slab, p, l, *, batch, seq):
    pair = lambda g: jnp.tile(g.astype(F32).reshape(1, HEAD_DIM), (1, LANES // HEAD_DIM))
    return _dil(slab, pair(p["dil_q_norm"][l]), pair(p["dil_k_norm"][l]), _dil_band(p["t5_bias"]),
                batch=batch, seq=seq)


def _dn_layer(slab, p, l, *, batch, seq):
    return _dn(slab, p["dn_conv"][l].astype(F32), _row(p["dn_a_log"][l], LANES), _row(p["dn_dt_bias"][l], LANES),
               jnp.tile(p["dn_o_norm"][l].astype(F32).reshape(1, HEAD_DIM), (1, N_HEADS)), batch=batch, seq=seq)


def _s5_layer(u, p, l, *, batch, seq):
    state_row = lambda v: v.astype(F32).reshape(1, S5_WIDTH)
    ldt = jnp.broadcast_to(p["s5_log_dt"][l][:, None], (S5_GROUPS, S5_STATE))
    bre = _block_diag(jnp.swapaxes(p["s5_b_re"][l], 1, 2).astype(F32))
    bim = _block_diag(jnp.swapaxes(p["s5_b_im"][l], 1, 2).astype(F32))
    cre = _block_diag(jnp.swapaxes(p["s5_c_re"][l], 1, 2)).astype(BF16)
    cim = _block_diag(jnp.swapaxes(p["s5_c_im"][l], 1, 2)).astype(BF16)
    return _s5(u, state_row(p["s5_lambda_re"][l]), state_row(p["s5_lambda_im"][l]), state_row(ldt),
               bre, bim, cre, cim, _row(p["s5_d"][l]), p["s5_w_glu"][l].astype(BF16), batch=batch, seq=seq,
               steps=min(256, seq))


def kernel(x, attn_norm, w_in, w_out, mla_q_norm, mla_kv_norm, mla_w_uq, mla_w_ukv, mla_qk_q, mla_qk_k,
           s5_lambda_re, s5_lambda_im, s5_log_dt, s5_b_re, s5_b_im, s5_c_re, s5_c_im, s5_d, s5_w_glu,
           dil_q_norm, dil_k_norm, t5_bias, dn_conv, dn_a_log, dn_dt_bias, dn_o_norm,
           ffn_norm, ffn_w1, ffn_w3, ffn_w2):
    p = dict(mla_q_norm=mla_q_norm, mla_kv_norm=mla_kv_norm, mla_w_uq=mla_w_uq, mla_w_ukv=mla_w_ukv,
             mla_qk_q=mla_qk_q, mla_qk_k=mla_qk_k, s5_lambda_re=s5_lambda_re, s5_lambda_im=s5_lambda_im,
             s5_log_dt=s5_log_dt, s5_b_re=s5_b_re, s5_b_im=s5_b_im, s5_c_re=s5_c_re, s5_c_im=s5_c_im, s5_d=s5_d,
             s5_w_glu=s5_w_glu, dil_q_norm=dil_q_norm, dil_k_norm=dil_k_norm, t5_bias=t5_bias, dn_conv=dn_conv,
             dn_a_log=dn_a_log, dn_dt_bias=dn_dt_bias, dn_o_norm=dn_o_norm)
    batch, seq, _ = x.shape
    h = x.reshape(batch * seq, D_MODEL)
    for l in range(attn_norm.shape[0]):
        mla_in, s5_in, dil_in, dn_in = _proj(h, _row(attn_norm[l]), _prep_w_in(w_in[l]), batch=batch, seq=seq)
        ys = [_mla_layer(mla_in, p, l, batch=batch, seq=seq),
              _s5_layer(s5_in, p, l, batch=batch, seq=seq),
              _dil_layer(dil_in, p, l, batch=batch, seq=seq),
              _dn_layer(dn_in, p, l, batch=batch, seq=seq)]
        h = _out_ffn(h, ys, w_out[l].astype(BF16), _row(ffn_norm[l]),
                     ffn_w1[l].astype(BF16), ffn_w3[l].astype(BF16), ffn_w2[l].astype(BF16), batch=batch, seq=seq)
    return h.reshape(batch, seq, D_MODEL)
```

```python
import functools
import math

import jax
import jax.numpy as jnp
import numpy as np
from jax import lax
from jax.experimental import pallas as pl
from jax.experimental.pallas import tpu as pltpu

F32 = jnp.float32
BF16 = jnp.bfloat16
HIGHEST = lax.Precision.HIGHEST

D_MODEL = 1024
GROUP_W = 256
HEAD_DIM = 64
N_HEADS = 4
EPS = 1e-6
NEG_INF = -1e30

MLA_NOPE = 64
MLA_ROPE = 32
MLA_DQK = MLA_NOPE + MLA_ROPE
MLA_KV_RANK = 128
ROPE_THETA = 10000.0

S5_GROUP_CH = 16
S5_GROUPS = 16
S5_STATE = 64
S5_WIDTH = S5_GROUPS * S5_STATE

DIL_PAIRS = ((128, 1), (512, 4), (2048, 16))
DIL_SPAN = 128
T5_BUCKETS = 32
T5_MAX_DIST = 2048

DN_CONV = 4
DN_CHUNK = 64

FFN_HIDDEN = 2816
FFN_CHUNK = 2816

VMEM_LIMIT_BYTES = 56 * 1024 * 1024
LANES = 128

MLA_SLAB = 512
S5_SLAB = 256
DIL_SLAB = 768
DN_SLAB = 1152
PROJ_COLS = MLA_SLAB + S5_SLAB + DIL_SLAB + DN_SLAB


def _dot(a, b, precision=None):
    return jnp.dot(a, b, preferred_element_type=F32, precision=precision)


def _dot_nt(a, b, precision=None):
    return lax.dot_general(a, b, (((1,), (1,)), ((), ())), preferred_element_type=F32, precision=precision)


def _const_spec(shape):
    nd = len(shape)
    return pl.BlockSpec(shape, lambda *_: (0,) * nd, pipeline_mode=pl.Buffered(1))


def _params(*sem):
    return pltpu.CompilerParams(dimension_semantics=sem, vmem_limit_bytes=VMEM_LIMIT_BYTES)


def _proj_body(x_ref, g_ref, w_ref, mla_ref, s5_ref, dil_ref, dn_ref):
    x = x_ref[...]
    n = x * lax.rsqrt(jnp.mean(x * x, axis=-1, keepdims=True) + EPS) * g_ref[...]
    nb = n.astype(BF16)
    start = 0
    for ref in (mla_ref, s5_ref, dil_ref, dn_ref):
        width = ref.shape[-1]
        ref[...] = _dot(nb, w_ref[:, start:start + width])
        start += width


def _proj(h, gain, w_big, *, batch, seq, tm=512):
    nt = seq // tm
    widths = (MLA_SLAB, S5_SLAB, DIL_SLAB, DN_SLAB)
    return pl.pallas_call(
        _proj_body,
        grid=(batch, nt),
        in_specs=[pl.BlockSpec((tm, D_MODEL), lambda b, i: (b * nt + i, 0)),
                  _const_spec((1, D_MODEL)),
                  _const_spec((D_MODEL, PROJ_COLS))],
        out_specs=[pl.BlockSpec((tm, w), lambda b, i: (i, b)) for w in widths],
        out_shape=[jax.ShapeDtypeStruct((seq, batch * w), F32) for w in widths],
        compiler_params=_params("parallel", "parallel"),
        name="proj",
    )(h, gain, w_big)


def _out_ffn_body(h_ref, y0_ref, y1_ref, y2_ref, y3_ref, wo_ref, g_ref, w1_ref, w3_ref, w2_ref, o_ref, acc_ref):
    h = h_ref[...]
    for i, y_ref in enumerate((y0_ref, y1_ref, y2_ref, y3_ref)):
        h = h + _dot(y_ref[...].astype(BF16), wo_ref[i * GROUP_W:(i + 1) * GROUP_W, :])
    n = h * lax.rsqrt(jnp.mean(h * h, axis=-1, keepdims=True) + EPS) * g_ref[...]
    nb = n.astype(BF16)
    acc_ref[...] = h

    def hidden_chunk(c, carry):
        cols = pl.ds(pl.multiple_of(c * FFN_CHUNK, FFN_CHUNK), FFN_CHUNK)
        a = _dot(nb, w1_ref[:, cols])
        b = _dot(nb, w3_ref[:, cols])
        z = (a * jax.nn.sigmoid(a) * b).astype(BF16)
        acc_ref[...] += _dot(z, w2_ref[cols, :])
        return carry
    lax.fori_loop(0, FFN_HIDDEN // FFN_CHUNK, hidden_chunk, 0)
    o_ref[...] = acc_ref[...]


def _out_ffn(h, ys, w_out, gain, w1, w3, w2, *, batch, seq, tm=512):
    nt = seq // tm
    row = pl.BlockSpec((tm, D_MODEL), lambda b, i: (b * nt + i, 0))
    slab = pl.BlockSpec((tm, GROUP_W), lambda b, i: (i, b))
    return pl.pallas_call(
        _out_ffn_body,
        grid=(batch, nt),
        in_specs=[row] + [slab] * 4 + [
            _const_spec((D_MODEL, D_MODEL)), _const_spec((1, D_MODEL)),
            _const_spec((D_MODEL, FFN_HIDDEN)), _const_spec((D_MODEL, FFN_HIDDEN)),
            _const_spec((FFN_HIDDEN, D_MODEL))],
        out_specs=row,
        out_shape=jax.ShapeDtypeStruct((batch * seq, D_MODEL), F32),
        scratch_shapes=[pltpu.VMEM((tm, D_MODEL), F32)],
        compiler_params=_params("parallel", "parallel"),
        name="out_ffn",
    )(h, *ys, w_out, gain, w1, w3, w2)


MLA_BLOCK = 256


def _mla_body(x_ref, qn_ref, kvn_ref, wuq_ref, wuk_ref, wuv_ref, gq_ref, gk_ref, cos_ref, sin_ref,
              o_ref, k_scr, vt_scr, *, seq):
    blk = MLA_BLOCK
    n_pairs = N_HEADS // 2
    qi = pl.program_id(1)
    lane = lax.broadcasted_iota(jnp.int32, (blk, LANES), 1)

    def rope(x, c, s):
        rot = jnp.where(lane < MLA_NOPE + MLA_ROPE // 2, pltpu.roll(x, LANES - MLA_ROPE // 2, 1),
                        pltpu.roll(x, MLA_ROPE // 2, 1))
        return x * c + rot * s

    def norm_head(x, g):
        ssq = jnp.sum(x * x, axis=-1, keepdims=True)
        return x * lax.rsqrt(ssq * (1.0 / MLA_DQK) + EPS) * g

    @pl.when(qi == 0)
    def _prepare_keys_values():
        def tile(i, carry):
            r0 = pl.multiple_of(i * blk, blk)
            rows = pl.ds(r0, blk)
            ckv = x_ref[rows, 256:384]
            k_rope = x_ref[rows, 384:512]
            kvn = ckv * lax.rsqrt(jnp.mean(ckv * ckv, axis=-1, keepdims=True) + EPS) * kvn_ref[...]
            kvn = kvn.astype(BF16)
            k_nope = _dot(kvn, wuk_ref[...])
            v = _dot(kvn, wuv_ref[...])
            k_rope = pltpu.roll(k_rope, MLA_NOPE, 1)
            c = cos_ref[rows, :]
            s = sin_ref[rows, :]
            for h in range(N_HEADS):
                k = rope(norm_head(k_nope[:, h * LANES:(h + 1) * LANES] + k_rope, gk_ref[...]), c, s)
                k_scr[h, rows, :] = k.astype(BF16)
            for pr in range(n_pairs):
                vt_scr[pr, :, rows] = v[:, pr * LANES:(pr + 1) * LANES].T.astype(BF16)
            return carry
        lax.fori_loop(0, seq // blk, tile, 0)

    r0 = pl.multiple_of(qi * blk, blk)
    rows = pl.ds(r0, blk)
    cq = x_ref[rows, 0:256]
    qn = cq * lax.rsqrt(jnp.mean(cq * cq, axis=-1, keepdims=True) + EPS) * qn_ref[...]
    q_all = _dot(qn.astype(BF16), wuq_ref[...])
    c = cos_ref[rows, :]
    s = sin_ref[rows, :]
    key_pos = lax.broadcasted_iota(jnp.int32, (blk, blk), 0)
    query_pos = lax.broadcasted_iota(jnp.int32, (blk, blk), 1)
    first_of_pair = lax.broadcasted_iota(jnp.int32, (LANES, blk), 0) < HEAD_DIM
    heads = range(N_HEADS)
    qs = []
    for h in heads:
        q = rope(norm_head(q_all[:, h * LANES:(h + 1) * LANES], gq_ref[...]), c, s)
        qs.append((q * (MLA_DQK ** -0.5)).astype(BF16))

    def kv_block(j, carry, diagonal):
        m_old, l_old, acc_old = carry
        keys = pl.ds(pl.multiple_of(j * blk, blk), blk)
        logits = [_dot_nt(k_scr[h, keys, :], qs[h]) for h in heads]
        if diagonal:
            logits = [jnp.where(key_pos <= query_pos, x, NEG_INF) for x in logits]
        m_new = [jnp.maximum(m_old[h], jnp.max(logits[h], axis=0, keepdims=True)) for h in heads]
        alpha = [jnp.exp(m_old[h] - m_new[h]) for h in heads]
        p = [jnp.exp(logits[h] - m_new[h]) for h in heads]
        l_new = [alpha[h] * l_old[h] + jnp.sum(p[h], axis=0, keepdims=True) for h in heads]
        pv = [_dot(vt_scr[h // 2, :, keys], p[h].astype(BF16)) for h in heads]
        acc_new = [jnp.where(first_of_pair, alpha[2 * pr] * acc_old[pr] + pv[2 * pr],
                             alpha[2 * pr + 1] * acc_old[pr] + pv[2 * pr + 1]) for pr in range(n_pairs)]
        return tuple(m_new), tuple(l_new), tuple(acc_new)

    init = (tuple(jnp.full((1, blk), NEG_INF, F32) for _ in heads), tuple(jnp.zeros((1, blk), F32) for _ in heads),
            tuple(jnp.zeros((LANES, blk), F32) for _ in range(n_pairs)))
    carry = lax.fori_loop(0, qi, functools.partial(kv_block, diagonal=False), init)
    _, l, acc = kv_block(qi, carry, diagonal=True)
    for pr in range(n_pairs):
        out_t = acc[pr] / jnp.where(first_of_pair, l[2 * pr], l[2 * pr + 1])
        o_ref[rows, pr * LANES:(pr + 1) * LANES] = out_t.T


def _mla(slab, qn, kvn, wuq, wuk, wuv, gq, gk, cos, sin, *, batch, seq):
    nq = seq // MLA_BLOCK
    return pl.pallas_call(
        functools.partial(_mla_body, seq=seq),
        grid=(batch, nq),
        in_specs=[pl.BlockSpec((seq, MLA_SLAB), lambda b, i: (0, b)),
                  _const_spec((1, 256)), _const_spec((1, MLA_KV_RANK)),
                  _const_spec((256, N_HEADS * LANES)), _const_spec((MLA_KV_RANK, N_HEADS * LANES)),
                  _const_spec((MLA_KV_RANK, GROUP_W)),
                  _const_spec((1, LANES)), _const_spec((1, LANES)),
                  _const_spec((seq, LANES)), _const_spec((seq, LANES))],
        out_specs=pl.BlockSpec((seq, GROUP_W), lambda b, i: (0, b)),
        out_shape=jax.ShapeDtypeStruct((seq, batch * GROUP_W), F32),
        scratch_shapes=[pltpu.VMEM((N_HEADS, seq, LANES), BF16), pltpu.VMEM((N_HEADS // 2, LANES, seq), BF16)],
        compiler_params=_params("parallel", "arbitrary"),
        name="mla",
    )(slab, qn, kvn, wuq, wuk, wuv, gq, gk, cos, sin)


def _s5_body(u_ref, lre_ref, lim_ref, ldt_ref, bre_ref, bim_ref, cre_ref, cim_ref, d_ref, wglu_ref, o_ref,
             a_scr, bbar_scr, h_scr, x_scr, *, batch, steps):
    n_tiles = S5_WIDTH // LANES

    @pl.when(pl.program_id(0) == 0)
    def _discretise():
        lr = lre_ref[...]
        li = lim_ref[...]
        dt = jnp.exp(ldt_ref[...])
        mag = jnp.exp(lr * dt)
        ar = mag * jnp.cos(li * dt)
        ai = mag * jnp.sin(li * dt)
        den = lr * lr + li * li
        nr = ar - 1.0
        zr = (nr * lr + ai * li) / den
        zi = (ai * lr - nr * li) / den
        for j in range(n_tiles):
            lanes = slice(j * LANES, (j + 1) * LANES)
            a_scr[0, j] = jnp.broadcast_to(ar[:, lanes], (batch, LANES))
            a_scr[1, j] = jnp.broadcast_to(ai[:, lanes], (batch, LANES))
        bre = bre_ref[...]
        bim = bim_ref[...]
        bbar_scr[0] = (zr * bre - zi * bim).astype(BF16)
        bbar_scr[1] = (zr * bim + zi * bre).astype(BF16)
        h_scr[...] = jnp.zeros_like(h_scr)

    for b in range(batch):
        ub = u_ref[:, b * GROUP_W:(b + 1) * GROUP_W].astype(BF16)
        rows_b = pl.ds(b, steps, stride=batch)
        for part in range(2):
            x = _dot(ub, bbar_scr[part])
            for j in range(n_tiles):
                x_scr[part, j, rows_b, :] = x[:, j * LANES:(j + 1) * LANES]
    ar = a_scr[0]
    ai = a_scr[1]

    def step(t, carry):
        hr, hi = carry
        rows = pl.ds(pl.multiple_of(t * batch, batch), batch)
        nhr = ar * hr - ai * hi + x_scr[0, :, rows, :]
        nhi = ar * hi + ai * hr + x_scr[1, :, rows, :]
        x_scr[0, :, rows, :] = nhr
        x_scr[1, :, rows, :] = nhi
        return nhr, nhi

    hr, hi = lax.fori_loop(0, steps, step, (h_scr[0], h_scr[1]), unroll=8)
    h_scr[0] = hr
    h_scr[1] = hi
    for b in range(batch):
        rows_b = pl.ds(b, steps, stride=batch)
        state = [jnp.concatenate([x_scr[part, j, rows_b, :] for j in range(n_tiles)], axis=1).astype(BF16)
                 for part in range(2)]
        u = u_ref[:, b * GROUP_W:(b + 1) * GROUP_W]
        y = _dot(state[0], cre_ref[...]) - _dot(state[1], cim_ref[...]) + d_ref[...] * u
        z = _dot(y.astype(BF16), wglu_ref[...])
        o_ref[:, b * GROUP_W:(b + 1) * GROUP_W] = z[:, :GROUP_W] * jax.nn.sigmoid(z[:, GROUP_W:])


def _s5(u, lre, lim, ldt, bre, bim, cre, cim, d, wglu, *, batch, seq, steps=256):
    n_tiles = S5_WIDTH // LANES
    state = lambda rows: pltpu.VMEM((2, n_tiles, rows, LANES), F32)
    return pl.pallas_call(
        functools.partial(_s5_body, batch=batch, steps=steps),
        grid=(seq // steps,),
        in_specs=[pl.BlockSpec((steps, batch * S5_SLAB), lambda i: (i, 0)),
                  _const_spec((1, S5_WIDTH)), _const_spec((1, S5_WIDTH)), _const_spec((1, S5_WIDTH)),
                  _const_spec((GROUP_W, S5_WIDTH)), _const_spec((GROUP_W, S5_WIDTH)),
                  _const_spec((S5_WIDTH, GROUP_W)), _const_spec((S5_WIDTH, GROUP_W)),
                  _const_spec((1, GROUP_W)), _const_spec((GROUP_W, 2 * GROUP_W))],
        out_specs=pl.BlockSpec((steps, batch * GROUP_W), lambda i: (i, 0)),
        out_shape=jax.ShapeDtypeStruct((seq, batch * GROUP_W), F32),
        scratch_shapes=[state(batch), pltpu.VMEM((2, GROUP_W, S5_WIDTH), BF16), state(batch),
                        state(steps * batch)],
        compiler_params=_params("arbitrary"),
        name="s5",
    )(u, lre, lim, ldt, bre, bim, cre, cim, d, wglu)


DIL_TILE = 256
DIL_GROUP = 2


def _pair_norm(x, gain2, lo_half):
    sq = x * x
    tot = jnp.sum(sq, axis=-1, keepdims=True)
    lo = jnp.sum(jnp.where(lo_half, sq, 0.0), axis=-1, keepdims=True)
    ms = jnp.where(lo_half, lo, tot - lo) * (1.0 / HEAD_DIM)
    return x * lax.rsqrt(ms + EPS) * gain2


def _dil_body(x_ref, gq_ref, gk_ref, band_ref, o_ref, bias_ref, q_scr, k_scr, v_scr, qd, kd, vd, ob, mb, lb,
              oa, ma, la, *, seq):
    span = DIL_SPAN
    n_blocks = seq // span
    n_tiles = GROUP_W // LANES
    lo_tile = lax.broadcasted_iota(jnp.int32, (DIL_TILE, LANES), 1) < HEAD_DIM
    lo_blk = lax.broadcasted_iota(jnp.int32, (span, LANES), 1) < HEAD_DIM

    for bh in range(len(DIL_PAIRS) * N_HEADS):
        profile = jnp.broadcast_to(band_ref[bh:bh + 1, :], (span, 2 * span))
        bias_ref[bh] = pltpu.roll(profile, 0, 1, stride=1, stride_axis=0)
    in_current = lax.broadcasted_iota(jnp.int32, (span, 2 * span), 1) >= span

    def norm_tile(i, carry):
        rows = pl.ds(pl.multiple_of(i * DIL_TILE, DIL_TILE), DIL_TILE)
        for j in range(n_tiles):
            cols = lambda part: slice(part * GROUP_W + j * LANES, part * GROUP_W + (j + 1) * LANES)
            q = _pair_norm(x_ref[rows, cols(0)], gq_ref[...], lo_tile)
            q_scr[j, rows, :] = q * (HEAD_DIM ** -0.5)
            k_scr[j, rows, :] = _pair_norm(x_ref[rows, cols(1)], gk_ref[...], lo_tile)
            v_scr[j, rows, :] = x_ref[rows, cols(2)]
        return carry
    lax.fori_loop(0, seq // DIL_TILE, norm_tile, 0)

    for bi, (window, dil) in enumerate(DIL_PAIRS):
        sub_len = seq // dil
        nb = sub_len // span
        chunks = [(r, c) for r in range(dil) for c in range(nb)]

        def natural(r, c, dil=dil):
            return pl.ds(r + dil * c * span, span, stride=dil) if dil > 1 else pl.ds(c * span, span)

        if dil > 1:
            for r, c in chunks:
                dst = pl.ds((r * nb + c) * span, span)
                for j in range(n_tiles):
                    qd[j, dst, :] = q_scr[j, natural(r, c), :]
                    kd[j, dst, :] = k_scr[j, natural(r, c), :]
                    vd[j, dst, :] = v_scr[j, natural(r, c), :]
            q_src, k_src, v_src = qd, kd, vd
        else:
            q_src, k_src, v_src = q_scr, k_scr, v_scr
        o_dst, m_dst, l_dst = (oa, ma, la) if bi == 0 else (ob, mb, lb)

        def block_group(i, carry, bi=bi, nb=nb, q_src=q_src, k_src=k_src, v_src=v_src,
                        o_dst=o_dst, m_dst=m_dst, l_dst=l_dst):
            todo = []
            for u in range(DIL_GROUP):
                t = i * DIL_GROUP + u
                rows = pl.ds(pl.multiple_of(t * span, span), span)
                prev = pl.ds(pl.multiple_of(jnp.maximum(t - 1, 0) * span, span), span)
                keep = jnp.logical_or(in_current, (t & (nb - 1)) != 0) if nb > 1 else None
                for j in range(n_tiles):
                    q2 = q_src[j, rows, :]
                    if nb > 1:
                        k_cat = jnp.concatenate([k_src[j, prev, :], k_src[j, rows, :]], axis=0).astype(BF16)
                        v_cat = jnp.concatenate([v_src[j, prev, :], v_src[j, rows, :]], axis=0).astype(BF16)
                    else:
                        k_cat = k_src[j, rows, :].astype(BF16)
                        v_cat = v_src[j, rows, :].astype(BF16)
                    todo.append((rows, j, q2, k_cat, v_cat, keep))
            logits = []
            for rows, j, q2, k_cat, v_cat, keep in todo:
                for a in range(2):
                    qa = jnp.where(lo_blk if a == 0 else jnp.logical_not(lo_blk), q2, 0.0).astype(BF16)
                    bias = bias_ref[bi * N_HEADS + 2 * j + a]
                    lg = _dot_nt(qa, k_cat) + (bias if nb > 1 else bias[:, span:])
                    logits.append(jnp.where(keep, lg, NEG_INF) if nb > 1 else lg)
            m = [jnp.max(lg, axis=-1, keepdims=True) for lg in logits]
            p = [jnp.exp(lg - mx) for lg, mx in zip(logits, m)]
            l = [jnp.sum(px, axis=-1, keepdims=True) for px in p]
            o = [_dot(p[2 * n + a].astype(BF16), todo[n][4]) for n in range(len(todo)) for a in range(2)]
            for n, (rows, j, *_) in enumerate(todo):
                o_dst[j, rows, :] = jnp.where(lo_blk, o[2 * n], o[2 * n + 1])
                m_dst[j, rows, :] = jnp.where(lo_blk, m[2 * n], m[2 * n + 1])
                l_dst[j, rows, :] = jnp.where(lo_blk, l[2 * n], l[2 * n + 1])
            return carry
        lax.fori_loop(0, n_blocks // DIL_GROUP, block_group, 0)

        if bi > 0:
            for r, c in chunks:
                src = pl.ds((r * nb + c) * span, span)
                nat = natural(r, c)
                for j in range(n_tiles):
                    m_old, m_in = ma[j, nat, :], mb[j, src, :]
                    m_new = jnp.maximum(m_old, m_in)
                    w_old, w_in = jnp.exp(m_old - m_new), jnp.exp(m_in - m_new)
                    oa[j, nat, :] = w_old * oa[j, nat, :] + w_in * ob[j, src, :]
                    la[j, nat, :] = w_old * la[j, nat, :] + w_in * lb[j, src, :]
                    ma[j, nat, :] = m_new

    def finish(i, carry):
        rows = pl.ds(pl.multiple_of(i * DIL_TILE, DIL_TILE), DIL_TILE)
        for j in range(n_tiles):
            o_ref[rows, j * LANES:(j + 1) * LANES] = oa[j, rows, :] / la[j, rows, :]
        return carry
    lax.fori_loop(0, seq // DIL_TILE, finish, 0)


def _dil(slab, gq2, gk2, band, *, batch, seq):
    big = lambda: pltpu.VMEM((GROUP_W // LANES, seq, LANES), F32)
    n_bias = len(DIL_PAIRS) * N_HEADS
    return pl.pallas_call(
        functools.partial(_dil_body, seq=seq),
        grid=(batch,),
        in_specs=[pl.BlockSpec((seq, DIL_SLAB), lambda b: (0, b)),
                  _const_spec((1, LANES)), _const_spec((1, LANES)),
                  _const_spec(band.shape)],
        out_specs=pl.BlockSpec((seq, GROUP_W), lambda b: (0, b)),
        out_shape=jax.ShapeDtypeStruct((seq, batch * GROUP_W), F32),
        scratch_shapes=[pltpu.VMEM((n_bias, DIL_SPAN, 2 * DIL_SPAN), F32)] + [big() for _ in range(12)],
        compiler_params=_params("parallel"),
        name="dilated",
    )(slab, gq2, gk2, band)


def _t5_bucket(dist):
    exact = T5_BUCKETS // 2
    df = jnp.maximum(dist, 1).astype(F32)
    large = exact + (jnp.log(df / exact) / math.log(T5_MAX_DIST / exact) * (T5_BUCKETS - exact)).astype(jnp.int32)
    large = jnp.minimum(large, T5_BUCKETS - 1)
    return jnp.where(dist < exact, dist, large)


def _dil_band(table):
    span = DIL_SPAN
    delta = span - jnp.arange(2 * span, dtype=jnp.int32)
    rows = []
    for _, dil in DIL_PAIRS:
        bucket = _t5_bucket(jnp.clip(delta, 0, span) * dil)
        onehot = (bucket[:, None] == jnp.arange(T5_BUCKETS, dtype=jnp.int32)[None, :]).astype(F32)
        vals = jnp.dot(onehot, table.astype(F32), precision=HIGHEST)
        rows.append(jnp.where((delta >= 0)[:, None], vals, NEG_INF).T)
    return jnp.concatenate(rows, axis=0)


DN_TILE = 256
DN_GROUP = 4


def _softplus(x):
    return jnp.maximum(x, 0.0) + jnp.log1p(jnp.exp(-jnp.abs(x)))


def _pair_l2(x, lo_half):
    sq = x * x
    tot = jnp.sum(sq, axis=-1, keepdims=True)
    lo = jnp.sum(jnp.where(lo_half, sq, 0.0), axis=-1, keepdims=True)
    return x * lax.rsqrt(jnp.where(lo_half, lo, tot - lo) + EPS)


def _dn_body(x_ref, cw_ref, alog_ref, dtb_ref, on_ref, o_ref, q_scr, k_scr, v_scr, g_scr, b_scr, w_scr, a_scr,
             s_scr, *, seq):
    c = DN_CHUNK
    w = GROUP_W
    lo_tile = lax.broadcasted_iota(jnp.int32, (DN_TILE, LANES), 1) < HEAD_DIM

    def prep_tile(i, carry):
        r0 = pl.multiple_of(i * DN_TILE, DN_TILE)
        rows = pl.ds(r0, DN_TILE)
        cur = x_ref[rows, 0:3 * w]
        halo_rows = pl.ds(pl.multiple_of(jnp.maximum(r0 - 8, 0), 8), 8)
        halo = jnp.where(i > 0, x_ref[halo_rows, 0:3 * w], 0.0)
        ext = jnp.concatenate([halo, cur], axis=0)
        acc = cw_ref[DN_CONV - 1:DN_CONV, :] * cur
        for j in range(DN_CONV - 1):
            acc = acc + cw_ref[j:j + 1, :] * pltpu.roll(ext, DN_CONV - 1 - j, 0)[8:, :]
        y = acc * jax.nn.sigmoid(acc)
        for j in range(w // LANES):
            cols = lambda part: slice(part * w + j * LANES, part * w + (j + 1) * LANES)
            q_scr[rows, j * LANES:(j + 1) * LANES] = _pair_l2(y[:, cols(0)], lo_tile) * (HEAD_DIM ** -0.5)
            k_scr[rows, j * LANES:(j + 1) * LANES] = _pair_l2(y[:, cols(1)], lo_tile)
            v_scr[rows, j * LANES:(j + 1) * LANES] = y[:, cols(2)]
        ab = x_ref[rows, 3 * w:3 * w + LANES]
        g = -jnp.exp(alog_ref[...]) * _softplus(ab + dtb_ref[...])
        g_scr[rows, :] = _dot(chunk_tril, _dot(g, expand, HIGHEST), HIGHEST)
        beta = pltpu.roll(jax.nn.sigmoid(ab), LANES - N_HEADS, 1)
        b_scr[rows, :] = _dot(beta, expand, HIGHEST)
        return carry

    ri = lax.broadcasted_iota(jnp.int32, (c, w), 0)
    ci = lax.broadcasted_iota(jnp.int32, (c, w), 1)
    cj = jnp.bitwise_and(ci, HEAD_DIM - 1)
    causal = ri >= cj
    strict = ri > cj
    eye4 = (ri == cj).astype(F32)
    same_sub = lax.shift_right_logical(ri, 4) == lax.shift_right_logical(cj, 4)
    bi_r = lax.broadcasted_iota(jnp.int32, (w, w), 0)
    bi_c = lax.broadcasted_iota(jnp.int32, (w, w), 1)
    same_head = lax.shift_right_logical(bi_r, 6) == lax.shift_right_logical(bi_c, 6)
    block_mask = same_head.astype(F32)
    chunk_tril = (same_head & (jnp.bitwise_and(bi_r, c - 1) >= jnp.bitwise_and(bi_c, c - 1))).astype(F32)
    expand = (lax.broadcasted_iota(jnp.int32, (LANES, w), 0)
              == lax.shift_right_logical(lax.broadcasted_iota(jnp.int32, (LANES, w), 1), 6)).astype(F32)
    lax.fori_loop(0, seq // DN_TILE, prep_tile, 0)

    def per_head(a, b):
        bd = jnp.where(same_head, jnp.concatenate([b.astype(BF16)] * N_HEADS, axis=0), 0.0)
        return _dot(a.astype(BF16), bd)

    def solve_group(i, carry):
        base = pl.multiple_of(i * (DN_GROUP * c), DN_GROUP * c)
        rows = [pl.ds(base + u * c, c) for u in range(DN_GROUP)]
        each = lambda f, *lists: [f(*args) for args in zip(*lists)]
        q, k, v, gc, beta = ([ref[r, :] for r in rows] for ref in (q_scr, k_scr, v_scr, g_scr, b_scr))
        g_row = each(lambda g: jnp.sum(g * eye4, axis=0, keepdims=True), gc)
        decay = each(lambda g, gr: jnp.exp(jnp.where(causal, g - gr, NEG_INF)), gc, g_row)
        kb = each(jnp.multiply, k, beta)
        k_bd = each(lambda x: jnp.where(same_head, jnp.concatenate([x.astype(BF16)] * N_HEADS, axis=0), 0.0), k)
        lmat = each(lambda a, b, d: jnp.where(strict, _dot_nt(a.astype(BF16), b) * d, 0.0), kb, k_bd, decay)
        a_qk = each(lambda a, b, d: jnp.where(causal, _dot_nt(a.astype(BF16), b) * d, 0.0), q, k_bd, decay)
        p = each(lambda l_: jnp.where(same_sub, -l_, 0.0), lmat)
        t_diag = each(lambda x: eye4 + x, p)
        for _ in range(3):
            p = each(per_head, p, p)
            t_diag = each(lambda t, x: t + per_head(t, x), t_diag, p)
        nil = each(lambda t, l_: per_head(t, jnp.where(same_sub, 0.0, l_)), t_diag, lmat)
        nil2 = each(per_head, nil, nil)
        nil3 = each(per_head, nil, nil2)
        t_inv = each(lambda n1, n2, n3, t: per_head(eye4 - n1 + n2 - n3, t), nil, nil2, nil3, t_diag)
        eg = each(jnp.exp, gc)
        w_c = each(lambda t, a, e: per_head(t, a * e), t_inv, kb, eg)
        u_c = each(lambda t, a, b: per_head(t, a * b), t_inv, v, beta)
        q_dec = each(jnp.multiply, q, eg)
        k_dec = each(lambda x, g: x * jnp.exp(g[c - 1:c, :] - g), k, gc)
        for ref, vals in zip((w_scr, v_scr, a_scr, q_scr, k_scr), (w_c, u_c, a_qk, q_dec, k_dec)):
            for r, val in zip(rows, vals):
                ref[r, :] = val
        return carry
    lax.fori_loop(0, seq // (DN_GROUP * c), solve_group, 0)

    s_scr[...] = jnp.zeros_like(s_scr)

    def state_chunk(n, carry):
        rows = pl.ds(pl.multiple_of(n * c, c), c)
        state = s_scr[...]
        state_b = state.astype(BF16)
        v_new = v_scr[rows, :] - _dot(w_scr[rows, :].astype(BF16), state_b)
        o_ref[rows, :] = _dot(q_scr[rows, :].astype(BF16), state_b) + per_head(a_scr[rows, :], v_new)
        upd = lax.dot_general(k_scr[rows, :].astype(BF16), v_new.astype(BF16), (((0,), (0,)), ((), ())),
                              preferred_element_type=F32)
        d_last = jnp.exp(g_scr[pl.ds(pl.multiple_of(n * c, c) + (c - 1), 1), :])
        s_scr[...] = state * d_last + upd * block_mask
        return carry
    lax.fori_loop(0, seq // c, state_chunk, 0, unroll=2)

    def finish_tile(i, carry):
        rows = pl.ds(pl.multiple_of(i * DN_TILE, DN_TILE), DN_TILE)
        o = o_ref[rows, :]
        ms = _dot(o * o, block_mask, HIGHEST) * (1.0 / HEAD_DIM)
        gate = x_ref[rows, 3 * w + LANES:4 * w + LANES]
        o_ref[rows, :] = o * lax.rsqrt(ms + EPS) * on_ref[...] * (gate * jax.nn.sigmoid(gate))
        return carry
    lax.fori_loop(0, seq // DN_TILE, finish_tile, 0)


def _dn(slab, conv_w, a_log, dt_bias, o_norm, *, batch, seq):
    wide = lambda: pltpu.VMEM((seq, GROUP_W), F32)
    return pl.pallas_call(
        functools.partial(_dn_body, seq=seq),
        grid=(batch,),
        in_specs=[pl.BlockSpec((seq, DN_SLAB), lambda b: (0, b)),
                  _const_spec((DN_CONV, 3 * GROUP_W)), _const_spec((1, LANES)), _const_spec((1, LANES)),
                  _const_spec((1, GROUP_W))],
        out_specs=pl.BlockSpec((seq, GROUP_W), lambda b: (0, b)),
        out_shape=jax.ShapeDtypeStruct((seq, batch * GROUP_W), F32),
        scratch_shapes=[wide() for _ in range(7)] + [pltpu.VMEM((GROUP_W, GROUP_W), F32)],
        compiler_params=_params("parallel"),
        name="deltanet",
    )(slab, conv_w, a_log, dt_bias, o_norm)


def _row(v, width=None):
    v = v.astype(F32).reshape(1, -1)
    if width is not None and v.shape[1] < width:
        v = jnp.pad(v, ((0, 0), (0, width - v.shape[1])))
    return v


def _prep_w_in(w):
    z = lambda n: jnp.zeros((D_MODEL, n), w.dtype)
    cols = [w[:, 0:416], z(96), w[:, 416:672], w[:, 672:1440], w[:, 1440:2208], w[:, 2208:2216], z(120),
            w[:, 2216:2472]]
    return jnp.concatenate(cols, axis=1).astype(BF16)


def _pad_heads(w, per_head, lo, hi):
    k = w.shape[0]
    w = w.reshape(k, N_HEADS, per_head)[:, :, lo:hi]
    w = jnp.pad(w, ((0, 0), (0, 0), (0, LANES - (hi - lo))))
    return w.reshape(k, N_HEADS * LANES).astype(BF16)


def _rope_tables(seq):
    half = MLA_ROPE // 2
    pos = jnp.arange(seq, dtype=F32)
    freqs = ROPE_THETA ** (-jnp.arange(half, dtype=F32) / half)
    ang = pos[:, None] * freqs[None, :]
    cos, sin = jnp.cos(ang), jnp.sin(ang)
    ones = jnp.ones((seq, MLA_NOPE), F32)
    zeros = jnp.zeros((seq, MLA_NOPE), F32)
    tail1 = jnp.ones((seq, LANES - MLA_DQK), F32)
    tail0 = jnp.zeros((seq, LANES - MLA_DQK), F32)
    return (jnp.concatenate([ones, cos, cos, tail1], axis=1),
            jnp.concatenate([zeros, -sin, sin, tail0], axis=1))


def _block_diag(blocks):
    g, r, c = blocks.shape
    eye = jnp.eye(g, dtype=blocks.dtype)
    return (blocks[:, :, None, :] * eye[:, None, :, None]).reshape(g * r, g * c)


def _mla_layer(slab, p, l, *, batch, seq):
    w_ukv = p["mla_w_ukv"][l]
    cos, sin = _rope_tables(seq)
    return _mla(slab, _row(p["mla_q_norm"][l]), _row(p["mla_kv_norm"][l]),
                _pad_heads(p["mla_w_uq"][l], MLA_DQK, 0, MLA_DQK),
                _pad_heads(w_ukv, MLA_NOPE + HEAD_DIM, 0, MLA_NOPE),
                w_ukv.reshape(MLA_KV_RANK, N_HEADS, MLA_NOPE + HEAD_DIM)[:, :, MLA_NOPE:]
                .reshape(MLA_KV_RANK, GROUP_W).astype(BF16),
                _row(p["mla_qk_q"][l], LANES), _row(p["mla_qk_k"][l], LANES), cos, sin, batch=batch, seq=seq)


def _dil_layer(slab, p, l, *, batch, seq):
    pair = lambda g: jnp.tile(g.astype(F32).reshape(1, HEAD_DIM), (1, LANES // HEAD_DIM))
    return _dil(slab, pair(p["dil_q_norm"][l]), pair(p["dil_k_norm"][l]), _dil_band(p["t5_bias"]),
                batch=batch, seq=seq)


def _dn_layer(slab, p, l, *, batch, seq):
    return _dn(slab, p["dn_conv"][l].astype(F32), _row(p["dn_a_log"][l], LANES), _row(p["dn_dt_bias"][l], LANES),
               jnp.tile(p["dn_o_norm"][l].astype(F32).reshape(1, HEAD_DIM), (1, N_HEADS)), batch=batch, seq=seq)


def _s5_layer(u, p, l, *, batch, seq):
    state_row = lambda v: v.astype(F32).reshape(1, S5_WIDTH)
    ldt = jnp.broadcast_to(p["s5_log_dt"][l][:, None], (S5_GROUPS, S5_STATE))
    bre = _block_diag(jnp.swapaxes(p["s5_b_re"][l], 1, 2).astype(F32))
    bim = _block_diag(jnp.swapaxes(p["s5_b_im"][l], 1, 2).astype(F32))
    cre = _block_diag(jnp.swapaxes(p["s5_c_re"][l], 1, 2)).astype(BF16)
    cim = _block_diag(jnp.swapaxes(p["s5_c_im"][l], 1, 2)).astype(BF16)
    return _s5(u, state_row(p["s5_lambda_re"][l]), state_row(p["s5_lambda_im"][l]), state_row(ldt),
               bre, bim, cre, cim, _row(p["s5_d"][l]), p["s5_w_glu"][l].astype(BF16), batch=batch, seq=seq,
               steps=min(256, seq))


def kernel(x, attn_norm, w_in, w_out, mla_q_norm, mla_kv_norm, mla_w_uq, mla_w_ukv, mla_qk_q, mla_qk_k,
           s5_lambda_re, s5_lambda_im, s5_log_dt, s5_b_re, s5_b_im, s5_c_re, s5_c_im, s5_d, s5_w_glu,
           dil_q_norm, dil_k_norm, t5_bias, dn_conv, dn_a_log, dn_dt_bias, dn_o_norm,
           ffn_norm, ffn_w1, ffn_w3, ffn_w2):
    p = dict(mla_q_norm=mla_q_norm, mla_kv_norm=mla_kv_norm, mla_w_uq=mla_w_uq, mla_w_ukv=mla_w_ukv,
             mla_qk_q=mla_qk_q, mla_qk_k=mla_qk_k, s5_lambda_re=s5_lambda_re, s5_lambda_im=s5_lambda_im,
             s5_log_dt=s5_log_dt, s5_b_re=s5_b_re, s5_b_im=s5_b_im, s5_c_re=s5_c_re, s5_c_im=s5_c_im, s5_d=s5_d,
             s5_w_glu=s5_w_glu, dil_q_norm=dil_q_norm, dil_k_norm=dil_k_norm, t5_bias=t5_bias, dn_conv=dn_conv,
             dn_a_log=dn_a_log, dn_dt_bias=dn_dt_bias, dn_o_norm=dn_o_norm)
    batch, seq, _ = x.shape
    h = x.reshape(batch * seq, D_MODEL)
    for l in range(attn_norm.shape[0]):
        mla_in, s5_in, dil_in, dn_in = _proj(h, _row(attn_norm[l]), _prep_w_in(w_in[l]), batch=batch, seq=seq)
        ys = [_mla_layer(mla_in, p, l, batch=batch, seq=seq),
              _s5_layer(s5_in, p, l, batch=batch, seq=seq),
              _dil_layer(dil_in, p, l, batch=batch, seq=seq),
              _dn_layer(dn_in, p, l, batch=batch, seq=seq)]
        h = _out_ffn(h, ys, w_out[l].astype(BF16), _row(ffn_norm[l]),
                     ffn_w1[l].astype(BF16), ffn_w3[l].astype(BF16), ffn_w2[l].astype(BF16), batch=batch, seq=seq)
    return h.reshape(batch, seq, D_MODEL)
```

```python
import functools
import math

import jax
import jax.numpy as jnp
import numpy as np
from jax import lax
from jax.experimental import pallas as pl
from jax.experimental.pallas import tpu as pltpu

F32 = jnp.float32
BF16 = jnp.bfloat16
HIGHEST = lax.Precision.HIGHEST

D_MODEL = 1024
GROUP_W = 256
HEAD_DIM = 64
N_HEADS = 4
EPS = 1e-6
NEG_INF = -1e30

MLA_NOPE = 64
MLA_ROPE = 32
MLA_DQK = MLA_NOPE + MLA_ROPE
MLA_KV_RANK = 128
ROPE_THETA = 10000.0

S5_GROUP_CH = 16
S5_GROUPS = 16
S5_STATE = 64
S5_WIDTH = S5_GROUPS * S5_STATE

DIL_PAIRS = ((128, 1), (512, 4), (2048, 16))
DIL_SPAN = 128
T5_BUCKETS = 32
T5_MAX_DIST = 2048

DN_CONV = 4
DN_CHUNK = 64

FFN_HIDDEN = 2816
FFN_CHUNK = 2816

VMEM_LIMIT_BYTES = 56 * 1024 * 1024
LANES = 128

MLA_SLAB = 512
S5_SLAB = 256
DIL_SLAB = 768
DN_SLAB = 1152
PROJ_COLS = MLA_SLAB + S5_SLAB + DIL_SLAB + DN_SLAB


def _dot(a, b, precision=None):
    return jnp.dot(a, b, preferred_element_type=F32, precision=precision)


def _dot_nt(a, b, precision=None):
    return lax.dot_general(a, b, (((1,), (1,)), ((), ())), preferred_element_type=F32, precision=precision)


def _const_spec(shape):
    nd = len(shape)
    return pl.BlockSpec(shape, lambda *_: (0,) * nd, pipeline_mode=pl.Buffered(1))


def _params(*sem):
    return pltpu.CompilerParams(dimension_semantics=sem, vmem_limit_bytes=VMEM_LIMIT_BYTES)


def _proj_body(x_ref, g_ref, w_ref, mla_ref, s5_ref, dil_ref, dn_ref):
    x = x_ref[...]
    n = x * lax.rsqrt(jnp.mean(x * x, axis=-1, keepdims=True) + EPS) * g_ref[...]
    nb = n.astype(BF16)
    start = 0
    for ref in (mla_ref, s5_ref, dil_ref, dn_ref):
        width = ref.shape[-1]
        ref[...] = _dot(nb, w_ref[:, start:start + width])
        start += width


def _proj(h, gain, w_big, *, batch, seq, tm=512):
    nt = seq // tm
    widths = (MLA_SLAB, S5_SLAB, DIL_SLAB, DN_SLAB)
    return pl.pallas_call(
        _proj_body,
        grid=(batch, nt),
        in_specs=[pl.BlockSpec((tm, D_MODEL), lambda b, i: (b * nt + i, 0)),
                  _const_spec((1, D_MODEL)),
                  _const_spec((D_MODEL, PROJ_COLS))],
        out_specs=[pl.BlockSpec((tm, w), lambda b, i: (i, b)) for w in widths],
        out_shape=[jax.ShapeDtypeStruct((seq, batch * w), F32) for w in widths],
        compiler_params=_params("parallel", "parallel"),
        name="proj",
    )(h, gain, w_big)


def _out_ffn_body(h_ref, y0_ref, y1_ref, y2_ref, y3_ref, wo_ref, g_ref, w1_ref, w3_ref, w2_ref, o_ref, acc_ref):
    h = h_ref[...]
    for i, y_ref in enumerate((y0_ref, y1_ref, y2_ref, y3_ref)):
        h = h + _dot(y_ref[...].astype(BF16), wo_ref[i * GROUP_W:(i + 1) * GROUP_W, :])
    n = h * lax.rsqrt(jnp.mean(h * h, axis=-1, keepdims=True) + EPS) * g_ref[...]
    nb = n.astype(BF16)
    acc_ref[...] = h

    def hidden_chunk(c, carry):
        cols = pl.ds(pl.multiple_of(c * FFN_CHUNK, FFN_CHUNK), FFN_CHUNK)
        a = _dot(nb, w1_ref[:, cols])
        b = _dot(nb, w3_ref[:, cols])
        z = (a * jax.nn.sigmoid(a) * b).astype(BF16)
        acc_ref[...] += _dot(z, w2_ref[cols, :])
        return carry
    lax.fori_loop(0, FFN_HIDDEN // FFN_CHUNK, hidden_chunk, 0)
    o_ref[...] = acc_ref[...]


def _out_ffn(h, ys, w_out, gain, w1, w3, w2, *, batch, seq, tm=512):
    nt = seq // tm
    row = pl.BlockSpec((tm, D_MODEL), lambda b, i: (b * nt + i, 0))
    slab = pl.BlockSpec((tm, GROUP_W), lambda b, i: (i, b))
    return pl.pallas_call(
        _out_ffn_body,
        grid=(batch, nt),
        in_specs=[row] + [slab] * 4 + [
            _const_spec((D_MODEL, D_MODEL)), _const_spec((1, D_MODEL)),
            _const_spec((D_MODEL, FFN_HIDDEN)), _const_spec((D_MODEL, FFN_HIDDEN)),
            _const_spec((FFN_HIDDEN, D_MODEL))],
        out_specs=row,
        out_shape=jax.ShapeDtypeStruct((batch * seq, D_MODEL), F32),
        scratch_shapes=[pltpu.VMEM((tm, D_MODEL), F32)],
        compiler_params=_params("parallel", "parallel"),
        name="out_ffn",
    )(h, *ys, w_out, gain, w1, w3, w2)


MLA_BLOCK = 256


def _mla_body(x_ref, qn_ref, kvn_ref, wuq_ref, wuk_ref, wuv_ref, gq_ref, gk_ref, cos_ref, sin_ref,
              o_ref, k_scr, vt_scr, *, seq):
    blk = MLA_BLOCK
    n_pairs = N_HEADS // 2
    qi = pl.program_id(1)
    lane = lax.broadcasted_iota(jnp.int32, (blk, LANES), 1)

    def rope(x, c, s):
        rot = jnp.where(lane < MLA_NOPE + MLA_ROPE // 2, pltpu.roll(x, LANES - MLA_ROPE // 2, 1),
                        pltpu.roll(x, MLA_ROPE // 2, 1))
        return x * c + rot * s

    def norm_head(x, g):
        ssq = jnp.sum(x * x, axis=-1, keepdims=True)
        return x * lax.rsqrt(ssq * (1.0 / MLA_DQK) + EPS) * g

    @pl.when(qi == 0)
    def _prepare_keys_values():
        def tile(i, carry):
            r0 = pl.multiple_of(i * blk, blk)
            rows = pl.ds(r0, blk)
            ckv = x_ref[rows, 256:384]
            k_rope = x_ref[rows, 384:512]
            kvn = ckv * lax.rsqrt(jnp.mean(ckv * ckv, axis=-1, keepdims=True) + EPS) * kvn_ref[...]
            kvn = kvn.astype(BF16)
            k_nope = _dot(kvn, wuk_ref[...])
            v = _dot(kvn, wuv_ref[...])
            k_rope = pltpu.roll(k_rope, MLA_NOPE, 1)
            c = cos_ref[rows, :]
            s = sin_ref[rows, :]
            for h in range(N_HEADS):
                k = rope(norm_head(k_nope[:, h * LANES:(h + 1) * LANES] + k_rope, gk_ref[...]), c, s)
                k_scr[h, rows, :] = k.astype(BF16)
            for pr in range(n_pairs):
                vt_scr[pr, :, rows] = v[:, pr * LANES:(pr + 1) * LANES].T.astype(BF16)
            return carry
        lax.fori_loop(0, seq // blk, tile, 0)

    r0 = pl.multiple_of(qi * blk, blk)
    rows = pl.ds(r0, blk)
    cq = x_ref[rows, 0:256]
    qn = cq * lax.rsqrt(jnp.mean(cq * cq, axis=-1, keepdims=True) + EPS) * qn_ref[...]
    q_all = _dot(qn.astype(BF16), wuq_ref[...])
    c = cos_ref[rows, :]
    s = sin_ref[rows, :]
    key_pos = lax.broadcasted_iota(jnp.int32, (blk, blk), 0)
    query_pos = lax.broadcasted_iota(jnp.int32, (blk, blk), 1)
    first_of_pair = lax.broadcasted_iota(jnp.int32, (LANES, blk), 0) < HEAD_DIM
    heads = range(N_HEADS)
    qs = []
    for h in heads:
        q = rope(norm_head(q_all[:, h * LANES:(h + 1) * LANES], gq_ref[...]), c, s)
        qs.append((q * (MLA_DQK ** -0.5)).astype(BF16))

    def kv_block(j, carry, diagonal):
        m_old, l_old, acc_old = carry
        keys = pl.ds(pl.multiple_of(j * blk, blk), blk)
        logits = [_dot_nt(k_scr[h, keys, :], qs[h]) for h in heads]
        if diagonal:
            logits = [jnp.where(key_pos <= query_pos, x, NEG_INF) for x in logits]
        m_new = [jnp.maximum(m_old[h], jnp.max(logits[h], axis=0, keepdims=True)) for h in heads]
        alpha = [jnp.exp(m_old[h] - m_new[h]) for h in heads]
        p = [jnp.exp(logits[h] - m_new[h]) for h in heads]
        l_new = [alpha[h] * l_old[h] + jnp.sum(p[h], axis=0, keepdims=True) for h in heads]
        pv = [_dot(vt_scr[h // 2, :, keys], p[h].astype(BF16)) for h in heads]
        acc_new = [jnp.where(first_of_pair, alpha[2 * pr] * acc_old[pr] + pv[2 * pr],
                             alpha[2 * pr + 1] * acc_old[pr] + pv[2 * pr + 1]) for pr in range(n_pairs)]
        return tuple(m_new), tuple(l_new), tuple(acc_new)

    init = (tuple(jnp.full((1, blk), NEG_INF, F32) for _ in heads), tuple(jnp.zeros((1, blk), F32) for _ in heads),
            tuple(jnp.zeros((LANES, blk), F32) for _ in range(n_pairs)))
    carry = lax.fori_loop(0, qi, functools.partial(kv_block, diagonal=False), init)
    _, l, acc = kv_block(qi, carry, diagonal=True)
    for pr in range(n_pairs):
        out_t = acc[pr] / jnp.where(first_of_pair, l[2 * pr], l[2 * pr + 1])
        o_ref[rows, pr * LANES:(pr + 1) * LANES] = out_t.T


def _mla(slab, qn, kvn, wuq, wuk, wuv, gq, gk, cos, sin, *, batch, seq):
    nq = seq // MLA_BLOCK
    return pl.pallas_call(
        functools.partial(_mla_body, seq=seq),
        grid=(batch, nq),
        in_specs=[pl.BlockSpec((seq, MLA_SLAB), lambda b, i: (0, b)),
                  _const_spec((1, 256)), _const_spec((1, MLA_KV_RANK)),
                  _const_spec((256, N_HEADS * LANES)), _const_spec((MLA_KV_RANK, N_HEADS * LANES)),
                  _const_spec((MLA_KV_RANK, GROUP_W)),
                  _const_spec((1, LANES)), _const_spec((1, LANES)),
                  _const_spec((seq, LANES)), _const_spec((seq, LANES))],
        out_specs=pl.BlockSpec((seq, GROUP_W), lambda b, i: (0, b)),
        out_shape=jax.ShapeDtypeStruct((seq, batch * GROUP_W), F32),
        scratch_shapes=[pltpu.VMEM((N_HEADS, seq, LANES), BF16), pltpu.VMEM((N_HEADS // 2, LANES, seq), BF16)],
        compiler_params=_params("parallel", "arbitrary"),
        name="mla",
    )(slab, qn, kvn, wuq, wuk, wuv, gq, gk, cos, sin)


def _s5_body(u_ref, lre_ref, lim_ref, ldt_ref, bre_ref, bim_ref, cre_ref, cim_ref, d_ref, wglu_ref, o_ref,
             a_scr, bbar_scr, h_scr, x_scr, *, batch, steps):
    n_tiles = S5_WIDTH // LANES
    fan = S5_WIDTH // GROUP_W

    @pl.when(pl.program_id(0) == 0)
    def _discretise():
        lr = lre_ref[...]
        li = lim_ref[...]
        dt = jnp.exp(ldt_ref[...])
        mag = jnp.exp(lr * dt)
        ar = mag * jnp.cos(li * dt)
        ai = mag * jnp.sin(li * dt)
        den = lr * lr + li * li
        nr = ar - 1.0
        zr = (nr * lr + ai * li) / den
        zi = (ai * lr - nr * li) / den
        for j in range(n_tiles):
            lanes = slice(j * LANES, (j + 1) * LANES)
            a_scr[0, j] = jnp.broadcast_to(ar[:, lanes], (batch, LANES))
            a_scr[1, j] = jnp.broadcast_to(ai[:, lanes], (batch, LANES))
        bre = bre_ref[...]
        bim = bim_ref[...]
        bbar_scr[0] = (zr * bre - zi * bim).astype(BF16)
        bbar_scr[1] = (zr * bim + zi * bre).astype(BF16)
        h_scr[...] = jnp.zeros_like(h_scr)

    for b in range(batch):
        ub = u_ref[:, b * GROUP_W:(b + 1) * GROUP_W].astype(BF16)
        rows_b = pl.ds(b, steps, stride=batch)
        for part in range(2):
            for t in range(GROUP_W // LANES):
                cols = slice(t * fan * LANES, (t + 1) * fan * LANES)
                x = _dot(ub[:, t * LANES:(t + 1) * LANES], bbar_scr[part, t * LANES:(t + 1) * LANES, cols])
                for j in range(fan):
                    x_scr[part, t * fan + j, rows_b, :] = x[:, j * LANES:(j + 1) * LANES]
    ar = a_scr[0]
    ai = a_scr[1]

    def step(t, carry):
        hr, hi = carry
        rows = pl.ds(pl.multiple_of(t * batch, batch), batch)
        nhr = ar * hr - ai * hi + x_scr[0, :, rows, :]
        nhi = ar * hi + ai * hr + x_scr[1, :, rows, :]
        x_scr[0, :, rows, :] = nhr
        x_scr[1, :, rows, :] = nhi
        return nhr, nhi

    hr, hi = lax.fori_loop(0, steps, step, (h_scr[0], h_scr[1]), unroll=8)
    h_scr[0] = hr
    h_scr[1] = hi
    ys = []
    for b in range(batch):
        rows_b = pl.ds(b, steps, stride=batch)
        y_tiles = []
        for t in range(GROUP_W // LANES):
            rows_c = slice(t * fan * LANES, (t + 1) * fan * LANES)
            out_cols = slice(t * LANES, (t + 1) * LANES)
            hr, hi = (jnp.concatenate([x_scr[part, t * fan + j, rows_b, :] for j in range(fan)], axis=1).astype(BF16)
                      for part in range(2))
            y_tiles.append(_dot(hr, cre_ref[rows_c, out_cols]) - _dot(hi, cim_ref[rows_c, out_cols]))
        u = u_ref[:, b * GROUP_W:(b + 1) * GROUP_W]
        ys.append((jnp.concatenate(y_tiles, axis=1) + d_ref[...] * u).astype(BF16))
    zs = [_dot(y, wglu_ref[...]) for y in ys]
    for b, z in enumerate(zs):
        o_ref[:, b * GROUP_W:(b + 1) * GROUP_W] = z[:, :GROUP_W] * jax.nn.sigmoid(z[:, GROUP_W:])


def _s5(u, lre, lim, ldt, bre, bim, cre, cim, d, wglu, *, batch, seq, steps=256):
    n_tiles = S5_WIDTH // LANES
    state = lambda rows: pltpu.VMEM((2, n_tiles, rows, LANES), F32)
    return pl.pallas_call(
        functools.partial(_s5_body, batch=batch, steps=steps),
        grid=(seq // steps,),
        in_specs=[pl.BlockSpec((steps, batch * S5_SLAB), lambda i: (i, 0)),
                  _const_spec((1, S5_WIDTH)), _const_spec((1, S5_WIDTH)), _const_spec((1, S5_WIDTH)),
                  _const_spec((GROUP_W, S5_WIDTH)), _const_spec((GROUP_W, S5_WIDTH)),
                  _const_spec((S5_WIDTH, GROUP_W)), _const_spec((S5_WIDTH, GROUP_W)),
                  _const_spec((1, GROUP_W)), _const_spec((GROUP_W, 2 * GROUP_W))],
        out_specs=pl.BlockSpec((steps, batch * GROUP_W), lambda i: (i, 0)),
        out_shape=jax.ShapeDtypeStruct((seq, batch * GROUP_W), F32),
        scratch_shapes=[state(batch), pltpu.VMEM((2, GROUP_W, S5_WIDTH), BF16), state(batch),
                        state(steps * batch)],
        compiler_params=_params("arbitrary"),
        name="s5",
    )(u, lre, lim, ldt, bre, bim, cre, cim, d, wglu)


DIL_TILE = 256
DIL_GROUP = 2


def _pair_norm(x, gain2, lo_half):
    sq = x * x
    tot = jnp.sum(sq, axis=-1, keepdims=True)
    lo = jnp.sum(jnp.where(lo_half, sq, 0.0), axis=-1, keepdims=True)
    ms = jnp.where(lo_half, lo, tot - lo) * (1.0 / HEAD_DIM)
    return x * lax.rsqrt(ms + EPS) * gain2


def _dil_body(x_ref, gq_ref, gk_ref, band_ref, o_ref, bias_ref, q_scr, k_scr, v_scr, qd, kd, vd, ob, mb, lb,
              oa, ma, la, *, seq):
    span = DIL_SPAN
    n_blocks = seq // span
    n_tiles = GROUP_W // LANES
    lo_tile = lax.broadcasted_iota(jnp.int32, (DIL_TILE, LANES), 1) < HEAD_DIM
    lo_blk = lax.broadcasted_iota(jnp.int32, (span, LANES), 1) < HEAD_DIM

    for bh in range(len(DIL_PAIRS) * N_HEADS):
        profile = jnp.broadcast_to(band_ref[bh:bh + 1, :], (span, 2 * span))
        bias_ref[bh] = pltpu.roll(profile, 0, 1, stride=1, stride_axis=0)
    in_current = lax.broadcasted_iota(jnp.int32, (span, 2 * span), 1) >= span

    def norm_tile(i, carry):
        rows = pl.ds(pl.multiple_of(i * DIL_TILE, DIL_TILE), DIL_TILE)
        for j in range(n_tiles):
            cols = lambda part: slice(part * GROUP_W + j * LANES, part * GROUP_W + (j + 1) * LANES)
            q = _pair_norm(x_ref[rows, cols(0)], gq_ref[...], lo_tile)
            q_scr[j, rows, :] = q * (HEAD_DIM ** -0.5)
            k_scr[j, rows, :] = _pair_norm(x_ref[rows, cols(1)], gk_ref[...], lo_tile)
            v_scr[j, rows, :] = x_ref[rows, cols(2)]
        return carry
    lax.fori_loop(0, seq // DIL_TILE, norm_tile, 0)

    for bi, (window, dil) in enumerate(DIL_PAIRS):
        sub_len = seq // dil
        nb = sub_len // span
        chunks = [(r, c) for r in range(dil) for c in range(nb)]

        def natural(r, c, dil=dil):
            return pl.ds(r + dil * c * span, span, stride=dil) if dil > 1 else pl.ds(c * span, span)

        if dil > 1:
            for r, c in chunks:
                dst = pl.ds((r * nb + c) * span, span)
                for j in range(n_tiles):
                    qd[j, dst, :] = q_scr[j, natural(r, c), :]
                    kd[j, dst, :] = k_scr[j, natural(r, c), :]
                    vd[j, dst, :] = v_scr[j, natural(r, c), :]
            q_src, k_src, v_src = qd, kd, vd
        else:
            q_src, k_src, v_src = q_scr, k_scr, v_scr
        o_dst, m_dst, l_dst = (oa, ma, la) if bi == 0 else (ob, mb, lb)

        def block_group(i, carry, bi=bi, nb=nb, q_src=q_src, k_src=k_src, v_src=v_src,
                        o_dst=o_dst, m_dst=m_dst, l_dst=l_dst):
            todo = []
            for u in range(DIL_GROUP):
                t = i * DIL_GROUP + u
                rows = pl.ds(pl.multiple_of(t * span, span), span)
                prev = pl.ds(pl.multiple_of(jnp.maximum(t - 1, 0) * span, span), span)
                keep = jnp.logical_or(in_current, (t & (nb - 1)) != 0) if nb > 1 else None
                for j in range(n_tiles):
                    q2 = q_src[j, rows, :]
                    if nb > 1:
                        k_cat = jnp.concatenate([k_src[j, prev, :], k_src[j, rows, :]], axis=0).astype(BF16)
                        v_cat = jnp.concatenate([v_src[j, prev, :], v_src[j, rows, :]], axis=0).astype(BF16)
                    else:
                        k_cat = k_src[j, rows, :].astype(BF16)
                        v_cat = v_src[j, rows, :].astype(BF16)
                    todo.append((rows, j, q2, k_cat, v_cat, keep))
            logits = []
            for rows, j, q2, k_cat, v_cat, keep in todo:
                for a in range(2):
                    qa = jnp.where(lo_blk if a == 0 else jnp.logical_not(lo_blk), q2, 0.0).astype(BF16)
                    bias = bias_ref[bi * N_HEADS + 2 * j + a]
                    lg = _dot_nt(qa, k_cat) + (bias if nb > 1 else bias[:, span:])
                    logits.append(jnp.where(keep, lg, NEG_INF) if nb > 1 else lg)
            m = [jnp.max(lg, axis=-1, keepdims=True) for lg in logits]
            p = [jnp.exp(lg - mx) for lg, mx in zip(logits, m)]
            l = [jnp.sum(px, axis=-1, keepdims=True) for px in p]
            o = [_dot(p[2 * n + a].astype(BF16), todo[n][4]) for n in range(len(todo)) for a in range(2)]
            for n, (rows, j, *_) in enumerate(todo):
                o_dst[j, rows, :] = jnp.where(lo_blk, o[2 * n], o[2 * n + 1])
                m_dst[j, rows, :] = jnp.where(lo_blk, m[2 * n], m[2 * n + 1])
                l_dst[j, rows, :] = jnp.where(lo_blk, l[2 * n], l[2 * n + 1])
            return carry
        lax.fori_loop(0, n_blocks // DIL_GROUP, block_group, 0)

        if bi > 0:
            for r, c in chunks:
                src = pl.ds((r * nb + c) * span, span)
                nat = natural(r, c)
                for j in range(n_tiles):
                    m_old, m_in = ma[j, nat, :], mb[j, src, :]
                    m_new = jnp.maximum(m_old, m_in)
                    w_old, w_in = jnp.exp(m_old - m_new), jnp.exp(m_in - m_new)
                    oa[j, nat, :] = w_old * oa[j, nat, :] + w_in * ob[j, src, :]
                    la[j, nat, :] = w_old * la[j, nat, :] + w_in * lb[j, src, :]
                    ma[j, nat, :] = m_new

    def finish(i, carry):
        rows = pl.ds(pl.multiple_of(i * DIL_TILE, DIL_TILE), DIL_TILE)
        for j in range(n_tiles):
            o_ref[rows, j * LANES:(j + 1) * LANES] = oa[j, rows, :] / la[j, rows, :]
        return carry
    lax.fori_loop(0, seq // DIL_TILE, finish, 0)


def _dil(slab, gq2, gk2, band, *, batch, seq):
    big = lambda: pltpu.VMEM((GROUP_W // LANES, seq, LANES), F32)
    n_bias = len(DIL_PAIRS) * N_HEADS
    return pl.pallas_call(
        functools.partial(_dil_body, seq=seq),
        grid=(batch,),
        in_specs=[pl.BlockSpec((seq, DIL_SLAB), lambda b: (0, b)),
                  _const_spec((1, LANES)), _const_spec((1, LANES)),
                  _const_spec(band.shape)],
        out_specs=pl.BlockSpec((seq, GROUP_W), lambda b: (0, b)),
        out_shape=jax.ShapeDtypeStruct((seq, batch * GROUP_W), F32),
        scratch_shapes=[pltpu.VMEM((n_bias, DIL_SPAN, 2 * DIL_SPAN), F32)] + [big() for _ in range(12)],
        compiler_params=_params("parallel"),
        name="dilated",
    )(slab, gq2, gk2, band)


def _t5_bucket(dist):
    exact = T5_BUCKETS // 2
    df = jnp.maximum(dist, 1).astype(F32)
    large = exact + (jnp.log(df / exact) / math.log(T5_MAX_DIST / exact) * (T5_BUCKETS - exact)).astype(jnp.int32)
    large = jnp.minimum(large, T5_BUCKETS - 1)
    return jnp.where(dist < exact, dist, large)


def _dil_band(table):
    span = DIL_SPAN
    delta = span - jnp.arange(2 * span, dtype=jnp.int32)
    rows = []
    for _, dil in DIL_PAIRS:
        bucket = _t5_bucket(jnp.clip(delta, 0, span) * dil)
        onehot = (bucket[:, None] == jnp.arange(T5_BUCKETS, dtype=jnp.int32)[None, :]).astype(F32)
        vals = jnp.dot(onehot, table.astype(F32), precision=HIGHEST)
        rows.append(jnp.where((delta >= 0)[:, None], vals, NEG_INF).T)
    return jnp.concatenate(rows, axis=0)


DN_TILE = 256
DN_GROUP = 4


def _softplus(x):
    return jnp.maximum(x, 0.0) + jnp.log1p(jnp.exp(-jnp.abs(x)))


def _pair_l2(x, lo_half):
    sq = x * x
    tot = jnp.sum(sq, axis=-1, keepdims=True)
    lo = jnp.sum(jnp.where(lo_half, sq, 0.0), axis=-1, keepdims=True)
    return x * lax.rsqrt(jnp.where(lo_half, lo, tot - lo) + EPS)


def _dn_body(x_ref, cw_ref, alog_ref, dtb_ref, on_ref, o_ref, q_scr, k_scr, v_scr, g_scr, b_scr, w_scr, a_scr,
             s_scr, *, seq):
    c = DN_CHUNK
    w = GROUP_W
    lo_tile = lax.broadcasted_iota(jnp.int32, (DN_TILE, LANES), 1) < HEAD_DIM

    def prep_tile(i, carry):
        r0 = pl.multiple_of(i * DN_TILE, DN_TILE)
        rows = pl.ds(r0, DN_TILE)
        cur = x_ref[rows, 0:3 * w]
        halo_rows = pl.ds(pl.multiple_of(jnp.maximum(r0 - 8, 0), 8), 8)
        halo = jnp.where(i > 0, x_ref[halo_rows, 0:3 * w], 0.0)
        ext = jnp.concatenate([halo, cur], axis=0)
        acc = cw_ref[DN_CONV - 1:DN_CONV, :] * cur
        for j in range(DN_CONV - 1):
            acc = acc + cw_ref[j:j + 1, :] * pltpu.roll(ext, DN_CONV - 1 - j, 0)[8:, :]
        y = acc * jax.nn.sigmoid(acc)
        for j in range(w // LANES):
            cols = lambda part: slice(part * w + j * LANES, part * w + (j + 1) * LANES)
            q_scr[rows, j * LANES:(j + 1) * LANES] = _pair_l2(y[:, cols(0)], lo_tile) * (HEAD_DIM ** -0.5)
            k_scr[rows, j * LANES:(j + 1) * LANES] = _pair_l2(y[:, cols(1)], lo_tile)
            v_scr[rows, j * LANES:(j + 1) * LANES] = y[:, cols(2)]
        ab = x_ref[rows, 3 * w:3 * w + LANES]
        g = -jnp.exp(alog_ref[...]) * _softplus(ab + dtb_ref[...])
        g_scr[rows, :] = _dot(chunk_tril, _dot(g, expand, HIGHEST), HIGHEST)
        beta = pltpu.roll(jax.nn.sigmoid(ab), LANES - N_HEADS, 1)
        b_scr[rows, :] = _dot(beta, expand, HIGHEST)
        return carry

    ri = lax.broadcasted_iota(jnp.int32, (c, w), 0)
    ci = lax.broadcasted_iota(jnp.int32, (c, w), 1)
    cj = jnp.bitwise_and(ci, HEAD_DIM - 1)
    causal = ri >= cj
    strict = ri > cj
    eye4 = (ri == cj).astype(F32)
    same_sub = lax.shift_right_logical(ri, 4) == lax.shift_right_logical(cj, 4)
    bi_r = lax.broadcasted_iota(jnp.int32, (w, w), 0)
    bi_c = lax.broadcasted_iota(jnp.int32, (w, w), 1)
    same_head = lax.shift_right_logical(bi_r, 6) == lax.shift_right_logical(bi_c, 6)
    block_mask = same_head.astype(F32)
    chunk_tril = (same_head & (jnp.bitwise_and(bi_r, c - 1) >= jnp.bitwise_and(bi_c, c - 1))).astype(F32)
    expand = (lax.broadcasted_iota(jnp.int32, (LANES, w), 0)
              == lax.shift_right_logical(lax.broadcasted_iota(jnp.int32, (LANES, w), 1), 6)).astype(F32)
    lax.fori_loop(0, seq // DN_TILE, prep_tile, 0)

    def per_head(a, b):
        bd = jnp.where(same_head, jnp.concatenate([b.astype(BF16)] * N_HEADS, axis=0), 0.0)
        return _dot(a.astype(BF16), bd)

    def group_rows(i):
        base = pl.multiple_of(i * (DN_GROUP * c), DN_GROUP * c)
        return [pl.ds(base + u * c, c) for u in range(DN_GROUP)]

    def solve_steps(i):
        rows = group_rows(i)
        each = lambda f, *lists: [f(*args) for args in zip(*lists)]
        q, k, v, gc, beta = ([ref[r, :] for r in rows] for ref in (q_scr, k_scr, v_scr, g_scr, b_scr))
        g_row = each(lambda g: jnp.sum(g * eye4, axis=0, keepdims=True), gc)
        decay = each(lambda g, gr: jnp.exp(jnp.where(causal, g - gr, NEG_INF)), gc, g_row)
        kb = each(jnp.multiply, k, beta)
        k_bd = each(lambda x: jnp.where(same_head, jnp.concatenate([x.astype(BF16)] * N_HEADS, axis=0), 0.0), k)
        lmat = each(lambda a, b, d: jnp.where(strict, _dot_nt(a.astype(BF16), b) * d, 0.0), kb, k_bd, decay)
        yield
        a_qk = each(lambda a, b, d: jnp.where(causal, _dot_nt(a.astype(BF16), b) * d, 0.0), q, k_bd, decay)
        yield
        p = each(lambda l_: jnp.where(same_sub, -l_, 0.0), lmat)
        t_diag = each(lambda x: eye4 + x, p)
        for _ in range(3):
            p = each(per_head, p, p)
            yield
            t_diag = each(lambda t, x: t + per_head(t, x), t_diag, p)
            yield
        nil = each(lambda t, l_: per_head(t, jnp.where(same_sub, 0.0, l_)), t_diag, lmat)
        yield
        nil2 = each(per_head, nil, nil)
        yield
        nil3 = each(per_head, nil, nil2)
        yield
        t_inv = each(lambda n1, n2, n3, t: per_head(eye4 - n1 + n2 - n3, t), nil, nil2, nil3, t_diag)
        yield
        eg = each(jnp.exp, gc)
        w_c = each(lambda t, a, e: per_head(t, a * e), t_inv, kb, eg)
        yield
        u_c = each(lambda t, a, b: per_head(t, a * b), t_inv, v, beta)
        yield
        q_dec = each(jnp.multiply, q, eg)
        k_dec = each(lambda x, g: x * jnp.exp(g[c - 1:c, :] - g), k, gc)
        for ref, vals in zip((w_scr, v_scr, a_scr, q_scr, k_scr), (w_c, u_c, a_qk, q_dec, k_dec)):
            for r, val in zip(rows, vals):
                ref[r, :] = val

    def state_steps(i):
        for rows in group_rows(i):
            state = s_scr[...]
            state_b = state.astype(BF16)
            w_s = _dot(w_scr[rows, :].astype(BF16), state_b)
            q_s = _dot(q_scr[rows, :].astype(BF16), state_b)
            yield
            v_new = v_scr[rows, :] - w_s
            o_intra = per_head(a_scr[rows, :], v_new)
            upd = lax.dot_general(k_scr[rows, :].astype(BF16), v_new.astype(BF16), (((0,), (0,)), ((), ())),
                                  preferred_element_type=F32)
            yield
            o_ref[rows, :] = q_s + o_intra
            s_scr[...] = state * jnp.exp(g_scr[rows, :][c - 1:c, :]) + upd * block_mask
            yield

    def weave(*steps):
        live = list(steps)
        while live:
            for gen in list(live):
                if next(gen, "done") == "done":
                    live.remove(gen)

    n_groups = seq // (DN_GROUP * c)
    s_scr[...] = jnp.zeros_like(s_scr)
    weave(solve_steps(0))

    def group(i, carry):
        weave(solve_steps(i), state_steps(i - 1))
        return carry
    lax.fori_loop(1, n_groups, group, 0)
    weave(state_steps(n_groups - 1))

    def finish_tile(i, carry):
        rows = pl.ds(pl.multiple_of(i * DN_TILE, DN_TILE), DN_TILE)
        o = o_ref[rows, :]
        ms = _dot(o * o, block_mask, HIGHEST) * (1.0 / HEAD_DIM)
        gate = x_ref[rows, 3 * w + LANES:4 * w + LANES]
        o_ref[rows, :] = o * lax.rsqrt(ms + EPS) * on_ref[...] * (gate * jax.nn.sigmoid(gate))
        return carry
    lax.fori_loop(0, seq // DN_TILE, finish_tile, 0)


def _dn(slab, conv_w, a_log, dt_bias, o_norm, *, batch, seq):
    wide = lambda: pltpu.VMEM((seq, GROUP_W), F32)
    return pl.pallas_call(
        functools.partial(_dn_body, seq=seq),
        grid=(batch,),
        in_specs=[pl.BlockSpec((seq, DN_SLAB), lambda b: (0, b)),
                  _const_spec((DN_CONV, 3 * GROUP_W)), _const_spec((1, LANES)), _const_spec((1, LANES)),
                  _const_spec((1, GROUP_W))],
        out_specs=pl.BlockSpec((seq, GROUP_W), lambda b: (0, b)),
        out_shape=jax.ShapeDtypeStruct((seq, batch * GROUP_W), F32),
        scratch_shapes=[wide() for _ in range(7)] + [pltpu.VMEM((GROUP_W, GROUP_W), F32)],
        compiler_params=_params("parallel"),
        name="deltanet",
    )(slab, conv_w, a_log, dt_bias, o_norm)


def _row(v, width=None):
    v = v.astype(F32).reshape(1, -1)
    if width is not None and v.shape[1] < width:
        v = jnp.pad(v, ((0, 0), (0, width - v.shape[1])))
    return v


def _prep_w_in(w):
    z = lambda n: jnp.zeros((D_MODEL, n), w.dtype)
    cols = [w[:, 0:416], z(96), w[:, 416:672], w[:, 672:1440], w[:, 1440:2208], w[:, 2208:2216], z(120),
            w[:, 2216:2472]]
    return jnp.concatenate(cols, axis=1).astype(BF16)


def _pad_heads(w, per_head, lo, hi):
    k = w.shape[0]
    w = w.reshape(k, N_HEADS, per_head)[:, :, lo:hi]
    w = jnp.pad(w, ((0, 0), (0, 0), (0, LANES - (hi - lo))))
    return w.reshape(k, N_HEADS * LANES).astype(BF16)


def _rope_tables(seq):
    half = MLA_ROPE // 2
    pos = jnp.arange(seq, dtype=F32)
    freqs = ROPE_THETA ** (-jnp.arange(half, dtype=F32) / half)
    ang = pos[:, None] * freqs[None, :]
    cos, sin = jnp.cos(ang), jnp.sin(ang)
    ones = jnp.ones((seq, MLA_NOPE), F32)
    zeros = jnp.zeros((seq, MLA_NOPE), F32)
    tail1 = jnp.ones((seq, LANES - MLA_DQK), F32)
    tail0 = jnp.zeros((seq, LANES - MLA_DQK), F32)
    return (jnp.concatenate([ones, cos, cos, tail1], axis=1),
            jnp.concatenate([zeros, -sin, sin, tail0], axis=1))


def _block_diag(blocks):
    g, r, c = blocks.shape
    eye = jnp.eye(g, dtype=blocks.dtype)
    return (blocks[:, :, None, :] * eye[:, None, :, None]).reshape(g * r, g * c)


def _mla_layer(slab, p, l, *, batch, seq):
    w_ukv = p["mla_w_ukv"][l]
    cos, sin = _rope_tables(seq)
    return _mla(slab, _row(p["mla_q_norm"][l]), _row(p["mla_kv_norm"][l]),
                _pad_heads(p["mla_w_uq"][l], MLA_DQK, 0, MLA_DQK),
                _pad_heads(w_ukv, MLA_NOPE + HEAD_DIM, 0, MLA_NOPE),
                w_ukv.reshape(MLA_KV_RANK, N_HEADS, MLA_NOPE + HEAD_DIM)[:, :, MLA_NOPE:]
                .reshape(MLA_KV_RANK, GROUP_W).astype(BF16),
                _row(p["mla_qk_q"][l], LANES), _row(p["mla_qk_k"][l], LANES), cos, sin, batch=batch, seq=seq)


def _dil_layer(slab, p, l, *, batch, seq):
    pair = lambda g: jnp.tile(g.astype(F32).reshape(1, HEAD_DIM), (1, LANES // HEAD_DIM))
    return _dil(slab, pair(p["dil_q_norm"][l]), pair(p["dil_k_norm"][l]), _dil_band(p["t5_bias"]),
                batch=batch, seq=seq)


def _dn_layer(slab, p, l, *, batch, seq):
    return _dn(slab, p["dn_conv"][l].astype(F32), _row(p["dn_a_log"][l], LANES), _row(p["dn_dt_bias"][l], LANES),
               jnp.tile(p["dn_o_norm"][l].astype(F32).reshape(1, HEAD_DIM), (1, N_HEADS)), batch=batch, seq=seq)


def _s5_layer(u, p, l, *, batch, seq):
    state_row = lambda v: v.astype(F32).reshape(1, S5_WIDTH)
    ldt = jnp.broadcast_to(p["s5_log_dt"][l][:, None], (S5_GROUPS, S5_STATE))
    bre = _block_diag(jnp.swapaxes(p["s5_b_re"][l], 1, 2).astype(F32))
    bim = _block_diag(jnp.swapaxes(p["s5_b_im"][l], 1, 2).astype(F32))
    cre = _block_diag(jnp.swapaxes(p["s5_c_re"][l], 1, 2)).astype(BF16)
    cim = _block_diag(jnp.swapaxes(p["s5_c_im"][l], 1, 2)).astype(BF16)
    return _s5(u, state_row(p["s5_lambda_re"][l]), state_row(p["s5_lambda_im"][l]), state_row(ldt),
               bre, bim, cre, cim, _row(p["s5_d"][l]), p["s5_w_glu"][l].astype(BF16), batch=batch, seq=seq,
               steps=min(256, seq))


def kernel(x, attn_norm, w_in, w_out, mla_q_norm, mla_kv_norm, mla_w_uq, mla_w_ukv, mla_qk_q, mla_qk_k,
           s5_lambda_re, s5_lambda_im, s5_log_dt, s5_b_re, s5_b_im, s5_c_re, s5_c_im, s5_d, s5_w_glu,
           dil_q_norm, dil_k_norm, t5_bias, dn_conv, dn_a_log, dn_dt_bias, dn_o_norm,
           ffn_norm, ffn_w1, ffn_w3, ffn_w2):
    p = dict(mla_q_norm=mla_q_norm, mla_kv_norm=mla_kv_norm, mla_w_uq=mla_w_uq, mla_w_ukv=mla_w_ukv,
             mla_qk_q=mla_qk_q, mla_qk_k=mla_qk_k, s5_lambda_re=s5_lambda_re, s5_lambda_im=s5_lambda_im,
             s5_log_dt=s5_log_dt, s5_b_re=s5_b_re, s5_b_im=s5_b_im, s5_c_re=s5_c_re, s5_c_im=s5_c_im, s5_d=s5_d,
             s5_w_glu=s5_w_glu, dil_q_norm=dil_q_norm, dil_k_norm=dil_k_norm, t5_bias=t5_bias, dn_conv=dn_conv,
             dn_a_log=dn_a_log, dn_dt_bias=dn_dt_bias, dn_o_norm=dn_o_norm)
    batch, seq, _ = x.shape
    h = x.reshape(batch * seq, D_MODEL)
    for l in range(attn_norm.shape[0]):
        mla_in, s5_in, dil_in, dn_in = _proj(h, _row(attn_norm[l]), _prep_w_in(w_in[l]), batch=batch, seq=seq)
        ys = [_mla_layer(mla_in, p, l, batch=batch, seq=seq),
              _s5_layer(s5_in, p, l, batch=batch, seq=seq),
              _dil_layer(dil_in, p, l, batch=batch, seq=seq),
              _dn_layer(dn_in, p, l, batch=batch, seq=seq)]
        h = _out_ffn(h, ys, w_out[l].astype(BF16), _row(ffn_norm[l]),
                     ffn_w1[l].astype(BF16), ffn_w3[l].astype(BF16), ffn_w2[l].astype(BF16), batch=batch, seq=seq)
    return h.reshape(batch, seq, D_MODEL)
```

```python
import functools
import math

import jax
import jax.numpy as jnp
import numpy as np
from jax import lax
from jax.experimental import pallas as pl
from jax.experimental.pallas import tpu as pltpu

F32 = jnp.float32
BF16 = jnp.bfloat16
HIGHEST = lax.Precision.HIGHEST

D_MODEL = 1024
GROUP_W = 256
HEAD_DIM = 64
N_HEADS = 4
EPS = 1e-6
NEG_INF = -1e30

MLA_NOPE = 64
MLA_ROPE = 32
MLA_DQK = MLA_NOPE + MLA_ROPE
MLA_KV_RANK = 128
ROPE_THETA = 10000.0

S5_GROUP_CH = 16
S5_GROUPS = 16
S5_STATE = 64
S5_WIDTH = S5_GROUPS * S5_STATE

DIL_PAIRS = ((128, 1), (512, 4), (2048, 16))
DIL_SPAN = 128
T5_BUCKETS = 32
T5_MAX_DIST = 2048

DN_CONV = 4
DN_CHUNK = 64

FFN_HIDDEN = 2816
FFN_CHUNK = 2816

VMEM_LIMIT_BYTES = 56 * 1024 * 1024
LANES = 128

MLA_SLAB = 512
S5_SLAB = 256
DIL_SLAB = 768
DN_SLAB = 1152
PROJ_COLS = MLA_SLAB + S5_SLAB + DIL_SLAB + DN_SLAB


def _dot(a, b, precision=None):
    return jnp.dot(a, b, preferred_element_type=F32, precision=precision)


def _dot_nt(a, b, precision=None):
    return lax.dot_general(a, b, (((1,), (1,)), ((), ())), preferred_element_type=F32, precision=precision)


def _const_spec(shape):
    nd = len(shape)
    return pl.BlockSpec(shape, lambda *_: (0,) * nd, pipeline_mode=pl.Buffered(1))


def _params(*sem):
    return pltpu.CompilerParams(dimension_semantics=sem, vmem_limit_bytes=VMEM_LIMIT_BYTES)


def _proj_body(x_ref, g_ref, w_ref, mla_ref, s5_ref, dil_ref, dn_ref):
    x = x_ref[...]
    n = x * lax.rsqrt(jnp.mean(x * x, axis=-1, keepdims=True) + EPS) * g_ref[...]
    nb = n.astype(BF16)
    start = 0
    for ref in (mla_ref, s5_ref, dil_ref, dn_ref):
        width = ref.shape[-1]
        ref[...] = _dot(nb, w_ref[:, start:start + width])
        start += width


def _proj(h, gain, w_big, *, batch, seq, tm=512):
    nt = seq // tm
    widths = (MLA_SLAB, S5_SLAB, DIL_SLAB, DN_SLAB)
    return pl.pallas_call(
        _proj_body,
        grid=(batch, nt),
        in_specs=[pl.BlockSpec((tm, D_MODEL), lambda b, i: (b * nt + i, 0)),
                  _const_spec((1, D_MODEL)),
                  _const_spec((D_MODEL, PROJ_COLS))],
        out_specs=[pl.BlockSpec((tm, w), lambda b, i: (i, b)) for w in widths],
        out_shape=[jax.ShapeDtypeStruct((seq, batch * w), F32) for w in widths],
        compiler_params=_params("parallel", "parallel"),
        name="proj",
    )(h, gain, w_big)


def _out_ffn_body(h_ref, y0_ref, y1_ref, y2_ref, y3_ref, wo_ref, g_ref, w1_ref, w3_ref, w2_ref, o_ref, acc_ref):
    h = h_ref[...]
    for i, y_ref in enumerate((y0_ref, y1_ref, y2_ref, y3_ref)):
        h = h + _dot(y_ref[...].astype(BF16), wo_ref[i * GROUP_W:(i + 1) * GROUP_W, :])
    n = h * lax.rsqrt(jnp.mean(h * h, axis=-1, keepdims=True) + EPS) * g_ref[...]
    nb = n.astype(BF16)
    acc_ref[...] = h

    def hidden_chunk(c, carry):
        cols = pl.ds(pl.multiple_of(c * FFN_CHUNK, FFN_CHUNK), FFN_CHUNK)
        a = _dot(nb, w1_ref[:, cols])
        b = _dot(nb, w3_ref[:, cols])
        z = (a * jax.nn.sigmoid(a) * b).astype(BF16)
        acc_ref[...] += _dot(z, w2_ref[cols, :])
        return carry
    lax.fori_loop(0, FFN_HIDDEN // FFN_CHUNK, hidden_chunk, 0)
    o_ref[...] = acc_ref[...]


def _out_ffn(h, ys, w_out, gain, w1, w3, w2, *, batch, seq, tm=512):
    nt = seq // tm
    row = pl.BlockSpec((tm, D_MODEL), lambda b, i: (b * nt + i, 0))
    slab = pl.BlockSpec((tm, GROUP_W), lambda b, i: (i, b))
    return pl.pallas_call(
        _out_ffn_body,
        grid=(batch, nt),
        in_specs=[row] + [slab] * 4 + [
            _const_spec((D_MODEL, D_MODEL)), _const_spec((1, D_MODEL)),
            _const_spec((D_MODEL, FFN_HIDDEN)), _const_spec((D_MODEL, FFN_HIDDEN)),
            _const_spec((FFN_HIDDEN, D_MODEL))],
        out_specs=row,
        out_shape=jax.ShapeDtypeStruct((batch * seq, D_MODEL), F32),
        scratch_shapes=[pltpu.VMEM((tm, D_MODEL), F32)],
        compiler_params=_params("parallel", "parallel"),
        name="out_ffn",
    )(h, *ys, w_out, gain, w1, w3, w2)


MLA_BLOCK = 256
MLA_SWEEP = 2


def _mla_body(x_ref, qn_ref, kvn_ref, wuq_ref, wuk_ref, wuv_ref, gq_ref, gk_ref, cos_ref, sin_ref,
              o_ref, k_scr, vt_scr, *, seq):
    blk = MLA_BLOCK
    n_pairs = N_HEADS // 2
    qi = pl.program_id(1)
    lane = lax.broadcasted_iota(jnp.int32, (blk, LANES), 1)

    def rope(x, c, s):
        rot = jnp.where(lane < MLA_NOPE + MLA_ROPE // 2, pltpu.roll(x, LANES - MLA_ROPE // 2, 1),
                        pltpu.roll(x, MLA_ROPE // 2, 1))
        return x * c + rot * s

    def norm_head(x, g):
        ssq = jnp.sum(x * x, axis=-1, keepdims=True)
        return x * lax.rsqrt(ssq * (1.0 / MLA_DQK) + EPS) * g

    @pl.when(qi == 0)
    def _prepare_keys_values():
        def tile(i, carry):
            r0 = pl.multiple_of(i * blk, blk)
            rows = pl.ds(r0, blk)
            ckv = x_ref[rows, 256:384]
            k_rope = x_ref[rows, 384:512]
            kvn = ckv * lax.rsqrt(jnp.mean(ckv * ckv, axis=-1, keepdims=True) + EPS) * kvn_ref[...]
            kvn = kvn.astype(BF16)
            k_nope = _dot(kvn, wuk_ref[...])
            v = _dot(kvn, wuv_ref[...])
            k_rope = pltpu.roll(k_rope, MLA_NOPE, 1)
            c = cos_ref[rows, :]
            s = sin_ref[rows, :]
            for h in range(N_HEADS):
                k = rope(norm_head(k_nope[:, h * LANES:(h + 1) * LANES] + k_rope, gk_ref[...]), c, s)
                k_scr[h, rows, :] = k.astype(BF16)
            for pr in range(n_pairs):
                vt_scr[pr, :, rows] = v[:, pr * LANES:(pr + 1) * LANES].T.astype(BF16)
            return carry
        lax.fori_loop(0, seq // blk, tile, 0)

    r0 = pl.multiple_of(qi * blk, blk)
    rows = pl.ds(r0, blk)
    cq = x_ref[rows, 0:256]
    qn = cq * lax.rsqrt(jnp.mean(cq * cq, axis=-1, keepdims=True) + EPS) * qn_ref[...]
    q_all = _dot(qn.astype(BF16), wuq_ref[...])
    c = cos_ref[rows, :]
    s = sin_ref[rows, :]
    key_pos = lax.broadcasted_iota(jnp.int32, (blk, blk), 0)
    query_pos = lax.broadcasted_iota(jnp.int32, (blk, blk), 1)
    first_of_pair = lax.broadcasted_iota(jnp.int32, (LANES, blk), 0) < HEAD_DIM
    heads = range(N_HEADS)
    qs = []
    for h in heads:
        q = rope(norm_head(q_all[:, h * LANES:(h + 1) * LANES], gq_ref[...]), c, s)
        qs.append((q * (MLA_DQK ** -0.5)).astype(BF16))

    def kv_step(first_key, width, carry, diagonal=False):
        m_old, l_old, acc_old = carry
        keys = pl.ds(pl.multiple_of(first_key, blk), width)
        logits = [_dot_nt(k_scr[h, keys, :], qs[h]) for h in heads]
        if diagonal:
            logits = [jnp.where(key_pos <= query_pos, x, NEG_INF) for x in logits]
        m_new = [jnp.maximum(m_old[h], jnp.max(logits[h], axis=0, keepdims=True)) for h in heads]
        alpha = [jnp.exp(m_old[h] - m_new[h]) for h in heads]
        p = [jnp.exp(logits[h] - m_new[h]) for h in heads]
        l_new = [alpha[h] * l_old[h] + jnp.sum(p[h], axis=0, keepdims=True) for h in heads]
        pv = [_dot(vt_scr[h // 2, :, keys], p[h].astype(BF16)) for h in heads]
        acc_new = [jnp.where(first_of_pair, alpha[2 * pr] * acc_old[pr] + pv[2 * pr],
                             alpha[2 * pr + 1] * acc_old[pr] + pv[2 * pr + 1]) for pr in range(n_pairs)]
        return tuple(m_new), tuple(l_new), tuple(acc_new)

    wide = MLA_SWEEP * blk
    carry = (tuple(jnp.full((1, blk), NEG_INF, F32) for _ in heads), tuple(jnp.zeros((1, blk), F32) for _ in heads),
             tuple(jnp.zeros((LANES, blk), F32) for _ in range(n_pairs)))
    n_wide = qi // MLA_SWEEP
    carry = lax.fori_loop(0, n_wide, lambda j, cr: kv_step(j * wide, wide, cr), carry)
    carry = lax.fori_loop(n_wide * MLA_SWEEP, qi, lambda j, cr: kv_step(j * blk, blk, cr), carry)
    _, l, acc = kv_step(qi * blk, blk, carry, diagonal=True)
    for pr in range(n_pairs):
        out_t = acc[pr] / jnp.where(first_of_pair, l[2 * pr], l[2 * pr + 1])
        o_ref[rows, pr * LANES:(pr + 1) * LANES] = out_t.T


def _mla(slab, qn, kvn, wuq, wuk, wuv, gq, gk, cos, sin, *, batch, seq):
    nq = seq // MLA_BLOCK
    return pl.pallas_call(
        functools.partial(_mla_body, seq=seq),
        grid=(batch, nq),
        in_specs=[pl.BlockSpec((seq, MLA_SLAB), lambda b, i: (0, b)),
                  _const_spec((1, 256)), _const_spec((1, MLA_KV_RANK)),
                  _const_spec((256, N_HEADS * LANES)), _const_spec((MLA_KV_RANK, N_HEADS * LANES)),
                  _const_spec((MLA_KV_RANK, GROUP_W)),
                  _const_spec((1, LANES)), _const_spec((1, LANES)),
                  _const_spec((seq, LANES)), _const_spec((seq, LANES))],
        out_specs=pl.BlockSpec((seq, GROUP_W), lambda b, i: (0, b)),
        out_shape=jax.ShapeDtypeStruct((seq, batch * GROUP_W), F32),
        scratch_shapes=[pltpu.VMEM((N_HEADS, seq, LANES), BF16), pltpu.VMEM((N_HEADS // 2, LANES, seq), BF16)],
        compiler_params=_params("parallel", "arbitrary"),
        name="mla",
    )(slab, qn, kvn, wuq, wuk, wuv, gq, gk, cos, sin)


def _s5_body(u_ref, lre_ref, lim_ref, ldt_ref, bre_ref, bim_ref, cre_ref, cim_ref, d_ref, wglu_ref, o_ref,
             a_scr, bbar_scr, h_scr, x_scr, *, batch, steps):
    n_tiles = S5_WIDTH // LANES
    fan = S5_WIDTH // GROUP_W

    @pl.when(pl.program_id(0) == 0)
    def _discretise():
        lr = lre_ref[...]
        li = lim_ref[...]
        dt = jnp.exp(ldt_ref[...])
        mag = jnp.exp(lr * dt)
        ar = mag * jnp.cos(li * dt)
        ai = mag * jnp.sin(li * dt)
        den = lr * lr + li * li
        nr = ar - 1.0
        zr = (nr * lr + ai * li) / den
        zi = (ai * lr - nr * li) / den
        for j in range(n_tiles):
            lanes = slice(j * LANES, (j + 1) * LANES)
            a_scr[0, j] = jnp.broadcast_to(ar[:, lanes], (batch, LANES))
            a_scr[1, j] = jnp.broadcast_to(ai[:, lanes], (batch, LANES))
        bre = bre_ref[...]
        bim = bim_ref[...]
        bbar_scr[0] = (zr * bre - zi * bim).astype(BF16)
        bbar_scr[1] = (zr * bim + zi * bre).astype(BF16)
        h_scr[...] = jnp.zeros_like(h_scr)

    for b in range(batch):
        ub = u_ref[:, b * GROUP_W:(b + 1) * GROUP_W].astype(BF16)
        rows_b = pl.ds(b, steps, stride=batch)
        for part in range(2):
            for t in range(GROUP_W // LANES):
                cols = slice(t * fan * LANES, (t + 1) * fan * LANES)
                x = _dot(ub[:, t * LANES:(t + 1) * LANES], bbar_scr[part, t * LANES:(t + 1) * LANES, cols])
                for j in range(fan):
                    x_scr[part, t * fan + j, rows_b, :] = x[:, j * LANES:(j + 1) * LANES]
    ar = a_scr[0]
    ai = a_scr[1]

    def step(t, carry):
        hr, hi = carry
        rows = pl.ds(pl.multiple_of(t * batch, batch), batch)
        nhr = ar * hr - ai * hi + x_scr[0, :, rows, :]
        nhi = ar * hi + ai * hr + x_scr[1, :, rows, :]
        x_scr[0, :, rows, :] = nhr
        x_scr[1, :, rows, :] = nhi
        return nhr, nhi

    hr, hi = lax.fori_loop(0, steps, step, (h_scr[0], h_scr[1]), unroll=8)
    h_scr[0] = hr
    h_scr[1] = hi
    ys = []
    for b in range(batch):
        rows_b = pl.ds(b, steps, stride=batch)
        y_tiles = []
        for t in range(GROUP_W // LANES):
            rows_c = slice(t * fan * LANES, (t + 1) * fan * LANES)
            out_cols = slice(t * LANES, (t + 1) * LANES)
            hr, hi = (jnp.concatenate([x_scr[part, t * fan + j, rows_b, :] for j in range(fan)], axis=1).astype(BF16)
                      for part in range(2))
            y_tiles.append(_dot(hr, cre_ref[rows_c, out_cols]) - _dot(hi, cim_ref[rows_c, out_cols]))
        u = u_ref[:, b * GROUP_W:(b + 1) * GROUP_W]
        ys.append((jnp.concatenate(y_tiles, axis=1) + d_ref[...] * u).astype(BF16))
    zs = [_dot(y, wglu_ref[...]) for y in ys]
    for b, z in enumerate(zs):
        o_ref[:, b * GROUP_W:(b + 1) * GROUP_W] = z[:, :GROUP_W] * jax.nn.sigmoid(z[:, GROUP_W:])


def _s5(u, lre, lim, ldt, bre, bim, cre, cim, d, wglu, *, batch, seq, steps=256):
    n_tiles = S5_WIDTH // LANES
    state = lambda rows: pltpu.VMEM((2, n_tiles, rows, LANES), F32)
    return pl.pallas_call(
        functools.partial(_s5_body, batch=batch, steps=steps),
        grid=(seq // steps,),
        in_specs=[pl.BlockSpec((steps, batch * S5_SLAB), lambda i: (i, 0)),
                  _const_spec((1, S5_WIDTH)), _const_spec((1, S5_WIDTH)), _const_spec((1, S5_WIDTH)),
                  _const_spec((GROUP_W, S5_WIDTH)), _const_spec((GROUP_W, S5_WIDTH)),
                  _const_spec((S5_WIDTH, GROUP_W)), _const_spec((S5_WIDTH, GROUP_W)),
                  _const_spec((1, GROUP_W)), _const_spec((GROUP_W, 2 * GROUP_W))],
        out_specs=pl.BlockSpec((steps, batch * GROUP_W), lambda i: (i, 0)),
        out_shape=jax.ShapeDtypeStruct((seq, batch * GROUP_W), F32),
        scratch_shapes=[state(batch), pltpu.VMEM((2, GROUP_W, S5_WIDTH), BF16), state(batch),
                        state(steps * batch)],
        compiler_params=_params("arbitrary"),
        name="s5",
    )(u, lre, lim, ldt, bre, bim, cre, cim, d, wglu)


DIL_TILE = 256
DIL_GROUP = 2


def _pair_norm(x, gain2, lo_half):
    sq = x * x
    tot = jnp.sum(sq, axis=-1, keepdims=True)
    lo = jnp.sum(jnp.where(lo_half, sq, 0.0), axis=-1, keepdims=True)
    ms = jnp.where(lo_half, lo, tot - lo) * (1.0 / HEAD_DIM)
    return x * lax.rsqrt(ms + EPS) * gain2


def _dil_body(x_ref, gq_ref, gk_ref, band_ref, o_ref, bias_ref, q_scr, k_scr, v_scr, qd, kd, vd, ob, mb, lb,
              oa, ma, la, *, seq):
    span = DIL_SPAN
    n_blocks = seq // span
    n_tiles = GROUP_W // LANES
    lo_tile = lax.broadcasted_iota(jnp.int32, (DIL_TILE, LANES), 1) < HEAD_DIM
    lo_blk = lax.broadcasted_iota(jnp.int32, (span, LANES), 1) < HEAD_DIM

    for bh in range(len(DIL_PAIRS) * N_HEADS):
        profile = jnp.broadcast_to(band_ref[bh:bh + 1, :], (span, 2 * span))
        bias_ref[bh] = pltpu.roll(profile, 0, 1, stride=1, stride_axis=0)
    in_current = lax.broadcasted_iota(jnp.int32, (span, 2 * span), 1) >= span

    def norm_tile(i, carry):
        rows = pl.ds(pl.multiple_of(i * DIL_TILE, DIL_TILE), DIL_TILE)
        for j in range(n_tiles):
            cols = lambda part: slice(part * GROUP_W + j * LANES, part * GROUP_W + (j + 1) * LANES)
            q = _pair_norm(x_ref[rows, cols(0)], gq_ref[...], lo_tile)
            q_scr[j, rows, :] = q * (HEAD_DIM ** -0.5)
            k_scr[j, rows, :] = _pair_norm(x_ref[rows, cols(1)], gk_ref[...], lo_tile)
            v_scr[j, rows, :] = x_ref[rows, cols(2)]
        return carry
    lax.fori_loop(0, seq // DIL_TILE, norm_tile, 0)

    for bi, (window, dil) in enumerate(DIL_PAIRS):
        sub_len = seq // dil
        nb = sub_len // span
        chunks = [(r, c) for r in range(dil) for c in range(nb)]

        def natural(r, c, dil=dil):
            return pl.ds(r + dil * c * span, span, stride=dil) if dil > 1 else pl.ds(c * span, span)

        if dil > 1:
            for r, c in chunks:
                dst = pl.ds((r * nb + c) * span, span)
                for j in range(n_tiles):
                    qd[j, dst, :] = q_scr[j, natural(r, c), :]
                    kd[j, dst, :] = k_scr[j, natural(r, c), :]
                    vd[j, dst, :] = v_scr[j, natural(r, c), :]
            q_src, k_src, v_src = qd, kd, vd
        else:
            q_src, k_src, v_src = q_scr, k_scr, v_scr
        o_dst, m_dst, l_dst = (oa, ma, la) if bi == 0 else (ob, mb, lb)

        def block_group(i, carry, bi=bi, nb=nb, q_src=q_src, k_src=k_src, v_src=v_src,
                        o_dst=o_dst, m_dst=m_dst, l_dst=l_dst):
            todo = []
            for u in range(DIL_GROUP):
                t = i * DIL_GROUP + u
                rows = pl.ds(pl.multiple_of(t * span, span), span)
                prev = pl.ds(pl.multiple_of(jnp.maximum(t - 1, 0) * span, span), span)
                keep = jnp.logical_or(in_current, (t & (nb - 1)) != 0) if nb > 1 else None
                for j in range(n_tiles):
                    q2 = q_src[j, rows, :]
                    if nb > 1:
                        k_cat = jnp.concatenate([k_src[j, prev, :], k_src[j, rows, :]], axis=0).astype(BF16)
                        v_cat = jnp.concatenate([v_src[j, prev, :], v_src[j, rows, :]], axis=0).astype(BF16)
                    else:
                        k_cat = k_src[j, rows, :].astype(BF16)
                        v_cat = v_src[j, rows, :].astype(BF16)
                    todo.append((rows, j, q2, k_cat, v_cat, keep))
            logits = []
            for rows, j, q2, k_cat, v_cat, keep in todo:
                for a in range(2):
                    qa = jnp.where(lo_blk if a == 0 else jnp.logical_not(lo_blk), q2, 0.0).astype(BF16)
                    bias = bias_ref[bi * N_HEADS + 2 * j + a]
                    lg = _dot_nt(qa, k_cat) + (bias if nb > 1 else bias[:, span:])
                    logits.append(jnp.where(keep, lg, NEG_INF) if nb > 1 else lg)
            m = [jnp.max(lg, axis=-1, keepdims=True) for lg in logits]
            p = [jnp.exp(lg - mx) for lg, mx in zip(logits, m)]
            l = [jnp.sum(px, axis=-1, keepdims=True) for px in p]
            o = [_dot(p[2 * n + a].astype(BF16), todo[n][4]) for n in range(len(todo)) for a in range(2)]
            for n, (rows, j, *_) in enumerate(todo):
                o_dst[j, rows, :] = jnp.where(lo_blk, o[2 * n], o[2 * n + 1])
                m_dst[j, rows, :] = jnp.where(lo_blk, m[2 * n], m[2 * n + 1])
                l_dst[j, rows, :] = jnp.where(lo_blk, l[2 * n], l[2 * n + 1])
            return carry
        lax.fori_loop(0, n_blocks // DIL_GROUP, block_group, 0)

        if bi > 0:
            for r, c in chunks:
                src = pl.ds((r * nb + c) * span, span)
                nat = natural(r, c)
                for j in range(n_tiles):
                    m_old, m_in = ma[j, nat, :], mb[j, src, :]
                    m_new = jnp.maximum(m_old, m_in)
                    w_old, w_in = jnp.exp(m_old - m_new), jnp.exp(m_in - m_new)
                    oa[j, nat, :] = w_old * oa[j, nat, :] + w_in * ob[j, src, :]
                    la[j, nat, :] = w_old * la[j, nat, :] + w_in * lb[j, src, :]
                    ma[j, nat, :] = m_new

    def finish(i, carry):
        rows = pl.ds(pl.multiple_of(i * DIL_TILE, DIL_TILE), DIL_TILE)
        for j in range(n_tiles):
            o_ref[rows, j * LANES:(j + 1) * LANES] = oa[j, rows, :] / la[j, rows, :]
        return carry
    lax.fori_loop(0, seq // DIL_TILE, finish, 0)


def _dil(slab, gq2, gk2, band, *, batch, seq):
    big = lambda: pltpu.VMEM((GROUP_W // LANES, seq, LANES), F32)
    n_bias = len(DIL_PAIRS) * N_HEADS
    return pl.pallas_call(
        functools.partial(_dil_body, seq=seq),
        grid=(batch,),
        in_specs=[pl.BlockSpec((seq, DIL_SLAB), lambda b: (0, b)),
                  _const_spec((1, LANES)), _const_spec((1, LANES)),
                  _const_spec(band.shape)],
        out_specs=pl.BlockSpec((seq, GROUP_W), lambda b: (0, b)),
        out_shape=jax.ShapeDtypeStruct((seq, batch * GROUP_W), F32),
        scratch_shapes=[pltpu.VMEM((n_bias, DIL_SPAN, 2 * DIL_SPAN), F32)] + [big() for _ in range(12)],
        compiler_params=_params("parallel"),
        name="dilated",
    )(slab, gq2, gk2, band)


def _t5_bucket(dist):
    exact = T5_BUCKETS // 2
    df = jnp.maximum(dist, 1).astype(F32)
    large = exact + (jnp.log(df / exact) / math.log(T5_MAX_DIST / exact) * (T5_BUCKETS - exact)).astype(jnp.int32)
    large = jnp.minimum(large, T5_BUCKETS - 1)
    return jnp.where(dist < exact, dist, large)


def _dil_band(table):
    span = DIL_SPAN
    delta = span - jnp.arange(2 * span, dtype=jnp.int32)
    rows = []
    for _, dil in DIL_PAIRS:
        bucket = _t5_bucket(jnp.clip(delta, 0, span) * dil)
        onehot = (bucket[:, None] == jnp.arange(T5_BUCKETS, dtype=jnp.int32)[None, :]).astype(F32)
        vals = jnp.dot(onehot, table.astype(F32), precision=HIGHEST)
        rows.append(jnp.where((delta >= 0)[:, None], vals, NEG_INF).T)
    return jnp.concatenate(rows, axis=0)


DN_TILE = 256
DN_GROUP = 4


def _softplus(x):
    return jnp.maximum(x, 0.0) + jnp.log1p(jnp.exp(-jnp.abs(x)))


def _pair_l2(x, lo_half):
    sq = x * x
    tot = jnp.sum(sq, axis=-1, keepdims=True)
    lo = jnp.sum(jnp.where(lo_half, sq, 0.0), axis=-1, keepdims=True)
    return x * lax.rsqrt(jnp.where(lo_half, lo, tot - lo) + EPS)


def _dn_body(x_ref, cw_ref, alog_ref, dtb_ref, on_ref, o_ref, q_scr, k_scr, v_scr, g_scr, b_scr, w_scr, a_scr,
             s_scr, *, seq):
    c = DN_CHUNK
    w = GROUP_W
    lo_tile = lax.broadcasted_iota(jnp.int32, (DN_TILE, LANES), 1) < HEAD_DIM

    def prep_tile(i, carry):
        r0 = pl.multiple_of(i * DN_TILE, DN_TILE)
        rows = pl.ds(r0, DN_TILE)
        cur = x_ref[rows, 0:3 * w]
        halo_rows = pl.ds(pl.multiple_of(jnp.maximum(r0 - 8, 0), 8), 8)
        halo = jnp.where(i > 0, x_ref[halo_rows, 0:3 * w], 0.0)
        ext = jnp.concatenate([halo, cur], axis=0)
        acc = cw_ref[DN_CONV - 1:DN_CONV, :] * cur
        for j in range(DN_CONV - 1):
            acc = acc + cw_ref[j:j + 1, :] * pltpu.roll(ext, DN_CONV - 1 - j, 0)[8:, :]
        y = acc * jax.nn.sigmoid(acc)
        for j in range(w // LANES):
            cols = lambda part: slice(part * w + j * LANES, part * w + (j + 1) * LANES)
            q_scr[rows, j * LANES:(j + 1) * LANES] = _pair_l2(y[:, cols(0)], lo_tile) * (HEAD_DIM ** -0.5)
            k_scr[rows, j * LANES:(j + 1) * LANES] = _pair_l2(y[:, cols(1)], lo_tile)
            v_scr[rows, j * LANES:(j + 1) * LANES] = y[:, cols(2)]
        ab = x_ref[rows, 3 * w:3 * w + LANES]
        g = -jnp.exp(alog_ref[...]) * _softplus(ab + dtb_ref[...])
        g_scr[rows, :] = _dot(chunk_tril, _dot(g, expand, HIGHEST), HIGHEST)
        beta = pltpu.roll(jax.nn.sigmoid(ab), LANES - N_HEADS, 1)
        b_scr[rows, :] = _dot(beta, expand, HIGHEST)
        return carry

    ri = lax.broadcasted_iota(jnp.int32, (c, w), 0)
    ci = lax.broadcasted_iota(jnp.int32, (c, w), 1)
    cj = jnp.bitwise_and(ci, HEAD_DIM - 1)
    causal = ri >= cj
    strict = ri > cj
    eye4 = (ri == cj).astype(F32)
    same_sub = lax.shift_right_logical(ri, 4) == lax.shift_right_logical(cj, 4)
    bi_r = lax.broadcasted_iota(jnp.int32, (w, w), 0)
    bi_c = lax.broadcasted_iota(jnp.int32, (w, w), 1)
    same_head = lax.shift_right_logical(bi_r, 6) == lax.shift_right_logical(bi_c, 6)
    block_mask = same_head.astype(F32)
    chunk_tril = (same_head & (jnp.bitwise_and(bi_r, c - 1) >= jnp.bitwise_and(bi_c, c - 1))).astype(F32)
    expand = (lax.broadcasted_iota(jnp.int32, (LANES, w), 0)
              == lax.shift_right_logical(lax.broadcasted_iota(jnp.int32, (LANES, w), 1), 6)).astype(F32)
    lax.fori_loop(0, seq // DN_TILE, prep_tile, 0)

    def per_head(a, b):
        bd = jnp.where(same_head, jnp.concatenate([b.astype(BF16)] * N_HEADS, axis=0), 0.0)
        return _dot(a.astype(BF16), bd)

    def group_rows(i):
        base = pl.multiple_of(i * (DN_GROUP * c), DN_GROUP * c)
        return [pl.ds(base + u * c, c) for u in range(DN_GROUP)]

    def solve_steps(i):
        rows = group_rows(i)
        each = lambda f, *lists: [f(*args) for args in zip(*lists)]
        q, k, v, gc, beta = ([ref[r, :] for r in rows] for ref in (q_scr, k_scr, v_scr, g_scr, b_scr))
        g_row = each(lambda g: jnp.sum(g * eye4, axis=0, keepdims=True), gc)
        decay = each(lambda g, gr: jnp.exp(jnp.where(causal, g - gr, NEG_INF)), gc, g_row)
        kb = each(jnp.multiply, k, beta)
        k_bd = each(lambda x: jnp.where(same_head, jnp.concatenate([x.astype(BF16)] * N_HEADS, axis=0), 0.0), k)
        lmat = each(lambda a, b, d: jnp.where(strict, _dot_nt(a.astype(BF16), b) * d, 0.0), kb, k_bd, decay)
        yield
        a_qk = each(lambda a, b, d: jnp.where(causal, _dot_nt(a.astype(BF16), b) * d, 0.0), q, k_bd, decay)
        yield
        p = each(lambda l_: jnp.where(same_sub, -l_, 0.0), lmat)
        t_diag = each(lambda x: eye4 + x, p)
        for _ in range(3):
            p = each(per_head, p, p)
            yield
            t_diag = each(lambda t, x: t + per_head(t, x), t_diag, p)
            yield
        nil = each(lambda t, l_: per_head(t, jnp.where(same_sub, 0.0, l_)), t_diag, lmat)
        yield
        nil2 = each(per_head, nil, nil)
        yield
        nil3 = each(per_head, nil, nil2)
        yield
        t_inv = each(lambda n1, n2, n3, t: per_head(eye4 - n1 + n2 - n3, t), nil, nil2, nil3, t_diag)
        yield
        eg = each(jnp.exp, gc)
        w_c = each(lambda t, a, e: per_head(t, a * e), t_inv, kb, eg)
        yield
        u_c = each(lambda t, a, b: per_head(t, a * b), t_inv, v, beta)
        yield
        q_dec = each(jnp.multiply, q, eg)
        k_dec = each(lambda x, g: x * jnp.exp(g[c - 1:c, :] - g), k, gc)
        for ref, vals in zip((w_scr, v_scr, a_scr, q_scr, k_scr), (w_c, u_c, a_qk, q_dec, k_dec)):
            for r, val in zip(rows, vals):
                ref[r, :] = val

    def state_steps(i):
        for rows in group_rows(i):
            state = s_scr[...]
            state_b = state.astype(BF16)
            w_s = _dot(w_scr[rows, :].astype(BF16), state_b)
            q_s = _dot(q_scr[rows, :].astype(BF16), state_b)
            yield
            v_new = v_scr[rows, :] - w_s
            o_intra = per_head(a_scr[rows, :], v_new)
            upd = lax.dot_general(k_scr[rows, :].astype(BF16), v_new.astype(BF16), (((0,), (0,)), ((), ())),
                                  preferred_element_type=F32)
            yield
            o_ref[rows, :] = q_s + o_intra
            s_scr[...] = state * jnp.exp(g_scr[rows, :][c - 1:c, :]) + upd * block_mask
            yield

    def weave(*steps):
        live = list(steps)
        while live:
            for gen in list(live):
                if next(gen, "done") == "done":
                    live.remove(gen)

    n_groups = seq // (DN_GROUP * c)
    s_scr[...] = jnp.zeros_like(s_scr)
    weave(solve_steps(0))

    def group(i, carry):
        weave(solve_steps(i), state_steps(i - 1))
        return carry
    lax.fori_loop(1, n_groups, group, 0)
    weave(state_steps(n_groups - 1))

    def finish_tile(i, carry):
        rows = pl.ds(pl.multiple_of(i * DN_TILE, DN_TILE), DN_TILE)
        o = o_ref[rows, :]
        ms = _dot(o * o, block_mask, HIGHEST) * (1.0 / HEAD_DIM)
        gate = x_ref[rows, 3 * w + LANES:4 * w + LANES]
        o_ref[rows, :] = o * lax.rsqrt(ms + EPS) * on_ref[...] * (gate * jax.nn.sigmoid(gate))
        return carry
    lax.fori_loop(0, seq // DN_TILE, finish_tile, 0)


def _dn(slab, conv_w, a_log, dt_bias, o_norm, *, batch, seq):
    wide = lambda: pltpu.VMEM((seq, GROUP_W), F32)
    return pl.pallas_call(
        functools.partial(_dn_body, seq=seq),
        grid=(batch,),
        in_specs=[pl.BlockSpec((seq, DN_SLAB), lambda b: (0, b)),
                  _const_spec((DN_CONV, 3 * GROUP_W)), _const_spec((1, LANES)), _const_spec((1, LANES)),
                  _const_spec((1, GROUP_W))],
        out_specs=pl.BlockSpec((seq, GROUP_W), lambda b: (0, b)),
        out_shape=jax.ShapeDtypeStruct((seq, batch * GROUP_W), F32),
        scratch_shapes=[wide() for _ in range(7)] + [pltpu.VMEM((GROUP_W, GROUP_W), F32)],
        compiler_params=_params("parallel"),
        name="deltanet",
    )(slab, conv_w, a_log, dt_bias, o_norm)


def _row(v, width=None):
    v = v.astype(F32).reshape(1, -1)
    if width is not None and v.shape[1] < width:
        v = jnp.pad(v, ((0, 0), (0, width - v.shape[1])))
    return v


def _prep_w_in(w):
    z = lambda n: jnp.zeros((D_MODEL, n), w.dtype)
    cols = [w[:, 0:416], z(96), w[:, 416:672], w[:, 672:1440], w[:, 1440:2208], w[:, 2208:2216], z(120),
            w[:, 2216:2472]]
    return jnp.concatenate(cols, axis=1).astype(BF16)


def _pad_heads(w, per_head, lo, hi):
    k = w.shape[0]
    w = w.reshape(k, N_HEADS, per_head)[:, :, lo:hi]
    w = jnp.pad(w, ((0, 0), (0, 0), (0, LANES - (hi - lo))))
    return w.reshape(k, N_HEADS * LANES).astype(BF16)


def _rope_tables(seq):
    half = MLA_ROPE // 2
    pos = jnp.arange(seq, dtype=F32)
    freqs = ROPE_THETA ** (-jnp.arange(half, dtype=F32) / half)
    ang = pos[:, None] * freqs[None, :]
    cos, sin = jnp.cos(ang), jnp.sin(ang)
    ones = jnp.ones((seq, MLA_NOPE), F32)
    zeros = jnp.zeros((seq, MLA_NOPE), F32)
    tail1 = jnp.ones((seq, LANES - MLA_DQK), F32)
    tail0 = jnp.zeros((seq, LANES - MLA_DQK), F32)
    return (jnp.concatenate([ones, cos, cos, tail1], axis=1),
            jnp.concatenate([zeros, -sin, sin, tail0], axis=1))


def _block_diag(blocks):
    g, r, c = blocks.shape
    eye = jnp.eye(g, dtype=blocks.dtype)
    return (blocks[:, :, None, :] * eye[:, None, :, None]).reshape(g * r, g * c)


def _mla_layer(slab, p, l, *, batch, seq):
    w_ukv = p["mla_w_ukv"][l]
    cos, sin = _rope_tables(seq)
    return _mla(slab, _row(p["mla_q_norm"][l]), _row(p["mla_kv_norm"][l]),
                _pad_heads(p["mla_w_uq"][l], MLA_DQK, 0, MLA_DQK),
                _pad_heads(w_ukv, MLA_NOPE + HEAD_DIM, 0, MLA_NOPE),
                w_ukv.reshape(MLA_KV_RANK, N_HEADS, MLA_NOPE + HEAD_DIM)[:, :, MLA_NOPE:]
                .reshape(MLA_KV_RANK, GROUP_W).astype(BF16),
                _row(p["mla_qk_q"][l], LANES), _row(p["mla_qk_k"][l], LANES), cos, sin, batch=batch, seq=seq)


def _dil_layer(slab, p, l, *, batch, seq):
    pair = lambda g: jnp.tile(g.astype(F32).reshape(1, HEAD_DIM), (1, LANES // HEAD_DIM))
    return _dil(slab, pair(p["dil_q_norm"][l]), pair(p["dil_k_norm"][l]), _dil_band(p["t5_bias"]),
                batch=batch, seq=seq)


def _dn_layer(slab, p, l, *, batch, seq):
    return _dn(slab, p["dn_conv"][l].astype(F32), _row(p["dn_a_log"][l], LANES), _row(p["dn_dt_bias"][l], LANES),
               jnp.tile(p["dn_o_norm"][l].astype(F32).reshape(1, HEAD_DIM), (1, N_HEADS)), batch=batch, seq=seq)


def _s5_layer(u, p, l, *, batch, seq):
    state_row = lambda v: v.astype(F32).reshape(1, S5_WIDTH)
    ldt = jnp.broadcast_to(p["s5_log_dt"][l][:, None], (S5_GROUPS, S5_STATE))
    bre = _block_diag(jnp.swapaxes(p["s5_b_re"][l], 1, 2).astype(F32))
    bim = _block_diag(jnp.swapaxes(p["s5_b_im"][l], 1, 2).astype(F32))
    cre = _block_diag(jnp.swapaxes(p["s5_c_re"][l], 1, 2)).astype(BF16)
    cim = _block_diag(jnp.swapaxes(p["s5_c_im"][l], 1, 2)).astype(BF16)
    return _s5(u, state_row(p["s5_lambda_re"][l]), state_row(p["s5_lambda_im"][l]), state_row(ldt),
               bre, bim, cre, cim, _row(p["s5_d"][l]), p["s5_w_glu"][l].astype(BF16), batch=batch, seq=seq,
               steps=min(256, seq))


def kernel(x, attn_norm, w_in, w_out, mla_q_norm, mla_kv_norm, mla_w_uq, mla_w_ukv, mla_qk_q, mla_qk_k,
           s5_lambda_re, s5_lambda_im, s5_log_dt, s5_b_re, s5_b_im, s5_c_re, s5_c_im, s5_d, s5_w_glu,
           dil_q_norm, dil_k_norm, t5_bias, dn_conv, dn_a_log, dn_dt_bias, dn_o_norm,
           ffn_norm, ffn_w1, ffn_w3, ffn_w2):
    p = dict(mla_q_norm=mla_q_norm, mla_kv_norm=mla_kv_norm, mla_w_uq=mla_w_uq, mla_w_ukv=mla_w_ukv,
             mla_qk_q=mla_qk_q, mla_qk_k=mla_qk_k, s5_lambda_re=s5_lambda_re, s5_lambda_im=s5_lambda_im,
             s5_log_dt=s5_log_dt, s5_b_re=s5_b_re, s5_b_im=s5_b_im, s5_c_re=s5_c_re, s5_c_im=s5_c_im, s5_d=s5_d,
             s5_w_glu=s5_w_glu, dil_q_norm=dil_q_norm, dil_k_norm=dil_k_norm, t5_bias=t5_bias, dn_conv=dn_conv,
             dn_a_log=dn_a_log, dn_dt_bias=dn_dt_bias, dn_o_norm=dn_o_norm)
    batch, seq, _ = x.shape
    h = x.reshape(batch * seq, D_MODEL)
    for l in range(attn_norm.shape[0]):
        mla_in, s5_in, dil_in, dn_in = _proj(h, _row(attn_norm[l]), _prep_w_in(w_in[l]), batch=batch, seq=seq)
        ys = [_mla_layer(mla_in, p, l, batch=batch, seq=seq),
              _s5_layer(s5_in, p, l, batch=batch, seq=seq),
              _dil_layer(dil_in, p, l, batch=batch, seq=seq),
              _dn_layer(dn_in, p, l, batch=batch, seq=seq)]
        h = _out_ffn(h, ys, w_out[l].astype(BF16), _row(ffn_norm[l]),
                     ffn_w1[l].astype(BF16), ffn_w3[l].astype(BF16), ffn_w2[l].astype(BF16), batch=batch, seq=seq)
    return h.reshape(batch, seq, D_MODEL)
```

```python
import functools
import math

import jax
import jax.numpy as jnp
import numpy as np
from jax import lax
from jax.experimental import pallas as pl
from jax.experimental.pallas import tpu as pltpu

F32 = jnp.float32
BF16 = jnp.bfloat16
HIGHEST = lax.Precision.HIGHEST

D_MODEL = 1024
GROUP_W = 256
HEAD_DIM = 64
N_HEADS = 4
EPS = 1e-6
NEG_INF = -1e30

MLA_NOPE = 64
MLA_ROPE = 32
MLA_DQK = MLA_NOPE + MLA_ROPE
MLA_KV_RANK = 128
ROPE_THETA = 10000.0

S5_GROUP_CH = 16
S5_GROUPS = 16
S5_STATE = 64
S5_WIDTH = S5_GROUPS * S5_STATE

DIL_PAIRS = ((128, 1), (512, 4), (2048, 16))
DIL_SPAN = 128
T5_BUCKETS = 32
T5_MAX_DIST = 2048

DN_CONV = 4
DN_CHUNK = 64

FFN_HIDDEN = 2816
FFN_CHUNK = 2816

VMEM_LIMIT_BYTES = 56 * 1024 * 1024
LANES = 128

MLA_SLAB = 512
S5_SLAB = 256
DIL_SLAB = 768
DN_SLAB = 1152
PROJ_COLS = MLA_SLAB + S5_SLAB + DIL_SLAB + DN_SLAB


def _dot(a, b, precision=None):
    return jnp.dot(a, b, preferred_element_type=F32, precision=precision)


def _dot_nt(a, b, precision=None):
    return lax.dot_general(a, b, (((1,), (1,)), ((), ())), preferred_element_type=F32, precision=precision)


def _dot_split(a, b, terms, split_rhs=False):
    x = b if split_rhs else a
    mask = (a if split_rhs else b).astype(BF16)
    out = None
    for _ in range(terms):
        piece = x.astype(BF16)
        part = _dot(mask, piece) if split_rhs else _dot(piece, mask)
        out = part if out is None else out + part
        x = x - piece.astype(F32)
    return out


def _const_spec(shape):
    nd = len(shape)
    return pl.BlockSpec(shape, lambda *_: (0,) * nd, pipeline_mode=pl.Buffered(1))


def _params(*sem):
    return pltpu.CompilerParams(dimension_semantics=sem, vmem_limit_bytes=VMEM_LIMIT_BYTES)


def _proj_body(x_ref, g_ref, w_ref, mla_ref, s5_ref, dil_ref, dn_ref):
    x = x_ref[...]
    n = x * lax.rsqrt(jnp.mean(x * x, axis=-1, keepdims=True) + EPS) * g_ref[...]
    nb = n.astype(BF16)
    start = 0
    for ref in (mla_ref, s5_ref, dil_ref, dn_ref):
        width = ref.shape[-1]
        ref[...] = _dot(nb, w_ref[:, start:start + width])
        start += width


def _proj(h, gain, w_big, *, batch, seq, tm=512):
    nt = seq // tm
    widths = (MLA_SLAB, S5_SLAB, DIL_SLAB, DN_SLAB)
    return pl.pallas_call(
        _proj_body,
        grid=(batch, nt),
        in_specs=[pl.BlockSpec((tm, D_MODEL), lambda b, i: (b * nt + i, 0)),
                  _const_spec((1, D_MODEL)),
                  _const_spec((D_MODEL, PROJ_COLS))],
        out_specs=[pl.BlockSpec((tm, w), lambda b, i: (i, b)) for w in widths],
        out_shape=[jax.ShapeDtypeStruct((seq, batch * w), F32) for w in widths],
        compiler_params=_params("parallel", "parallel"),
        name="proj",
    )(h, gain, w_big)


def _out_ffn_body(h_ref, y0_ref, y1_ref, y2_ref, y3_ref, wo_ref, g_ref, w1_ref, w3_ref, w2_ref, o_ref, acc_ref):
    h = h_ref[...]
    for i, y_ref in enumerate((y0_ref, y1_ref, y2_ref, y3_ref)):
        h = h + _dot(y_ref[...].astype(BF16), wo_ref[i * GROUP_W:(i + 1) * GROUP_W, :])
    n = h * lax.rsqrt(jnp.mean(h * h, axis=-1, keepdims=True) + EPS) * g_ref[...]
    nb = n.astype(BF16)
    acc_ref[...] = h

    def hidden_chunk(c, carry):
        cols = pl.ds(pl.multiple_of(c * FFN_CHUNK, FFN_CHUNK), FFN_CHUNK)
        a = _dot(nb, w1_ref[:, cols])
        b = _dot(nb, w3_ref[:, cols])
        z = (a * jax.nn.sigmoid(a) * b).astype(BF16)
        acc_ref[...] += _dot(z, w2_ref[cols, :])
        return carry
    lax.fori_loop(0, FFN_HIDDEN // FFN_CHUNK, hidden_chunk, 0)
    o_ref[...] = acc_ref[...]


def _out_ffn(h, ys, w_out, gain, w1, w3, w2, *, batch, seq, tm=512):
    nt = seq // tm
    row = pl.BlockSpec((tm, D_MODEL), lambda b, i: (b * nt + i, 0))
    slab = pl.BlockSpec((tm, GROUP_W), lambda b, i: (i, b))
    return pl.pallas_call(
        _out_ffn_body,
        grid=(batch, nt),
        in_specs=[row] + [slab] * 4 + [
            _const_spec((D_MODEL, D_MODEL)), _const_spec((1, D_MODEL)),
            _const_spec((D_MODEL, FFN_HIDDEN)), _const_spec((D_MODEL, FFN_HIDDEN)),
            _const_spec((FFN_HIDDEN, D_MODEL))],
        out_specs=row,
        out_shape=jax.ShapeDtypeStruct((batch * seq, D_MODEL), F32),
        scratch_shapes=[pltpu.VMEM((tm, D_MODEL), F32)],
        compiler_params=_params("parallel", "parallel"),
        name="out_ffn",
    )(h, *ys, w_out, gain, w1, w3, w2)


MLA_BLOCK = 256
MLA_SWEEP = 2


def _mla_body(x_ref, qn_ref, kvn_ref, wuq_ref, wuk_ref, wuv_ref, gq_ref, gk_ref, cos_ref, sin_ref,
              o_ref, k_scr, vt_scr, *, seq):
    blk = MLA_BLOCK
    n_pairs = N_HEADS // 2
    qi = pl.program_id(1)
    lane = lax.broadcasted_iota(jnp.int32, (blk, LANES), 1)

    def rope(x, c, s):
        rot = jnp.where(lane < MLA_NOPE + MLA_ROPE // 2, pltpu.roll(x, LANES - MLA_ROPE // 2, 1),
                        pltpu.roll(x, MLA_ROPE // 2, 1))
        return x * c + rot * s

    def norm_head(x, g):
        ssq = jnp.sum(x * x, axis=-1, keepdims=True)
        return x * lax.rsqrt(ssq * (1.0 / MLA_DQK) + EPS) * g

    @pl.when(qi == 0)
    def _prepare_keys_values():
        def tile(i, carry):
            r0 = pl.multiple_of(i * blk, blk)
            rows = pl.ds(r0, blk)
            ckv = x_ref[rows, 256:384]
            k_rope = x_ref[rows, 384:512]
            kvn = ckv * lax.rsqrt(jnp.mean(ckv * ckv, axis=-1, keepdims=True) + EPS) * kvn_ref[...]
            kvn = kvn.astype(BF16)
            k_nope = _dot(kvn, wuk_ref[...])
            v = _dot(kvn, wuv_ref[...])
            k_rope = pltpu.roll(k_rope, MLA_NOPE, 1)
            c = cos_ref[rows, :]
            s = sin_ref[rows, :]
            for h in range(N_HEADS):
                k = rope(norm_head(k_nope[:, h * LANES:(h + 1) * LANES] + k_rope, gk_ref[...]), c, s)
                k_scr[h, rows, :] = k.astype(BF16)
            for pr in range(n_pairs):
                vt_scr[pr, :, rows] = v[:, pr * LANES:(pr + 1) * LANES].T.astype(BF16)
            return carry
        lax.fori_loop(0, seq // blk, tile, 0)

    r0 = pl.multiple_of(qi * blk, blk)
    rows = pl.ds(r0, blk)
    cq = x_ref[rows, 0:256]
    qn = cq * lax.rsqrt(jnp.mean(cq * cq, axis=-1, keepdims=True) + EPS) * qn_ref[...]
    q_all = _dot(qn.astype(BF16), wuq_ref[...])
    c = cos_ref[rows, :]
    s = sin_ref[rows, :]
    key_pos = lax.broadcasted_iota(jnp.int32, (blk, blk), 0)
    query_pos = lax.broadcasted_iota(jnp.int32, (blk, blk), 1)
    first_of_pair = lax.broadcasted_iota(jnp.int32, (LANES, blk), 0) < HEAD_DIM
    heads = range(N_HEADS)
    qs = []
    for h in heads:
        q = rope(norm_head(q_all[:, h * LANES:(h + 1) * LANES], gq_ref[...]), c, s)
        qs.append((q * (MLA_DQK ** -0.5)).astype(BF16))

    def kv_step(first_key, width, carry, diagonal=False):
        m_old, l_old, acc_old = carry
        keys = pl.ds(pl.multiple_of(first_key, blk), width)
        logits = [_dot_nt(k_scr[h, keys, :], qs[h]) for h in heads]
        if diagonal:
            logits = [jnp.where(key_pos <= query_pos, x, NEG_INF) for x in logits]
        m_new = [jnp.maximum(m_old[h], jnp.max(logits[h], axis=0, keepdims=True)) for h in heads]
        alpha = [jnp.exp(m_old[h] - m_new[h]) for h in heads]
        p = [jnp.exp(logits[h] - m_new[h]) for h in heads]
        l_new = [alpha[h] * l_old[h] + jnp.sum(p[h], axis=0, keepdims=True) for h in heads]
        pv = [_dot(vt_scr[h // 2, :, keys], p[h].astype(BF16)) for h in heads]
        acc_new = [jnp.where(first_of_pair, alpha[2 * pr] * acc_old[pr] + pv[2 * pr],
                             alpha[2 * pr + 1] * acc_old[pr] + pv[2 * pr + 1]) for pr in range(n_pairs)]
        return tuple(m_new), tuple(l_new), tuple(acc_new)

    wide = MLA_SWEEP * blk
    carry = (tuple(jnp.full((1, blk), NEG_INF, F32) for _ in heads), tuple(jnp.zeros((1, blk), F32) for _ in heads),
             tuple(jnp.zeros((LANES, blk), F32) for _ in range(n_pairs)))
    n_wide = qi // MLA_SWEEP
    carry = lax.fori_loop(0, n_wide, lambda j, cr: kv_step(j * wide, wide, cr), carry)
    carry = lax.fori_loop(n_wide * MLA_SWEEP, qi, lambda j, cr: kv_step(j * blk, blk, cr), carry)
    _, l, acc = kv_step(qi * blk, blk, carry, diagonal=True)
    for pr in range(n_pairs):
        out_t = acc[pr] / jnp.where(first_of_pair, l[2 * pr], l[2 * pr + 1])
        o_ref[rows, pr * LANES:(pr + 1) * LANES] = out_t.T


def _mla(slab, qn, kvn, wuq, wuk, wuv, gq, gk, cos, sin, *, batch, seq):
    nq = seq // MLA_BLOCK
    return pl.pallas_call(
        functools.partial(_mla_body, seq=seq),
        grid=(batch, nq),
        in_specs=[pl.BlockSpec((seq, MLA_SLAB), lambda b, i: (0, b)),
                  _const_spec((1, 256)), _const_spec((1, MLA_KV_RANK)),
                  _const_spec((256, N_HEADS * LANES)), _const_spec((MLA_KV_RANK, N_HEADS * LANES)),
                  _const_spec((MLA_KV_RANK, GROUP_W)),
                  _const_spec((1, LANES)), _const_spec((1, LANES)),
                  _const_spec((seq, LANES)), _const_spec((seq, LANES))],
        out_specs=pl.BlockSpec((seq, GROUP_W), lambda b, i: (0, b)),
        out_shape=jax.ShapeDtypeStruct((seq, batch * GROUP_W), F32),
        scratch_shapes=[pltpu.VMEM((N_HEADS, seq, LANES), BF16), pltpu.VMEM((N_HEADS // 2, LANES, seq), BF16)],
        compiler_params=_params("parallel", "arbitrary"),
        name="mla",
    )(slab, qn, kvn, wuq, wuk, wuv, gq, gk, cos, sin)


def _s5_body(u_ref, lre_ref, lim_ref, ldt_ref, bre_ref, bim_ref, cre_ref, cim_ref, d_ref, wglu_ref, o_ref,
             a_scr, bbar_scr, h_scr, x_scr, *, batch, steps):
    n_tiles = S5_WIDTH // LANES
    fan = S5_WIDTH // GROUP_W

    @pl.when(pl.program_id(0) == 0)
    def _discretise():
        lr = lre_ref[...]
        li = lim_ref[...]
        dt = jnp.exp(ldt_ref[...])
        mag = jnp.exp(lr * dt)
        ar = mag * jnp.cos(li * dt)
        ai = mag * jnp.sin(li * dt)
        den = lr * lr + li * li
        nr = ar - 1.0
        zr = (nr * lr + ai * li) / den
        zi = (ai * lr - nr * li) / den
        for j in range(n_tiles):
            lanes = slice(j * LANES, (j + 1) * LANES)
            a_scr[0, j] = jnp.broadcast_to(ar[:, lanes], (batch, LANES))
            a_scr[1, j] = jnp.broadcast_to(ai[:, lanes], (batch, LANES))
        bre = bre_ref[...]
        bim = bim_ref[...]
        bbar_scr[0] = (zr * bre - zi * bim).astype(BF16)
        bbar_scr[1] = (zr * bim + zi * bre).astype(BF16)
        h_scr[...] = jnp.zeros_like(h_scr)

    for b in range(batch):
        ub = u_ref[:, b * GROUP_W:(b + 1) * GROUP_W].astype(BF16)
        rows_b = pl.ds(b, steps, stride=batch)
        for part in range(2):
            for t in range(GROUP_W // LANES):
                cols = slice(t * fan * LANES, (t + 1) * fan * LANES)
                x = _dot(ub[:, t * LANES:(t + 1) * LANES], bbar_scr[part, t * LANES:(t + 1) * LANES, cols])
                for j in range(fan):
                    x_scr[part, t * fan + j, rows_b, :] = x[:, j * LANES:(j + 1) * LANES]
    ar = a_scr[0]
    ai = a_scr[1]

    def step(t, carry):
        hr, hi = carry
        rows = pl.ds(pl.multiple_of(t * batch, batch), batch)
        nhr = ar * hr - ai * hi + x_scr[0, :, rows, :]
        nhi = ar * hi + ai * hr + x_scr[1, :, rows, :]
        x_scr[0, :, rows, :] = nhr
        x_scr[1, :, rows, :] = nhi
        return nhr, nhi

    hr, hi = lax.fori_loop(0, steps, step, (h_scr[0], h_scr[1]), unroll=8)
    h_scr[0] = hr
    h_scr[1] = hi
    ys = []
    for b in range(batch):
        rows_b = pl.ds(b, steps, stride=batch)
        y_tiles = []
        for t in range(GROUP_W // LANES):
            rows_c = slice(t * fan * LANES, (t + 1) * fan * LANES)
            out_cols = slice(t * LANES, (t + 1) * LANES)
            hr, hi = (jnp.concatenate([x_scr[part, t * fan + j, rows_b, :] for j in range(fan)], axis=1).astype(BF16)
                      for part in range(2))
            y_tiles.append(_dot(hr, cre_ref[rows_c, out_cols]) - _dot(hi, cim_ref[rows_c, out_cols]))
        u = u_ref[:, b * GROUP_W:(b + 1) * GROUP_W]
        ys.append((jnp.concatenate(y_tiles, axis=1) + d_ref[...] * u).astype(BF16))
    zs = [_dot(y, wglu_ref[...]) for y in ys]
    for b, z in enumerate(zs):
        o_ref[:, b * GROUP_W:(b + 1) * GROUP_W] = z[:, :GROUP_W] * jax.nn.sigmoid(z[:, GROUP_W:])


def _s5(u, lre, lim, ldt, bre, bim, cre, cim, d, wglu, *, batch, seq, steps=256):
    n_tiles = S5_WIDTH // LANES
    state = lambda rows: pltpu.VMEM((2, n_tiles, rows, LANES), F32)
    return pl.pallas_call(
        functools.partial(_s5_body, batch=batch, steps=steps),
        grid=(seq // steps,),
        in_specs=[pl.BlockSpec((steps, batch * S5_SLAB), lambda i: (i, 0)),
                  _const_spec((1, S5_WIDTH)), _const_spec((1, S5_WIDTH)), _const_spec((1, S5_WIDTH)),
                  _const_spec((GROUP_W, S5_WIDTH)), _const_spec((GROUP_W, S5_WIDTH)),
                  _const_spec((S5_WIDTH, GROUP_W)), _const_spec((S5_WIDTH, GROUP_W)),
                  _const_spec((1, GROUP_W)), _const_spec((GROUP_W, 2 * GROUP_W))],
        out_specs=pl.BlockSpec((steps, batch * GROUP_W), lambda i: (i, 0)),
        out_shape=jax.ShapeDtypeStruct((seq, batch * GROUP_W), F32),
        scratch_shapes=[state(batch), pltpu.VMEM((2, GROUP_W, S5_WIDTH), BF16), state(batch),
                        state(steps * batch)],
        compiler_params=_params("arbitrary"),
        name="s5",
    )(u, lre, lim, ldt, bre, bim, cre, cim, d, wglu)


DIL_TILE = 256
DIL_GROUP = 2


def _pair_norm(x, gain2, lo_half):
    sq = x * x
    tot = jnp.sum(sq, axis=-1, keepdims=True)
    lo = jnp.sum(jnp.where(lo_half, sq, 0.0), axis=-1, keepdims=True)
    ms = jnp.where(lo_half, lo, tot - lo) * (1.0 / HEAD_DIM)
    return x * lax.rsqrt(ms + EPS) * gain2


def _dil_body(x_ref, gq_ref, gk_ref, band_ref, o_ref, bias_ref, q_scr, k_scr, v_scr, oa, ma, la, *, seq):
    span = DIL_SPAN
    n_blocks = seq // span
    n_tiles = GROUP_W // LANES
    lo_tile = lax.broadcasted_iota(jnp.int32, (DIL_TILE, LANES), 1) < HEAD_DIM
    lo_blk = lax.broadcasted_iota(jnp.int32, (span, LANES), 1) < HEAD_DIM

    for bh in range(len(DIL_PAIRS) * N_HEADS):
        profile = jnp.broadcast_to(band_ref[bh:bh + 1, :], (span, 2 * span))
        bias_ref[bh] = pltpu.roll(profile, 0, 1, stride=1, stride_axis=0)
    in_current = lax.broadcasted_iota(jnp.int32, (span, 2 * span), 1) >= span

    def norm_tile(i, carry):
        rows = pl.ds(pl.multiple_of(i * DIL_TILE, DIL_TILE), DIL_TILE)
        for j in range(n_tiles):
            cols = lambda part: slice(part * GROUP_W + j * LANES, part * GROUP_W + (j + 1) * LANES)
            q = _pair_norm(x_ref[rows, cols(0)], gq_ref[...], lo_tile)
            q_scr[j, rows, :] = q * (HEAD_DIM ** -0.5)
            k_scr[j, rows, :] = _pair_norm(x_ref[rows, cols(1)], gk_ref[...], lo_tile)
            v_scr[j, rows, :] = x_ref[rows, cols(2)]
        return carry
    lax.fori_loop(0, seq // DIL_TILE, norm_tile, 0)

    for bi, (window, dil) in enumerate(DIL_PAIRS):
        sub_len = seq // dil
        nb = sub_len // span

        def block_group(i, carry, bi=bi, nb=nb, dil=dil):
            todo = []
            for u in range(DIL_GROUP):
                t = i * DIL_GROUP + u
                r = lax.shift_right_logical(t, int(math.log2(nb)))
                c = t & (nb - 1)
                first = r + c * (dil * span)
                stride = dil if dil > 1 else None
                rows = pl.ds(first, span, stride=stride)
                prev = pl.ds(jnp.maximum(first - dil * span, r), span, stride=stride)
                keep = jnp.logical_or(in_current, c != 0) if nb > 1 else None
                for j in range(n_tiles):
                    q2 = q_scr[j, rows, :]
                    if nb > 1:
                        k_cat = jnp.concatenate([k_scr[j, prev, :], k_scr[j, rows, :]], axis=0).astype(BF16)
                        v_cat = jnp.concatenate([v_scr[j, prev, :], v_scr[j, rows, :]], axis=0).astype(BF16)
                    else:
                        k_cat = k_scr[j, rows, :].astype(BF16)
                        v_cat = v_scr[j, rows, :].astype(BF16)
                    todo.append((rows, j, q2, k_cat, v_cat, keep))
            logits = []
            for rows, j, q2, k_cat, v_cat, keep in todo:
                for a in range(2):
                    qa = jnp.where(lo_blk if a == 0 else jnp.logical_not(lo_blk), q2, 0.0).astype(BF16)
                    bias = bias_ref[bi * N_HEADS + 2 * j + a]
                    lg = _dot_nt(qa, k_cat) + (bias if nb > 1 else bias[:, span:])
                    logits.append(jnp.where(keep, lg, NEG_INF) if nb > 1 else lg)
            m = [jnp.max(lg, axis=-1, keepdims=True) for lg in logits]
            p = [jnp.exp(lg - mx) for lg, mx in zip(logits, m)]
            l = [jnp.sum(px, axis=-1, keepdims=True) for px in p]
            o = [_dot(p[2 * n + a].astype(BF16), todo[n][4]) for n in range(len(todo)) for a in range(2)]
            for n, (rows, j, *_) in enumerate(todo):
                o_in = jnp.where(lo_blk, o[2 * n], o[2 * n + 1])
                m_in = jnp.where(lo_blk, m[2 * n], m[2 * n + 1])
                l_in = jnp.where(lo_blk, l[2 * n], l[2 * n + 1])
                if bi > 0:
                    m_old = ma[j, rows, :]
                    m_new = jnp.maximum(m_old, m_in)
                    w_old, w_in = jnp.exp(m_old - m_new), jnp.exp(m_in - m_new)
                    o_in = w_old * oa[j, rows, :] + w_in * o_in
                    l_in = w_old * la[j, rows, :] + w_in * l_in
                    m_in = m_new
                oa[j, rows, :] = o_in
                ma[j, rows, :] = m_in
                la[j, rows, :] = l_in
            return carry
        lax.fori_loop(0, n_blocks // DIL_GROUP, block_group, 0)

    def finish(i, carry):
        rows = pl.ds(pl.multiple_of(i * DIL_TILE, DIL_TILE), DIL_TILE)
        for j in range(n_tiles):
            o_ref[rows, j * LANES:(j + 1) * LANES] = oa[j, rows, :] / la[j, rows, :]
        return carry
    lax.fori_loop(0, seq // DIL_TILE, finish, 0)


def _dil(slab, gq2, gk2, band, *, batch, seq):
    big = lambda: pltpu.VMEM((GROUP_W // LANES, seq, LANES), F32)
    n_bias = len(DIL_PAIRS) * N_HEADS
    return pl.pallas_call(
        functools.partial(_dil_body, seq=seq),
        grid=(batch,),
        in_specs=[pl.BlockSpec((seq, DIL_SLAB), lambda b: (0, b)),
                  _const_spec((1, LANES)), _const_spec((1, LANES)),
                  _const_spec(band.shape)],
        out_specs=pl.BlockSpec((seq, GROUP_W), lambda b: (0, b)),
        out_shape=jax.ShapeDtypeStruct((seq, batch * GROUP_W), F32),
        scratch_shapes=[pltpu.VMEM((n_bias, DIL_SPAN, 2 * DIL_SPAN), F32)] + [big() for _ in range(6)],
        compiler_params=_params("parallel"),
        name="dilated",
    )(slab, gq2, gk2, band)


def _t5_bucket(dist):
    exact = T5_BUCKETS // 2
    df = jnp.maximum(dist, 1).astype(F32)
    large = exact + (jnp.log(df / exact) / math.log(T5_MAX_DIST / exact) * (T5_BUCKETS - exact)).astype(jnp.int32)
    large = jnp.minimum(large, T5_BUCKETS - 1)
    return jnp.where(dist < exact, dist, large)


def _dil_band(table):
    span = DIL_SPAN
    delta = span - jnp.arange(2 * span, dtype=jnp.int32)
    rows = []
    for _, dil in DIL_PAIRS:
        bucket = _t5_bucket(jnp.clip(delta, 0, span) * dil)
        onehot = (bucket[:, None] == jnp.arange(T5_BUCKETS, dtype=jnp.int32)[None, :]).astype(F32)
        vals = jnp.dot(onehot, table.astype(F32), precision=HIGHEST)
        rows.append(jnp.where((delta >= 0)[:, None], vals, NEG_INF).T)
    return jnp.concatenate(rows, axis=0)


DN_TILE = 256
DN_GROUP = 4
PREP_STORE_DELAY = 12


def _softplus(x):
    return jnp.maximum(x, 0.0) + jnp.log1p(jnp.exp(-jnp.abs(x)))


def _pair_l2(x, lo_half):
    sq = x * x
    tot = jnp.sum(sq, axis=-1, keepdims=True)
    lo = jnp.sum(jnp.where(lo_half, sq, 0.0), axis=-1, keepdims=True)
    return x * lax.rsqrt(jnp.where(lo_half, lo, tot - lo) + EPS)


def _dn_body(x_ref, cw_ref, alog_ref, dtb_ref, on_ref, o_ref, q_scr, k_scr, v_scr, g_scr, b_scr, w_scr, a_scr,
             s_scr, *, seq):
    c = DN_CHUNK
    w = GROUP_W
    lo_tile = lax.broadcasted_iota(jnp.int32, (DN_TILE, LANES), 1) < HEAD_DIM

    def prep_steps(i):
        if isinstance(i, int):
            first, halo_first = i * DN_TILE, max(i * DN_TILE - 8, 0)
        else:
            first = pl.multiple_of(i * DN_TILE, DN_TILE)
            halo_first = pl.multiple_of(jnp.maximum(first - 8, 0), 8)
        rows = pl.ds(first, DN_TILE)
        cur = x_ref[rows, 0:3 * w]
        halo = jnp.where(i > 0, x_ref[pl.ds(halo_first, 8), 0:3 * w], 0.0)
        ext = jnp.concatenate([halo, cur], axis=0)
        acc = cw_ref[DN_CONV - 1:DN_CONV, :] * cur
        for j in range(DN_CONV - 1):
            acc = acc + cw_ref[j:j + 1, :] * pltpu.roll(ext, DN_CONV - 1 - j, 0)[8:, :]
        y = acc * jax.nn.sigmoid(acc)
        yield
        tiles = range(w // LANES)
        cols = lambda part, j: slice(part * w + j * LANES, part * w + (j + 1) * LANES)
        q = [_pair_l2(y[:, cols(0, j)], lo_tile) * (HEAD_DIM ** -0.5) for j in tiles]
        k = [_pair_l2(y[:, cols(1, j)], lo_tile) for j in tiles]
        yield
        ab = x_ref[rows, 3 * w:3 * w + LANES]
        g = -jnp.exp(alog_ref[...]) * _softplus(ab + dtb_ref[...])
        gc = _dot_split(chunk_tril, g, 3, split_rhs=True)
        yield
        gc = per_head_lanes(gc, 0)
        beta = per_head_lanes(jax.nn.sigmoid(ab), N_HEADS)
        for _ in range(PREP_STORE_DELAY):
            yield
        for j in tiles:
            q_scr[rows, j * LANES:(j + 1) * LANES] = q[j]
            k_scr[rows, j * LANES:(j + 1) * LANES] = k[j]
            v_scr[rows, j * LANES:(j + 1) * LANES] = y[:, cols(2, j)]
        g_scr[rows, :] = gc
        b_scr[rows, :] = beta

    ri = lax.broadcasted_iota(jnp.int32, (c, w), 0)
    ci = lax.broadcasted_iota(jnp.int32, (c, w), 1)
    cj = jnp.bitwise_and(ci, HEAD_DIM - 1)
    causal = ri >= cj
    strict = ri > cj
    eye4 = (ri == cj).astype(F32)
    same_sub = lax.shift_right_logical(ri, 4) == lax.shift_right_logical(cj, 4)
    bi_r = lax.broadcasted_iota(jnp.int32, (w, w), 0)
    bi_c = lax.broadcasted_iota(jnp.int32, (w, w), 1)
    same_head = lax.shift_right_logical(bi_r, 6) == lax.shift_right_logical(bi_c, 6)
    block_mask = same_head.astype(F32)
    chunk_tril = (same_head & (jnp.bitwise_and(bi_r, c - 1) >= jnp.bitwise_and(bi_c, c - 1))).astype(F32)
    head_of_lane = lax.shift_right_logical(lax.broadcasted_iota(jnp.int32, (DN_TILE, w), 1), 6)

    def per_head_lanes(x, first_lane):
        out = None
        for h in range(N_HEADS):
            col = jnp.broadcast_to(x[:, first_lane + h:first_lane + h + 1], (DN_TILE, w))
            out = col if out is None else jnp.where(head_of_lane == h, col, out)
        return out

    def per_head(a, b):
        bd = jnp.where(same_head, jnp.concatenate([b.astype(BF16)] * N_HEADS, axis=0), 0.0)
        return _dot(a.astype(BF16), bd)

    def group_rows(i):
        base = pl.multiple_of(i * (DN_GROUP * c), DN_GROUP * c)
        return [pl.ds(base + u * c, c) for u in range(DN_GROUP)]

    def solve_steps(i):
        rows = group_rows(i)
        each = lambda f, *lists: [f(*args) for args in zip(*lists)]
        q, k, v, gc, beta = ([ref[r, :] for r in rows] for ref in (q_scr, k_scr, v_scr, g_scr, b_scr))
        g_row = each(lambda g: jnp.sum(g * eye4, axis=0, keepdims=True), gc)
        decay = each(lambda g, gr: jnp.exp(jnp.where(causal, g - gr, NEG_INF)), gc, g_row)
        kb = each(jnp.multiply, k, beta)
        k_bd = each(lambda x: jnp.where(same_head, jnp.concatenate([x.astype(BF16)] * N_HEADS, axis=0), 0.0), k)
        lmat = each(lambda a, b, d: jnp.where(strict, _dot_nt(a.astype(BF16), b) * d, 0.0), kb, k_bd, decay)
        yield
        a_qk = each(lambda a, b, d: jnp.where(causal, _dot_nt(a.astype(BF16), b) * d, 0.0), q, k_bd, decay)
        yield
        p = each(lambda l_: jnp.where(same_sub, -l_, 0.0), lmat)
        t_diag = each(lambda x: eye4 + x, p)
        for _ in range(3):
            p = each(per_head, p, p)
            yield
            t_diag = each(lambda t, x: t + per_head(t, x), t_diag, p)
            yield
        nil = each(lambda t, l_: per_head(t, jnp.where(same_sub, 0.0, l_)), t_diag, lmat)
        yield
        nil2 = each(per_head, nil, nil)
        yield
        nil3 = each(per_head, nil, nil2)
        yield
        t_inv = each(lambda n1, n2, n3, t: per_head(eye4 - n1 + n2 - n3, t), nil, nil2, nil3, t_diag)
        yield
        eg = each(jnp.exp, gc)
        w_c = each(lambda t, a, e: per_head(t, a * e), t_inv, kb, eg)
        yield
        u_c = each(lambda t, a, b: per_head(t, a * b), t_inv, v, beta)
        yield
        q_dec = each(jnp.multiply, q, eg)
        k_dec = each(lambda x, g: x * jnp.exp(g[c - 1:c, :] - g), k, gc)
        for ref, vals in zip((w_scr, v_scr, a_scr, q_scr, k_scr), (w_c, u_c, a_qk, q_dec, k_dec)):
            for r, val in zip(rows, vals):
                ref[r, :] = val

    def state_steps(i):
        for rows in group_rows(i):
            state = s_scr[...]
            state_b = state.astype(BF16)
            w_s = _dot(w_scr[rows, :].astype(BF16), state_b)
            q_s = _dot(q_scr[rows, :].astype(BF16), state_b)
            yield
            v_new = v_scr[rows, :] - w_s
            o_intra = per_head(a_scr[rows, :], v_new)
            upd = lax.dot_general(k_scr[rows, :].astype(BF16), v_new.astype(BF16), (((0,), (0,)), ((), ())),
                                  preferred_element_type=F32)
            yield
            o_ref[rows, :] = q_s + o_intra
            s_scr[...] = state * jnp.exp(g_scr[rows, :][c - 1:c, :]) + upd * block_mask
            yield

    def weave(*steps):
        live = list(steps)
        while live:
            for gen in list(live):
                if next(gen, "done") == "done":
                    live.remove(gen)

    n_groups = seq // (DN_GROUP * c)
    s_scr[...] = jnp.zeros_like(s_scr)
    weave(prep_steps(0))
    weave(solve_steps(0), prep_steps(1))

    def group(i, carry):
        weave(solve_steps(i), state_steps(i - 1), prep_steps(i + 1))
        return carry
    lax.fori_loop(1, n_groups - 1, group, 0)
    weave(solve_steps(n_groups - 1), state_steps(n_groups - 2))
    weave(state_steps(n_groups - 1))

    def finish_tile(i, carry):
        rows = pl.ds(pl.multiple_of(i * DN_TILE, DN_TILE), DN_TILE)
        o = o_ref[rows, :]
        ms = _dot_split(o * o, block_mask, 2) * (1.0 / HEAD_DIM)
        gate = x_ref[rows, 3 * w + LANES:4 * w + LANES]
        o_ref[rows, :] = o * lax.rsqrt(ms + EPS) * on_ref[...] * (gate * jax.nn.sigmoid(gate))
        return carry
    lax.fori_loop(0, seq // DN_TILE, finish_tile, 0)


def _dn(slab, conv_w, a_log, dt_bias, o_norm, *, batch, seq):
    wide = lambda: pltpu.VMEM((seq, GROUP_W), F32)
    return pl.pallas_call(
        functools.partial(_dn_body, seq=seq),
        grid=(batch,),
        in_specs=[pl.BlockSpec((seq, DN_SLAB), lambda b: (0, b)),
                  _const_spec((DN_CONV, 3 * GROUP_W)), _const_spec((1, LANES)), _const_spec((1, LANES)),
                  _const_spec((1, GROUP_W))],
        out_specs=pl.BlockSpec((seq, GROUP_W), lambda b: (0, b)),
        out_shape=jax.ShapeDtypeStruct((seq, batch * GROUP_W), F32),
        scratch_shapes=[wide() for _ in range(7)] + [pltpu.VMEM((GROUP_W, GROUP_W), F32)],
        compiler_params=_params("parallel"),
        name="deltanet",
    )(slab, conv_w, a_log, dt_bias, o_norm)


def _row(v, width=None):
    v = v.astype(F32).reshape(1, -1)
    if width is not None and v.shape[1] < width:
        v = jnp.pad(v, ((0, 0), (0, width - v.shape[1])))
    return v


def _prep_w_in(w):
    z = lambda n: jnp.zeros((D_MODEL, n), w.dtype)
    cols = [w[:, 0:416], z(96), w[:, 416:672], w[:, 672:1440], w[:, 1440:2208], w[:, 2208:2216], z(120),
            w[:, 2216:2472]]
    return jnp.concatenate(cols, axis=1).astype(BF16)


def _pad_heads(w, per_head, lo, hi):
    k = w.shape[0]
    w = w.reshape(k, N_HEADS, per_head)[:, :, lo:hi]
    w = jnp.pad(w, ((0, 0), (0, 0), (0, LANES - (hi - lo))))
    return w.reshape(k, N_HEADS * LANES).astype(BF16)


def _rope_tables(seq):
    half = MLA_ROPE // 2
    pos = jnp.arange(seq, dtype=F32)
    freqs = ROPE_THETA ** (-jnp.arange(half, dtype=F32) / half)
    ang = pos[:, None] * freqs[None, :]
    cos, sin = jnp.cos(ang), jnp.sin(ang)
    ones = jnp.ones((seq, MLA_NOPE), F32)
    zeros = jnp.zeros((seq, MLA_NOPE), F32)
    tail1 = jnp.ones((seq, LANES - MLA_DQK), F32)
    tail0 = jnp.zeros((seq, LANES - MLA_DQK), F32)
    return (jnp.concatenate([ones, cos, cos, tail1], axis=1),
            jnp.concatenate([zeros, -sin, sin, tail0], axis=1))


def _block_diag(blocks):
    g, r, c = blocks.shape
    eye = jnp.eye(g, dtype=blocks.dtype)
    return (blocks[:, :, None, :] * eye[:, None, :, None]).reshape(g * r, g * c)


def _mla_layer(slab, p, l, *, batch, seq):
    w_ukv = p["mla_w_ukv"][l]
    cos, sin = _rope_tables(seq)
    return _mla(slab, _row(p["mla_q_norm"][l]), _row(p["mla_kv_norm"][l]),
                _pad_heads(p["mla_w_uq"][l], MLA_DQK, 0, MLA_DQK),
                _pad_heads(w_ukv, MLA_NOPE + HEAD_DIM, 0, MLA_NOPE),
                w_ukv.reshape(MLA_KV_RANK, N_HEADS, MLA_NOPE + HEAD_DIM)[:, :, MLA_NOPE:]
                .reshape(MLA_KV_RANK, GROUP_W).astype(BF16),
                _row(p["mla_qk_q"][l], LANES), _row(p["mla_qk_k"][l], LANES), cos, sin, batch=batch, seq=seq)


def _dil_layer(slab, p, l, *, batch, seq):
    pair = lambda g: jnp.tile(g.astype(F32).reshape(1, HEAD_DIM), (1, LANES // HEAD_DIM))
    return _dil(slab, pair(p["dil_q_norm"][l]), pair(p["dil_k_norm"][l]), _dil_band(p["t5_bias"]),
                batch=batch, seq=seq)


def _dn_layer(slab, p, l, *, batch, seq):
    return _dn(slab, p["dn_conv"][l].astype(F32), _row(p["dn_a_log"][l], LANES), _row(p["dn_dt_bias"][l], LANES),
               jnp.tile(p["dn_o_norm"][l].astype(F32).reshape(1, HEAD_DIM), (1, N_HEADS)), batch=batch, seq=seq)


def _s5_layer(u, p, l, *, batch, seq):
    state_row = lambda v: v.astype(F32).reshape(1, S5_WIDTH)
    ldt = jnp.broadcast_to(p["s5_log_dt"][l][:, None], (S5_GROUPS, S5_STATE))
    bre = _block_diag(jnp.swapaxes(p["s5_b_re"][l], 1, 2).astype(F32))
    bim = _block_diag(jnp.swapaxes(p["s5_b_im"][l], 1, 2).astype(F32))
    cre = _block_diag(jnp.swapaxes(p["s5_c_re"][l], 1, 2)).astype(BF16)
    cim = _block_diag(jnp.swapaxes(p["s5_c_im"][l], 1, 2)).astype(BF16)
    return _s5(u, state_row(p["s5_lambda_re"][l]), state_row(p["s5_lambda_im"][l]), state_row(ldt),
               bre, bim, cre, cim, _row(p["s5_d"][l]), p["s5_w_glu"][l].astype(BF16), batch=batch, seq=seq,
               steps=min(256, seq))


def kernel(x, attn_norm, w_in, w_out, mla_q_norm, mla_kv_norm, mla_w_uq, mla_w_ukv, mla_qk_q, mla_qk_k,
           s5_lambda_re, s5_lambda_im, s5_log_dt, s5_b_re, s5_b_im, s5_c_re, s5_c_im, s5_d, s5_w_glu,
           dil_q_norm, dil_k_norm, t5_bias, dn_conv, dn_a_log, dn_dt_bias, dn_o_norm,
           ffn_norm, ffn_w1, ffn_w3, ffn_w2):
    p = dict(mla_q_norm=mla_q_norm, mla_kv_norm=mla_kv_norm, mla_w_uq=mla_w_uq, mla_w_ukv=mla_w_ukv,
             mla_qk_q=mla_qk_q, mla_qk_k=mla_qk_k, s5_lambda_re=s5_lambda_re, s5_lambda_im=s5_lambda_im,
             s5_log_dt=s5_log_dt, s5_b_re=s5_b_re, s5_b_im=s5_b_im, s5_c_re=s5_c_re, s5_c_im=s5_c_im, s5_d=s5_d,
             s5_w_glu=s5_w_glu, dil_q_norm=dil_q_norm, dil_k_norm=dil_k_norm, t5_bias=t5_bias, dn_conv=dn_conv,
             dn_a_log=dn_a_log, dn_dt_bias=dn_dt_bias, dn_o_norm=dn_o_norm)
    batch, seq, _ = x.shape
    h = x.reshape(batch * seq, D_MODEL)
    for l in range(attn_norm.shape[0]):
        mla_in, s5_in, dil_in, dn_in = _proj(h, _row(attn_norm[l]), _prep_w_in(w_in[l]), batch=batch, seq=seq)
        ys = [_mla_layer(mla_in, p, l, batch=batch, seq=seq),
              _s5_layer(s5_in, p, l, batch=batch, seq=seq),
              _dil_layer(dil_in, p, l, batch=batch, seq=seq),
              _dn_layer(dn_in, p, l, batch=batch, seq=seq)]
        h = _out_ffn(h, ys, w_out[l].astype(BF16), _row(ffn_norm[l]),
                     ffn_w1[l].astype(BF16), ffn_w3[l].astype(BF16), ffn_w2[l].astype(BF16), batch=batch, seq=seq)
    return h.reshape(batch, seq, D_MODEL)
```

```python
import functools
import math

import jax
import jax.numpy as jnp
import numpy as np
from jax import lax
from jax.experimental import pallas as pl
from jax.experimental.pallas import tpu as pltpu

F32 = jnp.float32
BF16 = jnp.bfloat16
HIGHEST = lax.Precision.HIGHEST

D_MODEL = 1024
GROUP_W = 256
HEAD_DIM = 64
N_HEADS = 4
EPS = 1e-6
NEG_INF = -1e30

MLA_NOPE = 64
MLA_ROPE = 32
MLA_DQK = MLA_NOPE + MLA_ROPE
MLA_KV_RANK = 128
ROPE_THETA = 10000.0

S5_GROUP_CH = 16
S5_GROUPS = 16
S5_STATE = 64
S5_WIDTH = S5_GROUPS * S5_STATE

DIL_PAIRS = ((128, 1), (512, 4), (2048, 16))
DIL_SPAN = 128
T5_BUCKETS = 32
T5_MAX_DIST = 2048

DN_CONV = 4
DN_CHUNK = 64

FFN_HIDDEN = 2816
FFN_CHUNK = 2816

VMEM_LIMIT_BYTES = 56 * 1024 * 1024
LANES = 128

MLA_SLAB = 512
S5_SLAB = 256
DIL_SLAB = 768
DN_SLAB = 1152
PROJ_COLS = MLA_SLAB + S5_SLAB + DIL_SLAB + DN_SLAB


def _dot(a, b, precision=None):
    return jnp.dot(a, b, preferred_element_type=F32, precision=precision)


def _dot_nt(a, b, precision=None):
    return lax.dot_general(a, b, (((1,), (1,)), ((), ())), preferred_element_type=F32, precision=precision)


def _dot_split(a, b, terms, split_rhs=False):
    x = b if split_rhs else a
    mask = (a if split_rhs else b).astype(BF16)
    out = None
    for _ in range(terms):
        piece = x.astype(BF16)
        part = _dot(mask, piece) if split_rhs else _dot(piece, mask)
        out = part if out is None else out + part
        x = x - piece.astype(F32)
    return out


def _const_spec(shape):
    nd = len(shape)
    return pl.BlockSpec(shape, lambda *_: (0,) * nd, pipeline_mode=pl.Buffered(1))


def _params(*sem):
    return pltpu.CompilerParams(dimension_semantics=sem, vmem_limit_bytes=VMEM_LIMIT_BYTES)


def _proj_body(x_ref, g_ref, w_ref, mla_ref, s5_ref, dil_ref, dn_ref):
    x = x_ref[...]
    n = x * lax.rsqrt(jnp.mean(x * x, axis=-1, keepdims=True) + EPS) * g_ref[...]
    nb = n.astype(BF16)
    start = 0
    for ref in (mla_ref, s5_ref, dil_ref, dn_ref):
        width = ref.shape[-1]
        ref[...] = _dot(nb, w_ref[:, start:start + width])
        start += width


def _proj(h, gain, w_big, *, batch, seq, tm=512):
    nt = seq // tm
    widths = (MLA_SLAB, S5_SLAB, DIL_SLAB, DN_SLAB)
    return pl.pallas_call(
        _proj_body,
        grid=(batch, nt),
        in_specs=[pl.BlockSpec((tm, D_MODEL), lambda b, i: (b * nt + i, 0)),
                  _const_spec((1, D_MODEL)),
                  _const_spec((D_MODEL, PROJ_COLS))],
        out_specs=[pl.BlockSpec((tm, w), lambda b, i: (i, b)) for w in widths],
        out_shape=[jax.ShapeDtypeStruct((seq, batch * w), F32) for w in widths],
        compiler_params=_params("parallel", "parallel"),
        name="proj",
    )(h, gain, w_big)


def _out_ffn_body(h_ref, y0_ref, y1_ref, y2_ref, y3_ref, wo_ref, g_ref, w1_ref, w3_ref, w2_ref, o_ref, acc_ref):
    h = h_ref[...]
    for i, y_ref in enumerate((y0_ref, y1_ref, y2_ref, y3_ref)):
        h = h + _dot(y_ref[...].astype(BF16), wo_ref[i * GROUP_W:(i + 1) * GROUP_W, :])
    n = h * lax.rsqrt(jnp.mean(h * h, axis=-1, keepdims=True) + EPS) * g_ref[...]
    nb = n.astype(BF16)
    acc_ref[...] = h

    def hidden_chunk(c, carry):
        cols = pl.ds(pl.multiple_of(c * FFN_CHUNK, FFN_CHUNK), FFN_CHUNK)
        a = _dot(nb, w1_ref[:, cols])
        b = _dot(nb, w3_ref[:, cols])
        z = (a * jax.nn.sigmoid(a) * b).astype(BF16)
        acc_ref[...] += _dot(z, w2_ref[cols, :])
        return carry
    lax.fori_loop(0, FFN_HIDDEN // FFN_CHUNK, hidden_chunk, 0)
    o_ref[...] = acc_ref[...]


def _out_ffn(h, ys, w_out, gain, w1, w3, w2, *, batch, seq, tm=512):
    nt = seq // tm
    row = pl.BlockSpec((tm, D_MODEL), lambda b, i: (b * nt + i, 0))
    slab = pl.BlockSpec((tm, GROUP_W), lambda b, i: (i, b))
    return pl.pallas_call(
        _out_ffn_body,
        grid=(batch, nt),
        in_specs=[row] + [slab] * 4 + [
            _const_spec((D_MODEL, D_MODEL)), _const_spec((1, D_MODEL)),
            _const_spec((D_MODEL, FFN_HIDDEN)), _const_spec((D_MODEL, FFN_HIDDEN)),
            _const_spec((FFN_HIDDEN, D_MODEL))],
        out_specs=row,
        out_shape=jax.ShapeDtypeStruct((batch * seq, D_MODEL), F32),
        scratch_shapes=[pltpu.VMEM((tm, D_MODEL), F32)],
        compiler_params=_params("parallel", "parallel"),
        name="out_ffn",
    )(h, *ys, w_out, gain, w1, w3, w2)


MLA_BLOCK = 256
MLA_SWEEP = 2


def _mla_body(x_ref, qn_ref, kvn_ref, wuq_ref, wuk_ref, wuv_ref, gq_ref, gk_ref, cos_ref, sin_ref,
              o_ref, k_scr, vt_scr, q_scr, *, seq):
    blk = MLA_BLOCK
    n_pairs = N_HEADS // 2
    qi = pl.program_id(1)
    lane = lax.broadcasted_iota(jnp.int32, (blk, LANES), 1)

    def rope(x, c, s):
        rot = jnp.where(lane < MLA_NOPE + MLA_ROPE // 2, pltpu.roll(x, LANES - MLA_ROPE // 2, 1),
                        pltpu.roll(x, MLA_ROPE // 2, 1))
        return x * c + rot * s

    def norm_head(x, g):
        ssq = jnp.sum(x * x, axis=-1, keepdims=True)
        return x * lax.rsqrt(ssq * (1.0 / MLA_DQK) + EPS) * g

    heads = range(N_HEADS)

    def prepare_steps(i):
        rows = pl.ds(i * blk if isinstance(i, int) else pl.multiple_of(i * blk, blk), blk)
        ckv = x_ref[rows, 256:384]
        cq = x_ref[rows, 0:256]
        kvn = ckv * lax.rsqrt(jnp.mean(ckv * ckv, axis=-1, keepdims=True) + EPS) * kvn_ref[...]
        qn = cq * lax.rsqrt(jnp.mean(cq * cq, axis=-1, keepdims=True) + EPS) * qn_ref[...]
        kvn = kvn.astype(BF16)
        k_nope = _dot(kvn, wuk_ref[...])
        v = _dot(kvn, wuv_ref[...])
        q_all = _dot(qn.astype(BF16), wuq_ref[...])
        yield
        k_rope = pltpu.roll(x_ref[rows, 384:512], MLA_NOPE, 1)
        c = cos_ref[rows, :]
        s = sin_ref[rows, :]
        ks = [rope(norm_head(k_nope[:, h * LANES:(h + 1) * LANES] + k_rope, gk_ref[...]), c, s) for h in heads]
        yield
        qs_new = [rope(norm_head(q_all[:, h * LANES:(h + 1) * LANES], gq_ref[...]), c, s) * (MLA_DQK ** -0.5)
                  for h in heads]
        vts = [v[:, pr * LANES:(pr + 1) * LANES].T for pr in range(n_pairs)]
        yield
        yield
        for h in heads:
            k_scr[h, rows, :] = ks[h].astype(BF16)
            q_scr[h] = qs_new[h].astype(BF16)
        for pr in range(n_pairs):
            vt_scr[pr, :, rows] = vts[pr].astype(BF16)

    def weave(*steps):
        live = list(steps)
        while live:
            for gen in list(live):
                if next(gen, "done") == "done":
                    live.remove(gen)

    @pl.when(qi == 0)
    def _first_block():
        weave(prepare_steps(0))

    rows = pl.ds(pl.multiple_of(qi * blk, blk), blk)
    key_pos = lax.broadcasted_iota(jnp.int32, (blk, blk), 0)
    query_pos = lax.broadcasted_iota(jnp.int32, (blk, blk), 1)
    first_of_pair = lax.broadcasted_iota(jnp.int32, (LANES, blk), 0) < HEAD_DIM
    qs = [q_scr[h] for h in heads]

    def kv_steps(first_key, width, carry, result, diagonal=False):
        m_old, l_old, acc_old = carry
        keys = pl.ds(pl.multiple_of(first_key, blk), width)
        logits = [_dot_nt(k_scr[h, keys, :], qs[h]) for h in heads]
        yield
        if diagonal:
            logits = [jnp.where(key_pos <= query_pos, x, NEG_INF) for x in logits]
        m_new = [jnp.maximum(m_old[h], jnp.max(logits[h], axis=0, keepdims=True)) for h in heads]
        alpha = [jnp.exp(m_old[h] - m_new[h]) for h in heads]
        p = [jnp.exp(logits[h] - m_new[h]) for h in heads]
        l_new = [alpha[h] * l_old[h] + jnp.sum(p[h], axis=0, keepdims=True) for h in heads]
        pv = [_dot(vt_scr[h // 2, :, keys], p[h].astype(BF16)) for h in heads]
        yield
        acc_new = [jnp.where(first_of_pair, alpha[2 * pr] * acc_old[pr] + pv[2 * pr],
                             alpha[2 * pr + 1] * acc_old[pr] + pv[2 * pr + 1]) for pr in range(n_pairs)]
        result.append((tuple(m_new), tuple(l_new), tuple(acc_new)))

    def kv_step(first_key, width, carry):
        result = []
        weave(kv_steps(first_key, width, carry, result))
        return result[0]

    def last_steps(carry):
        result = []
        yield from kv_steps(qi * blk, blk, carry, result, diagonal=True)
        _, l, acc = result[0]
        for pr in range(n_pairs):
            out_t = acc[pr] / jnp.where(first_of_pair, l[2 * pr], l[2 * pr + 1])
            o_ref[rows, pr * LANES:(pr + 1) * LANES] = out_t.T

    wide = MLA_SWEEP * blk
    carry = (tuple(jnp.full((1, blk), NEG_INF, F32) for _ in heads), tuple(jnp.zeros((1, blk), F32) for _ in heads),
             tuple(jnp.zeros((LANES, blk), F32) for _ in range(n_pairs)))
    n_wide = qi // MLA_SWEEP
    carry = lax.fori_loop(0, n_wide, lambda j, cr: kv_step(j * wide, wide, cr), carry)
    carry = lax.fori_loop(n_wide * MLA_SWEEP, qi, lambda j, cr: kv_step(j * blk, blk, cr), carry)

    @pl.when(qi + 1 < seq // blk)
    def _last_and_prepare_next():
        weave(last_steps(carry), prepare_steps(qi + 1))

    @pl.when(qi + 1 == seq // blk)
    def _last():
        weave(last_steps(carry))


def _mla(slab, qn, kvn, wuq, wuk, wuv, gq, gk, cos, sin, *, batch, seq):
    nq = seq // MLA_BLOCK
    return pl.pallas_call(
        functools.partial(_mla_body, seq=seq),
        grid=(batch, nq),
        in_specs=[pl.BlockSpec((seq, MLA_SLAB), lambda b, i: (0, b)),
                  _const_spec((1, 256)), _const_spec((1, MLA_KV_RANK)),
                  _const_spec((256, N_HEADS * LANES)), _const_spec((MLA_KV_RANK, N_HEADS * LANES)),
                  _const_spec((MLA_KV_RANK, GROUP_W)),
                  _const_spec((1, LANES)), _const_spec((1, LANES)),
                  _const_spec((seq, LANES)), _const_spec((seq, LANES))],
        out_specs=pl.BlockSpec((seq, GROUP_W), lambda b, i: (0, b)),
        out_shape=jax.ShapeDtypeStruct((seq, batch * GROUP_W), F32),
        scratch_shapes=[pltpu.VMEM((N_HEADS, seq, LANES), BF16), pltpu.VMEM((N_HEADS // 2, LANES, seq), BF16),
                        pltpu.VMEM((N_HEADS, MLA_BLOCK, LANES), BF16)],
        compiler_params=_params("parallel", "arbitrary"),
        name="mla",
    )(slab, qn, kvn, wuq, wuk, wuv, gq, gk, cos, sin)


def _s5_body(u_ref, lre_ref, lim_ref, ldt_ref, bre_ref, bim_ref, cre_ref, cim_ref, d_ref, wglu_ref, o_ref,
             a_scr, bbar_scr, h_scr, x_scr, *, batch, steps):
    n_tiles = S5_WIDTH // LANES
    fan = S5_WIDTH // GROUP_W

    @pl.when(pl.program_id(0) == 0)
    def _discretise():
        lr = lre_ref[...]
        li = lim_ref[...]
        dt = jnp.exp(ldt_ref[...])
        mag = jnp.exp(lr * dt)
        ar = mag * jnp.cos(li * dt)
        ai = mag * jnp.sin(li * dt)
        den = lr * lr + li * li
        nr = ar - 1.0
        zr = (nr * lr + ai * li) / den
        zi = (ai * lr - nr * li) / den
        for j in range(n_tiles):
            lanes = slice(j * LANES, (j + 1) * LANES)
            a_scr[0, j] = jnp.broadcast_to(ar[:, lanes], (batch, LANES))
            a_scr[1, j] = jnp.broadcast_to(ai[:, lanes], (batch, LANES))
        bre = bre_ref[...]
        bim = bim_ref[...]
        bbar_scr[0] = (zr * bre - zi * bim).astype(BF16)
        bbar_scr[1] = (zr * bim + zi * bre).astype(BF16)
        h_scr[...] = jnp.zeros_like(h_scr)

    for b in range(batch):
        ub = u_ref[:, b * GROUP_W:(b + 1) * GROUP_W].astype(BF16)
        rows_b = pl.ds(b, steps, stride=batch)
        for part in range(2):
            for t in range(GROUP_W // LANES):
                cols = slice(t * fan * LANES, (t + 1) * fan * LANES)
                x = _dot(ub[:, t * LANES:(t + 1) * LANES], bbar_scr[part, t * LANES:(t + 1) * LANES, cols])
                for j in range(fan):
                    x_scr[part, t * fan + j, rows_b, :] = x[:, j * LANES:(j + 1) * LANES]
    ar = a_scr[0]
    ai = a_scr[1]

    def step(t, carry):
        hr, hi = carry
        rows = pl.ds(pl.multiple_of(t * batch, batch), batch)
        nhr = ar * hr - ai * hi + x_scr[0, :, rows, :]
        nhi = ar * hi + ai * hr + x_scr[1, :, rows, :]
        x_scr[0, :, rows, :] = nhr
        x_scr[1, :, rows, :] = nhi
        return nhr, nhi

    hr, hi = lax.fori_loop(0, steps, step, (h_scr[0], h_scr[1]), unroll=8)
    h_scr[0] = hr
    h_scr[1] = hi
    ys = []
    for b in range(batch):
        rows_b = pl.ds(b, steps, stride=batch)
        y_tiles = []
        for t in range(GROUP_W // LANES):
            rows_c = slice(t * fan * LANES, (t + 1) * fan * LANES)
            out_cols = slice(t * LANES, (t + 1) * LANES)
            hr, hi = (jnp.concatenate([x_scr[part, t * fan + j, rows_b, :] for j in range(fan)], axis=1).astype(BF16)
                      for part in range(2))
            y_tiles.append(_dot(hr, cre_ref[rows_c, out_cols]) - _dot(hi, cim_ref[rows_c, out_cols]))
        u = u_ref[:, b * GROUP_W:(b + 1) * GROUP_W]
        ys.append((jnp.concatenate(y_tiles, axis=1) + d_ref[...] * u).astype(BF16))
    zs = [_dot(y, wglu_ref[...]) for y in ys]
    for b, z in enumerate(zs):
        o_ref[:, b * GROUP_W:(b + 1) * GROUP_W] = z[:, :GROUP_W] * jax.nn.sigmoid(z[:, GROUP_W:])


def _s5(u, lre, lim, ldt, bre, bim, cre, cim, d, wglu, *, batch, seq, steps=256):
    n_tiles = S5_WIDTH // LANES
    state = lambda rows: pltpu.VMEM((2, n_tiles, rows, LANES), F32)
    return pl.pallas_call(
        functools.partial(_s5_body, batch=batch, steps=steps),
        grid=(seq // steps,),
        in_specs=[pl.BlockSpec((steps, batch * S5_SLAB), lambda i: (i, 0)),
                  _const_spec((1, S5_WIDTH)), _const_spec((1, S5_WIDTH)), _const_spec((1, S5_WIDTH)),
                  _const_spec((GROUP_W, S5_WIDTH)), _const_spec((GROUP_W, S5_WIDTH)),
                  _const_spec((S5_WIDTH, GROUP_W)), _const_spec((S5_WIDTH, GROUP_W)),
                  _const_spec((1, GROUP_W)), _const_spec((GROUP_W, 2 * GROUP_W))],
        out_specs=pl.BlockSpec((steps, batch * GROUP_W), lambda i: (i, 0)),
        out_shape=jax.ShapeDtypeStruct((seq, batch * GROUP_W), F32),
        scratch_shapes=[state(batch), pltpu.VMEM((2, GROUP_W, S5_WIDTH), BF16), state(batch),
                        state(steps * batch)],
        compiler_params=_params("arbitrary"),
        name="s5",
    )(u, lre, lim, ldt, bre, bim, cre, cim, d, wglu)


DIL_TILE = 256
DIL_GROUP = 2


def _pair_norm(x, gain2, lo_half):
    sq = x * x
    tot = jnp.sum(sq, axis=-1, keepdims=True)
    lo = jnp.sum(jnp.where(lo_half, sq, 0.0), axis=-1, keepdims=True)
    ms = jnp.where(lo_half, lo, tot - lo) * (1.0 / HEAD_DIM)
    return x * lax.rsqrt(ms + EPS) * gain2


def _dil_body(x_ref, gq_ref, gk_ref, band_ref, o_ref, bias_ref, q_scr, k_scr, v_scr, oa, ma, la, *, seq):
    span = DIL_SPAN
    n_blocks = seq // span
    n_tiles = GROUP_W // LANES
    lo_tile = lax.broadcasted_iota(jnp.int32, (DIL_TILE, LANES), 1) < HEAD_DIM
    lo_blk = lax.broadcasted_iota(jnp.int32, (span, LANES), 1) < HEAD_DIM

    for bh in range(len(DIL_PAIRS) * N_HEADS):
        profile = jnp.broadcast_to(band_ref[bh:bh + 1, :], (span, 2 * span))
        bias_ref[bh] = pltpu.roll(profile, 0, 1, stride=1, stride_axis=0)
    in_current = lax.broadcasted_iota(jnp.int32, (span, 2 * span), 1) >= span

    def norm_tile(i, carry):
        rows = pl.ds(pl.multiple_of(i * DIL_TILE, DIL_TILE), DIL_TILE)
        for j in range(n_tiles):
            cols = lambda part: slice(part * GROUP_W + j * LANES, part * GROUP_W + (j + 1) * LANES)
            q = _pair_norm(x_ref[rows, cols(0)], gq_ref[...], lo_tile)
            q_scr[j, rows, :] = q * (HEAD_DIM ** -0.5)
            k_scr[j, rows, :] = _pair_norm(x_ref[rows, cols(1)], gk_ref[...], lo_tile)
            v_scr[j, rows, :] = x_ref[rows, cols(2)]
        return carry
    lax.fori_loop(0, seq // DIL_TILE, norm_tile, 0)

    for bi, (window, dil) in enumerate(DIL_PAIRS):
        sub_len = seq // dil
        nb = sub_len // span

        def block_group(i, carry, bi=bi, nb=nb, dil=dil):
            todo = []
            for u in range(DIL_GROUP):
                t = i * DIL_GROUP + u
                r = lax.shift_right_logical(t, int(math.log2(nb)))
                c = t & (nb - 1)
                first = r + c * (dil * span)
                stride = dil if dil > 1 else None
                rows = pl.ds(first, span, stride=stride)
                prev = pl.ds(jnp.maximum(first - dil * span, r), span, stride=stride)
                keep = jnp.logical_or(in_current, c != 0) if nb > 1 else None
                for j in range(n_tiles):
                    q2 = q_scr[j, rows, :]
                    if nb > 1:
                        k_cat = jnp.concatenate([k_scr[j, prev, :], k_scr[j, rows, :]], axis=0).astype(BF16)
                        v_cat = jnp.concatenate([v_scr[j, prev, :], v_scr[j, rows, :]], axis=0).astype(BF16)
                    else:
                        k_cat = k_scr[j, rows, :].astype(BF16)
                        v_cat = v_scr[j, rows, :].astype(BF16)
                    todo.append((rows, j, q2, k_cat, v_cat, keep))
            logits = []
            for rows, j, q2, k_cat, v_cat, keep in todo:
                for a in range(2):
                    qa = jnp.where(lo_blk if a == 0 else jnp.logical_not(lo_blk), q2, 0.0).astype(BF16)
                    bias = bias_ref[bi * N_HEADS + 2 * j + a]
                    lg = _dot_nt(qa, k_cat) + (bias if nb > 1 else bias[:, span:])
                    logits.append(jnp.where(keep, lg, NEG_INF) if nb > 1 else lg)
            m = [jnp.max(lg, axis=-1, keepdims=True) for lg in logits]
            p = [jnp.exp(lg - mx) for lg, mx in zip(logits, m)]
            l = [jnp.sum(px, axis=-1, keepdims=True) for px in p]
            o = [_dot(p[2 * n + a].astype(BF16), todo[n][4]) for n in range(len(todo)) for a in range(2)]
            for n, (rows, j, *_) in enumerate(todo):
                o_in = jnp.where(lo_blk, o[2 * n], o[2 * n + 1])
                m_in = jnp.where(lo_blk, m[2 * n], m[2 * n + 1])
                l_in = jnp.where(lo_blk, l[2 * n], l[2 * n + 1])
                if bi > 0:
                    m_old = ma[j, rows, :]
                    m_new = jnp.maximum(m_old, m_in)
                    w_old, w_in = jnp.exp(m_old - m_new), jnp.exp(m_in - m_new)
                    o_in = w_old * oa[j, rows, :] + w_in * o_in
                    l_in = w_old * la[j, rows, :] + w_in * l_in
                    m_in = m_new
                oa[j, rows, :] = o_in
                ma[j, rows, :] = m_in
                la[j, rows, :] = l_in
            return carry
        lax.fori_loop(0, n_blocks // DIL_GROUP, block_group, 0)

    def finish(i, carry):
        rows = pl.ds(pl.multiple_of(i * DIL_TILE, DIL_TILE), DIL_TILE)
        for j in range(n_tiles):
            o_ref[rows, j * LANES:(j + 1) * LANES] = oa[j, rows, :] / la[j, rows, :]
        return carry
    lax.fori_loop(0, seq // DIL_TILE, finish, 0)


def _dil(slab, gq2, gk2, band, *, batch, seq):
    big = lambda: pltpu.VMEM((GROUP_W // LANES, seq, LANES), F32)
    n_bias = len(DIL_PAIRS) * N_HEADS
    return pl.pallas_call(
        functools.partial(_dil_body, seq=seq),
        grid=(batch,),
        in_specs=[pl.BlockSpec((seq, DIL_SLAB), lambda b: (0, b)),
                  _const_spec((1, LANES)), _const_spec((1, LANES)),
                  _const_spec(band.shape)],
        out_specs=pl.BlockSpec((seq, GROUP_W), lambda b: (0, b)),
        out_shape=jax.ShapeDtypeStruct((seq, batch * GROUP_W), F32),
        scratch_shapes=[pltpu.VMEM((n_bias, DIL_SPAN, 2 * DIL_SPAN), F32)] + [big() for _ in range(6)],
        compiler_params=_params("parallel"),
        name="dilated",
    )(slab, gq2, gk2, band)


def _t5_bucket(dist):
    exact = T5_BUCKETS // 2
    df = jnp.maximum(dist, 1).astype(F32)
    large = exact + (jnp.log(df / exact) / math.log(T5_MAX_DIST / exact) * (T5_BUCKETS - exact)).astype(jnp.int32)
    large = jnp.minimum(large, T5_BUCKETS - 1)
    return jnp.where(dist < exact, dist, large)


def _dil_band(table):
    span = DIL_SPAN
    delta = span - jnp.arange(2 * span, dtype=jnp.int32)
    rows = []
    for _, dil in DIL_PAIRS:
        bucket = _t5_bucket(jnp.clip(delta, 0, span) * dil)
        onehot = (bucket[:, None] == jnp.arange(T5_BUCKETS, dtype=jnp.int32)[None, :]).astype(F32)
        vals = jnp.dot(onehot, table.astype(F32), precision=HIGHEST)
        rows.append(jnp.where((delta >= 0)[:, None], vals, NEG_INF).T)
    return jnp.concatenate(rows, axis=0)


DN_TILE = 256
DN_GROUP = 4
PREP_STORE_DELAY = 12


def _softplus(x):
    return jnp.maximum(x, 0.0) + jnp.log1p(jnp.exp(-jnp.abs(x)))


def _pair_l2(x, lo_half):
    sq = x * x
    tot = jnp.sum(sq, axis=-1, keepdims=True)
    lo = jnp.sum(jnp.where(lo_half, sq, 0.0), axis=-1, keepdims=True)
    return x * lax.rsqrt(jnp.where(lo_half, lo, tot - lo) + EPS)


def _dn_body(x_ref, cw_ref, alog_ref, dtb_ref, on_ref, o_ref, q_scr, k_scr, v_scr, g_scr, b_scr, w_scr, a_scr,
             s_scr, *, seq):
    c = DN_CHUNK
    w = GROUP_W
    lo_tile = lax.broadcasted_iota(jnp.int32, (DN_TILE, LANES), 1) < HEAD_DIM

    def prep_steps(i):
        if isinstance(i, int):
            first, halo_first = i * DN_TILE, max(i * DN_TILE - 8, 0)
        else:
            first = pl.multiple_of(i * DN_TILE, DN_TILE)
            halo_first = pl.multiple_of(jnp.maximum(first - 8, 0), 8)
        rows = pl.ds(first, DN_TILE)
        cur = x_ref[rows, 0:3 * w]
        halo = jnp.where(i > 0, x_ref[pl.ds(halo_first, 8), 0:3 * w], 0.0)
        ext = jnp.concatenate([halo, cur], axis=0)
        acc = cw_ref[DN_CONV - 1:DN_CONV, :] * cur
        for j in range(DN_CONV - 1):
            acc = acc + cw_ref[j:j + 1, :] * pltpu.roll(ext, DN_CONV - 1 - j, 0)[8:, :]
        y = acc * jax.nn.sigmoid(acc)
        yield
        tiles = range(w // LANES)
        cols = lambda part, j: slice(part * w + j * LANES, part * w + (j + 1) * LANES)
        q = [_pair_l2(y[:, cols(0, j)], lo_tile) * (HEAD_DIM ** -0.5) for j in tiles]
        k = [_pair_l2(y[:, cols(1, j)], lo_tile) for j in tiles]
        yield
        ab = x_ref[rows, 3 * w:3 * w + LANES]
        g = -jnp.exp(alog_ref[...]) * _softplus(ab + dtb_ref[...])
        gc = _dot_split(chunk_tril, g, 3, split_rhs=True)
        yield
        gc = per_head_lanes(gc, 0)
        beta = per_head_lanes(jax.nn.sigmoid(ab), N_HEADS)
        for _ in range(PREP_STORE_DELAY):
            yield
        for j in tiles:
            q_scr[rows, j * LANES:(j + 1) * LANES] = q[j]
            k_scr[rows, j * LANES:(j + 1) * LANES] = k[j]
            v_scr[rows, j * LANES:(j + 1) * LANES] = y[:, cols(2, j)]
        g_scr[rows, :] = gc
        b_scr[rows, :] = beta

    ri = lax.broadcasted_iota(jnp.int32, (c, w), 0)
    ci = lax.broadcasted_iota(jnp.int32, (c, w), 1)
    cj = jnp.bitwise_and(ci, HEAD_DIM - 1)
    causal = ri >= cj
    strict = ri > cj
    eye4 = (ri == cj).astype(F32)
    same_sub = lax.shift_right_logical(ri, 4) == lax.shift_right_logical(cj, 4)
    bi_r = lax.broadcasted_iota(jnp.int32, (w, w), 0)
    bi_c = lax.broadcasted_iota(jnp.int32, (w, w), 1)
    same_head = lax.shift_right_logical(bi_r, 6) == lax.shift_right_logical(bi_c, 6)
    block_mask = same_head.astype(F32)
    chunk_tril = (same_head & (jnp.bitwise_and(bi_r, c - 1) >= jnp.bitwise_and(bi_c, c - 1))).astype(F32)
    head_of_lane = lax.shift_right_logical(lax.broadcasted_iota(jnp.int32, (DN_TILE, w), 1), 6)

    def per_head_lanes(x, first_lane):
        out = None
        for h in range(N_HEADS):
            col = jnp.broadcast_to(x[:, first_lane + h:first_lane + h + 1], (DN_TILE, w))
            out = col if out is None else jnp.where(head_of_lane == h, col, out)
        return out

    def per_head(a, b):
        bd = jnp.where(same_head, jnp.concatenate([b.astype(BF16)] * N_HEADS, axis=0), 0.0)
        return _dot(a.astype(BF16), bd)

    def group_rows(i):
        base = pl.multiple_of(i * (DN_GROUP * c), DN_GROUP * c)
        return [pl.ds(base + u * c, c) for u in range(DN_GROUP)]

    def solve_steps(i):
        rows = group_rows(i)
        each = lambda f, *lists: [f(*args) for args in zip(*lists)]
        q, k, v, gc, beta = ([ref[r, :] for r in rows] for ref in (q_scr, k_scr, v_scr, g_scr, b_scr))
        g_row = each(lambda g: jnp.sum(g * eye4, axis=0, keepdims=True), gc)
        decay = each(lambda g, gr: jnp.exp(jnp.where(causal, g - gr, NEG_INF)), gc, g_row)
        kb = each(jnp.multiply, k, beta)
        k_bd = each(lambda x: jnp.where(same_head, jnp.concatenate([x.astype(BF16)] * N_HEADS, axis=0), 0.0), k)
        lmat = each(lambda a, b, d: jnp.where(strict, _dot_nt(a.astype(BF16), b) * d, 0.0), kb, k_bd, decay)
        yield
        a_qk = each(lambda a, b, d: jnp.where(causal, _dot_nt(a.astype(BF16), b) * d, 0.0), q, k_bd, decay)
        yield
        p = each(lambda l_: jnp.where(same_sub, -l_, 0.0), lmat)
        t_diag = each(lambda x: eye4 + x, p)
        for _ in range(3):
            p = each(per_head, p, p)
            yield
            t_diag = each(lambda t, x: t + per_head(t, x), t_diag, p)
            yield
        nil = each(lambda t, l_: per_head(t, jnp.where(same_sub, 0.0, l_)), t_diag, lmat)
        yield
        nil2 = each(per_head, nil, nil)
        yield
        nil3 = each(per_head, nil, nil2)
        yield
        t_inv = each(lambda n1, n2, n3, t: per_head(eye4 - n1 + n2 - n3, t), nil, nil2, nil3, t_diag)
        yield
        eg = each(jnp.exp, gc)
        w_c = each(lambda t, a, e: per_head(t, a * e), t_inv, kb, eg)
        yield
        u_c = each(lambda t, a, b: per_head(t, a * b), t_inv, v, beta)
        yield
        q_dec = each(jnp.multiply, q, eg)
        k_dec = each(lambda x, g: x * jnp.exp(g[c - 1:c, :] - g), k, gc)
        for ref, vals in zip((w_scr, v_scr, a_scr, q_scr, k_scr), (w_c, u_c, a_qk, q_dec, k_dec)):
            for r, val in zip(rows, vals):
                ref[r, :] = val

    def state_steps(i):
        for rows in group_rows(i):
            state = s_scr[...]
            state_b = state.astype(BF16)
            w_s = _dot(w_scr[rows, :].astype(BF16), state_b)
            q_s = _dot(q_scr[rows, :].astype(BF16), state_b)
            yield
            v_new = v_scr[rows, :] - w_s
            o_intra = per_head(a_scr[rows, :], v_new)
            upd = lax.dot_general(k_scr[rows, :].astype(BF16), v_new.astype(BF16), (((0,), (0,)), ((), ())),
                                  preferred_element_type=F32)
            yield
            o_ref[rows, :] = q_s + o_intra
            s_scr[...] = state * jnp.exp(g_scr[rows, :][c - 1:c, :]) + upd * block_mask
            yield

    def weave(*steps):
        live = list(steps)
        while live:
            for gen in list(live):
                if next(gen, "done") == "done":
                    live.remove(gen)

    n_groups = seq // (DN_GROUP * c)
    s_scr[...] = jnp.zeros_like(s_scr)
    weave(prep_steps(0))
    weave(solve_steps(0), prep_steps(1))

    def group(i, carry):
        weave(solve_steps(i), state_steps(i - 1), prep_steps(i + 1))
        return carry
    lax.fori_loop(1, n_groups - 1, group, 0)
    weave(solve_steps(n_groups - 1), state_steps(n_groups - 2))
    weave(state_steps(n_groups - 1))

    def finish_tile(i, carry):
        rows = pl.ds(pl.multiple_of(i * DN_TILE, DN_TILE), DN_TILE)
        o = o_ref[rows, :]
        ms = _dot_split(o * o, block_mask, 2) * (1.0 / HEAD_DIM)
        gate = x_ref[rows, 3 * w + LANES:4 * w + LANES]
        o_ref[rows, :] = o * lax.rsqrt(ms + EPS) * on_ref[...] * (gate * jax.nn.sigmoid(gate))
        return carry
    lax.fori_loop(0, seq // DN_TILE, finish_tile, 0)


def _dn(slab, conv_w, a_log, dt_bias, o_norm, *, batch, seq):
    wide = lambda: pltpu.VMEM((seq, GROUP_W), F32)
    return pl.pallas_call(
        functools.partial(_dn_body, seq=seq),
        grid=(batch,),
        in_specs=[pl.BlockSpec((seq, DN_SLAB), lambda b: (0, b)),
                  _const_spec((DN_CONV, 3 * GROUP_W)), _const_spec((1, LANES)), _const_spec((1, LANES)),
                  _const_spec((1, GROUP_W))],
        out_specs=pl.BlockSpec((seq, GROUP_W), lambda b: (0, b)),
        out_shape=jax.ShapeDtypeStruct((seq, batch * GROUP_W), F32),
        scratch_shapes=[wide() for _ in range(7)] + [pltpu.VMEM((GROUP_W, GROUP_W), F32)],
        compiler_params=_params("parallel"),
        name="deltanet",
    )(slab, conv_w, a_log, dt_bias, o_norm)


def _row(v, width=None):
    v = v.astype(F32).reshape(1, -1)
    if width is not None and v.shape[1] < width:
        v = jnp.pad(v, ((0, 0), (0, width - v.shape[1])))
    return v


def _prep_w_in(w):
    w = w.astype(BF16)
    z = lambda n: jnp.zeros((D_MODEL, n), BF16)
    return jnp.concatenate([w[:, 0:416], z(96), w[:, 416:2216], z(120), w[:, 2216:2472]], axis=1)


def _pad_heads(w, per_head, lo, hi):
    k = w.shape[0]
    w = w.reshape(k, N_HEADS, per_head)[:, :, lo:hi]
    w = jnp.pad(w, ((0, 0), (0, 0), (0, LANES - (hi - lo))))
    return w.reshape(k, N_HEADS * LANES).astype(BF16)


def _rope_tables(seq):
    half = MLA_ROPE // 2
    pos = jnp.arange(seq, dtype=F32)
    freqs = ROPE_THETA ** (-jnp.arange(half, dtype=F32) / half)
    ang = pos[:, None] * freqs[None, :]
    cos, sin = jnp.cos(ang), jnp.sin(ang)
    ones = jnp.ones((seq, MLA_NOPE), F32)
    zeros = jnp.zeros((seq, MLA_NOPE), F32)
    tail1 = jnp.ones((seq, LANES - MLA_DQK), F32)
    tail0 = jnp.zeros((seq, LANES - MLA_DQK), F32)
    return (jnp.concatenate([ones, cos, cos, tail1], axis=1),
            jnp.concatenate([zeros, -sin, sin, tail0], axis=1))


def _block_diag(blocks):
    g, r, c = blocks.shape
    eye = jnp.eye(g, dtype=blocks.dtype)
    return (blocks[:, :, None, :] * eye[:, None, :, None]).reshape(g * r, g * c)


def _mla_layer(slab, p, l, *, batch, seq):
    w_ukv = p["mla_w_ukv"][l]
    cos, sin = _rope_tables(seq)
    return _mla(slab, _row(p["mla_q_norm"][l]), _row(p["mla_kv_norm"][l]),
                _pad_heads(p["mla_w_uq"][l], MLA_DQK, 0, MLA_DQK),
                _pad_heads(w_ukv, MLA_NOPE + HEAD_DIM, 0, MLA_NOPE),
                w_ukv.reshape(MLA_KV_RANK, N_HEADS, MLA_NOPE + HEAD_DIM)[:, :, MLA_NOPE:]
                .reshape(MLA_KV_RANK, GROUP_W).astype(BF16),
                _row(p["mla_qk_q"][l], LANES), _row(p["mla_qk_k"][l], LANES), cos, sin, batch=batch, seq=seq)


def _dil_layer(slab, p, l, *, batch, seq):
    pair = lambda g: jnp.tile(g.astype(F32).reshape(1, HEAD_DIM), (1, LANES // HEAD_DIM))
    return _dil(slab, pair(p["dil_q_norm"][l]), pair(p["dil_k_norm"][l]), _dil_band(p["t5_bias"]),
                batch=batch, seq=seq)


def _dn_layer(slab, p, l, *, batch, seq):
    return _dn(slab, p["dn_conv"][l].astype(F32), _row(p["dn_a_log"][l], LANES), _row(p["dn_dt_bias"][l], LANES),
               jnp.tile(p["dn_o_norm"][l].astype(F32).reshape(1, HEAD_DIM), (1, N_HEADS)), batch=batch, seq=seq)


def _s5_layer(u, p, l, *, batch, seq):
    state_row = lambda v: v.astype(F32).reshape(1, S5_WIDTH)
    ldt = jnp.broadcast_to(p["s5_log_dt"][l][:, None], (S5_GROUPS, S5_STATE))
    bre = _block_diag(jnp.swapaxes(p["s5_b_re"][l], 1, 2).astype(F32))
    bim = _block_diag(jnp.swapaxes(p["s5_b_im"][l], 1, 2).astype(F32))
    cre = _block_diag(jnp.swapaxes(p["s5_c_re"][l], 1, 2)).astype(BF16)
    cim = _block_diag(jnp.swapaxes(p["s5_c_im"][l], 1, 2)).astype(BF16)
    return _s5(u, state_row(p["s5_lambda_re"][l]), state_row(p["s5_lambda_im"][l]), state_row(ldt),
               bre, bim, cre, cim, _row(p["s5_d"][l]), p["s5_w_glu"][l].astype(BF16), batch=batch, seq=seq,
               steps=min(256, seq))


def kernel(x, attn_norm, w_in, w_out, mla_q_norm, mla_kv_norm, mla_w_uq, mla_w_ukv, mla_qk_q, mla_qk_k,
           s5_lambda_re, s5_lambda_im, s5_log_dt, s5_b_re, s5_b_im, s5_c_re, s5_c_im, s5_d, s5_w_glu,
           dil_q_norm, dil_k_norm, t5_bias, dn_conv, dn_a_log, dn_dt_bias, dn_o_norm,
           ffn_norm, ffn_w1, ffn_w3, ffn_w2):
    p = dict(mla_q_norm=mla_q_norm, mla_kv_norm=mla_kv_norm, mla_w_uq=mla_w_uq, mla_w_ukv=mla_w_ukv,
             mla_qk_q=mla_qk_q, mla_qk_k=mla_qk_k, s5_lambda_re=s5_lambda_re, s5_lambda_im=s5_lambda_im,
             s5_log_dt=s5_log_dt, s5_b_re=s5_b_re, s5_b_im=s5_b_im, s5_c_re=s5_c_re, s5_c_im=s5_c_im, s5_d=s5_d,
             s5_w_glu=s5_w_glu, dil_q_norm=dil_q_norm, dil_k_norm=dil_k_norm, t5_bias=t5_bias, dn_conv=dn_conv,
             dn_a_log=dn_a_log, dn_dt_bias=dn_dt_bias, dn_o_norm=dn_o_norm)
    batch, seq, _ = x.shape
    h = x.reshape(batch * seq, D_MODEL)
    for l in range(attn_norm.shape[0]):
        mla_in, s5_in, dil_in, dn_in = _proj(h, _row(attn_norm[l]), _prep_w_in(w_in[l]), batch=batch, seq=seq)
        ys = [_mla_layer(mla_in, p, l, batch=batch, seq=seq),
              _s5_layer(s5_in, p, l, batch=batch, seq=seq),
              _dil_layer(dil_in, p, l, batch=batch, seq=seq),
              _dn_layer(dn_in, p, l, batch=batch, seq=seq)]
        h = _out_ffn(h, ys, w_out[l].astype(BF16), _row(ffn_norm[l]),
                     ffn_w1[l].astype(BF16), ffn_w3[l].astype(BF16), ffn_w2[l].astype(BF16), batch=batch, seq=seq)
    return h.reshape(batch, seq, D_MODEL)
```

```python
import functools
import math

import jax
import jax.numpy as jnp
import numpy as np
from jax import lax
from jax.experimental import pallas as pl
from jax.experimental.pallas import tpu as pltpu

F32 = jnp.float32
BF16 = jnp.bfloat16
HIGHEST = lax.Precision.HIGHEST

D_MODEL = 1024
GROUP_W = 256
HEAD_DIM = 64
N_HEADS = 4
EPS = 1e-6
NEG_INF = -1e30

MLA_NOPE = 64
MLA_ROPE = 32
MLA_DQK = MLA_NOPE + MLA_ROPE
MLA_KV_RANK = 128
ROPE_THETA = 10000.0

S5_GROUP_CH = 16
S5_GROUPS = 16
S5_STATE = 64
S5_WIDTH = S5_GROUPS * S5_STATE

DIL_PAIRS = ((128, 1), (512, 4), (2048, 16))
DIL_SPAN = 128
T5_BUCKETS = 32
T5_MAX_DIST = 2048

DN_CONV = 4
DN_CHUNK = 64

FFN_HIDDEN = 2816
FFN_CHUNK = 2816

VMEM_LIMIT_BYTES = 56 * 1024 * 1024
LANES = 128

MLA_SLAB = 512
S5_SLAB = 256
DIL_SLAB = 768
DN_SLAB = 1152
PROJ_COLS = MLA_SLAB + S5_SLAB + DIL_SLAB + DN_SLAB


def _dot(a, b, precision=None):
    return jnp.dot(a, b, preferred_element_type=F32, precision=precision)


def _dot_nt(a, b, precision=None):
    return lax.dot_general(a, b, (((1,), (1,)), ((), ())), preferred_element_type=F32, precision=precision)


def _dot_split(a, b, terms, split_rhs=False):
    x = b if split_rhs else a
    mask = (a if split_rhs else b).astype(BF16)
    out = None
    for _ in range(terms):
        piece = x.astype(BF16)
        part = _dot(mask, piece) if split_rhs else _dot(piece, mask)
        out = part if out is None else out + part
        x = x - piece.astype(F32)
    return out


def _const_spec(shape):
    nd = len(shape)
    return pl.BlockSpec(shape, lambda *_: (0,) * nd, pipeline_mode=pl.Buffered(1))


def _params(*sem):
    return pltpu.CompilerParams(dimension_semantics=sem, vmem_limit_bytes=VMEM_LIMIT_BYTES)


def _proj_body(x_ref, g_ref, w_ref, mla_ref, s5_ref, dil_ref, dn_ref):
    x = x_ref[...]
    n = x * lax.rsqrt(jnp.mean(x * x, axis=-1, keepdims=True) + EPS) * g_ref[...]
    nb = n.astype(BF16)
    start = 0
    for ref in (mla_ref, s5_ref, dil_ref, dn_ref):
        width = ref.shape[-1]
        ref[...] = _dot(nb, w_ref[:, start:start + width])
        start += width


def _proj(h, gain, w_big, *, batch, seq, tm=512):
    nt = seq // tm
    widths = (MLA_SLAB, S5_SLAB, DIL_SLAB, DN_SLAB)
    return pl.pallas_call(
        _proj_body,
        grid=(batch, nt),
        in_specs=[pl.BlockSpec((tm, D_MODEL), lambda b, i: (b * nt + i, 0)),
                  _const_spec((1, D_MODEL)),
                  _const_spec((D_MODEL, PROJ_COLS))],
        out_specs=[pl.BlockSpec((tm, w), lambda b, i: (i, b)) for w in widths],
        out_shape=[jax.ShapeDtypeStruct((seq, batch * w), F32) for w in widths],
        compiler_params=_params("parallel", "parallel"),
        name="proj",
    )(h, gain, w_big)


def _out_ffn_body(h_ref, y0_ref, y1_ref, y2_ref, y3_ref, wo_ref, g_ref, w1_ref, w3_ref, w2_ref, o_ref, acc_ref):
    h = h_ref[...]
    for i, y_ref in enumerate((y0_ref, y1_ref, y2_ref, y3_ref)):
        h = h + _dot(y_ref[...].astype(BF16), wo_ref[i * GROUP_W:(i + 1) * GROUP_W, :])
    n = h * lax.rsqrt(jnp.mean(h * h, axis=-1, keepdims=True) + EPS) * g_ref[...]
    nb = n.astype(BF16)
    acc_ref[...] = h

    def hidden_chunk(c, carry):
        cols = pl.ds(pl.multiple_of(c * FFN_CHUNK, FFN_CHUNK), FFN_CHUNK)
        a = _dot(nb, w1_ref[:, cols])
        b = _dot(nb, w3_ref[:, cols])
        z = (a * jax.nn.sigmoid(a) * b).astype(BF16)
        acc_ref[...] += _dot(z, w2_ref[cols, :])
        return carry
    lax.fori_loop(0, FFN_HIDDEN // FFN_CHUNK, hidden_chunk, 0)
    o_ref[...] = acc_ref[...]


def _out_ffn(h, ys, w_out, gain, w1, w3, w2, *, batch, seq, tm=512):
    nt = seq // tm
    row = pl.BlockSpec((tm, D_MODEL), lambda b, i: (b * nt + i, 0))
    slab = pl.BlockSpec((tm, GROUP_W), lambda b, i: (i, b))
    return pl.pallas_call(
        _out_ffn_body,
        grid=(batch, nt),
        in_specs=[row] + [slab] * 4 + [
            _const_spec((D_MODEL, D_MODEL)), _const_spec((1, D_MODEL)),
            _const_spec((D_MODEL, FFN_HIDDEN)), _const_spec((D_MODEL, FFN_HIDDEN)),
            _const_spec((FFN_HIDDEN, D_MODEL))],
        out_specs=row,
        out_shape=jax.ShapeDtypeStruct((batch * seq, D_MODEL), F32),
        scratch_shapes=[pltpu.VMEM((tm, D_MODEL), F32)],
        compiler_params=_params("parallel", "parallel"),
        name="out_ffn",
    )(h, *ys, w_out, gain, w1, w3, w2)


MLA_BLOCK = 256
MLA_SWEEP = 2


def _mla_body(x_ref, qn_ref, kvn_ref, wuq_ref, wuk_ref, wuv_ref, gq_ref, gk_ref, cos_ref, sin_ref,
              o_ref, k_scr, vt_scr, q_scr, *, seq):
    blk = MLA_BLOCK
    n_pairs = N_HEADS // 2
    qi = pl.program_id(1)
    lane = lax.broadcasted_iota(jnp.int32, (blk, LANES), 1)

    def rope(x, c, s):
        rot = jnp.where(lane < MLA_NOPE + MLA_ROPE // 2, pltpu.roll(x, LANES - MLA_ROPE // 2, 1),
                        pltpu.roll(x, MLA_ROPE // 2, 1))
        return x * c + rot * s

    def norm_head(x, g):
        ssq = jnp.sum(x * x, axis=-1, keepdims=True)
        return x * lax.rsqrt(ssq * (1.0 / MLA_DQK) + EPS) * g

    heads = range(N_HEADS)

    def prepare_steps(i):
        rows = pl.ds(i * blk if isinstance(i, int) else pl.multiple_of(i * blk, blk), blk)
        ckv = x_ref[rows, 256:384]
        cq = x_ref[rows, 0:256]
        kvn = ckv * lax.rsqrt(jnp.mean(ckv * ckv, axis=-1, keepdims=True) + EPS) * kvn_ref[...]
        qn = cq * lax.rsqrt(jnp.mean(cq * cq, axis=-1, keepdims=True) + EPS) * qn_ref[...]
        kvn = kvn.astype(BF16)
        k_nope = _dot(kvn, wuk_ref[...])
        v = _dot(kvn, wuv_ref[...])
        q_all = _dot(qn.astype(BF16), wuq_ref[...])
        yield
        k_rope = pltpu.roll(x_ref[rows, 384:512], MLA_NOPE, 1)
        c = cos_ref[rows, :]
        s = sin_ref[rows, :]
        ks = [rope(norm_head(k_nope[:, h * LANES:(h + 1) * LANES] + k_rope, gk_ref[...]), c, s) for h in heads]
        yield
        qs_new = [rope(norm_head(q_all[:, h * LANES:(h + 1) * LANES], gq_ref[...]), c, s) * (MLA_DQK ** -0.5)
                  for h in heads]
        vts = [v[:, pr * LANES:(pr + 1) * LANES].T for pr in range(n_pairs)]
        yield
        yield
        for h in heads:
            k_scr[h, rows, :] = ks[h].astype(BF16)
            q_scr[h] = qs_new[h].astype(BF16)
        for pr in range(n_pairs):
            vt_scr[pr, :, rows] = vts[pr].astype(BF16)

    def weave(*steps):
        live = list(steps)
        while live:
            for gen in list(live):
                if next(gen, "done") == "done":
                    live.remove(gen)

    @pl.when(qi == 0)
    def _first_block():
        weave(prepare_steps(0))

    rows = pl.ds(pl.multiple_of(qi * blk, blk), blk)
    key_pos = lax.broadcasted_iota(jnp.int32, (blk, blk), 0)
    query_pos = lax.broadcasted_iota(jnp.int32, (blk, blk), 1)
    first_of_pair = lax.broadcasted_iota(jnp.int32, (LANES, blk), 0) < HEAD_DIM
    qs = [q_scr[h] for h in heads]

    def kv_steps(first_key, width, carry, result, diagonal=False):
        m_old, l_old, acc_old = carry
        keys = pl.ds(pl.multiple_of(first_key, blk), width)
        logits = [_dot_nt(k_scr[h, keys, :], qs[h]) for h in heads]
        yield
        if diagonal:
            logits = [jnp.where(key_pos <= query_pos, x, NEG_INF) for x in logits]
        m_new = [jnp.maximum(m_old[h], jnp.max(logits[h], axis=0, keepdims=True)) for h in heads]
        alpha = [jnp.exp(m_old[h] - m_new[h]) for h in heads]
        p = [jnp.exp(logits[h] - m_new[h]) for h in heads]
        l_new = [alpha[h] * l_old[h] + jnp.sum(p[h], axis=0, keepdims=True) for h in heads]
        pv = [_dot(vt_scr[h // 2, :, keys], p[h].astype(BF16)) for h in heads]
        yield
        acc_new = [jnp.where(first_of_pair, alpha[2 * pr] * acc_old[pr] + pv[2 * pr],
                             alpha[2 * pr + 1] * acc_old[pr] + pv[2 * pr + 1]) for pr in range(n_pairs)]
        result.append((tuple(m_new), tuple(l_new), tuple(acc_new)))

    def kv_step(first_key, width, carry):
        result = []
        weave(kv_steps(first_key, width, carry, result))
        return result[0]

    def last_steps(carry):
        result = []
        yield from kv_steps(qi * blk, blk, carry, result, diagonal=True)
        _, l, acc = result[0]
        for pr in range(n_pairs):
            out_t = acc[pr] / jnp.where(first_of_pair, l[2 * pr], l[2 * pr + 1])
            o_ref[rows, pr * LANES:(pr + 1) * LANES] = out_t.T

    wide = MLA_SWEEP * blk
    carry = (tuple(jnp.full((1, blk), NEG_INF, F32) for _ in heads), tuple(jnp.zeros((1, blk), F32) for _ in heads),
             tuple(jnp.zeros((LANES, blk), F32) for _ in range(n_pairs)))
    n_wide = qi // MLA_SWEEP
    carry = lax.fori_loop(0, n_wide, lambda j, cr: kv_step(j * wide, wide, cr), carry)
    carry = lax.fori_loop(n_wide * MLA_SWEEP, qi, lambda j, cr: kv_step(j * blk, blk, cr), carry)

    @pl.when(qi + 1 < seq // blk)
    def _last_and_prepare_next():
        weave(last_steps(carry), prepare_steps(qi + 1))

    @pl.when(qi + 1 == seq // blk)
    def _last():
        weave(last_steps(carry))


def _mla(slab, qn, kvn, wuq, wuk, wuv, gq, gk, cos, sin, *, batch, seq):
    nq = seq // MLA_BLOCK
    return pl.pallas_call(
        functools.partial(_mla_body, seq=seq),
        grid=(batch, nq),
        in_specs=[pl.BlockSpec((seq, MLA_SLAB), lambda b, i: (0, b)),
                  _const_spec((1, 256)), _const_spec((1, MLA_KV_RANK)),
                  _const_spec((256, N_HEADS * LANES)), _const_spec((MLA_KV_RANK, N_HEADS * LANES)),
                  _const_spec((MLA_KV_RANK, GROUP_W)),
                  _const_spec((1, LANES)), _const_spec((1, LANES)),
                  _const_spec((seq, LANES)), _const_spec((seq, LANES))],
        out_specs=pl.BlockSpec((seq, GROUP_W), lambda b, i: (0, b)),
        out_shape=jax.ShapeDtypeStruct((seq, batch * GROUP_W), F32),
        scratch_shapes=[pltpu.VMEM((N_HEADS, seq, LANES), BF16), pltpu.VMEM((N_HEADS // 2, LANES, seq), BF16),
                        pltpu.VMEM((N_HEADS, MLA_BLOCK, LANES), BF16)],
        compiler_params=_params("parallel", "arbitrary"),
        name="mla",
    )(slab, qn, kvn, wuq, wuk, wuv, gq, gk, cos, sin)


def _s5_body(u_ref, lre_ref, lim_ref, ldt_ref, bre_ref, bim_ref, cre_ref, cim_ref, d_ref, wglu_ref, o_ref,
             a_scr, bbar_scr, h_scr, x_scr, io_scr, *, batch, steps):
    io_tiles = GROUP_W // LANES

    @pl.when(pl.program_id(0) == 0)
    def _discretise():
        lr = lre_ref[...]
        li = lim_ref[...]
        dt = jnp.exp(ldt_ref[...])
        mag = jnp.exp(lr * dt)
        ar = mag * jnp.cos(li * dt)
        ai = mag * jnp.sin(li * dt)
        den = lr * lr + li * li
        nr = ar - 1.0
        zr = (nr * lr + ai * li) / den
        zi = (ai * lr - nr * li) / den
        a_scr[0] = jnp.broadcast_to(ar, (batch, S5_WIDTH))
        a_scr[1] = jnp.broadcast_to(ai, (batch, S5_WIDTH))
        bre = bre_ref[...]
        bim = bim_ref[...]
        bbar_scr[0] = (zr * bre - zi * bim).astype(BF16)
        bbar_scr[1] = (zr * bim + zi * bre).astype(BF16)
        h_scr[...] = jnp.zeros_like(h_scr)

    for b in range(batch):
        for j in range(io_tiles):
            io_scr[j, pl.ds(b, steps, stride=batch), :] = u_ref[:, b * GROUP_W + j * LANES:b * GROUP_W + (j + 1) * LANES]
    u = jnp.concatenate([io_scr[j] for j in range(io_tiles)], axis=1)
    ub = u.astype(BF16)
    x_scr[0] = _dot(ub, bbar_scr[0])
    x_scr[1] = _dot(ub, bbar_scr[1])
    ar = a_scr[0]
    ai = a_scr[1]

    def step(t, carry):
        hr, hi = carry
        rows = pl.ds(pl.multiple_of(t * batch, batch), batch)
        nhr = ar * hr - ai * hi + x_scr[0, rows, :]
        nhi = ar * hi + ai * hr + x_scr[1, rows, :]
        x_scr[0, rows, :] = nhr
        x_scr[1, rows, :] = nhi
        return nhr, nhi

    hr, hi = lax.fori_loop(0, steps, step, (h_scr[0], h_scr[1]), unroll=8)
    h_scr[0] = hr
    h_scr[1] = hi
    y = _dot(x_scr[0].astype(BF16), cre_ref[...]) - _dot(x_scr[1].astype(BF16), cim_ref[...]) + d_ref[...] * u
    z = _dot(y.astype(BF16), wglu_ref[...])
    out = z[:, :GROUP_W] * jax.nn.sigmoid(z[:, GROUP_W:])
    for j in range(io_tiles):
        io_scr[j] = out[:, j * LANES:(j + 1) * LANES]
    for b in range(batch):
        for j in range(io_tiles):
            o_ref[:, b * GROUP_W + j * LANES:b * GROUP_W + (j + 1) * LANES] = io_scr[j, pl.ds(b, steps, stride=batch), :]


def _s5(u, lre, lim, ldt, bre, bim, cre, cim, d, wglu, *, batch, seq, steps=256):
    state = lambda rows: pltpu.VMEM((2, rows, S5_WIDTH), F32)
    return pl.pallas_call(
        functools.partial(_s5_body, batch=batch, steps=steps),
        grid=(seq // steps,),
        in_specs=[pl.BlockSpec((steps, batch * S5_SLAB), lambda i: (i, 0)),
                  _const_spec((1, S5_WIDTH)), _const_spec((1, S5_WIDTH)), _const_spec((1, S5_WIDTH)),
                  _const_spec((GROUP_W, S5_WIDTH)), _const_spec((GROUP_W, S5_WIDTH)),
                  _const_spec((S5_WIDTH, GROUP_W)), _const_spec((S5_WIDTH, GROUP_W)),
                  _const_spec((1, GROUP_W)), _const_spec((GROUP_W, 2 * GROUP_W))],
        out_specs=pl.BlockSpec((steps, batch * GROUP_W), lambda i: (i, 0)),
        out_shape=jax.ShapeDtypeStruct((seq, batch * GROUP_W), F32),
        scratch_shapes=[state(batch), pltpu.VMEM((2, GROUP_W, S5_WIDTH), BF16), state(batch),
                        state(steps * batch), pltpu.VMEM((GROUP_W // LANES, steps * batch, LANES), F32)],
        compiler_params=_params("arbitrary"),
        name="s5",
    )(u, lre, lim, ldt, bre, bim, cre, cim, d, wglu)


DIL_TILE = 256
DIL_GROUP = 2


def _pair_norm(x, gain2, lo_half):
    sq = x * x
    tot = jnp.sum(sq, axis=-1, keepdims=True)
    lo = jnp.sum(jnp.where(lo_half, sq, 0.0), axis=-1, keepdims=True)
    ms = jnp.where(lo_half, lo, tot - lo) * (1.0 / HEAD_DIM)
    return x * lax.rsqrt(ms + EPS) * gain2


def _dil_body(x_ref, gq_ref, gk_ref, band_ref, o_ref, bias_ref, q_scr, k_scr, v_scr, oa, ma, la, *, seq):
    span = DIL_SPAN
    n_blocks = seq // span
    n_tiles = GROUP_W // LANES
    lo_tile = lax.broadcasted_iota(jnp.int32, (DIL_TILE, LANES), 1) < HEAD_DIM
    lo_blk = lax.broadcasted_iota(jnp.int32, (span, LANES), 1) < HEAD_DIM

    for bh in range(len(DIL_PAIRS) * N_HEADS):
        profile = jnp.broadcast_to(band_ref[bh:bh + 1, :], (span, 2 * span))
        bias_ref[bh] = pltpu.roll(profile, 0, 1, stride=1, stride_axis=0)
    in_current = lax.broadcasted_iota(jnp.int32, (span, 2 * span), 1) >= span

    def norm_tile(i, carry):
        rows = pl.ds(pl.multiple_of(i * DIL_TILE, DIL_TILE), DIL_TILE)
        for j in range(n_tiles):
            cols = lambda part: slice(part * GROUP_W + j * LANES, part * GROUP_W + (j + 1) * LANES)
            q = _pair_norm(x_ref[rows, cols(0)], gq_ref[...], lo_tile)
            q_scr[j, rows, :] = q * (HEAD_DIM ** -0.5)
            k_scr[j, rows, :] = _pair_norm(x_ref[rows, cols(1)], gk_ref[...], lo_tile)
            v_scr[j, rows, :] = x_ref[rows, cols(2)]
        return carry
    lax.fori_loop(0, seq // DIL_TILE, norm_tile, 0)

    for bi, (window, dil) in enumerate(DIL_PAIRS):
        sub_len = seq // dil
        nb = sub_len // span

        def block_group(i, carry, bi=bi, nb=nb, dil=dil):
            todo = []
            for u in range(DIL_GROUP):
                t = i * DIL_GROUP + u
                r = lax.shift_right_logical(t, int(math.log2(nb)))
                c = t & (nb - 1)
                first = r + c * (dil * span)
                stride = dil if dil > 1 else None
                rows = pl.ds(first, span, stride=stride)
                prev = pl.ds(jnp.maximum(first - dil * span, r), span, stride=stride)
                keep = jnp.logical_or(in_current, c != 0) if nb > 1 else None
                for j in range(n_tiles):
                    q2 = q_scr[j, rows, :]
                    if nb > 1:
                        k_cat = jnp.concatenate([k_scr[j, prev, :], k_scr[j, rows, :]], axis=0).astype(BF16)
                        v_cat = jnp.concatenate([v_scr[j, prev, :], v_scr[j, rows, :]], axis=0).astype(BF16)
                    else:
                        k_cat = k_scr[j, rows, :].astype(BF16)
                        v_cat = v_scr[j, rows, :].astype(BF16)
                    todo.append((rows, j, q2, k_cat, v_cat, keep))
            logits = []
            for rows, j, q2, k_cat, v_cat, keep in todo:
                for a in range(2):
                    qa = jnp.where(lo_blk if a == 0 else jnp.logical_not(lo_blk), q2, 0.0).astype(BF16)
                    bias = bias_ref[bi * N_HEADS + 2 * j + a]
                    lg = _dot_nt(qa, k_cat) + (bias if nb > 1 else bias[:, span:])
                    logits.append(jnp.where(keep, lg, NEG_INF) if nb > 1 else lg)
            m = [jnp.max(lg, axis=-1, keepdims=True) for lg in logits]
            p = [jnp.exp(lg - mx) for lg, mx in zip(logits, m)]
            l = [jnp.sum(px, axis=-1, keepdims=True) for px in p]
            o = [_dot(p[2 * n + a].astype(BF16), todo[n][4]) for n in range(len(todo)) for a in range(2)]
            for n, (rows, j, *_) in enumerate(todo):
                o_in = jnp.where(lo_blk, o[2 * n], o[2 * n + 1])
                m_in = jnp.where(lo_blk, m[2 * n], m[2 * n + 1])
                l_in = jnp.where(lo_blk, l[2 * n], l[2 * n + 1])
                if bi > 0:
                    m_old = ma[j, rows, :]
                    m_new = jnp.maximum(m_old, m_in)
                    w_old, w_in = jnp.exp(m_old - m_new), jnp.exp(m_in - m_new)
                    o_in = w_old * oa[j, rows, :] + w_in * o_in
                    l_in = w_old * la[j, rows, :] + w_in * l_in
                    m_in = m_new
                oa[j, rows, :] = o_in
                ma[j, rows, :] = m_in
                la[j, rows, :] = l_in
            return carry
        lax.fori_loop(0, n_blocks // DIL_GROUP, block_group, 0)

    def finish(i, carry):
        rows = pl.ds(pl.multiple_of(i * DIL_TILE, DIL_TILE), DIL_TILE)
        for j in range(n_tiles):
            o_ref[rows, j * LANES:(j + 1) * LANES] = oa[j, rows, :] / la[j, rows, :]
        return carry
    lax.fori_loop(0, seq // DIL_TILE, finish, 0)


def _dil(slab, gq2, gk2, band, *, batch, seq):
    big = lambda: pltpu.VMEM((GROUP_W // LANES, seq, LANES), F32)
    n_bias = len(DIL_PAIRS) * N_HEADS
    return pl.pallas_call(
        functools.partial(_dil_body, seq=seq),
        grid=(batch,),
        in_specs=[pl.BlockSpec((seq, DIL_SLAB), lambda b: (0, b)),
                  _const_spec((1, LANES)), _const_spec((1, LANES)),
                  _const_spec(band.shape)],
        out_specs=pl.BlockSpec((seq, GROUP_W), lambda b: (0, b)),
        out_shape=jax.ShapeDtypeStruct((seq, batch * GROUP_W), F32),
        scratch_shapes=[pltpu.VMEM((n_bias, DIL_SPAN, 2 * DIL_SPAN), F32)] + [big() for _ in range(6)],
        compiler_params=_params("parallel"),
        name="dilated",
    )(slab, gq2, gk2, band)


def _t5_bucket(dist):
    exact = T5_BUCKETS // 2
    df = jnp.maximum(dist, 1).astype(F32)
    large = exact + (jnp.log(df / exact) / math.log(T5_MAX_DIST / exact) * (T5_BUCKETS - exact)).astype(jnp.int32)
    large = jnp.minimum(large, T5_BUCKETS - 1)
    return jnp.where(dist < exact, dist, large)


def _dil_band(table):
    span = DIL_SPAN
    delta = span - jnp.arange(2 * span, dtype=jnp.int32)
    rows = []
    for _, dil in DIL_PAIRS:
        bucket = _t5_bucket(jnp.clip(delta, 0, span) * dil)
        onehot = (bucket[:, None] == jnp.arange(T5_BUCKETS, dtype=jnp.int32)[None, :]).astype(F32)
        vals = jnp.dot(onehot, table.astype(F32), precision=HIGHEST)
        rows.append(jnp.where((delta >= 0)[:, None], vals, NEG_INF).T)
    return jnp.concatenate(rows, axis=0)


DN_TILE = 256
DN_GROUP = 4
PREP_STORE_DELAY = 12


def _softplus(x):
    return jnp.maximum(x, 0.0) + jnp.log1p(jnp.exp(-jnp.abs(x)))


def _pair_l2(x, lo_half):
    sq = x * x
    tot = jnp.sum(sq, axis=-1, keepdims=True)
    lo = jnp.sum(jnp.where(lo_half, sq, 0.0), axis=-1, keepdims=True)
    return x * lax.rsqrt(jnp.where(lo_half, lo, tot - lo) + EPS)


def _dn_body(x_ref, cw_ref, alog_ref, dtb_ref, on_ref, o_ref, q_scr, k_scr, v_scr, g_scr, b_scr, w_scr, a_scr,
             s_scr, *, seq):
    c = DN_CHUNK
    w = GROUP_W
    lo_tile = lax.broadcasted_iota(jnp.int32, (DN_TILE, LANES), 1) < HEAD_DIM

    def prep_steps(i):
        if isinstance(i, int):
            first, halo_first = i * DN_TILE, max(i * DN_TILE - 8, 0)
        else:
            first = pl.multiple_of(i * DN_TILE, DN_TILE)
            halo_first = pl.multiple_of(jnp.maximum(first - 8, 0), 8)
        rows = pl.ds(first, DN_TILE)
        cur = x_ref[rows, 0:3 * w]
        halo = jnp.where(i > 0, x_ref[pl.ds(halo_first, 8), 0:3 * w], 0.0)
        ext = jnp.concatenate([halo, cur], axis=0)
        acc = cw_ref[DN_CONV - 1:DN_CONV, :] * cur
        for j in range(DN_CONV - 1):
            acc = acc + cw_ref[j:j + 1, :] * pltpu.roll(ext, DN_CONV - 1 - j, 0)[8:, :]
        y = acc * jax.nn.sigmoid(acc)
        yield
        tiles = range(w // LANES)
        cols = lambda part, j: slice(part * w + j * LANES, part * w + (j + 1) * LANES)
        q = [_pair_l2(y[:, cols(0, j)], lo_tile) * (HEAD_DIM ** -0.5) for j in tiles]
        k = [_pair_l2(y[:, cols(1, j)], lo_tile) for j in tiles]
        yield
        ab = x_ref[rows, 3 * w:3 * w + LANES]
        g = -jnp.exp(alog_ref[...]) * _softplus(ab + dtb_ref[...])
        gc = _dot_split(chunk_tril, g, 3, split_rhs=True)
        yield
        gc = per_head_lanes(gc, 0)
        beta = per_head_lanes(jax.nn.sigmoid(ab), N_HEADS)
        for _ in range(PREP_STORE_DELAY):
            yield
        for j in tiles:
            q_scr[rows, j * LANES:(j + 1) * LANES] = q[j]
            k_scr[rows, j * LANES:(j + 1) * LANES] = k[j]
            v_scr[rows, j * LANES:(j + 1) * LANES] = y[:, cols(2, j)]
        g_scr[rows, :] = gc
        b_scr[rows, :] = beta

    ri = lax.broadcasted_iota(jnp.int32, (c, w), 0)
    ci = lax.broadcasted_iota(jnp.int32, (c, w), 1)
    cj = jnp.bitwise_and(ci, HEAD_DIM - 1)
    causal = ri >= cj
    strict = ri > cj
    eye4 = (ri == cj).astype(F32)
    same_sub = lax.shift_right_logical(ri, 4) == lax.shift_right_logical(cj, 4)
    bi_r = lax.broadcasted_iota(jnp.int32, (w, w), 0)
    bi_c = lax.broadcasted_iota(jnp.int32, (w, w), 1)
    same_head = lax.shift_right_logical(bi_r, 6) == lax.shift_right_logical(bi_c, 6)
    block_mask = same_head.astype(F32)
    chunk_tril = (same_head & (jnp.bitwise_and(bi_r, c - 1) >= jnp.bitwise_and(bi_c, c - 1))).astype(F32)
    head_of_lane = lax.shift_right_logical(lax.broadcasted_iota(jnp.int32, (DN_TILE, w), 1), 6)

    def per_head_lanes(x, first_lane):
        out = None
        for h in range(N_HEADS):
            col = jnp.broadcast_to(x[:, first_lane + h:first_lane + h + 1], (DN_TILE, w))
            out = col if out is None else jnp.where(head_of_lane == h, col, out)
        return out

    def per_head(a, b):
        bd = jnp.where(same_head, jnp.concatenate([b.astype(BF16)] * N_HEADS, axis=0), 0.0)
        return _dot(a.astype(BF16), bd)

    def group_rows(i):
        base = pl.multiple_of(i * (DN_GROUP * c), DN_GROUP * c)
        return [pl.ds(base + u * c, c) for u in range(DN_GROUP)]

    def solve_steps(i):
        rows = group_rows(i)
        each = lambda f, *lists: [f(*args) for args in zip(*lists)]
        q, k, v, gc, beta = ([ref[r, :] for r in rows] for ref in (q_scr, k_scr, v_scr, g_scr, b_scr))
        g_row = each(lambda g: jnp.sum(g * eye4, axis=0, keepdims=True), gc)
        decay = each(lambda g, gr: jnp.exp(jnp.where(causal, g - gr, NEG_INF)), gc, g_row)
        kb = each(jnp.multiply, k, beta)
        k_bd = each(lambda x: jnp.where(same_head, jnp.concatenate([x.astype(BF16)] * N_HEADS, axis=0), 0.0), k)
        lmat = each(lambda a, b, d: jnp.where(strict, _dot_nt(a.astype(BF16), b) * d, 0.0), kb, k_bd, decay)
        yield
        a_qk = each(lambda a, b, d: jnp.where(causal, _dot_nt(a.astype(BF16), b) * d, 0.0), q, k_bd, decay)
        yield
        p = each(lambda l_: jnp.where(same_sub, -l_, 0.0), lmat)
        t_diag = each(lambda x: eye4 + x, p)
        for _ in range(3):
            p = each(per_head, p, p)
            yield
            t_diag = each(lambda t, x: t + per_head(t, x), t_diag, p)
            yield
        nil = each(lambda t, l_: per_head(t, jnp.where(same_sub, 0.0, l_)), t_diag, lmat)
        yield
        nil2 = each(per_head, nil, nil)
        yield
        nil3 = each(per_head, nil, nil2)
        yield
        t_inv = each(lambda n1, n2, n3, t: per_head(eye4 - n1 + n2 - n3, t), nil, nil2, nil3, t_diag)
        yield
        eg = each(jnp.exp, gc)
        w_c = each(lambda t, a, e: per_head(t, a * e), t_inv, kb, eg)
        yield
        u_c = each(lambda t, a, b: per_head(t, a * b), t_inv, v, beta)
        yield
        q_dec = each(jnp.multiply, q, eg)
        k_dec = each(lambda x, g: x * jnp.exp(g[c - 1:c, :] - g), k, gc)
        for ref, vals in zip((w_scr, v_scr, a_scr, q_scr, k_scr), (w_c, u_c, a_qk, q_dec, k_dec)):
            for r, val in zip(rows, vals):
                ref[r, :] = val

    def state_steps(i):
        for rows in group_rows(i):
            state = s_scr[...]
            state_b = state.astype(BF16)
            w_s = _dot(w_scr[rows, :].astype(BF16), state_b)
            q_s = _dot(q_scr[rows, :].astype(BF16), state_b)
            yield
            v_new = v_scr[rows, :] - w_s
            o_intra = per_head(a_scr[rows, :], v_new)
            upd = lax.dot_general(k_scr[rows, :].astype(BF16), v_new.astype(BF16), (((0,), (0,)), ((), ())),
                                  preferred_element_type=F32)
            yield
            o_ref[rows, :] = q_s + o_intra
            s_scr[...] = state * jnp.exp(g_scr[rows, :][c - 1:c, :]) + upd * block_mask
            yield

    def weave(*steps):
        live = list(steps)
        while live:
            for gen in list(live):
                if next(gen, "done") == "done":
                    live.remove(gen)

    n_groups = seq // (DN_GROUP * c)
    s_scr[...] = jnp.zeros_like(s_scr)
    weave(prep_steps(0))
    weave(solve_steps(0), prep_steps(1))

    def group(i, carry):
        weave(solve_steps(i), state_steps(i - 1), prep_steps(i + 1))
        return carry
    lax.fori_loop(1, n_groups - 1, group, 0)
    weave(solve_steps(n_groups - 1), state_steps(n_groups - 2))
    weave(state_steps(n_groups - 1))

    def finish_tile(i, carry):
        rows = pl.ds(pl.multiple_of(i * DN_TILE, DN_TILE), DN_TILE)
        o = o_ref[rows, :]
        ms = _dot_split(o * o, block_mask, 2) * (1.0 / HEAD_DIM)
        gate = x_ref[rows, 3 * w + LANES:4 * w + LANES]
        o_ref[rows, :] = o * lax.rsqrt(ms + EPS) * on_ref[...] * (gate * jax.nn.sigmoid(gate))
        return carry
    lax.fori_loop(0, seq // DN_TILE, finish_tile, 0)


def _dn(slab, conv_w, a_log, dt_bias, o_norm, *, batch, seq):
    wide = lambda: pltpu.VMEM((seq, GROUP_W), F32)
    return pl.pallas_call(
        functools.partial(_dn_body, seq=seq),
        grid=(batch,),
        in_specs=[pl.BlockSpec((seq, DN_SLAB), lambda b: (0, b)),
                  _const_spec((DN_CONV, 3 * GROUP_W)), _const_spec((1, LANES)), _const_spec((1, LANES)),
                  _const_spec((1, GROUP_W))],
        out_specs=pl.BlockSpec((seq, GROUP_W), lambda b: (0, b)),
        out_shape=jax.ShapeDtypeStruct((seq, batch * GROUP_W), F32),
        scratch_shapes=[wide() for _ in range(7)] + [pltpu.VMEM((GROUP_W, GROUP_W), F32)],
        compiler_params=_params("parallel"),
        name="deltanet",
    )(slab, conv_w, a_log, dt_bias, o_norm)


def _row(v, width=None):
    v = v.astype(F32).reshape(1, -1)
    if width is not None and v.shape[1] < width:
        v = jnp.pad(v, ((0, 0), (0, width - v.shape[1])))
    return v


def _prep_w_in(w):
    w = w.astype(BF16)
    z = lambda n: jnp.zeros((D_MODEL, n), BF16)
    return jnp.concatenate([w[:, 0:416], z(96), w[:, 416:2216], z(120), w[:, 2216:2472]], axis=1)


def _pad_heads(w, per_head, lo, hi):
    k = w.shape[0]
    w = w.reshape(k, N_HEADS, per_head)[:, :, lo:hi]
    w = jnp.pad(w, ((0, 0), (0, 0), (0, LANES - (hi - lo))))
    return w.reshape(k, N_HEADS * LANES).astype(BF16)


def _rope_tables(seq):
    half = MLA_ROPE // 2
    pos = jnp.arange(seq, dtype=F32)
    freqs = ROPE_THETA ** (-jnp.arange(half, dtype=F32) / half)
    ang = pos[:, None] * freqs[None, :]
    cos, sin = jnp.cos(ang), jnp.sin(ang)
    ones = jnp.ones((seq, MLA_NOPE), F32)
    zeros = jnp.zeros((seq, MLA_NOPE), F32)
    tail1 = jnp.ones((seq, LANES - MLA_DQK), F32)
    tail0 = jnp.zeros((seq, LANES - MLA_DQK), F32)
    return (jnp.concatenate([ones, cos, cos, tail1], axis=1),
            jnp.concatenate([zeros, -sin, sin, tail0], axis=1))


def _block_diag(blocks):
    g, r, c = blocks.shape
    eye = jnp.eye(g, dtype=blocks.dtype)
    return (blocks[:, :, None, :] * eye[:, None, :, None]).reshape(g * r, g * c)


def _mla_layer(slab, p, l, *, batch, seq):
    w_ukv = p["mla_w_ukv"][l]
    cos, sin = _rope_tables(seq)
    return _mla(slab, _row(p["mla_q_norm"][l]), _row(p["mla_kv_norm"][l]),
                _pad_heads(p["mla_w_uq"][l], MLA_DQK, 0, MLA_DQK),
                _pad_heads(w_ukv, MLA_NOPE + HEAD_DIM, 0, MLA_NOPE),
                w_ukv.reshape(MLA_KV_RANK, N_HEADS, MLA_NOPE + HEAD_DIM)[:, :, MLA_NOPE:]
                .reshape(MLA_KV_RANK, GROUP_W).astype(BF16),
                _row(p["mla_qk_q"][l], LANES), _row(p["mla_qk_k"][l], LANES), cos, sin, batch=batch, seq=seq)


def _dil_layer(slab, p, l, *, batch, seq):
    pair = lambda g: jnp.tile(g.astype(F32).reshape(1, HEAD_DIM), (1, LANES // HEAD_DIM))
    return _dil(slab, pair(p["dil_q_norm"][l]), pair(p["dil_k_norm"][l]), _dil_band(p["t5_bias"]),
                batch=batch, seq=seq)


def _dn_layer(slab, p, l, *, batch, seq):
    return _dn(slab, p["dn_conv"][l].astype(F32), _row(p["dn_a_log"][l], LANES), _row(p["dn_dt_bias"][l], LANES),
               jnp.tile(p["dn_o_norm"][l].astype(F32).reshape(1, HEAD_DIM), (1, N_HEADS)), batch=batch, seq=seq)


def _s5_layer(u, p, l, *, batch, seq):
    state_row = lambda v: v.astype(F32).reshape(1, S5_WIDTH)
    ldt = jnp.broadcast_to(p["s5_log_dt"][l][:, None], (S5_GROUPS, S5_STATE))
    bre = _block_diag(jnp.swapaxes(p["s5_b_re"][l], 1, 2).astype(F32))
    bim = _block_diag(jnp.swapaxes(p["s5_b_im"][l], 1, 2).astype(F32))
    cre = _block_diag(jnp.swapaxes(p["s5_c_re"][l], 1, 2)).astype(BF16)
    cim = _block_diag(jnp.swapaxes(p["s5_c_im"][l], 1, 2)).astype(BF16)
    return _s5(u, state_row(p["s5_lambda_re"][l]), state_row(p["s5_lambda_im"][l]), state_row(ldt),
               bre, bim, cre, cim, _row(p["s5_d"][l]), p["s5_w_glu"][l].astype(BF16), batch=batch, seq=seq,
               steps=min(256, seq))


def kernel(x, attn_norm, w_in, w_out, mla_q_norm, mla_kv_norm, mla_w_uq, mla_w_ukv, mla_qk_q, mla_qk_k,
           s5_lambda_re, s5_lambda_im, s5_log_dt, s5_b_re, s5_b_im, s5_c_re, s5_c_im, s5_d, s5_w_glu,
           dil_q_norm, dil_k_norm, t5_bias, dn_conv, dn_a_log, dn_dt_bias, dn_o_norm,
           ffn_norm, ffn_w1, ffn_w3, ffn_w2):
    p = dict(mla_q_norm=mla_q_norm, mla_kv_norm=mla_kv_norm, mla_w_uq=mla_w_uq, mla_w_ukv=mla_w_ukv,
             mla_qk_q=mla_qk_q, mla_qk_k=mla_qk_k, s5_lambda_re=s5_lambda_re, s5_lambda_im=s5_lambda_im,
             s5_log_dt=s5_log_dt, s5_b_re=s5_b_re, s5_b_im=s5_b_im, s5_c_re=s5_c_re, s5_c_im=s5_c_im, s5_d=s5_d,
             s5_w_glu=s5_w_glu, dil_q_norm=dil_q_norm, dil_k_norm=dil_k_norm, t5_bias=t5_bias, dn_conv=dn_conv,
             dn_a_log=dn_a_log, dn_dt_bias=dn_dt_bias, dn_o_norm=dn_o_norm)
    batch, seq, _ = x.shape
    h = x.reshape(batch * seq, D_MODEL)
    for l in range(attn_norm.shape[0]):
        mla_in, s5_in, dil_in, dn_in = _proj(h, _row(attn_norm[l]), _prep_w_in(w_in[l]), batch=batch, seq=seq)
        ys = [_mla_layer(mla_in, p, l, batch=batch, seq=seq),
              _s5_layer(s5_in, p, l, batch=batch, seq=seq),
              _dil_layer(dil_in, p, l, batch=batch, seq=seq),
              _dn_layer(dn_in, p, l, batch=batch, seq=seq)]
        h = _out_ffn(h, ys, w_out[l].astype(BF16), _row(ffn_norm[l]),
                     ffn_w1[l].astype(BF16), ffn_w3[l].astype(BF16), ffn_w2[l].astype(BF16), batch=batch, seq=seq)
    return h.reshape(batch, seq, D_MODEL)
```

```python
import functools
import math

import jax
import jax.numpy as jnp
import numpy as np
from jax import lax
from jax.experimental import pallas as pl
from jax.experimental.pallas import tpu as pltpu

F32 = jnp.float32
BF16 = jnp.bfloat16
HIGHEST = lax.Precision.HIGHEST

D_MODEL = 1024
GROUP_W = 256
HEAD_DIM = 64
N_HEADS = 4
EPS = 1e-6
NEG_INF = -1e30

MLA_NOPE = 64
MLA_ROPE = 32
MLA_DQK = MLA_NOPE + MLA_ROPE
MLA_KV_RANK = 128
ROPE_THETA = 10000.0

S5_GROUP_CH = 16
S5_GROUPS = 16
S5_STATE = 64
S5_WIDTH = S5_GROUPS * S5_STATE

DIL_PAIRS = ((128, 1), (512, 4), (2048, 16))
DIL_SPAN = 128
T5_BUCKETS = 32
T5_MAX_DIST = 2048

DN_CONV = 4
DN_CHUNK = 64

FFN_HIDDEN = 2816
FFN_CHUNK = 2816

VMEM_LIMIT_BYTES = 56 * 1024 * 1024
LANES = 128

MLA_SLAB = 512
S5_SLAB = 256
DIL_SLAB = 768
DN_SLAB = 1152
PROJ_COLS = MLA_SLAB + S5_SLAB + DIL_SLAB + DN_SLAB


def _dot(a, b, precision=None):
    return jnp.dot(a, b, preferred_element_type=F32, precision=precision)


def _dot_nt(a, b, precision=None):
    return lax.dot_general(a, b, (((1,), (1,)), ((), ())), preferred_element_type=F32, precision=precision)


def _dot_split(a, b, terms, split_rhs=False):
    x = b if split_rhs else a
    mask = (a if split_rhs else b).astype(BF16)
    out = None
    for _ in range(terms):
        piece = x.astype(BF16)
        part = _dot(mask, piece) if split_rhs else _dot(piece, mask)
        out = part if out is None else out + part
        x = x - piece.astype(F32)
    return out


def _const_spec(shape):
    nd = len(shape)
    return pl.BlockSpec(shape, lambda *_: (0,) * nd, pipeline_mode=pl.Buffered(1))


def _params(*sem):
    return pltpu.CompilerParams(dimension_semantics=sem, vmem_limit_bytes=VMEM_LIMIT_BYTES)


def _proj_body(x_ref, g_ref, w_ref, mla_ref, s5_ref, dil_ref, dn_ref):
    x = x_ref[...]
    n = x * lax.rsqrt(jnp.mean(x * x, axis=-1, keepdims=True) + EPS) * g_ref[...]
    nb = n.astype(BF16)
    start = 0
    for ref in (mla_ref, s5_ref, dil_ref, dn_ref):
        width = ref.shape[-1]
        ref[...] = _dot(nb, w_ref[:, start:start + width])
        start += width


def _layer_spec(shape, layer):
    nd = len(shape)
    return pl.BlockSpec((None,) + tuple(shape), lambda *_: (layer,) + (0,) * nd, pipeline_mode=pl.Buffered(1))


def _proj(h, gain, w_big, layer, *, batch, seq, tm=512):
    nt = seq // tm
    widths = (MLA_SLAB, S5_SLAB, DIL_SLAB, DN_SLAB)
    return pl.pallas_call(
        _proj_body,
        grid=(batch, nt),
        in_specs=[pl.BlockSpec((tm, D_MODEL), lambda b, i: (b * nt + i, 0)),
                  _const_spec((1, D_MODEL)),
                  _layer_spec((D_MODEL, PROJ_COLS), layer)],
        out_specs=[pl.BlockSpec((tm, w), lambda b, i: (i, b)) for w in widths],
        out_shape=[jax.ShapeDtypeStruct((seq, batch * w), F32) for w in widths],
        compiler_params=_params("parallel", "parallel"),
        name="proj",
    )(h, gain, w_big)


def _out_ffn_body(h_ref, y0_ref, y1_ref, y2_ref, y3_ref, wo_ref, g_ref, w1_ref, w3_ref, w2_ref, o_ref, acc_ref):
    h = h_ref[...]
    for i, y_ref in enumerate((y0_ref, y1_ref, y2_ref, y3_ref)):
        h = h + _dot(y_ref[...].astype(BF16), wo_ref[i * GROUP_W:(i + 1) * GROUP_W, :])
    n = h * lax.rsqrt(jnp.mean(h * h, axis=-1, keepdims=True) + EPS) * g_ref[...]
    nb = n.astype(BF16)
    acc_ref[...] = h

    def hidden_chunk(c, carry):
        cols = pl.ds(pl.multiple_of(c * FFN_CHUNK, FFN_CHUNK), FFN_CHUNK)
        a = _dot(nb, w1_ref[:, cols])
        b = _dot(nb, w3_ref[:, cols])
        z = (a * jax.nn.sigmoid(a) * b).astype(BF16)
        acc_ref[...] += _dot(z, w2_ref[cols, :])
        return carry
    lax.fori_loop(0, FFN_HIDDEN // FFN_CHUNK, hidden_chunk, 0)
    o_ref[...] = acc_ref[...]


def _out_ffn(h, ys, w_out, gain, w1, w3, w2, layer, *, batch, seq, tm=512):
    nt = seq // tm
    row = pl.BlockSpec((tm, D_MODEL), lambda b, i: (b * nt + i, 0))
    slab = pl.BlockSpec((tm, GROUP_W), lambda b, i: (i, b))
    return pl.pallas_call(
        _out_ffn_body,
        grid=(batch, nt),
        in_specs=[row] + [slab] * 4 + [
            _layer_spec((D_MODEL, D_MODEL), layer), _const_spec((1, D_MODEL)),
            _layer_spec((D_MODEL, FFN_HIDDEN), layer), _layer_spec((D_MODEL, FFN_HIDDEN), layer),
            _layer_spec((FFN_HIDDEN, D_MODEL), layer)],
        out_specs=row,
        out_shape=jax.ShapeDtypeStruct((batch * seq, D_MODEL), F32),
        scratch_shapes=[pltpu.VMEM((tm, D_MODEL), F32)],
        compiler_params=_params("parallel", "parallel"),
        name="out_ffn",
    )(h, *ys, w_out, gain, w1, w3, w2)


MLA_BLOCK = 256
MLA_SWEEP = 2


def _mla_body(x_ref, qn_ref, kvn_ref, wuq_ref, wuk_ref, wuv_ref, gq_ref, gk_ref, cos_ref, sin_ref,
              o_ref, k_scr, vt_scr, q_scr, *, seq):
    blk = MLA_BLOCK
    n_pairs = N_HEADS // 2
    qi = pl.program_id(1)
    lane = lax.broadcasted_iota(jnp.int32, (blk, LANES), 1)

    def rope(x, c, s):
        rot = jnp.where(lane < MLA_NOPE + MLA_ROPE // 2, pltpu.roll(x, LANES - MLA_ROPE // 2, 1),
                        pltpu.roll(x, MLA_ROPE // 2, 1))
        return x * c + rot * s

    def norm_head(x, g):
        ssq = jnp.sum(x * x, axis=-1, keepdims=True)
        return x * lax.rsqrt(ssq * (1.0 / MLA_DQK) + EPS) * g

    heads = range(N_HEADS)

    def prepare_steps(i):
        rows = pl.ds(i * blk if isinstance(i, int) else pl.multiple_of(i * blk, blk), blk)
        ckv = x_ref[rows, 256:384]
        cq = x_ref[rows, 0:256]
        kvn = ckv * lax.rsqrt(jnp.mean(ckv * ckv, axis=-1, keepdims=True) + EPS) * kvn_ref[...]
        qn = cq * lax.rsqrt(jnp.mean(cq * cq, axis=-1, keepdims=True) + EPS) * qn_ref[...]
        kvn = kvn.astype(BF16)
        k_nope = _dot(kvn, wuk_ref[...])
        v = _dot(kvn, wuv_ref[...])
        q_all = _dot(qn.astype(BF16), wuq_ref[...])
        yield
        k_rope = pltpu.roll(x_ref[rows, 384:512], MLA_NOPE, 1)
        c = cos_ref[rows, :]
        s = sin_ref[rows, :]
        ks = [rope(norm_head(k_nope[:, h * LANES:(h + 1) * LANES] + k_rope, gk_ref[...]), c, s) for h in heads]
        yield
        qs_new = [rope(norm_head(q_all[:, h * LANES:(h + 1) * LANES], gq_ref[...]), c, s) * (MLA_DQK ** -0.5)
                  for h in heads]
        vts = [v[:, pr * LANES:(pr + 1) * LANES].T for pr in range(n_pairs)]
        yield
        yield
        for h in heads:
            k_scr[h, rows, :] = ks[h].astype(BF16)
            q_scr[h] = qs_new[h].astype(BF16)
        for pr in range(n_pairs):
            vt_scr[pr, :, rows] = vts[pr].astype(BF16)

    def weave(*steps):
        live = list(steps)
        while live:
            for gen in list(live):
                if next(gen, "done") == "done":
                    live.remove(gen)

    @pl.when(qi == 0)
    def _first_block():
        weave(prepare_steps(0))

    rows = pl.ds(pl.multiple_of(qi * blk, blk), blk)
    key_pos = lax.broadcasted_iota(jnp.int32, (blk, blk), 0)
    query_pos = lax.broadcasted_iota(jnp.int32, (blk, blk), 1)
    first_of_pair = lax.broadcasted_iota(jnp.int32, (LANES, blk), 0) < HEAD_DIM
    qs = [q_scr[h] for h in heads]

    def kv_steps(first_key, width, carry, result, diagonal=False):
        m_old, l_old, acc_old = carry
        keys = pl.ds(pl.multiple_of(first_key, blk), width)
        logits = [_dot_nt(k_scr[h, keys, :], qs[h]) for h in heads]
        yield
        if diagonal:
            logits = [jnp.where(key_pos <= query_pos, x, NEG_INF) for x in logits]
        m_new = [jnp.maximum(m_old[h], jnp.max(logits[h], axis=0, keepdims=True)) for h in heads]
        alpha = [jnp.exp(m_old[h] - m_new[h]) for h in heads]
        p = [jnp.exp(logits[h] - m_new[h]) for h in heads]
        l_new = [alpha[h] * l_old[h] + jnp.sum(p[h], axis=0, keepdims=True) for h in heads]
        pv = [_dot(vt_scr[h // 2, :, keys], p[h].astype(BF16)) for h in heads]
        yield
        acc_new = [jnp.where(first_of_pair, alpha[2 * pr] * acc_old[pr] + pv[2 * pr],
                             alpha[2 * pr + 1] * acc_old[pr] + pv[2 * pr + 1]) for pr in range(n_pairs)]
        result.append((tuple(m_new), tuple(l_new), tuple(acc_new)))

    def kv_step(first_key, width, carry):
        result = []
        weave(kv_steps(first_key, width, carry, result))
        return result[0]

    def last_steps(carry):
        result = []
        yield from kv_steps(qi * blk, blk, carry, result, diagonal=True)
        _, l, acc = result[0]
        for pr in range(n_pairs):
            out_t = acc[pr] / jnp.where(first_of_pair, l[2 * pr], l[2 * pr + 1])
            o_ref[rows, pr * LANES:(pr + 1) * LANES] = out_t.T

    wide = MLA_SWEEP * blk
    carry = (tuple(jnp.full((1, blk), NEG_INF, F32) for _ in heads), tuple(jnp.zeros((1, blk), F32) for _ in heads),
             tuple(jnp.zeros((LANES, blk), F32) for _ in range(n_pairs)))
    n_wide = qi // MLA_SWEEP
    carry = lax.fori_loop(0, n_wide, lambda j, cr: kv_step(j * wide, wide, cr), carry)
    carry = lax.fori_loop(n_wide * MLA_SWEEP, qi, lambda j, cr: kv_step(j * blk, blk, cr), carry)

    @pl.when(qi + 1 < seq // blk)
    def _last_and_prepare_next():
        weave(last_steps(carry), prepare_steps(qi + 1))

    @pl.when(qi + 1 == seq // blk)
    def _last():
        weave(last_steps(carry))


def _mla(slab, qn, kvn, wuq, wuk, wuv, gq, gk, cos, sin, *, batch, seq):
    nq = seq // MLA_BLOCK
    return pl.pallas_call(
        functools.partial(_mla_body, seq=seq),
        grid=(batch, nq),
        in_specs=[pl.BlockSpec((seq, MLA_SLAB), lambda b, i: (0, b)),
                  _const_spec((1, 256)), _const_spec((1, MLA_KV_RANK)),
                  _const_spec((256, N_HEADS * LANES)), _const_spec((MLA_KV_RANK, N_HEADS * LANES)),
                  _const_spec((MLA_KV_RANK, GROUP_W)),
                  _const_spec((1, LANES)), _const_spec((1, LANES)),
                  _const_spec((seq, LANES)), _const_spec((seq, LANES))],
        out_specs=pl.BlockSpec((seq, GROUP_W), lambda b, i: (0, b)),
        out_shape=jax.ShapeDtypeStruct((seq, batch * GROUP_W), F32),
        scratch_shapes=[pltpu.VMEM((N_HEADS, seq, LANES), BF16), pltpu.VMEM((N_HEADS // 2, LANES, seq), BF16),
                        pltpu.VMEM((N_HEADS, MLA_BLOCK, LANES), BF16)],
        compiler_params=_params("parallel", "arbitrary"),
        name="mla",
    )(slab, qn, kvn, wuq, wuk, wuv, gq, gk, cos, sin)


def _s5_body(u_ref, lre_ref, lim_ref, ldt_ref, bre_ref, bim_ref, cre_ref, cim_ref, d_ref, wglu_ref, o_ref,
             a_scr, bbar_scr, h_scr, x_scr, io_scr, *, batch, steps):
    io_tiles = GROUP_W // LANES

    @pl.when(pl.program_id(0) == 0)
    def _discretise():
        lr = lre_ref[...]
        li = lim_ref[...]
        dt = jnp.exp(ldt_ref[...])
        mag = jnp.exp(lr * dt)
        ar = mag * jnp.cos(li * dt)
        ai = mag * jnp.sin(li * dt)
        den = lr * lr + li * li
        nr = ar - 1.0
        zr = (nr * lr + ai * li) / den
        zi = (ai * lr - nr * li) / den
        a_scr[0] = jnp.broadcast_to(ar, (batch, S5_WIDTH))
        a_scr[1] = jnp.broadcast_to(ai, (batch, S5_WIDTH))
        bre = bre_ref[...]
        bim = bim_ref[...]
        bbar_scr[0] = (zr * bre - zi * bim).astype(BF16)
        bbar_scr[1] = (zr * bim + zi * bre).astype(BF16)
        h_scr[...] = jnp.zeros_like(h_scr)

    for b in range(batch):
        for j in range(io_tiles):
            io_scr[j, pl.ds(b, steps, stride=batch), :] = u_ref[:, b * GROUP_W + j * LANES:b * GROUP_W + (j + 1) * LANES]
    u = jnp.concatenate([io_scr[j] for j in range(io_tiles)], axis=1)
    ub = u.astype(BF16)
    x_scr[0] = _dot(ub, bbar_scr[0])
    x_scr[1] = _dot(ub, bbar_scr[1])
    ar = a_scr[0]
    ai = a_scr[1]

    def step(t, carry):
        hr, hi = carry
        rows = pl.ds(pl.multiple_of(t * batch, batch), batch)
        nhr = ar * hr - ai * hi + x_scr[0, rows, :]
        nhi = ar * hi + ai * hr + x_scr[1, rows, :]
        x_scr[0, rows, :] = nhr
        x_scr[1, rows, :] = nhi
        return nhr, nhi

    hr, hi = lax.fori_loop(0, steps, step, (h_scr[0], h_scr[1]), unroll=8)
    h_scr[0] = hr
    h_scr[1] = hi
    y = _dot(x_scr[0].astype(BF16), cre_ref[...]) - _dot(x_scr[1].astype(BF16), cim_ref[...]) + d_ref[...] * u
    z = _dot(y.astype(BF16), wglu_ref[...])
    out = z[:, :GROUP_W] * jax.nn.sigmoid(z[:, GROUP_W:])
    for j in range(io_tiles):
        io_scr[j] = out[:, j * LANES:(j + 1) * LANES]
    for b in range(batch):
        for j in range(io_tiles):
            o_ref[:, b * GROUP_W + j * LANES:b * GROUP_W + (j + 1) * LANES] = io_scr[j, pl.ds(b, steps, stride=batch), :]


def _s5(u, lre, lim, ldt, bre, bim, cre, cim, d, wglu, *, batch, seq, steps=256):
    state = lambda rows: pltpu.VMEM((2, rows, S5_WIDTH), F32)
    return pl.pallas_call(
        functools.partial(_s5_body, batch=batch, steps=steps),
        grid=(seq // steps,),
        in_specs=[pl.BlockSpec((steps, batch * S5_SLAB), lambda i: (i, 0)),
                  _const_spec((1, S5_WIDTH)), _const_spec((1, S5_WIDTH)), _const_spec((1, S5_WIDTH)),
                  _const_spec((GROUP_W, S5_WIDTH)), _const_spec((GROUP_W, S5_WIDTH)),
                  _const_spec((S5_WIDTH, GROUP_W)), _const_spec((S5_WIDTH, GROUP_W)),
                  _const_spec((1, GROUP_W)), _const_spec((GROUP_W, 2 * GROUP_W))],
        out_specs=pl.BlockSpec((steps, batch * GROUP_W), lambda i: (i, 0)),
        out_shape=jax.ShapeDtypeStruct((seq, batch * GROUP_W), F32),
        scratch_shapes=[state(batch), pltpu.VMEM((2, GROUP_W, S5_WIDTH), BF16), state(batch),
                        state(steps * batch), pltpu.VMEM((GROUP_W // LANES, steps * batch, LANES), F32)],
        compiler_params=_params("arbitrary"),
        name="s5",
    )(u, lre, lim, ldt, bre, bim, cre, cim, d, wglu)


DIL_TILE = 256
DIL_GROUP = 2


def _pair_norm(x, gain2, lo_half):
    sq = x * x
    tot = jnp.sum(sq, axis=-1, keepdims=True)
    lo = jnp.sum(jnp.where(lo_half, sq, 0.0), axis=-1, keepdims=True)
    ms = jnp.where(lo_half, lo, tot - lo) * (1.0 / HEAD_DIM)
    return x * lax.rsqrt(ms + EPS) * gain2


def _dil_body(x_ref, gq_ref, gk_ref, band_ref, o_ref, bias_ref, q_scr, k_scr, v_scr, oa, ma, la, *, seq):
    span = DIL_SPAN
    n_blocks = seq // span
    n_tiles = GROUP_W // LANES
    lo_tile = lax.broadcasted_iota(jnp.int32, (DIL_TILE, LANES), 1) < HEAD_DIM
    lo_blk = lax.broadcasted_iota(jnp.int32, (span, LANES), 1) < HEAD_DIM

    for bh in range(len(DIL_PAIRS) * N_HEADS):
        profile = jnp.broadcast_to(band_ref[bh:bh + 1, :], (span, 2 * span))
        bias_ref[bh] = pltpu.roll(profile, 0, 1, stride=1, stride_axis=0)
    in_current = lax.broadcasted_iota(jnp.int32, (span, 2 * span), 1) >= span

    def norm_tile(i, carry):
        rows = pl.ds(pl.multiple_of(i * DIL_TILE, DIL_TILE), DIL_TILE)
        for j in range(n_tiles):
            cols = lambda part: slice(part * GROUP_W + j * LANES, part * GROUP_W + (j + 1) * LANES)
            q = _pair_norm(x_ref[rows, cols(0)], gq_ref[...], lo_tile)
            q_scr[j, rows, :] = q * (HEAD_DIM ** -0.5)
            k_scr[j, rows, :] = _pair_norm(x_ref[rows, cols(1)], gk_ref[...], lo_tile)
            v_scr[j, rows, :] = x_ref[rows, cols(2)]
        return carry
    lax.fori_loop(0, seq // DIL_TILE, norm_tile, 0)

    for bi, (window, dil) in enumerate(DIL_PAIRS):
        sub_len = seq // dil
        nb = sub_len // span

        def block_group(i, carry, bi=bi, nb=nb, dil=dil):
            todo = []
            for u in range(DIL_GROUP):
                t = i * DIL_GROUP + u
                r = lax.shift_right_logical(t, int(math.log2(nb)))
                c = t & (nb - 1)
                first = r + c * (dil * span)
                stride = dil if dil > 1 else None
                rows = pl.ds(first, span, stride=stride)
                prev = pl.ds(jnp.maximum(first - dil * span, r), span, stride=stride)
                keep = jnp.logical_or(in_current, c != 0) if nb > 1 else None
                for j in range(n_tiles):
                    q2 = q_scr[j, rows, :]
                    if nb > 1:
                        k_cat = jnp.concatenate([k_scr[j, prev, :], k_scr[j, rows, :]], axis=0).astype(BF16)
                        v_cat = jnp.concatenate([v_scr[j, prev, :], v_scr[j, rows, :]], axis=0).astype(BF16)
                    else:
                        k_cat = k_scr[j, rows, :].astype(BF16)
                        v_cat = v_scr[j, rows, :].astype(BF16)
                    todo.append((rows, j, q2, k_cat, v_cat, keep))
            logits = []
            for rows, j, q2, k_cat, v_cat, keep in todo:
                for a in range(2):
                    qa = jnp.where(lo_blk if a == 0 else jnp.logical_not(lo_blk), q2, 0.0).astype(BF16)
                    bias = bias_ref[bi * N_HEADS + 2 * j + a]
                    lg = _dot_nt(qa, k_cat) + (bias if nb > 1 else bias[:, span:])
                    logits.append(jnp.where(keep, lg, NEG_INF) if nb > 1 else lg)
            m = [jnp.max(lg, axis=-1, keepdims=True) for lg in logits]
            p = [jnp.exp(lg - mx) for lg, mx in zip(logits, m)]
            l = [jnp.sum(px, axis=-1, keepdims=True) for px in p]
            o = [_dot(p[2 * n + a].astype(BF16), todo[n][4]) for n in range(len(todo)) for a in range(2)]
            for n, (rows, j, *_) in enumerate(todo):
                o_in = jnp.where(lo_blk, o[2 * n], o[2 * n + 1])
                m_in = jnp.where(lo_blk, m[2 * n], m[2 * n + 1])
                l_in = jnp.where(lo_blk, l[2 * n], l[2 * n + 1])
                if bi > 0:
                    m_old = ma[j, rows, :]
                    m_new = jnp.maximum(m_old, m_in)
                    w_old, w_in = jnp.exp(m_old - m_new), jnp.exp(m_in - m_new)
                    o_in = w_old * oa[j, rows, :] + w_in * o_in
                    l_in = w_old * la[j, rows, :] + w_in * l_in
                    m_in = m_new
                oa[j, rows, :] = o_in
                ma[j, rows, :] = m_in
                la[j, rows, :] = l_in
            return carry
        lax.fori_loop(0, n_blocks // DIL_GROUP, block_group, 0)

    def finish(i, carry):
        rows = pl.ds(pl.multiple_of(i * DIL_TILE, DIL_TILE), DIL_TILE)
        for j in range(n_tiles):
            o_ref[rows, j * LANES:(j + 1) * LANES] = oa[j, rows, :] / la[j, rows, :]
        return carry
    lax.fori_loop(0, seq // DIL_TILE, finish, 0)


def _dil(slab, gq2, gk2, band, *, batch, seq):
    big = lambda: pltpu.VMEM((GROUP_W // LANES, seq, LANES), F32)
    n_bias = len(DIL_PAIRS) * N_HEADS
    return pl.pallas_call(
        functools.partial(_dil_body, seq=seq),
        grid=(batch,),
        in_specs=[pl.BlockSpec((seq, DIL_SLAB), lambda b: (0, b)),
                  _const_spec((1, LANES)), _const_spec((1, LANES)),
                  _const_spec(band.shape)],
        out_specs=pl.BlockSpec((seq, GROUP_W), lambda b: (0, b)),
        out_shape=jax.ShapeDtypeStruct((seq, batch * GROUP_W), F32),
        scratch_shapes=[pltpu.VMEM((n_bias, DIL_SPAN, 2 * DIL_SPAN), F32)] + [big() for _ in range(6)],
        compiler_params=_params("parallel"),
        name="dilated",
    )(slab, gq2, gk2, band)


def _t5_bucket(dist):
    exact = T5_BUCKETS // 2
    df = jnp.maximum(dist, 1).astype(F32)
    large = exact + (jnp.log(df / exact) / math.log(T5_MAX_DIST / exact) * (T5_BUCKETS - exact)).astype(jnp.int32)
    large = jnp.minimum(large, T5_BUCKETS - 1)
    return jnp.where(dist < exact, dist, large)


def _dil_band(table):
    span = DIL_SPAN
    delta = span - jnp.arange(2 * span, dtype=jnp.int32)
    rows = []
    for _, dil in DIL_PAIRS:
        bucket = _t5_bucket(jnp.clip(delta, 0, span) * dil)
        onehot = (bucket[:, None] == jnp.arange(T5_BUCKETS, dtype=jnp.int32)[None, :]).astype(F32)
        vals = jnp.dot(onehot, table.astype(F32), precision=HIGHEST)
        rows.append(jnp.where((delta >= 0)[:, None], vals, NEG_INF).T)
    return jnp.concatenate(rows, axis=0)


DN_TILE = 256
DN_GROUP = 4
PREP_STORE_DELAY = 12


def _softplus(x):
    return jnp.maximum(x, 0.0) + jnp.log1p(jnp.exp(-jnp.abs(x)))


def _pair_l2(x, lo_half):
    sq = x * x
    tot = jnp.sum(sq, axis=-1, keepdims=True)
    lo = jnp.sum(jnp.where(lo_half, sq, 0.0), axis=-1, keepdims=True)
    return x * lax.rsqrt(jnp.where(lo_half, lo, tot - lo) + EPS)


def _dn_body(x_ref, cw_ref, alog_ref, dtb_ref, on_ref, o_ref, q_scr, k_scr, v_scr, g_scr, b_scr, w_scr, a_scr,
             s_scr, *, seq):
    c = DN_CHUNK
    w = GROUP_W
    lo_tile = lax.broadcasted_iota(jnp.int32, (DN_TILE, LANES), 1) < HEAD_DIM

    def prep_steps(i):
        if isinstance(i, int):
            first, halo_first = i * DN_TILE, max(i * DN_TILE - 8, 0)
        else:
            first = pl.multiple_of(i * DN_TILE, DN_TILE)
            halo_first = pl.multiple_of(jnp.maximum(first - 8, 0), 8)
        rows = pl.ds(first, DN_TILE)
        cur = x_ref[rows, 0:3 * w]
        halo = jnp.where(i > 0, x_ref[pl.ds(halo_first, 8), 0:3 * w], 0.0)
        ext = jnp.concatenate([halo, cur], axis=0)
        acc = cw_ref[DN_CONV - 1:DN_CONV, :] * cur
        for j in range(DN_CONV - 1):
            acc = acc + cw_ref[j:j + 1, :] * pltpu.roll(ext, DN_CONV - 1 - j, 0)[8:, :]
        y = acc * jax.nn.sigmoid(acc)
        yield
        tiles = range(w // LANES)
        cols = lambda part, j: slice(part * w + j * LANES, part * w + (j + 1) * LANES)
        q = [_pair_l2(y[:, cols(0, j)], lo_tile) * (HEAD_DIM ** -0.5) for j in tiles]
        k = [_pair_l2(y[:, cols(1, j)], lo_tile) for j in tiles]
        yield
        ab = x_ref[rows, 3 * w:3 * w + LANES]
        g = -jnp.exp(alog_ref[...]) * _softplus(ab + dtb_ref[...])
        gc = _dot_split(chunk_tril, g, 3, split_rhs=True)
        yield
        gc = per_head_lanes(gc, 0)
        beta = per_head_lanes(jax.nn.sigmoid(ab), N_HEADS)
        for _ in range(PREP_STORE_DELAY):
            yield
        for j in tiles:
            q_scr[rows, j * LANES:(j + 1) * LANES] = q[j]
            k_scr[rows, j * LANES:(j + 1) * LANES] = k[j]
            v_scr[rows, j * LANES:(j + 1) * LANES] = y[:, cols(2, j)]
        g_scr[rows, :] = gc
        b_scr[rows, :] = beta

    ri = lax.broadcasted_iota(jnp.int32, (c, w), 0)
    ci = lax.broadcasted_iota(jnp.int32, (c, w), 1)
    cj = jnp.bitwise_and(ci, HEAD_DIM - 1)
    causal = ri >= cj
    strict = ri > cj
    eye4 = (ri == cj).astype(F32)
    same_sub = lax.shift_right_logical(ri, 4) == lax.shift_right_logical(cj, 4)
    bi_r = lax.broadcasted_iota(jnp.int32, (w, w), 0)
    bi_c = lax.broadcasted_iota(jnp.int32, (w, w), 1)
    same_head = lax.shift_right_logical(bi_r, 6) == lax.shift_right_logical(bi_c, 6)
    block_mask = same_head.astype(F32)
    chunk_tril = (same_head & (jnp.bitwise_and(bi_r, c - 1) >= jnp.bitwise_and(bi_c, c - 1))).astype(F32)
    head_of_lane = lax.shift_right_logical(lax.broadcasted_iota(jnp.int32, (DN_TILE, w), 1), 6)

    def per_head_lanes(x, first_lane):
        out = None
        for h in range(N_HEADS):
            col = jnp.broadcast_to(x[:, first_lane + h:first_lane + h + 1], (DN_TILE, w))
            out = col if out is None else jnp.where(head_of_lane == h, col, out)
        return out

    def per_head(a, b):
        bd = jnp.where(same_head, jnp.concatenate([b.astype(BF16)] * N_HEADS, axis=0), 0.0)
        return _dot(a.astype(BF16), bd)

    def group_rows(i):
        base = pl.multiple_of(i * (DN_GROUP * c), DN_GROUP * c)
        return [pl.ds(base + u * c, c) for u in range(DN_GROUP)]

    def solve_steps(i):
        rows = group_rows(i)
        each = lambda f, *lists: [f(*args) for args in zip(*lists)]
        q, k, v, gc, beta = ([ref[r, :] for r in rows] for ref in (q_scr, k_scr, v_scr, g_scr, b_scr))
        g_row = each(lambda g: jnp.sum(g * eye4, axis=0, keepdims=True), gc)
        decay = each(lambda g, gr: jnp.exp(jnp.where(causal, g - gr, NEG_INF)), gc, g_row)
        kb = each(jnp.multiply, k, beta)
        k_bd = each(lambda x: jnp.where(same_head, jnp.concatenate([x.astype(BF16)] * N_HEADS, axis=0), 0.0), k)
        lmat = each(lambda a, b, d: jnp.where(strict, _dot_nt(a.astype(BF16), b) * d, 0.0), kb, k_bd, decay)
        yield
        a_qk = each(lambda a, b, d: jnp.where(causal, _dot_nt(a.astype(BF16), b) * d, 0.0), q, k_bd, decay)
        yield
        p = each(lambda l_: jnp.where(same_sub, -l_, 0.0), lmat)
        t_diag = each(lambda x: eye4 + x, p)
        for _ in range(3):
            p = each(per_head, p, p)
            yield
            t_diag = each(lambda t, x: t + per_head(t, x), t_diag, p)
            yield
        nil = each(lambda t, l_: per_head(t, jnp.where(same_sub, 0.0, l_)), t_diag, lmat)
        yield
        nil2 = each(per_head, nil, nil)
        yield
        nil3 = each(per_head, nil, nil2)
        yield
        t_inv = each(lambda n1, n2, n3, t: per_head(eye4 - n1 + n2 - n3, t), nil, nil2, nil3, t_diag)
        yield
        eg = each(jnp.exp, gc)
        w_c = each(lambda t, a, e: per_head(t, a * e), t_inv, kb, eg)
        yield
        u_c = each(lambda t, a, b: per_head(t, a * b), t_inv, v, beta)
        yield
        q_dec = each(jnp.multiply, q, eg)
        k_dec = each(lambda x, g: x * jnp.exp(g[c - 1:c, :] - g), k, gc)
        for ref, vals in zip((w_scr, v_scr, a_scr, q_scr, k_scr), (w_c, u_c, a_qk, q_dec, k_dec)):
            for r, val in zip(rows, vals):
                ref[r, :] = val

    def state_steps(i):
        for rows in group_rows(i):
            state = s_scr[...]
            state_b = state.astype(BF16)
            w_s = _dot(w_scr[rows, :].astype(BF16), state_b)
            q_s = _dot(q_scr[rows, :].astype(BF16), state_b)
            yield
            v_new = v_scr[rows, :] - w_s
            o_intra = per_head(a_scr[rows, :], v_new)
            upd = lax.dot_general(k_scr[rows, :].astype(BF16), v_new.astype(BF16), (((0,), (0,)), ((), ())),
                                  preferred_element_type=F32)
            yield
            o_ref[rows, :] = q_s + o_intra
            s_scr[...] = state * jnp.exp(g_scr[rows, :][c - 1:c, :]) + upd * block_mask
            yield

    def weave(*steps):
        live = list(steps)
        while live:
            for gen in list(live):
                if next(gen, "done") == "done":
                    live.remove(gen)

    n_groups = seq // (DN_GROUP * c)
    s_scr[...] = jnp.zeros_like(s_scr)
    weave(prep_steps(0))
    weave(solve_steps(0), prep_steps(1))

    def group(i, carry):
        weave(solve_steps(i), state_steps(i - 1), prep_steps(i + 1))
        return carry
    lax.fori_loop(1, n_groups - 1, group, 0)
    weave(solve_steps(n_groups - 1), state_steps(n_groups - 2))
    weave(state_steps(n_groups - 1))

    def finish_tile(i, carry):
        rows = pl.ds(pl.multiple_of(i * DN_TILE, DN_TILE), DN_TILE)
        o = o_ref[rows, :]
        ms = _dot_split(o * o, block_mask, 2) * (1.0 / HEAD_DIM)
        gate = x_ref[rows, 3 * w + LANES:4 * w + LANES]
        o_ref[rows, :] = o * lax.rsqrt(ms + EPS) * on_ref[...] * (gate * jax.nn.sigmoid(gate))
        return carry
    lax.fori_loop(0, seq // DN_TILE, finish_tile, 0)


def _dn(slab, conv_w, a_log, dt_bias, o_norm, *, batch, seq):
    wide = lambda: pltpu.VMEM((seq, GROUP_W), F32)
    return pl.pallas_call(
        functools.partial(_dn_body, seq=seq),
        grid=(batch,),
        in_specs=[pl.BlockSpec((seq, DN_SLAB), lambda b: (0, b)),
                  _const_spec((DN_CONV, 3 * GROUP_W)), _const_spec((1, LANES)), _const_spec((1, LANES)),
                  _const_spec((1, GROUP_W))],
        out_specs=pl.BlockSpec((seq, GROUP_W), lambda b: (0, b)),
        out_shape=jax.ShapeDtypeStruct((seq, batch * GROUP_W), F32),
        scratch_shapes=[wide() for _ in range(7)] + [pltpu.VMEM((GROUP_W, GROUP_W), F32)],
        compiler_params=_params("parallel"),
        name="deltanet",
    )(slab, conv_w, a_log, dt_bias, o_norm)


def _row(v, width=None):
    v = v.astype(F32).reshape(1, -1)
    if width is not None and v.shape[1] < width:
        v = jnp.pad(v, ((0, 0), (0, width - v.shape[1])))
    return v


def _prep_w_in(w):
    w = w.astype(BF16)
    z = lambda n: jnp.zeros(w.shape[:2] + (n,), BF16)
    return jnp.concatenate([w[..., 0:416], z(96), w[..., 416:2216], z(120), w[..., 2216:2472]], axis=-1)


def _pad_heads(w, per_head, lo, hi):
    k = w.shape[0]
    w = w.reshape(k, N_HEADS, per_head)[:, :, lo:hi]
    w = jnp.pad(w, ((0, 0), (0, 0), (0, LANES - (hi - lo))))
    return w.reshape(k, N_HEADS * LANES).astype(BF16)


def _rope_tables(seq):
    half = MLA_ROPE // 2
    pos = jnp.arange(seq, dtype=F32)
    freqs = ROPE_THETA ** (-jnp.arange(half, dtype=F32) / half)
    ang = pos[:, None] * freqs[None, :]
    cos, sin = jnp.cos(ang), jnp.sin(ang)
    ones = jnp.ones((seq, MLA_NOPE), F32)
    zeros = jnp.zeros((seq, MLA_NOPE), F32)
    tail1 = jnp.ones((seq, LANES - MLA_DQK), F32)
    tail0 = jnp.zeros((seq, LANES - MLA_DQK), F32)
    return (jnp.concatenate([ones, cos, cos, tail1], axis=1),
            jnp.concatenate([zeros, -sin, sin, tail0], axis=1))


def _block_diag(blocks):
    g, r, c = blocks.shape
    eye = jnp.eye(g, dtype=blocks.dtype)
    return (blocks[:, :, None, :] * eye[:, None, :, None]).reshape(g * r, g * c)


def _mla_layer(slab, p, l, *, batch, seq):
    w_ukv = p["mla_w_ukv"][l]
    cos, sin = _rope_tables(seq)
    return _mla(slab, _row(p["mla_q_norm"][l]), _row(p["mla_kv_norm"][l]),
                _pad_heads(p["mla_w_uq"][l], MLA_DQK, 0, MLA_DQK),
                _pad_heads(w_ukv, MLA_NOPE + HEAD_DIM, 0, MLA_NOPE),
                w_ukv.reshape(MLA_KV_RANK, N_HEADS, MLA_NOPE + HEAD_DIM)[:, :, MLA_NOPE:]
                .reshape(MLA_KV_RANK, GROUP_W).astype(BF16),
                _row(p["mla_qk_q"][l], LANES), _row(p["mla_qk_k"][l], LANES), cos, sin, batch=batch, seq=seq)


def _dil_layer(slab, p, l, *, batch, seq):
    pair = lambda g: jnp.tile(g.astype(F32).reshape(1, HEAD_DIM), (1, LANES // HEAD_DIM))
    return _dil(slab, pair(p["dil_q_norm"][l]), pair(p["dil_k_norm"][l]), _dil_band(p["t5_bias"]),
                batch=batch, seq=seq)


def _dn_layer(slab, p, l, *, batch, seq):
    return _dn(slab, p["dn_conv"][l].astype(F32), _row(p["dn_a_log"][l], LANES), _row(p["dn_dt_bias"][l], LANES),
               jnp.tile(p["dn_o_norm"][l].astype(F32).reshape(1, HEAD_DIM), (1, N_HEADS)), batch=batch, seq=seq)


def _s5_layer(u, p, l, *, batch, seq):
    state_row = lambda v: v.astype(F32).reshape(1, S5_WIDTH)
    ldt = jnp.broadcast_to(p["s5_log_dt"][l][:, None], (S5_GROUPS, S5_STATE))
    bre = _block_diag(jnp.swapaxes(p["s5_b_re"][l], 1, 2).astype(F32))
    bim = _block_diag(jnp.swapaxes(p["s5_b_im"][l], 1, 2).astype(F32))
    cre = _block_diag(jnp.swapaxes(p["s5_c_re"][l], 1, 2)).astype(BF16)
    cim = _block_diag(jnp.swapaxes(p["s5_c_im"][l], 1, 2)).astype(BF16)
    return _s5(u, state_row(p["s5_lambda_re"][l]), state_row(p["s5_lambda_im"][l]), state_row(ldt),
               bre, bim, cre, cim, _row(p["s5_d"][l]), p["s5_w_glu"][l].astype(BF16), batch=batch, seq=seq,
               steps=min(256, seq))


def kernel(x, attn_norm, w_in, w_out, mla_q_norm, mla_kv_norm, mla_w_uq, mla_w_ukv, mla_qk_q, mla_qk_k,
           s5_lambda_re, s5_lambda_im, s5_log_dt, s5_b_re, s5_b_im, s5_c_re, s5_c_im, s5_d, s5_w_glu,
           dil_q_norm, dil_k_norm, t5_bias, dn_conv, dn_a_log, dn_dt_bias, dn_o_norm,
           ffn_norm, ffn_w1, ffn_w3, ffn_w2):
    p = dict(mla_q_norm=mla_q_norm, mla_kv_norm=mla_kv_norm, mla_w_uq=mla_w_uq, mla_w_ukv=mla_w_ukv,
             mla_qk_q=mla_qk_q, mla_qk_k=mla_qk_k, s5_lambda_re=s5_lambda_re, s5_lambda_im=s5_lambda_im,
             s5_log_dt=s5_log_dt, s5_b_re=s5_b_re, s5_b_im=s5_b_im, s5_c_re=s5_c_re, s5_c_im=s5_c_im, s5_d=s5_d,
             s5_w_glu=s5_w_glu, dil_q_norm=dil_q_norm, dil_k_norm=dil_k_norm, t5_bias=t5_bias, dn_conv=dn_conv,
             dn_a_log=dn_a_log, dn_dt_bias=dn_dt_bias, dn_o_norm=dn_o_norm)
    batch, seq, _ = x.shape
    h = x.reshape(batch * seq, D_MODEL)
    w_in_b, w_out_b = _prep_w_in(w_in), w_out.astype(BF16)
    w1_b, w3_b, w2_b = ffn_w1.astype(BF16), ffn_w3.astype(BF16), ffn_w2.astype(BF16)
    for l in range(attn_norm.shape[0]):
        mla_in, s5_in, dil_in, dn_in = _proj(h, _row(attn_norm[l]), w_in_b, l, batch=batch, seq=seq)
        ys = [_mla_layer(mla_in, p, l, batch=batch, seq=seq),
              _s5_layer(s5_in, p, l, batch=batch, seq=seq),
              _dil_layer(dil_in, p, l, batch=batch, seq=seq),
              _dn_layer(dn_in, p, l, batch=batch, seq=seq)]
        h = _out_ffn(h, ys, w_out_b, _row(ffn_norm[l]), w1_b, w3_b, w2_b, l, batch=batch, seq=seq)
    return h.reshape(batch, seq, D_MODEL)
```

```python
import functools
import math

import jax
import jax.numpy as jnp
import numpy as np
from jax import lax
from jax.experimental import pallas as pl
from jax.experimental.pallas import tpu as pltpu

F32 = jnp.float32
BF16 = jnp.bfloat16
HIGHEST = lax.Precision.HIGHEST

D_MODEL = 1024
GROUP_W = 256
HEAD_DIM = 64
N_HEADS = 4
EPS = 1e-6
NEG_INF = -1e30

MLA_NOPE = 64
MLA_ROPE = 32
MLA_DQK = MLA_NOPE + MLA_ROPE
MLA_KV_RANK = 128
ROPE_THETA = 10000.0

S5_GROUP_CH = 16
S5_GROUPS = 16
S5_STATE = 64
S5_WIDTH = S5_GROUPS * S5_STATE

DIL_PAIRS = ((128, 1), (512, 4), (2048, 16))
DIL_SPAN = 128
T5_BUCKETS = 32
T5_MAX_DIST = 2048

DN_CONV = 4
DN_CHUNK = 64

FFN_HIDDEN = 2816
FFN_CHUNK = 2816

VMEM_LIMIT_BYTES = 56 * 1024 * 1024
LANES = 128

MLA_SLAB = 512
S5_SLAB = 256
DIL_SLAB = 768
DN_SLAB = 1152
PROJ_COLS = MLA_SLAB + S5_SLAB + DIL_SLAB + DN_SLAB


def _dot(a, b, precision=None):
    return jnp.dot(a, b, preferred_element_type=F32, precision=precision)


def _dot_nt(a, b, precision=None):
    return lax.dot_general(a, b, (((1,), (1,)), ((), ())), preferred_element_type=F32, precision=precision)


def _dot_split(a, b, terms, split_rhs=False):
    x = b if split_rhs else a
    mask = (a if split_rhs else b).astype(BF16)
    out = None
    for _ in range(terms):
        piece = x.astype(BF16)
        part = _dot(mask, piece) if split_rhs else _dot(piece, mask)
        out = part if out is None else out + part
        x = x - piece.astype(F32)
    return out


def _const_spec(shape):
    nd = len(shape)
    return pl.BlockSpec(shape, lambda *_: (0,) * nd, pipeline_mode=pl.Buffered(1))


def _params(*sem):
    return pltpu.CompilerParams(dimension_semantics=sem, vmem_limit_bytes=VMEM_LIMIT_BYTES)


def _proj_body(x_ref, g_ref, w_ref, mla_ref, s5_ref, dil_ref, dn_ref):
    x = x_ref[...]
    n = x * lax.rsqrt(jnp.mean(x * x, axis=-1, keepdims=True) + EPS) * g_ref[...]
    nb = n.astype(BF16)
    start = 0
    for ref in (mla_ref, s5_ref, dil_ref, dn_ref):
        width = ref.shape[-1]
        ref[...] = _dot(nb, w_ref[:, start:start + width])
        start += width


def _layer_spec(shape, layer):
    nd = len(shape)
    return pl.BlockSpec((None,) + tuple(shape), lambda *_: (layer,) + (0,) * nd, pipeline_mode=pl.Buffered(1))


def _proj(h, gain, w_big, layer, *, batch, seq, tm=512):
    nt = seq // tm
    widths = (MLA_SLAB, S5_SLAB, DIL_SLAB, DN_SLAB)
    return pl.pallas_call(
        _proj_body,
        grid=(batch, nt),
        in_specs=[pl.BlockSpec((tm, D_MODEL), lambda b, i: (b * nt + i, 0)),
                  _const_spec((1, D_MODEL)),
                  _layer_spec((D_MODEL, PROJ_COLS), layer)],
        out_specs=[pl.BlockSpec((tm, w), lambda b, i: (i, b)) for w in widths],
        out_shape=[jax.ShapeDtypeStruct((seq, batch * w), F32) for w in widths],
        compiler_params=_params("parallel", "parallel"),
        name="proj",
    )(h, gain, w_big)


def _out_ffn_body(h_ref, y0_ref, y1_ref, y2_ref, y3_ref, wo_ref, g_ref, w1_ref, w3_ref, w2_ref, o_ref, acc_ref):
    mixed = jnp.concatenate([y_ref[...].astype(BF16) for y_ref in (y0_ref, y1_ref, y2_ref, y3_ref)], axis=1)
    h = h_ref[...] + _dot(mixed, wo_ref[...])
    n = h * lax.rsqrt(jnp.mean(h * h, axis=-1, keepdims=True) + EPS) * g_ref[...]
    nb = n.astype(BF16)
    acc_ref[...] = h

    def hidden_chunk(c, carry):
        cols = pl.ds(pl.multiple_of(c * FFN_CHUNK, FFN_CHUNK), FFN_CHUNK)
        a = _dot(nb, w1_ref[:, cols])
        b = _dot(nb, w3_ref[:, cols])
        z = (a * jax.nn.sigmoid(a) * b).astype(BF16)
        acc_ref[...] += _dot(z, w2_ref[cols, :])
        return carry
    lax.fori_loop(0, FFN_HIDDEN // FFN_CHUNK, hidden_chunk, 0)
    o_ref[...] = acc_ref[...]


def _out_ffn(h, ys, w_out, gain, w1, w3, w2, layer, *, batch, seq, tm=512):
    nt = seq // tm
    row = pl.BlockSpec((tm, D_MODEL), lambda b, i: (b * nt + i, 0))
    slab = pl.BlockSpec((tm, GROUP_W), lambda b, i: (i, b))
    return pl.pallas_call(
        _out_ffn_body,
        grid=(batch, nt),
        in_specs=[row] + [slab] * 4 + [
            _layer_spec((D_MODEL, D_MODEL), layer), _const_spec((1, D_MODEL)),
            _layer_spec((D_MODEL, FFN_HIDDEN), layer), _layer_spec((D_MODEL, FFN_HIDDEN), layer),
            _layer_spec((FFN_HIDDEN, D_MODEL), layer)],
        out_specs=row,
        out_shape=jax.ShapeDtypeStruct((batch * seq, D_MODEL), F32),
        scratch_shapes=[pltpu.VMEM((tm, D_MODEL), F32)],
        compiler_params=_params("parallel", "parallel"),
        name="out_ffn",
    )(h, *ys, w_out, gain, w1, w3, w2)


MLA_BLOCK = 256
MLA_SWEEP = (4, 2, 1)


def _mla_body(x_ref, qn_ref, kvn_ref, wuq_ref, wuk_ref, wuv_ref, gq_ref, gk_ref, cos_ref, sin_ref,
              o_ref, k_scr, vt_scr, q_scr, *, seq):
    blk = MLA_BLOCK
    n_pairs = N_HEADS // 2
    qi = pl.program_id(1)
    lane = lax.broadcasted_iota(jnp.int32, (blk, LANES), 1)

    def rope(x, c, s):
        rot = jnp.where(lane < MLA_NOPE + MLA_ROPE // 2, pltpu.roll(x, LANES - MLA_ROPE // 2, 1),
                        pltpu.roll(x, MLA_ROPE // 2, 1))
        return x * c + rot * s

    def norm_head(x, g):
        ssq = jnp.sum(x * x, axis=-1, keepdims=True)
        return x * lax.rsqrt(ssq * (1.0 / MLA_DQK) + EPS) * g

    heads = range(N_HEADS)

    def prepare_steps(i):
        rows = pl.ds(i * blk if isinstance(i, int) else pl.multiple_of(i * blk, blk), blk)
        ckv = x_ref[rows, 256:384]
        cq = x_ref[rows, 0:256]
        kvn = ckv * lax.rsqrt(jnp.mean(ckv * ckv, axis=-1, keepdims=True) + EPS) * kvn_ref[...]
        qn = cq * lax.rsqrt(jnp.mean(cq * cq, axis=-1, keepdims=True) + EPS) * qn_ref[...]
        kvn = kvn.astype(BF16)
        k_nope = _dot(kvn, wuk_ref[...])
        v = _dot(kvn, wuv_ref[...])
        q_all = _dot(qn.astype(BF16), wuq_ref[...])
        yield
        k_rope = pltpu.roll(x_ref[rows, 384:512], MLA_NOPE, 1)
        c = cos_ref[rows, :]
        s = sin_ref[rows, :]
        ks = [rope(norm_head(k_nope[:, h * LANES:(h + 1) * LANES] + k_rope, gk_ref[...]), c, s) for h in heads]
        yield
        qs_new = [rope(norm_head(q_all[:, h * LANES:(h + 1) * LANES], gq_ref[...]), c, s) * (MLA_DQK ** -0.5)
                  for h in heads]
        vts = [v[:, pr * LANES:(pr + 1) * LANES].T for pr in range(n_pairs)]
        yield
        yield
        for h in heads:
            k_scr[h, rows, :] = ks[h].astype(BF16)
            q_scr[h] = qs_new[h].astype(BF16)
        for pr in range(n_pairs):
            vt_scr[pr, :, rows] = vts[pr].astype(BF16)

    def weave(*steps):
        live = list(steps)
        while live:
            for gen in list(live):
                if next(gen, "done") == "done":
                    live.remove(gen)

    @pl.when(qi == 0)
    def _first_block():
        weave(prepare_steps(0))

    rows = pl.ds(pl.multiple_of(qi * blk, blk), blk)
    key_pos = lax.broadcasted_iota(jnp.int32, (blk, blk), 0)
    query_pos = lax.broadcasted_iota(jnp.int32, (blk, blk), 1)
    first_of_pair = lax.broadcasted_iota(jnp.int32, (LANES, blk), 0) < HEAD_DIM
    qs = [q_scr[h] for h in heads]

    def kv_steps(first_key, width, carry, result, diagonal=False):
        m_old, l_old, acc_old = carry
        keys = pl.ds(pl.multiple_of(first_key, blk), width)
        logits = [_dot_nt(k_scr[h, keys, :], qs[h]) for h in heads]
        yield
        if diagonal:
            logits = [jnp.where(key_pos <= query_pos, x, NEG_INF) for x in logits]
        m_new = [jnp.maximum(m_old[h], jnp.max(logits[h], axis=0, keepdims=True)) for h in heads]
        alpha = [jnp.exp(m_old[h] - m_new[h]) for h in heads]
        p = [jnp.exp(logits[h] - m_new[h]) for h in heads]
        l_new = [alpha[h] * l_old[h] + jnp.sum(p[h], axis=0, keepdims=True) for h in heads]
        pv = [_dot(vt_scr[h // 2, :, keys], p[h].astype(BF16)) for h in heads]
        yield
        acc_new = [jnp.where(first_of_pair, alpha[2 * pr] * acc_old[pr] + pv[2 * pr],
                             alpha[2 * pr + 1] * acc_old[pr] + pv[2 * pr + 1]) for pr in range(n_pairs)]
        result.append((tuple(m_new), tuple(l_new), tuple(acc_new)))

    def kv_step(first_key, width, carry):
        result = []
        weave(kv_steps(first_key, width, carry, result))
        return result[0]

    def last_steps(carry):
        result = []
        yield from kv_steps(qi * blk, blk, carry, result, diagonal=True)
        _, l, acc = result[0]
        for pr in range(n_pairs):
            out_t = acc[pr] / jnp.where(first_of_pair, l[2 * pr], l[2 * pr + 1])
            o_ref[rows, pr * LANES:(pr + 1) * LANES] = out_t.T

    carry = (tuple(jnp.full((1, blk), NEG_INF, F32) for _ in heads), tuple(jnp.zeros((1, blk), F32) for _ in heads),
             tuple(jnp.zeros((LANES, blk), F32) for _ in range(n_pairs)))
    done = 0
    for n_blk in MLA_SWEEP:
        todo = (qi - done) // n_blk
        carry = lax.fori_loop(0, todo, lambda j, cr, first=done, n_blk=n_blk:
                              kv_step((first + j * n_blk) * blk, n_blk * blk, cr), carry)
        done = done + todo * n_blk

    @pl.when(qi + 1 < seq // blk)
    def _last_and_prepare_next():
        weave(last_steps(carry), prepare_steps(qi + 1))

    @pl.when(qi + 1 == seq // blk)
    def _last():
        weave(last_steps(carry))


def _mla(slab, qn, kvn, wuq, wuk, wuv, gq, gk, cos, sin, *, batch, seq):
    nq = seq // MLA_BLOCK
    return pl.pallas_call(
        functools.partial(_mla_body, seq=seq),
        grid=(batch, nq),
        in_specs=[pl.BlockSpec((seq, MLA_SLAB), lambda b, i: (0, b)),
                  _const_spec((1, 256)), _const_spec((1, MLA_KV_RANK)),
                  _const_spec((256, N_HEADS * LANES)), _const_spec((MLA_KV_RANK, N_HEADS * LANES)),
                  _const_spec((MLA_KV_RANK, GROUP_W)),
                  _const_spec((1, LANES)), _const_spec((1, LANES)),
                  _const_spec((seq, LANES)), _const_spec((seq, LANES))],
        out_specs=pl.BlockSpec((seq, GROUP_W), lambda b, i: (0, b)),
        out_shape=jax.ShapeDtypeStruct((seq, batch * GROUP_W), F32),
        scratch_shapes=[pltpu.VMEM((N_HEADS, seq, LANES), BF16), pltpu.VMEM((N_HEADS // 2, LANES, seq), BF16),
                        pltpu.VMEM((N_HEADS, MLA_BLOCK, LANES), BF16)],
        compiler_params=_params("parallel", "arbitrary"),
        name="mla",
    )(slab, qn, kvn, wuq, wuk, wuv, gq, gk, cos, sin)


def _s5_body(u_ref, lre_ref, lim_ref, ldt_ref, bre_ref, bim_ref, cre_ref, cim_ref, d_ref, wglu_ref, o_ref,
             a_scr, bbar_scr, h_scr, x_scr, io_scr, *, batch, steps):
    io_tiles = GROUP_W // LANES

    @pl.when(pl.program_id(0) == 0)
    def _discretise():
        lr = lre_ref[...]
        li = lim_ref[...]
        dt = jnp.exp(ldt_ref[...])
        mag = jnp.exp(lr * dt)
        ar = mag * jnp.cos(li * dt)
        ai = mag * jnp.sin(li * dt)
        den = lr * lr + li * li
        nr = ar - 1.0
        zr = (nr * lr + ai * li) / den
        zi = (ai * lr - nr * li) / den
        a_scr[0] = jnp.broadcast_to(ar, (batch, S5_WIDTH))
        a_scr[1] = jnp.broadcast_to(ai, (batch, S5_WIDTH))
        bre = bre_ref[...]
        bim = bim_ref[...]
        bbar_scr[0] = (zr * bre - zi * bim).astype(BF16)
        bbar_scr[1] = (zr * bim + zi * bre).astype(BF16)
        h_scr[...] = jnp.zeros_like(h_scr)

    for b in range(batch):
        for j in range(io_tiles):
            io_scr[j, pl.ds(b, steps, stride=batch), :] = u_ref[:, b * GROUP_W + j * LANES:b * GROUP_W + (j + 1) * LANES]
    u = jnp.concatenate([io_scr[j] for j in range(io_tiles)], axis=1)
    ub = u.astype(BF16)
    x_scr[0] = _dot(ub, bbar_scr[0])
    x_scr[1] = _dot(ub, bbar_scr[1])
    ar = a_scr[0]
    ai = a_scr[1]

    def step(t, carry):
        hr, hi = carry
        rows = pl.ds(pl.multiple_of(t * batch, batch), batch)
        nhr = ar * hr - ai * hi + x_scr[0, rows, :]
        nhi = ar * hi + ai * hr + x_scr[1, rows, :]
        x_scr[0, rows, :] = nhr
        x_scr[1, rows, :] = nhi
        return nhr, nhi

    hr, hi = lax.fori_loop(0, steps, step, (h_scr[0], h_scr[1]), unroll=8)
    h_scr[0] = hr
    h_scr[1] = hi
    y = _dot(x_scr[0].astype(BF16), cre_ref[...]) - _dot(x_scr[1].astype(BF16), cim_ref[...]) + d_ref[...] * u
    z = _dot(y.astype(BF16), wglu_ref[...])
    out = z[:, :GROUP_W] * jax.nn.sigmoid(z[:, GROUP_W:])
    for j in range(io_tiles):
        io_scr[j] = out[:, j * LANES:(j + 1) * LANES]
    for b in range(batch):
        for j in range(io_tiles):
            o_ref[:, b * GROUP_W + j * LANES:b * GROUP_W + (j + 1) * LANES] = io_scr[j, pl.ds(b, steps, stride=batch), :]


def _s5(u, lre, lim, ldt, bre, bim, cre, cim, d, wglu, *, batch, seq, steps=256):
    state = lambda rows: pltpu.VMEM((2, rows, S5_WIDTH), F32)
    return pl.pallas_call(
        functools.partial(_s5_body, batch=batch, steps=steps),
        grid=(seq // steps,),
        in_specs=[pl.BlockSpec((steps, batch * S5_SLAB), lambda i: (i, 0)),
                  _const_spec((1, S5_WIDTH)), _const_spec((1, S5_WIDTH)), _const_spec((1, S5_WIDTH)),
                  _const_spec((GROUP_W, S5_WIDTH)), _const_spec((GROUP_W, S5_WIDTH)),
                  _const_spec((S5_WIDTH, GROUP_W)), _const_spec((S5_WIDTH, GROUP_W)),
                  _const_spec((1, GROUP_W)), _const_spec((GROUP_W, 2 * GROUP_W))],
        out_specs=pl.BlockSpec((steps, batch * GROUP_W), lambda i: (i, 0)),
        out_shape=jax.ShapeDtypeStruct((seq, batch * GROUP_W), F32),
        scratch_shapes=[state(batch), pltpu.VMEM((2, GROUP_W, S5_WIDTH), BF16), state(batch),
                        state(steps * batch), pltpu.VMEM((GROUP_W // LANES, steps * batch, LANES), F32)],
        compiler_params=_params("arbitrary"),
        name="s5",
    )(u, lre, lim, ldt, bre, bim, cre, cim, d, wglu)


DIL_TILE = 256
DIL_GROUP = 2


def _pair_norm(x, gain2, same_head):
    ms = _dot_split(x * x, same_head, 2) * (1.0 / HEAD_DIM)
    return x * lax.rsqrt(ms + EPS) * gain2


def _dil_body(x_ref, gq_ref, gk_ref, band_ref, o_ref, bias_ref, q_scr, k_scr, v_scr, oa, ma, la, *, seq):
    span = DIL_SPAN
    n_blocks = seq // span
    n_tiles = GROUP_W // LANES
    lo_blk = lax.broadcasted_iota(jnp.int32, (span, LANES), 1) < HEAD_DIM
    same_head = ((lax.broadcasted_iota(jnp.int32, (LANES, LANES), 0) < HEAD_DIM)
                 == (lax.broadcasted_iota(jnp.int32, (LANES, LANES), 1) < HEAD_DIM)).astype(F32)

    for bh in range(len(DIL_PAIRS) * N_HEADS):
        profile = jnp.broadcast_to(band_ref[bh:bh + 1, :], (span, 2 * span))
        bias_ref[bh] = pltpu.roll(profile, 0, 1, stride=1, stride_axis=0)
    in_current = lax.broadcasted_iota(jnp.int32, (span, 2 * span), 1) >= span

    def norm_tile(i, carry):
        rows = pl.ds(pl.multiple_of(i * DIL_TILE, DIL_TILE), DIL_TILE)
        for j in range(n_tiles):
            cols = lambda part: slice(part * GROUP_W + j * LANES, part * GROUP_W + (j + 1) * LANES)
            q = _pair_norm(x_ref[rows, cols(0)], gq_ref[...], same_head)
            q_scr[j, rows, :] = q * (HEAD_DIM ** -0.5)
            k_scr[j, rows, :] = _pair_norm(x_ref[rows, cols(1)], gk_ref[...], same_head)
            v_scr[j, rows, :] = x_ref[rows, cols(2)]
        return carry
    lax.fori_loop(0, seq // DIL_TILE, norm_tile, 0)

    for bi, (window, dil) in enumerate(DIL_PAIRS):
        sub_len = seq // dil
        nb = sub_len // span

        def block_group(i, carry, bi=bi, nb=nb, dil=dil):
            todo = []
            for u in range(DIL_GROUP):
                t = i * DIL_GROUP + u
                r = lax.shift_right_logical(t, int(math.log2(nb)))
                c = t & (nb - 1)
                first = r + c * (dil * span)
                stride = dil if dil > 1 else None
                rows = pl.ds(first, span, stride=stride)
                prev = pl.ds(jnp.maximum(first - dil * span, r), span, stride=stride)
                keep = jnp.logical_or(in_current, c != 0) if nb > 1 else None
                for j in range(n_tiles):
                    q2 = q_scr[j, rows, :]
                    if nb > 1:
                        k_cat = jnp.concatenate([k_scr[j, prev, :], k_scr[j, rows, :]], axis=0).astype(BF16)
                        v_cat = jnp.concatenate([v_scr[j, prev, :], v_scr[j, rows, :]], axis=0).astype(BF16)
                    else:
                        k_cat = k_scr[j, rows, :].astype(BF16)
                        v_cat = v_scr[j, rows, :].astype(BF16)
                    todo.append((rows, j, q2, k_cat, v_cat, keep))
            logits = []
            for rows, j, q2, k_cat, v_cat, keep in todo:
                for a in range(2):
                    qa = jnp.where(lo_blk if a == 0 else jnp.logical_not(lo_blk), q2, 0.0).astype(BF16)
                    bias = bias_ref[bi * N_HEADS + 2 * j + a]
                    lg = _dot_nt(qa, k_cat) + (bias if nb > 1 else bias[:, span:])
                    logits.append(jnp.where(keep, lg, NEG_INF) if nb > 1 else lg)
            m = [jnp.max(lg, axis=-1, keepdims=True) for lg in logits]
            p = [jnp.exp(lg - mx) for lg, mx in zip(logits, m)]
            l = [jnp.sum(px, axis=-1, keepdims=True) for px in p]
            o = [_dot(p[2 * n + a].astype(BF16), todo[n][4]) for n in range(len(todo)) for a in range(2)]
            for n, (rows, j, *_) in enumerate(todo):
                o_in = jnp.where(lo_blk, o[2 * n], o[2 * n + 1])
                m_in = jnp.where(lo_blk, m[2 * n], m[2 * n + 1])
                l_in = jnp.where(lo_blk, l[2 * n], l[2 * n + 1])
                if bi > 0:
                    m_old = ma[j, rows, :]
                    m_new = jnp.maximum(m_old, m_in)
                    w_old, w_in = jnp.exp(m_old - m_new), jnp.exp(m_in - m_new)
                    o_in = w_old * oa[j, rows, :] + w_in * o_in
                    l_in = w_old * la[j, rows, :] + w_in * l_in
                    m_in = m_new
                oa[j, rows, :] = o_in
                ma[j, rows, :] = m_in
                la[j, rows, :] = l_in
            return carry
        lax.fori_loop(0, n_blocks // DIL_GROUP, block_group, 0)

    def finish(i, carry):
        rows = pl.ds(pl.multiple_of(i * DIL_TILE, DIL_TILE), DIL_TILE)
        for j in range(n_tiles):
            o_ref[rows, j * LANES:(j + 1) * LANES] = oa[j, rows, :] / la[j, rows, :]
        return carry
    lax.fori_loop(0, seq // DIL_TILE, finish, 0)


def _dil(slab, gq2, gk2, band, *, batch, seq):
    big = lambda: pltpu.VMEM((GROUP_W // LANES, seq, LANES), F32)
    n_bias = len(DIL_PAIRS) * N_HEADS
    return pl.pallas_call(
        functools.partial(_dil_body, seq=seq),
        grid=(batch,),
        in_specs=[pl.BlockSpec((seq, DIL_SLAB), lambda b: (0, b)),
                  _const_spec((1, LANES)), _const_spec((1, LANES)),
                  _const_spec(band.shape)],
        out_specs=pl.BlockSpec((seq, GROUP_W), lambda b: (0, b)),
        out_shape=jax.ShapeDtypeStruct((seq, batch * GROUP_W), F32),
        scratch_shapes=[pltpu.VMEM((n_bias, DIL_SPAN, 2 * DIL_SPAN), F32)] + [big() for _ in range(6)],
        compiler_params=_params("parallel"),
        name="dilated",
    )(slab, gq2, gk2, band)


def _t5_bucket(dist):
    exact = T5_BUCKETS // 2
    df = jnp.maximum(dist, 1).astype(F32)
    large = exact + (jnp.log(df / exact) / math.log(T5_MAX_DIST / exact) * (T5_BUCKETS - exact)).astype(jnp.int32)
    large = jnp.minimum(large, T5_BUCKETS - 1)
    return jnp.where(dist < exact, dist, large)


def _dil_band(table):
    span = DIL_SPAN
    delta = span - jnp.arange(2 * span, dtype=jnp.int32)
    rows = []
    for _, dil in DIL_PAIRS:
        bucket = _t5_bucket(jnp.clip(delta, 0, span) * dil)
        onehot = (bucket[:, None] == jnp.arange(T5_BUCKETS, dtype=jnp.int32)[None, :]).astype(F32)
        vals = jnp.dot(onehot, table.astype(F32), precision=HIGHEST)
        rows.append(jnp.where((delta >= 0)[:, None], vals, NEG_INF).T)
    return jnp.concatenate(rows, axis=0)


DN_TILE = 256
DN_GROUP = 4
PREP_STORE_DELAY = 12


def _softplus(x):
    return jnp.maximum(x, 0.0) + jnp.log1p(jnp.exp(-jnp.abs(x)))


def _pair_l2(x, lo_half):
    sq = x * x
    tot = jnp.sum(sq, axis=-1, keepdims=True)
    lo = jnp.sum(jnp.where(lo_half, sq, 0.0), axis=-1, keepdims=True)
    return x * lax.rsqrt(jnp.where(lo_half, lo, tot - lo) + EPS)


def _dn_body(x_ref, cw_ref, alog_ref, dtb_ref, on_ref, o_ref, q_scr, k_scr, v_scr, g_scr, b_scr, w_scr, a_scr,
             s_scr, *, seq):
    c = DN_CHUNK
    w = GROUP_W
    lo_tile = lax.broadcasted_iota(jnp.int32, (DN_TILE, LANES), 1) < HEAD_DIM

    def prep_steps(i):
        if isinstance(i, int):
            first, halo_first = i * DN_TILE, max(i * DN_TILE - 8, 0)
        else:
            first = pl.multiple_of(i * DN_TILE, DN_TILE)
            halo_first = pl.multiple_of(jnp.maximum(first - 8, 0), 8)
        rows = pl.ds(first, DN_TILE)
        cur = x_ref[rows, 0:3 * w]
        halo = jnp.where(i > 0, x_ref[pl.ds(halo_first, 8), 0:3 * w], 0.0)
        ext = jnp.concatenate([halo, cur], axis=0)
        acc = cw_ref[DN_CONV - 1:DN_CONV, :] * cur
        for j in range(DN_CONV - 1):
            acc = acc + cw_ref[j:j + 1, :] * pltpu.roll(ext, DN_CONV - 1 - j, 0)[8:, :]
        y = acc * jax.nn.sigmoid(acc)
        yield
        tiles = range(w // LANES)
        cols = lambda part, j: slice(part * w + j * LANES, part * w + (j + 1) * LANES)
        q = [_pair_l2(y[:, cols(0, j)], lo_tile) * (HEAD_DIM ** -0.5) for j in tiles]
        k = [_pair_l2(y[:, cols(1, j)], lo_tile) for j in tiles]
        yield
        ab = x_ref[rows, 3 * w:3 * w + LANES]
        g = -jnp.exp(alog_ref[...]) * _softplus(ab + dtb_ref[...])
        gc = _dot_split(chunk_tril, g, 3, split_rhs=True)
        yield
        gc = per_head_lanes(gc, 0)
        beta = per_head_lanes(jax.nn.sigmoid(ab), N_HEADS)
        for _ in range(PREP_STORE_DELAY):
            yield
        for j in tiles:
            q_scr[rows, j * LANES:(j + 1) * LANES] = q[j]
            k_scr[rows, j * LANES:(j + 1) * LANES] = k[j]
            v_scr[rows, j * LANES:(j + 1) * LANES] = y[:, cols(2, j)]
        g_scr[rows, :] = gc
        b_scr[rows, :] = beta

    ri = lax.broadcasted_iota(jnp.int32, (c, w), 0)
    ci = lax.broadcasted_iota(jnp.int32, (c, w), 1)
    cj = jnp.bitwise_and(ci, HEAD_DIM - 1)
    causal = ri >= cj
    strict = ri > cj
    eye4 = (ri == cj).astype(F32)
    same_sub = lax.shift_right_logical(ri, 4) == lax.shift_right_logical(cj, 4)
    bi_r = lax.broadcasted_iota(jnp.int32, (w, w), 0)
    bi_c = lax.broadcasted_iota(jnp.int32, (w, w), 1)
    same_head = lax.shift_right_logical(bi_r, 6) == lax.shift_right_logical(bi_c, 6)
    block_mask = same_head.astype(F32)
    chunk_tril = (same_head & (jnp.bitwise_and(bi_r, c - 1) >= jnp.bitwise_and(bi_c, c - 1))).astype(F32)
    head_of_lane = lax.shift_right_logical(lax.broadcasted_iota(jnp.int32, (DN_TILE, w), 1), 6)

    def per_head_lanes(x, first_lane):
        out = None
        for h in range(N_HEADS):
            col = jnp.broadcast_to(x[:, first_lane + h:first_lane + h + 1], (DN_TILE, w))
            out = col if out is None else jnp.where(head_of_lane == h, col, out)
        return out

    def per_head(a, b):
        bd = jnp.where(same_head, jnp.concatenate([b.astype(BF16)] * N_HEADS, axis=0), 0.0)
        return _dot(a.astype(BF16), bd)

    def group_rows(i):
        base = pl.multiple_of(i * (DN_GROUP * c), DN_GROUP * c)
        return [pl.ds(base + u * c, c) for u in range(DN_GROUP)]

    def solve_steps(i):
        rows = group_rows(i)
        each = lambda f, *lists: [f(*args) for args in zip(*lists)]
        q, k, v, gc, beta = ([ref[r, :] for r in rows] for ref in (q_scr, k_scr, v_scr, g_scr, b_scr))
        g_row = each(lambda g: jnp.sum(g * eye4, axis=0, keepdims=True), gc)
        decay = each(lambda g, gr: jnp.exp(jnp.where(causal, g - gr, NEG_INF)), gc, g_row)
        kb = each(jnp.multiply, k, beta)
        k_bd = each(lambda x: jnp.where(same_head, jnp.concatenate([x.astype(BF16)] * N_HEADS, axis=0), 0.0), k)
        lmat = each(lambda a, b, d: jnp.where(strict, _dot_nt(a.astype(BF16), b) * d, 0.0), kb, k_bd, decay)
        yield
        a_qk = each(lambda a, b, d: jnp.where(causal, _dot_nt(a.astype(BF16), b) * d, 0.0), q, k_bd, decay)
        yield
        p = each(lambda l_: jnp.where(same_sub, -l_, 0.0), lmat)
        t_diag = each(lambda x: eye4 + x, p)
        for _ in range(3):
            p = each(per_head, p, p)
            yield
            t_diag = each(lambda t, x: t + per_head(t, x), t_diag, p)
            yield
        nil = each(lambda t, l_: per_head(t, jnp.where(same_sub, 0.0, l_)), t_diag, lmat)
        yield
        nil2 = each(per_head, nil, nil)
        yield
        nil3 = each(per_head, nil, nil2)
        yield
        t_inv = each(lambda n1, n2, n3, t: per_head(eye4 - n1 + n2 - n3, t), nil, nil2, nil3, t_diag)
        yield
        eg = each(jnp.exp, gc)
        w_c = each(lambda t, a, e: per_head(t, a * e), t_inv, kb, eg)
        yield
        u_c = each(lambda t, a, b: per_head(t, a * b), t_inv, v, beta)
        yield
        q_dec = each(jnp.multiply, q, eg)
        k_dec = each(lambda x, g: x * jnp.exp(g[c - 1:c, :] - g), k, gc)
        for ref, vals in zip((w_scr, v_scr, a_scr, q_scr, k_scr), (w_c, u_c, a_qk, q_dec, k_dec)):
            for r, val in zip(rows, vals):
                ref[r, :] = val

    def state_steps(i):
        for rows in group_rows(i):
            state = s_scr[...]
            state_b = state.astype(BF16)
            w_s = _dot(w_scr[rows, :].astype(BF16), state_b)
            q_s = _dot(q_scr[rows, :].astype(BF16), state_b)
            yield
            v_new = v_scr[rows, :] - w_s
            o_intra = per_head(a_scr[rows, :], v_new)
            upd = lax.dot_general(k_scr[rows, :].astype(BF16), v_new.astype(BF16), (((0,), (0,)), ((), ())),
                                  preferred_element_type=F32)
            yield
            o_ref[rows, :] = q_s + o_intra
            s_scr[...] = state * jnp.exp(g_scr[rows, :][c - 1:c, :]) + upd * block_mask
            yield

    def weave(*steps):
        live = list(steps)
        while live:
            for gen in list(live):
                if next(gen, "done") == "done":
                    live.remove(gen)

    n_groups = seq // (DN_GROUP * c)
    s_scr[...] = jnp.zeros_like(s_scr)
    weave(prep_steps(0))
    weave(solve_steps(0), prep_steps(1))

    def group(i, carry):
        weave(solve_steps(i), state_steps(i - 1), prep_steps(i + 1))
        return carry
    lax.fori_loop(1, n_groups - 1, group, 0)
    weave(solve_steps(n_groups - 1), state_steps(n_groups - 2))
    weave(state_steps(n_groups - 1))

    def finish_tile(i, carry):
        rows = pl.ds(pl.multiple_of(i * DN_TILE, DN_TILE), DN_TILE)
        o = o_ref[rows, :]
        ms = _dot_split(o * o, block_mask, 2) * (1.0 / HEAD_DIM)
        gate = x_ref[rows, 3 * w + LANES:4 * w + LANES]
        o_ref[rows, :] = o * lax.rsqrt(ms + EPS) * on_ref[...] * (gate * jax.nn.sigmoid(gate))
        return carry
    lax.fori_loop(0, seq // DN_TILE, finish_tile, 0)


def _dn(slab, conv_w, a_log, dt_bias, o_norm, *, batch, seq):
    wide = lambda: pltpu.VMEM((seq, GROUP_W), F32)
    return pl.pallas_call(
        functools.partial(_dn_body, seq=seq),
        grid=(batch,),
        in_specs=[pl.BlockSpec((seq, DN_SLAB), lambda b: (0, b)),
                  _const_spec((DN_CONV, 3 * GROUP_W)), _const_spec((1, LANES)), _const_spec((1, LANES)),
                  _const_spec((1, GROUP_W))],
        out_specs=pl.BlockSpec((seq, GROUP_W), lambda b: (0, b)),
        out_shape=jax.ShapeDtypeStruct((seq, batch * GROUP_W), F32),
        scratch_shapes=[wide() for _ in range(7)] + [pltpu.VMEM((GROUP_W, GROUP_W), F32)],
        compiler_params=_params("parallel"),
        name="deltanet",
    )(slab, conv_w, a_log, dt_bias, o_norm)


def _row(v, width=None):
    v = v.astype(F32).reshape(1, -1)
    if width is not None and v.shape[1] < width:
        v = jnp.pad(v, ((0, 0), (0, width - v.shape[1])))
    return v


def _prep_w_in(w):
    w = w.astype(BF16)
    z = lambda n: jnp.zeros(w.shape[:2] + (n,), BF16)
    return jnp.concatenate([w[..., 0:416], z(96), w[..., 416:2216], z(120), w[..., 2216:2472]], axis=-1)


def _pad_heads(w, per_head, lo, hi):
    k = w.shape[0]
    w = w.reshape(k, N_HEADS, per_head)[:, :, lo:hi]
    w = jnp.pad(w, ((0, 0), (0, 0), (0, LANES - (hi - lo))))
    return w.reshape(k, N_HEADS * LANES).astype(BF16)


def _rope_tables(seq):
    half = MLA_ROPE // 2
    pos = jnp.arange(seq, dtype=F32)
    freqs = ROPE_THETA ** (-jnp.arange(half, dtype=F32) / half)
    ang = pos[:, None] * freqs[None, :]
    cos, sin = jnp.cos(ang), jnp.sin(ang)
    ones = jnp.ones((seq, MLA_NOPE), F32)
    zeros = jnp.zeros((seq, MLA_NOPE), F32)
    tail1 = jnp.ones((seq, LANES - MLA_DQK), F32)
    tail0 = jnp.zeros((seq, LANES - MLA_DQK), F32)
    return (jnp.concatenate([ones, cos, cos, tail1], axis=1),
            jnp.concatenate([zeros, -sin, sin, tail0], axis=1))


def _block_diag(blocks):
    g, r, c = blocks.shape
    eye = jnp.eye(g, dtype=blocks.dtype)
    return (blocks[:, :, None, :] * eye[:, None, :, None]).reshape(g * r, g * c)


def _mla_layer(slab, p, l, *, batch, seq):
    w_ukv = p["mla_w_ukv"][l]
    cos, sin = _rope_tables(seq)
    return _mla(slab, _row(p["mla_q_norm"][l]), _row(p["mla_kv_norm"][l]),
                _pad_heads(p["mla_w_uq"][l], MLA_DQK, 0, MLA_DQK),
                _pad_heads(w_ukv, MLA_NOPE + HEAD_DIM, 0, MLA_NOPE),
                w_ukv.reshape(MLA_KV_RANK, N_HEADS, MLA_NOPE + HEAD_DIM)[:, :, MLA_NOPE:]
                .reshape(MLA_KV_RANK, GROUP_W).astype(BF16),
                _row(p["mla_qk_q"][l], LANES), _row(p["mla_qk_k"][l], LANES), cos, sin, batch=batch, seq=seq)


def _dil_layer(slab, p, l, *, batch, seq):
    pair = lambda g: jnp.tile(g.astype(F32).reshape(1, HEAD_DIM), (1, LANES // HEAD_DIM))
    return _dil(slab, pair(p["dil_q_norm"][l]), pair(p["dil_k_norm"][l]), _dil_band(p["t5_bias"]),
                batch=batch, seq=seq)


def _dn_layer(slab, p, l, *, batch, seq):
    return _dn(slab, p["dn_conv"][l].astype(F32), _row(p["dn_a_log"][l], LANES), _row(p["dn_dt_bias"][l], LANES),
               jnp.tile(p["dn_o_norm"][l].astype(F32).reshape(1, HEAD_DIM), (1, N_HEADS)), batch=batch, seq=seq)


def _s5_layer(u, p, l, *, batch, seq):
    state_row = lambda v: v.astype(F32).reshape(1, S5_WIDTH)
    ldt = jnp.broadcast_to(p["s5_log_dt"][l][:, None], (S5_GROUPS, S5_STATE))
    bre = _block_diag(jnp.swapaxes(p["s5_b_re"][l], 1, 2).astype(F32))
    bim = _block_diag(jnp.swapaxes(p["s5_b_im"][l], 1, 2).astype(F32))
    cre = _block_diag(jnp.swapaxes(p["s5_c_re"][l], 1, 2)).astype(BF16)
    cim = _block_diag(jnp.swapaxes(p["s5_c_im"][l], 1, 2)).astype(BF16)
    return _s5(u, state_row(p["s5_lambda_re"][l]), state_row(p["s5_lambda_im"][l]), state_row(ldt),
               bre, bim, cre, cim, _row(p["s5_d"][l]), p["s5_w_glu"][l].astype(BF16), batch=batch, seq=seq,
               steps=min(256, seq))


def kernel(x, attn_norm, w_in, w_out, mla_q_norm, mla_kv_norm, mla_w_uq, mla_w_ukv, mla_qk_q, mla_qk_k,
           s5_lambda_re, s5_lambda_im, s5_log_dt, s5_b_re, s5_b_im, s5_c_re, s5_c_im, s5_d, s5_w_glu,
           dil_q_norm, dil_k_norm, t5_bias, dn_conv, dn_a_log, dn_dt_bias, dn_o_norm,
           ffn_norm, ffn_w1, ffn_w3, ffn_w2):
    p = dict(mla_q_norm=mla_q_norm, mla_kv_norm=mla_kv_norm, mla_w_uq=mla_w_uq, mla_w_ukv=mla_w_ukv,
             mla_qk_q=mla_qk_q, mla_qk_k=mla_qk_k, s5_lambda_re=s5_lambda_re, s5_lambda_im=s5_lambda_im,
             s5_log_dt=s5_log_dt, s5_b_re=s5_b_re, s5_b_im=s5_b_im, s5_c_re=s5_c_re, s5_c_im=s5_c_im, s5_d=s5_d,
             s5_w_glu=s5_w_glu, dil_q_norm=dil_q_norm, dil_k_norm=dil_k_norm, t5_bias=t5_bias, dn_conv=dn_conv,
             dn_a_log=dn_a_log, dn_dt_bias=dn_dt_bias, dn_o_norm=dn_o_norm)
    batch, seq, _ = x.shape
    h = x.reshape(batch * seq, D_MODEL)
    w_in_b, w_out_b = _prep_w_in(w_in), w_out.astype(BF16)
    w1_b, w3_b, w2_b = ffn_w1.astype(BF16), ffn_w3.astype(BF16), ffn_w2.astype(BF16)
    for l in range(attn_norm.shape[0]):
        mla_in, s5_in, dil_in, dn_in = _proj(h, _row(attn_norm[l]), w_in_b, l, batch=batch, seq=seq)
        ys = [_mla_layer(mla_in, p, l, batch=batch, seq=seq),
              _s5_layer(s5_in, p, l, batch=batch, seq=seq),
              _dil_layer(dil_in, p, l, batch=batch, seq=seq),
              _dn_layer(dn_in, p, l, batch=batch, seq=seq)]
        h = _out_ffn(h, ys, w_out_b, _row(ffn_norm[l]), w1_b, w3_b, w2_b, l, batch=batch, seq=seq)
    return h.reshape(batch, seq, D_MODEL)
```

```python
import functools
import math

import jax
import jax.numpy as jnp
import numpy as np
from jax import lax
from jax.experimental import pallas as pl
from jax.experimental.pallas import tpu as pltpu

F32 = jnp.float32
BF16 = jnp.bfloat16
HIGHEST = lax.Precision.HIGHEST

D_MODEL = 1024
GROUP_W = 256
HEAD_DIM = 64
N_HEADS = 4
EPS = 1e-6
NEG_INF = -1e30

MLA_NOPE = 64
MLA_ROPE = 32
MLA_DQK = MLA_NOPE + MLA_ROPE
MLA_KV_RANK = 128
ROPE_THETA = 10000.0

S5_GROUP_CH = 16
S5_GROUPS = 16
S5_STATE = 64
S5_WIDTH = S5_GROUPS * S5_STATE

DIL_PAIRS = ((128, 1), (512, 4), (2048, 16))
DIL_SPAN = 128
T5_BUCKETS = 32
T5_MAX_DIST = 2048

DN_CONV = 4
DN_CHUNK = 64

FFN_HIDDEN = 2816
FFN_CHUNK = 2816

VMEM_LIMIT_BYTES = 56 * 1024 * 1024
LANES = 128

MLA_SLAB = 512
S5_SLAB = 256
DIL_SLAB = 768
DN_SLAB = 1152
PROJ_COLS = MLA_SLAB + S5_SLAB + DIL_SLAB + DN_SLAB


def _dot(a, b, precision=None):
    return jnp.dot(a, b, preferred_element_type=F32, precision=precision)


def _dot_nt(a, b, precision=None):
    return lax.dot_general(a, b, (((1,), (1,)), ((), ())), preferred_element_type=F32, precision=precision)


def _dot_split(a, b, terms, split_rhs=False):
    x = b if split_rhs else a
    mask = (a if split_rhs else b).astype(BF16)
    out = None
    for _ in range(terms):
        piece = x.astype(BF16)
        part = _dot(mask, piece) if split_rhs else _dot(piece, mask)
        out = part if out is None else out + part
        x = x - piece.astype(F32)
    return out


def _const_spec(shape):
    nd = len(shape)
    return pl.BlockSpec(shape, lambda *_: (0,) * nd, pipeline_mode=pl.Buffered(1))


def _params(*sem):
    return pltpu.CompilerParams(dimension_semantics=sem, vmem_limit_bytes=VMEM_LIMIT_BYTES)


def _proj_body(x_ref, g_ref, w_ref, mla_ref, s5_ref, dil_ref, dn_ref):
    x = x_ref[...]
    n = x * lax.rsqrt(jnp.mean(x * x, axis=-1, keepdims=True) + EPS) * g_ref[...]
    nb = n.astype(BF16)
    start = 0
    for ref in (mla_ref, s5_ref, dil_ref, dn_ref):
        width = ref.shape[-1]
        ref[...] = _dot(nb, w_ref[:, start:start + width])
        start += width


def _layer_spec(shape, layer):
    nd = len(shape)
    return pl.BlockSpec((None,) + tuple(shape), lambda *_: (layer,) + (0,) * nd, pipeline_mode=pl.Buffered(1))


def _proj(h, gain, w_big, layer, *, batch, seq, tm=1024):
    nt = seq // tm
    widths = (MLA_SLAB, S5_SLAB, DIL_SLAB, DN_SLAB)
    return pl.pallas_call(
        _proj_body,
        grid=(batch, nt),
        in_specs=[pl.BlockSpec((tm, D_MODEL), lambda b, i: (b * nt + i, 0)),
                  _const_spec((1, D_MODEL)),
                  _layer_spec((D_MODEL, PROJ_COLS), layer)],
        out_specs=[pl.BlockSpec((tm, w), lambda b, i: (i, b)) for w in widths],
        out_shape=[jax.ShapeDtypeStruct((seq, batch * w), F32) for w in widths],
        compiler_params=_params("parallel", "parallel"),
        name="proj",
    )(h, gain, w_big)


def _out_ffn_body(h_ref, y0_ref, y1_ref, y2_ref, y3_ref, wo_ref, g_ref, w1_ref, w3_ref, w2_ref, o_ref, acc_ref):
    mixed = jnp.concatenate([y_ref[...].astype(BF16) for y_ref in (y0_ref, y1_ref, y2_ref, y3_ref)], axis=1)
    h = h_ref[...] + _dot(mixed, wo_ref[...])
    n = h * lax.rsqrt(jnp.mean(h * h, axis=-1, keepdims=True) + EPS) * g_ref[...]
    nb = n.astype(BF16)
    acc_ref[...] = h

    def hidden_chunk(c, carry):
        cols = pl.ds(pl.multiple_of(c * FFN_CHUNK, FFN_CHUNK), FFN_CHUNK)
        a = _dot(nb, w1_ref[:, cols])
        b = _dot(nb, w3_ref[:, cols])
        z = (a * jax.nn.sigmoid(a) * b).astype(BF16)
        acc_ref[...] += _dot(z, w2_ref[cols, :])
        return carry
    lax.fori_loop(0, FFN_HIDDEN // FFN_CHUNK, hidden_chunk, 0)
    o_ref[...] = acc_ref[...]


def _out_ffn(h, ys, w_out, gain, w1, w3, w2, layer, *, batch, seq, tm=512):
    nt = seq // tm
    row = pl.BlockSpec((tm, D_MODEL), lambda b, i: (b * nt + i, 0))
    slab = pl.BlockSpec((tm, GROUP_W), lambda b, i: (i, b))
    return pl.pallas_call(
        _out_ffn_body,
        grid=(batch, nt),
        in_specs=[row] + [slab] * 4 + [
            _layer_spec((D_MODEL, D_MODEL), layer), _const_spec((1, D_MODEL)),
            _layer_spec((D_MODEL, FFN_HIDDEN), layer), _layer_spec((D_MODEL, FFN_HIDDEN), layer),
            _layer_spec((FFN_HIDDEN, D_MODEL), layer)],
        out_specs=row,
        out_shape=jax.ShapeDtypeStruct((batch * seq, D_MODEL), F32),
        scratch_shapes=[pltpu.VMEM((tm, D_MODEL), F32)],
        compiler_params=_params("parallel", "parallel"),
        name="out_ffn",
    )(h, *ys, w_out, gain, w1, w3, w2)


MLA_BLOCK = 256
MLA_SWEEP = (4, 2, 1)


def _mla_body(x_ref, qn_ref, kvn_ref, wuq_ref, wuk_ref, wuv_ref, gq_ref, gk_ref, cos_ref, sin_ref,
              o_ref, k_scr, vt_scr, q_scr, *, seq):
    blk = MLA_BLOCK
    n_pairs = N_HEADS // 2
    lane = lax.broadcasted_iota(jnp.int32, (blk, LANES), 1)

    def rope(x, c, s):
        rot = jnp.where(lane < MLA_NOPE + MLA_ROPE // 2, pltpu.roll(x, LANES - MLA_ROPE // 2, 1),
                        pltpu.roll(x, MLA_ROPE // 2, 1))
        return x * c + rot * s

    def norm_head(x, g):
        ssq = jnp.sum(x * x, axis=-1, keepdims=True)
        return x * lax.rsqrt(ssq * (1.0 / MLA_DQK) + EPS) * g

    heads = range(N_HEADS)

    def prepare_steps(i):
        rows = pl.ds(i * blk if isinstance(i, int) else pl.multiple_of(i * blk, blk), blk)
        ckv = x_ref[rows, 256:384]
        cq = x_ref[rows, 0:256]
        kvn = ckv * lax.rsqrt(jnp.mean(ckv * ckv, axis=-1, keepdims=True) + EPS) * kvn_ref[...]
        qn = cq * lax.rsqrt(jnp.mean(cq * cq, axis=-1, keepdims=True) + EPS) * qn_ref[...]
        kvn = kvn.astype(BF16)
        k_nope = _dot(kvn, wuk_ref[...])
        v = _dot(kvn, wuv_ref[...])
        q_all = _dot(qn.astype(BF16), wuq_ref[...])
        yield
        k_rope = pltpu.roll(x_ref[rows, 384:512], MLA_NOPE, 1)
        c = cos_ref[rows, :]
        s = sin_ref[rows, :]
        ks = [rope(norm_head(k_nope[:, h * LANES:(h + 1) * LANES] + k_rope, gk_ref[...]), c, s) for h in heads]
        yield
        qs_new = [rope(norm_head(q_all[:, h * LANES:(h + 1) * LANES], gq_ref[...]), c, s) * (MLA_DQK ** -0.5)
                  for h in heads]
        vts = [v[:, pr * LANES:(pr + 1) * LANES].T for pr in range(n_pairs)]
        yield
        yield
        for h in heads:
            k_scr[h, rows, :] = ks[h].astype(BF16)
            q_scr[h] = qs_new[h].astype(BF16)
        for pr in range(n_pairs):
            vt_scr[pr, :, rows] = vts[pr].astype(BF16)

    def weave(*steps):
        live = list(steps)
        while live:
            for gen in list(live):
                if next(gen, "done") == "done":
                    live.remove(gen)

    key_pos = lax.broadcasted_iota(jnp.int32, (blk, blk), 0)
    query_pos = lax.broadcasted_iota(jnp.int32, (blk, blk), 1)
    first_of_pair = lax.broadcasted_iota(jnp.int32, (LANES, blk), 0) < HEAD_DIM

    def query_block(qi, _):
        rows = pl.ds(pl.multiple_of(qi * blk, blk), blk)
        qs = [q_scr[h] for h in heads]

        def kv_steps(first_key, width, carry, result, diagonal=False):
            m_old, l_old, acc_old = carry
            keys = pl.ds(pl.multiple_of(first_key, blk), width)
            logits = [_dot_nt(k_scr[h, keys, :], qs[h]) for h in heads]
            yield
            if diagonal:
                logits = [jnp.where(key_pos <= query_pos, x, NEG_INF) for x in logits]
            m_new = [jnp.maximum(m_old[h], jnp.max(logits[h], axis=0, keepdims=True)) for h in heads]
            alpha = [jnp.exp(m_old[h] - m_new[h]) for h in heads]
            p = [jnp.exp(logits[h] - m_new[h]) for h in heads]
            l_new = [alpha[h] * l_old[h] + jnp.sum(p[h], axis=0, keepdims=True) for h in heads]
            pv = [_dot(vt_scr[h // 2, :, keys], p[h].astype(BF16)) for h in heads]
            yield
            acc_new = [jnp.where(first_of_pair, alpha[2 * pr] * acc_old[pr] + pv[2 * pr],
                                 alpha[2 * pr + 1] * acc_old[pr] + pv[2 * pr + 1]) for pr in range(n_pairs)]
            result.append((tuple(m_new), tuple(l_new), tuple(acc_new)))

        def kv_step(first_key, width, carry):
            result = []
            weave(kv_steps(first_key, width, carry, result))
            return result[0]

        def last_steps(carry):
            result = []
            yield from kv_steps(qi * blk, blk, carry, result, diagonal=True)
            _, l, acc = result[0]
            for pr in range(n_pairs):
                out_t = acc[pr] / jnp.where(first_of_pair, l[2 * pr], l[2 * pr + 1])
                o_ref[rows, pr * LANES:(pr + 1) * LANES] = out_t.T

        carry = (tuple(jnp.full((1, blk), NEG_INF, F32) for _ in heads),
                 tuple(jnp.zeros((1, blk), F32) for _ in heads),
                 tuple(jnp.zeros((LANES, blk), F32) for _ in range(n_pairs)))
        done = 0
        for n_blk in MLA_SWEEP:
            todo = (qi - done) // n_blk
            carry = lax.fori_loop(0, todo, lambda j, cr, first=done, n_blk=n_blk:
                                  kv_step((first + j * n_blk) * blk, n_blk * blk, cr), carry)
            done = done + todo * n_blk

        @pl.when(qi + 1 < seq // blk)
        def _last_and_prepare_next():
            weave(last_steps(carry), prepare_steps(qi + 1))

        @pl.when(qi + 1 == seq // blk)
        def _last():
            weave(last_steps(carry))
        return 0

    weave(prepare_steps(0))
    lax.fori_loop(0, seq // blk, query_block, 0)


def _mla(slab, qn, kvn, wuq, wuk, wuv, gq, gk, cos, sin, *, batch, seq):
    nq = seq // MLA_BLOCK
    return pl.pallas_call(
        functools.partial(_mla_body, seq=seq),
        grid=(batch,),
        in_specs=[pl.BlockSpec((seq, MLA_SLAB), lambda b: (0, b)),
                  _const_spec((1, 256)), _const_spec((1, MLA_KV_RANK)),
                  _const_spec((256, N_HEADS * LANES)), _const_spec((MLA_KV_RANK, N_HEADS * LANES)),
                  _const_spec((MLA_KV_RANK, GROUP_W)),
                  _const_spec((1, LANES)), _const_spec((1, LANES)),
                  _const_spec((seq, LANES)), _const_spec((seq, LANES))],
        out_specs=pl.BlockSpec((seq, GROUP_W), lambda b: (0, b)),
        out_shape=jax.ShapeDtypeStruct((seq, batch * GROUP_W), F32),
        scratch_shapes=[pltpu.VMEM((N_HEADS, seq, LANES), BF16), pltpu.VMEM((N_HEADS // 2, LANES, seq), BF16),
                        pltpu.VMEM((N_HEADS, MLA_BLOCK, LANES), BF16)],
        compiler_params=_params("parallel"),
        name="mla",
    )(slab, qn, kvn, wuq, wuk, wuv, gq, gk, cos, sin)


def _s5_body(u_ref, lre_ref, lim_ref, ldt_ref, bre_ref, bim_ref, cre_ref, cim_ref, d_ref, wglu_ref, o_ref,
             a_scr, bbar_scr, h_scr, x_scr, io_scr, *, batch, steps):
    io_tiles = GROUP_W // LANES

    @pl.when(pl.program_id(0) == 0)
    def _discretise():
        lr = lre_ref[...]
        li = lim_ref[...]
        dt = jnp.exp(ldt_ref[...])
        mag = jnp.exp(lr * dt)
        ar = mag * jnp.cos(li * dt)
        ai = mag * jnp.sin(li * dt)
        den = lr * lr + li * li
        nr = ar - 1.0
        zr = (nr * lr + ai * li) / den
        zi = (ai * lr - nr * li) / den
        a_scr[0] = jnp.broadcast_to(ar, (batch, S5_WIDTH))
        a_scr[1] = jnp.broadcast_to(ai, (batch, S5_WIDTH))
        bre = bre_ref[...]
        bim = bim_ref[...]
        bbar_scr[0] = (zr * bre - zi * bim).astype(BF16)
        bbar_scr[1] = (zr * bim + zi * bre).astype(BF16)
        h_scr[...] = jnp.zeros_like(h_scr)

    for b in range(batch):
        for j in range(io_tiles):
            io_scr[j, pl.ds(b, steps, stride=batch), :] = u_ref[:, b * GROUP_W + j * LANES:b * GROUP_W + (j + 1) * LANES]
    u = jnp.concatenate([io_scr[j] for j in range(io_tiles)], axis=1)
    ub = u.astype(BF16)
    x_scr[0] = _dot(ub, bbar_scr[0])
    x_scr[1] = _dot(ub, bbar_scr[1])
    ar = a_scr[0]
    ai = a_scr[1]

    def step(t, carry):
        hr, hi = carry
        rows = pl.ds(pl.multiple_of(t * batch, batch), batch)
        nhr = ar * hr - ai * hi + x_scr[0, rows, :]
        nhi = ar * hi + ai * hr + x_scr[1, rows, :]
        x_scr[0, rows, :] = nhr
        x_scr[1, rows, :] = nhi
        return nhr, nhi

    hr, hi = lax.fori_loop(0, steps, step, (h_scr[0], h_scr[1]), unroll=8)
    h_scr[0] = hr
    h_scr[1] = hi
    y = _dot(x_scr[0].astype(BF16), cre_ref[...]) - _dot(x_scr[1].astype(BF16), cim_ref[...]) + d_ref[...] * u
    z = _dot(y.astype(BF16), wglu_ref[...])
    out = z[:, :GROUP_W] * jax.nn.sigmoid(z[:, GROUP_W:])
    for j in range(io_tiles):
        io_scr[j] = out[:, j * LANES:(j + 1) * LANES]
    for b in range(batch):
        for j in range(io_tiles):
            o_ref[:, b * GROUP_W + j * LANES:b * GROUP_W + (j + 1) * LANES] = io_scr[j, pl.ds(b, steps, stride=batch), :]


def _s5(u, lre, lim, ldt, bre, bim, cre, cim, d, wglu, *, batch, seq, steps=256):
    state = lambda rows: pltpu.VMEM((2, rows, S5_WIDTH), F32)
    return pl.pallas_call(
        functools.partial(_s5_body, batch=batch, steps=steps),
        grid=(seq // steps,),
        in_specs=[pl.BlockSpec((steps, batch * S5_SLAB), lambda i: (i, 0)),
                  _const_spec((1, S5_WIDTH)), _const_spec((1, S5_WIDTH)), _const_spec((1, S5_WIDTH)),
                  _const_spec((GROUP_W, S5_WIDTH)), _const_spec((GROUP_W, S5_WIDTH)),
                  _const_spec((S5_WIDTH, GROUP_W)), _const_spec((S5_WIDTH, GROUP_W)),
                  _const_spec((1, GROUP_W)), _const_spec((GROUP_W, 2 * GROUP_W))],
        out_specs=pl.BlockSpec((steps, batch * GROUP_W), lambda i: (i, 0)),
        out_shape=jax.ShapeDtypeStruct((seq, batch * GROUP_W), F32),
        scratch_shapes=[state(batch), pltpu.VMEM((2, GROUP_W, S5_WIDTH), BF16), state(batch),
                        state(steps * batch), pltpu.VMEM((GROUP_W // LANES, steps * batch, LANES), F32)],
        compiler_params=_params("arbitrary"),
        name="s5",
    )(u, lre, lim, ldt, bre, bim, cre, cim, d, wglu)


DIL_TILE = 256
DIL_GROUP = 2


def _pair_norm(x, gain2, same_head):
    ms = _dot_split(x * x, same_head, 2) * (1.0 / HEAD_DIM)
    return x * lax.rsqrt(ms + EPS) * gain2


def _dil_body(x_ref, gq_ref, gk_ref, band_ref, o_ref, bias_ref, q_scr, k_scr, v_scr, oa, ma, la, *, seq):
    span = DIL_SPAN
    n_blocks = seq // span
    n_tiles = GROUP_W // LANES
    lo_blk = lax.broadcasted_iota(jnp.int32, (span, LANES), 1) < HEAD_DIM
    same_head = ((lax.broadcasted_iota(jnp.int32, (LANES, LANES), 0) < HEAD_DIM)
                 == (lax.broadcasted_iota(jnp.int32, (LANES, LANES), 1) < HEAD_DIM)).astype(F32)

    for bh in range(len(DIL_PAIRS) * N_HEADS):
        profile = jnp.broadcast_to(band_ref[bh:bh + 1, :], (span, 2 * span))
        bias_ref[bh] = pltpu.roll(profile, 0, 1, stride=1, stride_axis=0)
    in_current = lax.broadcasted_iota(jnp.int32, (span, 2 * span), 1) >= span

    def norm_tile(i, carry):
        rows = pl.ds(pl.multiple_of(i * DIL_TILE, DIL_TILE), DIL_TILE)
        for j in range(n_tiles):
            cols = lambda part: slice(part * GROUP_W + j * LANES, part * GROUP_W + (j + 1) * LANES)
            q = _pair_norm(x_ref[rows, cols(0)], gq_ref[...], same_head)
            q_scr[j, rows, :] = q * (HEAD_DIM ** -0.5)
            k_scr[j, rows, :] = _pair_norm(x_ref[rows, cols(1)], gk_ref[...], same_head)
            v_scr[j, rows, :] = x_ref[rows, cols(2)]
        return carry
    lax.fori_loop(0, seq // DIL_TILE, norm_tile, 0)

    for bi, (window, dil) in enumerate(DIL_PAIRS):
        sub_len = seq // dil
        nb = sub_len // span

        def block_group(i, carry, bi=bi, nb=nb, dil=dil):
            todo = []
            for u in range(DIL_GROUP):
                t = i * DIL_GROUP + u
                r = lax.shift_right_logical(t, int(math.log2(nb)))
                c = t & (nb - 1)
                first = r + c * (dil * span)
                stride = dil if dil > 1 else None
                rows = pl.ds(first, span, stride=stride)
                prev = pl.ds(jnp.maximum(first - dil * span, r), span, stride=stride)
                keep = jnp.logical_or(in_current, c != 0) if nb > 1 else None
                for j in range(n_tiles):
                    q2 = q_scr[j, rows, :]
                    if nb > 1:
                        k_cat = jnp.concatenate([k_scr[j, prev, :], k_scr[j, rows, :]], axis=0).astype(BF16)
                        v_cat = jnp.concatenate([v_scr[j, prev, :], v_scr[j, rows, :]], axis=0).astype(BF16)
                    else:
                        k_cat = k_scr[j, rows, :].astype(BF16)
                        v_cat = v_scr[j, rows, :].astype(BF16)
                    todo.append((rows, j, q2, k_cat, v_cat, keep))
            logits = []
            for rows, j, q2, k_cat, v_cat, keep in todo:
                for a in range(2):
                    qa = jnp.where(lo_blk if a == 0 else jnp.logical_not(lo_blk), q2, 0.0).astype(BF16)
                    bias = bias_ref[bi * N_HEADS + 2 * j + a]
                    lg = _dot_nt(qa, k_cat) + (bias if nb > 1 else bias[:, span:])
                    logits.append(jnp.where(keep, lg, NEG_INF) if nb > 1 else lg)
            m = [jnp.max(lg, axis=-1, keepdims=True) for lg in logits]
            p = [jnp.exp(lg - mx) for lg, mx in zip(logits, m)]
            l = [jnp.sum(px, axis=-1, keepdims=True) for px in p]
            o = [_dot(p[2 * n + a].astype(BF16), todo[n][4]) for n in range(len(todo)) for a in range(2)]
            for n, (rows, j, *_) in enumerate(todo):
                o_in = jnp.where(lo_blk, o[2 * n], o[2 * n + 1])
                m_in = jnp.where(lo_blk, m[2 * n], m[2 * n + 1])
                l_in = jnp.where(lo_blk, l[2 * n], l[2 * n + 1])
                if bi > 0:
                    m_old = ma[j, rows, :]
                    m_new = jnp.maximum(m_old, m_in)
                    w_old, w_in = jnp.exp(m_old - m_new), jnp.exp(m_in - m_new)
                    o_in = w_old * oa[j, rows, :] + w_in * o_in
                    l_in = w_old * la[j, rows, :] + w_in * l_in
                    m_in = m_new
                oa[j, rows, :] = o_in
                ma[j, rows, :] = m_in
                la[j, rows, :] = l_in
            return carry
        lax.fori_loop(0, n_blocks // DIL_GROUP, block_group, 0)

    def finish(i, carry):
        rows = pl.ds(pl.multiple_of(i * DIL_TILE, DIL_TILE), DIL_TILE)
        for j in range(n_tiles):
            o_ref[rows, j * LANES:(j + 1) * LANES] = oa[j, rows, :] / la[j, rows, :]
        return carry
    lax.fori_loop(0, seq // DIL_TILE, finish, 0)


def _dil(slab, gq2, gk2, band, *, batch, seq):
    big = lambda: pltpu.VMEM((GROUP_W // LANES, seq, LANES), F32)
    n_bias = len(DIL_PAIRS) * N_HEADS
    return pl.pallas_call(
        functools.partial(_dil_body, seq=seq),
        grid=(batch,),
        in_specs=[pl.BlockSpec((seq, DIL_SLAB), lambda b: (0, b)),
                  _const_spec((1, LANES)), _const_spec((1, LANES)),
                  _const_spec(band.shape)],
        out_specs=pl.BlockSpec((seq, GROUP_W), lambda b: (0, b)),
        out_shape=jax.ShapeDtypeStruct((seq, batch * GROUP_W), F32),
        scratch_shapes=[pltpu.VMEM((n_bias, DIL_SPAN, 2 * DIL_SPAN), F32)] + [big() for _ in range(6)],
        compiler_params=_params("parallel"),
        name="dilated",
    )(slab, gq2, gk2, band)


def _t5_bucket(dist):
    exact = T5_BUCKETS // 2
    df = jnp.maximum(dist, 1).astype(F32)
    large = exact + (jnp.log(df / exact) / math.log(T5_MAX_DIST / exact) * (T5_BUCKETS - exact)).astype(jnp.int32)
    large = jnp.minimum(large, T5_BUCKETS - 1)
    return jnp.where(dist < exact, dist, large)


def _dil_band(table):
    span = DIL_SPAN
    delta = span - jnp.arange(2 * span, dtype=jnp.int32)
    rows = []
    for _, dil in DIL_PAIRS:
        bucket = _t5_bucket(jnp.clip(delta, 0, span) * dil)
        onehot = (bucket[:, None] == jnp.arange(T5_BUCKETS, dtype=jnp.int32)[None, :]).astype(F32)
        vals = jnp.dot(onehot, table.astype(F32), precision=HIGHEST)
        rows.append(jnp.where((delta >= 0)[:, None], vals, NEG_INF).T)
    return jnp.concatenate(rows, axis=0)


DN_TILE = 256
DN_GROUP = 4
PREP_STORE_DELAY = 12


def _softplus(x):
    return jnp.maximum(x, 0.0) + jnp.log1p(jnp.exp(-jnp.abs(x)))


def _pair_l2(x, lo_half):
    sq = x * x
    tot = jnp.sum(sq, axis=-1, keepdims=True)
    lo = jnp.sum(jnp.where(lo_half, sq, 0.0), axis=-1, keepdims=True)
    return x * lax.rsqrt(jnp.where(lo_half, lo, tot - lo) + EPS)


def _dn_body(x_ref, cw_ref, alog_ref, dtb_ref, on_ref, o_ref, q_scr, k_scr, v_scr, g_scr, b_scr, w_scr, a_scr,
             s_scr, *, seq):
    c = DN_CHUNK
    w = GROUP_W
    lo_tile = lax.broadcasted_iota(jnp.int32, (DN_TILE, LANES), 1) < HEAD_DIM

    def prep_steps(i):
        if isinstance(i, int):
            first, halo_first = i * DN_TILE, max(i * DN_TILE - 8, 0)
        else:
            first = pl.multiple_of(i * DN_TILE, DN_TILE)
            halo_first = pl.multiple_of(jnp.maximum(first - 8, 0), 8)
        rows = pl.ds(first, DN_TILE)
        cur = x_ref[rows, 0:3 * w]
        halo = jnp.where(i > 0, x_ref[pl.ds(halo_first, 8), 0:3 * w], 0.0)
        ext = jnp.concatenate([halo, cur], axis=0)
        acc = cw_ref[DN_CONV - 1:DN_CONV, :] * cur
        for j in range(DN_CONV - 1):
            acc = acc + cw_ref[j:j + 1, :] * pltpu.roll(ext, DN_CONV - 1 - j, 0)[8:, :]
        y = acc * jax.nn.sigmoid(acc)
        yield
        tiles = range(w // LANES)
        cols = lambda part, j: slice(part * w + j * LANES, part * w + (j + 1) * LANES)
        q = [_pair_l2(y[:, cols(0, j)], lo_tile) * (HEAD_DIM ** -0.5) for j in tiles]
        k = [_pair_l2(y[:, cols(1, j)], lo_tile) for j in tiles]
        yield
        ab = x_ref[rows, 3 * w:3 * w + LANES]
        g = -jnp.exp(alog_ref[...]) * _softplus(ab + dtb_ref[...])
        gc = _dot_split(chunk_tril, g, 3, split_rhs=True)
        yield
        gc = per_head_lanes(gc, 0)
        beta = per_head_lanes(jax.nn.sigmoid(ab), N_HEADS)
        for _ in range(PREP_STORE_DELAY):
            yield
        for j in tiles:
            q_scr[rows, j * LANES:(j + 1) * LANES] = q[j]
            k_scr[rows, j * LANES:(j + 1) * LANES] = k[j]
            v_scr[rows, j * LANES:(j + 1) * LANES] = y[:, cols(2, j)]
        g_scr[rows, :] = gc
        b_scr[rows, :] = beta

    ri = lax.broadcasted_iota(jnp.int32, (c, w), 0)
    ci = lax.broadcasted_iota(jnp.int32, (c, w), 1)
    cj = jnp.bitwise_and(ci, HEAD_DIM - 1)
    causal = ri >= cj
    strict = ri > cj
    eye4 = (ri == cj).astype(F32)
    same_sub = lax.shift_right_logical(ri, 4) == lax.shift_right_logical(cj, 4)
    bi_r = lax.broadcasted_iota(jnp.int32, (w, w), 0)
    bi_c = lax.broadcasted_iota(jnp.int32, (w, w), 1)
    same_head = lax.shift_right_logical(bi_r, 6) == lax.shift_right_logical(bi_c, 6)
    block_mask = same_head.astype(F32)
    chunk_tril = (same_head & (jnp.bitwise_and(bi_r, c - 1) >= jnp.bitwise_and(bi_c, c - 1))).astype(F32)
    head_of_lane = lax.shift_right_logical(lax.broadcasted_iota(jnp.int32, (DN_TILE, w), 1), 6)

    def per_head_lanes(x, first_lane):
        out = None
        for h in range(N_HEADS):
            col = jnp.broadcast_to(x[:, first_lane + h:first_lane + h + 1], (DN_TILE, w))
            out = col if out is None else jnp.where(head_of_lane == h, col, out)
        return out

    def per_head(a, b):
        bd = jnp.where(same_head, jnp.concatenate([b.astype(BF16)] * N_HEADS, axis=0), 0.0)
        return _dot(a.astype(BF16), bd)

    def group_rows(i):
        base = pl.multiple_of(i * (DN_GROUP * c), DN_GROUP * c)
        return [pl.ds(base + u * c, c) for u in range(DN_GROUP)]

    def solve_steps(i):
        rows = group_rows(i)
        each = lambda f, *lists: [f(*args) for args in zip(*lists)]
        q, k, v, gc, beta = ([ref[r, :] for r in rows] for ref in (q_scr, k_scr, v_scr, g_scr, b_scr))
        g_row = each(lambda g: jnp.sum(g * eye4, axis=0, keepdims=True), gc)
        decay = each(lambda g, gr: jnp.exp(jnp.where(causal, g - gr, NEG_INF)), gc, g_row)
        kb = each(jnp.multiply, k, beta)
        k_bd = each(lambda x: jnp.where(same_head, jnp.concatenate([x.astype(BF16)] * N_HEADS, axis=0), 0.0), k)
        lmat = each(lambda a, b, d: jnp.where(strict, _dot_nt(a.astype(BF16), b) * d, 0.0), kb, k_bd, decay)
        yield
        a_qk = each(lambda a, b, d: jnp.where(causal, _dot_nt(a.astype(BF16), b) * d, 0.0), q, k_bd, decay)
        yield
        p = each(lambda l_: jnp.where(same_sub, -l_, 0.0), lmat)
        t_diag = each(lambda x: eye4 + x, p)
        for _ in range(3):
            p = each(per_head, p, p)
            yield
            t_diag = each(lambda t, x: t + per_head(t, x), t_diag, p)
            yield
        nil = each(lambda t, l_: per_head(t, jnp.where(same_sub, 0.0, l_)), t_diag, lmat)
        yield
        nil2 = each(per_head, nil, nil)
        yield
        nil3 = each(per_head, nil, nil2)
        yield
        t_inv = each(lambda n1, n2, n3, t: per_head(eye4 - n1 + n2 - n3, t), nil, nil2, nil3, t_diag)
        yield
        eg = each(jnp.exp, gc)
        w_c = each(lambda t, a, e: per_head(t, a * e), t_inv, kb, eg)
        yield
        u_c = each(lambda t, a, b: per_head(t, a * b), t_inv, v, beta)
        yield
        q_dec = each(jnp.multiply, q, eg)
        k_dec = each(lambda x, g: x * jnp.exp(g[c - 1:c, :] - g), k, gc)
        for ref, vals in zip((w_scr, v_scr, a_scr, q_scr, k_scr), (w_c, u_c, a_qk, q_dec, k_dec)):
            for r, val in zip(rows, vals):
                ref[r, :] = val

    def state_steps(i):
        for rows in group_rows(i):
            state = s_scr[...]
            state_b = state.astype(BF16)
            w_s = _dot(w_scr[rows, :].astype(BF16), state_b)
            q_s = _dot(q_scr[rows, :].astype(BF16), state_b)
            yield
            v_new = v_scr[rows, :] - w_s
            o_intra = per_head(a_scr[rows, :], v_new)
            upd = lax.dot_general(k_scr[rows, :].astype(BF16), v_new.astype(BF16), (((0,), (0,)), ((), ())),
                                  preferred_element_type=F32)
            yield
            o_ref[rows, :] = q_s + o_intra
            s_scr[...] = state * jnp.exp(g_scr[rows, :][c - 1:c, :]) + upd * block_mask
            yield

    def weave(*steps):
        live = list(steps)
        while live:
            for gen in list(live):
                if next(gen, "done") == "done":
                    live.remove(gen)

    n_groups = seq // (DN_GROUP * c)
    s_scr[...] = jnp.zeros_like(s_scr)
    weave(prep_steps(0))
    weave(solve_steps(0), prep_steps(1))

    def group(i, carry):
        weave(solve_steps(i), state_steps(i - 1), prep_steps(i + 1))
        return carry
    lax.fori_loop(1, n_groups - 1, group, 0)
    weave(solve_steps(n_groups - 1), state_steps(n_groups - 2))
    weave(state_steps(n_groups - 1))

    def finish_tile(i, carry):
        rows = pl.ds(pl.multiple_of(i * DN_TILE, DN_TILE), DN_TILE)
        o = o_ref[rows, :]
        ms = _dot_split(o * o, block_mask, 2) * (1.0 / HEAD_DIM)
        gate = x_ref[rows, 3 * w + LANES:4 * w + LANES]
        o_ref[rows, :] = o * lax.rsqrt(ms + EPS) * on_ref[...] * (gate * jax.nn.sigmoid(gate))
        return carry
    lax.fori_loop(0, seq // DN_TILE, finish_tile, 0)


def _dn(slab, conv_w, a_log, dt_bias, o_norm, *, batch, seq):
    wide = lambda: pltpu.VMEM((seq, GROUP_W), F32)
    return pl.pallas_call(
        functools.partial(_dn_body, seq=seq),
        grid=(batch,),
        in_specs=[pl.BlockSpec((seq, DN_SLAB), lambda b: (0, b)),
                  _const_spec((DN_CONV, 3 * GROUP_W)), _const_spec((1, LANES)), _const_spec((1, LANES)),
                  _const_spec((1, GROUP_W))],
        out_specs=pl.BlockSpec((seq, GROUP_W), lambda b: (0, b)),
        out_shape=jax.ShapeDtypeStruct((seq, batch * GROUP_W), F32),
        scratch_shapes=[wide() for _ in range(7)] + [pltpu.VMEM((GROUP_W, GROUP_W), F32)],
        compiler_params=_params("parallel"),
        name="deltanet",
    )(slab, conv_w, a_log, dt_bias, o_norm)


def _row(v, width=None):
    v = v.astype(F32).reshape(1, -1)
    if width is not None and v.shape[1] < width:
        v = jnp.pad(v, ((0, 0), (0, width - v.shape[1])))
    return v


def _prep_w_in(w):
    w = w.astype(BF16)
    z = lambda n: jnp.zeros(w.shape[:2] + (n,), BF16)
    return jnp.concatenate([w[..., 0:416], z(96), w[..., 416:2216], z(120), w[..., 2216:2472]], axis=-1)


def _pad_heads(w, per_head, lo, hi):
    k = w.shape[0]
    w = w.reshape(k, N_HEADS, per_head)[:, :, lo:hi]
    w = jnp.pad(w, ((0, 0), (0, 0), (0, LANES - (hi - lo))))
    return w.reshape(k, N_HEADS * LANES).astype(BF16)


def _rope_tables(seq):
    half = MLA_ROPE // 2
    pos = jnp.arange(seq, dtype=F32)
    freqs = ROPE_THETA ** (-jnp.arange(half, dtype=F32) / half)
    ang = pos[:, None] * freqs[None, :]
    cos, sin = jnp.cos(ang), jnp.sin(ang)
    ones = jnp.ones((seq, MLA_NOPE), F32)
    zeros = jnp.zeros((seq, MLA_NOPE), F32)
    tail1 = jnp.ones((seq, LANES - MLA_DQK), F32)
    tail0 = jnp.zeros((seq, LANES - MLA_DQK), F32)
    return (jnp.concatenate([ones, cos, cos, tail1], axis=1),
            jnp.concatenate([zeros, -sin, sin, tail0], axis=1))


def _block_diag(blocks):
    g, r, c = blocks.shape
    eye = jnp.eye(g, dtype=blocks.dtype)
    return (blocks[:, :, None, :] * eye[:, None, :, None]).reshape(g * r, g * c)


def _mla_layer(slab, p, l, *, batch, seq):
    w_ukv = p["mla_w_ukv"][l]
    cos, sin = _rope_tables(seq)
    return _mla(slab, _row(p["mla_q_norm"][l]), _row(p["mla_kv_norm"][l]),
                _pad_heads(p["mla_w_uq"][l], MLA_DQK, 0, MLA_DQK),
                _pad_heads(w_ukv, MLA_NOPE + HEAD_DIM, 0, MLA_NOPE),
                w_ukv.reshape(MLA_KV_RANK, N_HEADS, MLA_NOPE + HEAD_DIM)[:, :, MLA_NOPE:]
                .reshape(MLA_KV_RANK, GROUP_W).astype(BF16),
                _row(p["mla_qk_q"][l], LANES), _row(p["mla_qk_k"][l], LANES), cos, sin, batch=batch, seq=seq)


def _dil_layer(slab, p, l, *, batch, seq):
    pair = lambda g: jnp.tile(g.astype(F32).reshape(1, HEAD_DIM), (1, LANES // HEAD_DIM))
    return _dil(slab, pair(p["dil_q_norm"][l]), pair(p["dil_k_norm"][l]), _dil_band(p["t5_bias"]),
                batch=batch, seq=seq)


def _dn_layer(slab, p, l, *, batch, seq):
    return _dn(slab, p["dn_conv"][l].astype(F32), _row(p["dn_a_log"][l], LANES), _row(p["dn_dt_bias"][l], LANES),
               jnp.tile(p["dn_o_norm"][l].astype(F32).reshape(1, HEAD_DIM), (1, N_HEADS)), batch=batch, seq=seq)


def _s5_layer(u, p, l, *, batch, seq):
    state_row = lambda v: v.astype(F32).reshape(1, S5_WIDTH)
    ldt = jnp.broadcast_to(p["s5_log_dt"][l][:, None], (S5_GROUPS, S5_STATE))
    bre = _block_diag(jnp.swapaxes(p["s5_b_re"][l], 1, 2).astype(F32))
    bim = _block_diag(jnp.swapaxes(p["s5_b_im"][l], 1, 2).astype(F32))
    cre = _block_diag(jnp.swapaxes(p["s5_c_re"][l], 1, 2)).astype(BF16)
    cim = _block_diag(jnp.swapaxes(p["s5_c_im"][l], 1, 2)).astype(BF16)
    return _s5(u, state_row(p["s5_lambda_re"][l]), state_row(p["s5_lambda_im"][l]), state_row(ldt),
               bre, bim, cre, cim, _row(p["s5_d"][l]), p["s5_w_glu"][l].astype(BF16), batch=batch, seq=seq,
               steps=min(256, seq))


def kernel(x, attn_norm, w_in, w_out, mla_q_norm, mla_kv_norm, mla_w_uq, mla_w_ukv, mla_qk_q, mla_qk_k,
           s5_lambda_re, s5_lambda_im, s5_log_dt, s5_b_re, s5_b_im, s5_c_re, s5_c_im, s5_d, s5_w_glu,
           dil_q_norm, dil_k_norm, t5_bias, dn_conv, dn_a_log, dn_dt_bias, dn_o_norm,
           ffn_norm, ffn_w1, ffn_w3, ffn_w2):
    p = dict(mla_q_norm=mla_q_norm, mla_kv_norm=mla_kv_norm, mla_w_uq=mla_w_uq, mla_w_ukv=mla_w_ukv,
             mla_qk_q=mla_qk_q, mla_qk_k=mla_qk_k, s5_lambda_re=s5_lambda_re, s5_lambda_im=s5_lambda_im,
             s5_log_dt=s5_log_dt, s5_b_re=s5_b_re, s5_b_im=s5_b_im, s5_c_re=s5_c_re, s5_c_im=s5_c_im, s5_d=s5_d,
             s5_w_glu=s5_w_glu, dil_q_norm=dil_q_norm, dil_k_norm=dil_k_norm, t5_bias=t5_bias, dn_conv=dn_conv,
             dn_a_log=dn_a_log, dn_dt_bias=dn_dt_bias, dn_o_norm=dn_o_norm)
    batch, seq, _ = x.shape
    h = x.reshape(batch * seq, D_MODEL)
    w_in_b, w_out_b = _prep_w_in(w_in), w_out.astype(BF16)
    w1_b, w3_b, w2_b = ffn_w1.astype(BF16), ffn_w3.astype(BF16), ffn_w2.astype(BF16)
    for l in range(attn_norm.shape[0]):
        mla_in, s5_in, dil_in, dn_in = _proj(h, _row(attn_norm[l]), w_in_b, l, batch=batch, seq=seq)
        ys = [_mla_layer(mla_in, p, l, batch=batch, seq=seq),
              _s5_layer(s5_in, p, l, batch=batch, seq=seq),
              _dil_layer(dil_in, p, l, batch=batch, seq=seq),
              _dn_layer(dn_in, p, l, batch=batch, seq=seq)]
        h = _out_ffn(h, ys, w_out_b, _row(ffn_norm[l]), w1_b, w3_b, w2_b, l, batch=batch, seq=seq)
    return h.reshape(batch, seq, D_MODEL)
```

```python
import functools
import math

import jax
import jax.numpy as jnp
from jax import lax
from jax.experimental import pallas as pl
from jax.experimental.pallas import tpu as pltpu

F32 = jnp.float32
BF16 = jnp.bfloat16
HIGHEST = lax.Precision.HIGHEST

D_MODEL = 1024
GROUP_W = 256
HEAD_DIM = 64
N_HEADS = 4
EPS = 1e-6
NEG_INF = -1e30

MLA_NOPE = 64
MLA_ROPE = 32
MLA_DQK = MLA_NOPE + MLA_ROPE
MLA_KV_RANK = 128
ROPE_THETA = 10000.0

S5_GROUPS = 16
S5_STATE = 64
S5_WIDTH = S5_GROUPS * S5_STATE

DIL_PAIRS = ((128, 1), (512, 4), (2048, 16))
DIL_SPAN = 128
T5_BUCKETS = 32
T5_MAX_DIST = 2048

DN_CONV = 4
DN_CHUNK = 64

FFN_HIDDEN = 2816
FFN_CHUNK = 2816

VMEM_LIMIT_BYTES = 56 * 1024 * 1024
LANES = 128

MLA_SLAB = 512
S5_SLAB = 256
DIL_SLAB = 768
DN_SLAB = 1152
PROJ_COLS = MLA_SLAB + S5_SLAB + DIL_SLAB + DN_SLAB


def _dot(a, b, precision=None):
    return jnp.dot(a, b, preferred_element_type=F32, precision=precision)


def _dot_nt(a, b, precision=None):
    return lax.dot_general(a, b, (((1,), (1,)), ((), ())), preferred_element_type=F32, precision=precision)


def _dot_split(a, b, terms, split_rhs=False):
    x = b if split_rhs else a
    mask = (a if split_rhs else b).astype(BF16)
    out = None
    for _ in range(terms):
        piece = x.astype(BF16)
        part = _dot(mask, piece) if split_rhs else _dot(piece, mask)
        out = part if out is None else out + part
        x = x - piece.astype(F32)
    return out


def _const_spec(shape):
    nd = len(shape)
    return pl.BlockSpec(shape, lambda *_: (0,) * nd, pipeline_mode=pl.Buffered(1))


def _params(*sem):
    return pltpu.CompilerParams(dimension_semantics=sem, vmem_limit_bytes=VMEM_LIMIT_BYTES)


def _proj_body(x_ref, g_ref, w_ref, mla_ref, s5_ref, dil_ref, dn_ref):
    x = x_ref[...]
    n = x * lax.rsqrt(jnp.mean(x * x, axis=-1, keepdims=True) + EPS) * g_ref[...]
    nb = n.astype(BF16)
    start = 0
    for ref in (mla_ref, s5_ref, dil_ref, dn_ref):
        width = ref.shape[-1]
        ref[...] = _dot(nb, w_ref[:, start:start + width])
        start += width


def _layer_spec(shape, layer):
    nd = len(shape)
    return pl.BlockSpec((None,) + tuple(shape), lambda *_: (layer,) + (0,) * nd, pipeline_mode=pl.Buffered(1))


def _proj(h, gain, w_big, layer, *, batch, seq, tm=1024):
    nt = seq // tm
    widths = (MLA_SLAB, S5_SLAB, DIL_SLAB, DN_SLAB)
    return pl.pallas_call(
        _proj_body,
        grid=(batch, nt),
        in_specs=[pl.BlockSpec((tm, D_MODEL), lambda b, i: (b * nt + i, 0)),
                  _const_spec((1, D_MODEL)),
                  _layer_spec((D_MODEL, PROJ_COLS), layer)],
        out_specs=[pl.BlockSpec((tm, w), lambda b, i: (i, b)) for w in widths],
        out_shape=[jax.ShapeDtypeStruct((seq, batch * w), F32) for w in widths],
        compiler_params=_params("parallel", "parallel"),
        name="proj",
    )(h, gain, w_big)


def _out_ffn_body(h_ref, y0_ref, y1_ref, y2_ref, y3_ref, wo_ref, g_ref, w1_ref, w3_ref, w2_ref, o_ref, acc_ref):
    mixed = jnp.concatenate([y_ref[...].astype(BF16) for y_ref in (y0_ref, y1_ref, y2_ref, y3_ref)], axis=1)
    h = h_ref[...] + _dot(mixed, wo_ref[...])
    n = h * lax.rsqrt(jnp.mean(h * h, axis=-1, keepdims=True) + EPS) * g_ref[...]
    nb = n.astype(BF16)
    acc_ref[...] = h

    def hidden_chunk(c, carry):
        cols = pl.ds(pl.multiple_of(c * FFN_CHUNK, FFN_CHUNK), FFN_CHUNK)
        a = _dot(nb, w1_ref[:, cols])
        b = _dot(nb, w3_ref[:, cols])
        z = (a * jax.nn.sigmoid(a) * b).astype(BF16)
        acc_ref[...] += _dot(z, w2_ref[cols, :])
        return carry
    lax.fori_loop(0, FFN_HIDDEN // FFN_CHUNK, hidden_chunk, 0)
    o_ref[...] = acc_ref[...]


def _out_ffn(h, ys, w_out, gain, w1, w3, w2, layer, *, batch, seq, tm=512):
    nt = seq // tm
    row = pl.BlockSpec((tm, D_MODEL), lambda b, i: (b * nt + i, 0))
    slab = pl.BlockSpec((tm, GROUP_W), lambda b, i: (i, b))
    return pl.pallas_call(
        _out_ffn_body,
        grid=(batch, nt),
        in_specs=[row] + [slab] * 4 + [
            _layer_spec((D_MODEL, D_MODEL), layer), _const_spec((1, D_MODEL)),
            _layer_spec((D_MODEL, FFN_HIDDEN), layer), _layer_spec((D_MODEL, FFN_HIDDEN), layer),
            _layer_spec((FFN_HIDDEN, D_MODEL), layer)],
        out_specs=row,
        out_shape=jax.ShapeDtypeStruct((batch * seq, D_MODEL), F32),
        scratch_shapes=[pltpu.VMEM((tm, D_MODEL), F32)],
        compiler_params=_params("parallel", "parallel"),
        name="out_ffn",
    )(h, *ys, w_out, gain, w1, w3, w2)


MLA_BLOCK = 256
MLA_SWEEP = (4, 2, 1)


def _mla_body(x_ref, qn_ref, kvn_ref, wuq_ref, wuk_ref, wuv_ref, gq_ref, gk_ref, cos_ref, sin_ref,
              o_ref, k_scr, vt_scr, q_scr, *, seq):
    blk = MLA_BLOCK
    n_pairs = N_HEADS // 2
    lane = lax.broadcasted_iota(jnp.int32, (blk, LANES), 1)

    def rope(x, c, s):
        rot = jnp.where(lane < MLA_NOPE + MLA_ROPE // 2, pltpu.roll(x, LANES - MLA_ROPE // 2, 1),
                        pltpu.roll(x, MLA_ROPE // 2, 1))
        return x * c + rot * s

    def norm_head(x, g):
        ssq = jnp.sum(x * x, axis=-1, keepdims=True)
        return x * lax.rsqrt(ssq * (1.0 / MLA_DQK) + EPS) * g

    heads = range(N_HEADS)

    def prepare_steps(i):
        rows = pl.ds(i * blk if isinstance(i, int) else pl.multiple_of(i * blk, blk), blk)
        ckv = x_ref[rows, 256:384]
        cq = x_ref[rows, 0:256]
        kvn = ckv * lax.rsqrt(jnp.mean(ckv * ckv, axis=-1, keepdims=True) + EPS) * kvn_ref[...]
        qn = cq * lax.rsqrt(jnp.mean(cq * cq, axis=-1, keepdims=True) + EPS) * qn_ref[...]
        kvn = kvn.astype(BF16)
        k_nope = _dot(kvn, wuk_ref[...])
        v = _dot(kvn, wuv_ref[...])
        q_all = _dot(qn.astype(BF16), wuq_ref[...])
        yield
        k_rope = pltpu.roll(x_ref[rows, 384:512], MLA_NOPE, 1)
        c = cos_ref[rows, :]
        s = sin_ref[rows, :]
        ks = [rope(norm_head(k_nope[:, h * LANES:(h + 1) * LANES] + k_rope, gk_ref[...]), c, s) for h in heads]
        yield
        qs_new = [rope(norm_head(q_all[:, h * LANES:(h + 1) * LANES], gq_ref[...]), c, s) * (MLA_DQK ** -0.5)
                  for h in heads]
        vts = [v[:, pr * LANES:(pr + 1) * LANES].T for pr in range(n_pairs)]
        yield
        yield
        for h in heads:
            k_scr[h, rows, :] = ks[h].astype(BF16)
            q_scr[h] = qs_new[h].astype(BF16)
        for pr in range(n_pairs):
            vt_scr[pr, :, rows] = vts[pr].astype(BF16)

    def weave(*steps):
        live = list(steps)
        while live:
            for gen in list(live):
                if next(gen, "done") == "done":
                    live.remove(gen)

    key_pos = lax.broadcasted_iota(jnp.int32, (blk, blk), 0)
    query_pos = lax.broadcasted_iota(jnp.int32, (blk, blk), 1)
    first_of_pair = lax.broadcasted_iota(jnp.int32, (LANES, blk), 0) < HEAD_DIM

    def query_block(qi, _):
        rows = pl.ds(pl.multiple_of(qi * blk, blk), blk)
        qs = [q_scr[h] for h in heads]

        def kv_steps(first_key, width, carry, result, diagonal=False):
            m_old, l_old, acc_old = carry
            keys = pl.ds(pl.multiple_of(first_key, blk), width)
            logits = [_dot_nt(k_scr[h, keys, :], qs[h]) for h in heads]
            yield
            if diagonal:
                logits = [jnp.where(key_pos <= query_pos, x, NEG_INF) for x in logits]
            m_new = [jnp.maximum(m_old[h], jnp.max(logits[h], axis=0, keepdims=True)) for h in heads]
            alpha = [jnp.exp(m_old[h] - m_new[h]) for h in heads]
            p = [jnp.exp(logits[h] - m_new[h]) for h in heads]
            l_new = [alpha[h] * l_old[h] + jnp.sum(p[h], axis=0, keepdims=True) for h in heads]
            pv = [_dot(vt_scr[h // 2, :, keys], p[h].astype(BF16)) for h in heads]
            yield
            acc_new = [jnp.where(first_of_pair, alpha[2 * pr] * acc_old[pr] + pv[2 * pr],
                                 alpha[2 * pr + 1] * acc_old[pr] + pv[2 * pr + 1]) for pr in range(n_pairs)]
            result.append((tuple(m_new), tuple(l_new), tuple(acc_new)))

        def kv_step(first_key, width, carry):
            result = []
            weave(kv_steps(first_key, width, carry, result))
            return result[0]

        def last_steps(carry):
            result = []
            yield from kv_steps(qi * blk, blk, carry, result, diagonal=True)
            _, l, acc = result[0]
            for pr in range(n_pairs):
                out_t = acc[pr] / jnp.where(first_of_pair, l[2 * pr], l[2 * pr + 1])
                o_ref[rows, pr * LANES:(pr + 1) * LANES] = out_t.T

        carry = (tuple(jnp.full((1, blk), NEG_INF, F32) for _ in heads),
                 tuple(jnp.zeros((1, blk), F32) for _ in heads),
                 tuple(jnp.zeros((LANES, blk), F32) for _ in range(n_pairs)))
        done = 0
        for n_blk in MLA_SWEEP:
            todo = (qi - done) // n_blk
            carry = lax.fori_loop(0, todo, lambda j, cr, first=done, n_blk=n_blk:
                                  kv_step((first + j * n_blk) * blk, n_blk * blk, cr), carry)
            done = done + todo * n_blk

        @pl.when(qi + 1 < seq // blk)
        def _last_and_prepare_next():
            weave(last_steps(carry), prepare_steps(qi + 1))

        @pl.when(qi + 1 == seq // blk)
        def _last():
            weave(last_steps(carry))
        return 0

    weave(prepare_steps(0))
    lax.fori_loop(0, seq // blk, query_block, 0)


def _mla(slab, qn, kvn, wuq, wuk, wuv, gq, gk, cos, sin, *, batch, seq):
    nq = seq // MLA_BLOCK
    return pl.pallas_call(
        functools.partial(_mla_body, seq=seq),
        grid=(batch,),
        in_specs=[pl.BlockSpec((seq, MLA_SLAB), lambda b: (0, b)),
                  _const_spec((1, 256)), _const_spec((1, MLA_KV_RANK)),
                  _const_spec((256, N_HEADS * LANES)), _const_spec((MLA_KV_RANK, N_HEADS * LANES)),
                  _const_spec((MLA_KV_RANK, GROUP_W)),
                  _const_spec((1, LANES)), _const_spec((1, LANES)),
                  _const_spec((seq, LANES)), _const_spec((seq, LANES))],
        out_specs=pl.BlockSpec((seq, GROUP_W), lambda b: (0, b)),
        out_shape=jax.ShapeDtypeStruct((seq, batch * GROUP_W), F32),
        scratch_shapes=[pltpu.VMEM((N_HEADS, seq, LANES), BF16), pltpu.VMEM((N_HEADS // 2, LANES, seq), BF16),
                        pltpu.VMEM((N_HEADS, MLA_BLOCK, LANES), BF16)],
        compiler_params=_params("parallel"),
        name="mla",
    )(slab, qn, kvn, wuq, wuk, wuv, gq, gk, cos, sin)


def _s5_body(u_ref, lre_ref, lim_ref, ldt_ref, bre_ref, bim_ref, cre_ref, cim_ref, d_ref, wglu_ref, o_ref,
             a_scr, bbar_scr, h_scr, x_scr, io_scr, *, batch, steps):
    io_tiles = GROUP_W // LANES

    @pl.when(pl.program_id(0) == 0)
    def _discretise():
        lr = lre_ref[...]
        li = lim_ref[...]
        dt = jnp.exp(ldt_ref[...])
        mag = jnp.exp(lr * dt)
        ar = mag * jnp.cos(li * dt)
        ai = mag * jnp.sin(li * dt)
        den = lr * lr + li * li
        nr = ar - 1.0
        zr = (nr * lr + ai * li) / den
        zi = (ai * lr - nr * li) / den
        a_scr[0] = jnp.broadcast_to(ar, (batch, S5_WIDTH))
        a_scr[1] = jnp.broadcast_to(ai, (batch, S5_WIDTH))
        bre = bre_ref[...]
        bim = bim_ref[...]
        bbar_scr[0] = (zr * bre - zi * bim).astype(BF16)
        bbar_scr[1] = (zr * bim + zi * bre).astype(BF16)
        h_scr[...] = jnp.zeros_like(h_scr)

    for b in range(batch):
        for j in range(io_tiles):
            io_scr[j, pl.ds(b, steps, stride=batch), :] = u_ref[:, b * GROUP_W + j * LANES:b * GROUP_W + (j + 1) * LANES]
    u = jnp.concatenate([io_scr[j] for j in range(io_tiles)], axis=1)
    ub = u.astype(BF16)
    x_scr[0] = _dot(ub, bbar_scr[0])
    x_scr[1] = _dot(ub, bbar_scr[1])
    ar = a_scr[0]
    ai = a_scr[1]

    def step(t, carry):
        hr, hi = carry
        rows = pl.ds(pl.multiple_of(t * batch, batch), batch)
        nhr = ar * hr - ai * hi + x_scr[0, rows, :]
        nhi = ar * hi + ai * hr + x_scr[1, rows, :]
        x_scr[0, rows, :] = nhr
        x_scr[1, rows, :] = nhi
        return nhr, nhi

    hr, hi = lax.fori_loop(0, steps, step, (h_scr[0], h_scr[1]), unroll=8)
    h_scr[0] = hr
    h_scr[1] = hi
    y = _dot(x_scr[0].astype(BF16), cre_ref[...]) - _dot(x_scr[1].astype(BF16), cim_ref[...]) + d_ref[...] * u
    z = _dot(y.astype(BF16), wglu_ref[...])
    out = z[:, :GROUP_W] * jax.nn.sigmoid(z[:, GROUP_W:])
    for j in range(io_tiles):
        io_scr[j] = out[:, j * LANES:(j + 1) * LANES]
    for b in range(batch):
        for j in range(io_tiles):
            o_ref[:, b * GROUP_W + j * LANES:b * GROUP_W + (j + 1) * LANES] = io_scr[j, pl.ds(b, steps, stride=batch), :]


def _s5(u, lre, lim, ldt, bre, bim, cre, cim, d, wglu, *, batch, seq, steps=256):
    state = lambda rows: pltpu.VMEM((2, rows, S5_WIDTH), F32)
    return pl.pallas_call(
        functools.partial(_s5_body, batch=batch, steps=steps),
        grid=(seq // steps,),
        in_specs=[pl.BlockSpec((steps, batch * S5_SLAB), lambda i: (i, 0)),
                  _const_spec((1, S5_WIDTH)), _const_spec((1, S5_WIDTH)), _const_spec((1, S5_WIDTH)),
                  _const_spec((GROUP_W, S5_WIDTH)), _const_spec((GROUP_W, S5_WIDTH)),
                  _const_spec((S5_WIDTH, GROUP_W)), _const_spec((S5_WIDTH, GROUP_W)),
                  _const_spec((1, GROUP_W)), _const_spec((GROUP_W, 2 * GROUP_W))],
        out_specs=pl.BlockSpec((steps, batch * GROUP_W), lambda i: (i, 0)),
        out_shape=jax.ShapeDtypeStruct((seq, batch * GROUP_W), F32),
        scratch_shapes=[state(batch), pltpu.VMEM((2, GROUP_W, S5_WIDTH), BF16), state(batch),
                        state(steps * batch), pltpu.VMEM((GROUP_W // LANES, steps * batch, LANES), F32)],
        compiler_params=_params("arbitrary"),
        name="s5",
    )(u, lre, lim, ldt, bre, bim, cre, cim, d, wglu)


DIL_TILE = 256
DIL_GROUP = (4, 2, 2)


def _pair_norm(x, gain2, same_head):
    ms = _dot_split(x * x, same_head, 2) * (1.0 / HEAD_DIM)
    return x * lax.rsqrt(ms + EPS) * gain2


def _dil_body(x_ref, gq_ref, gk_ref, band_ref, o_ref, bias_ref, q_scr, k_scr, v_scr, oa, ma, la, *, seq):
    span = DIL_SPAN
    n_blocks = seq // span
    n_tiles = GROUP_W // LANES
    lo_blk = lax.broadcasted_iota(jnp.int32, (span, LANES), 1) < HEAD_DIM
    same_head = ((lax.broadcasted_iota(jnp.int32, (LANES, LANES), 0) < HEAD_DIM)
                 == (lax.broadcasted_iota(jnp.int32, (LANES, LANES), 1) < HEAD_DIM)).astype(F32)

    @pl.when(pl.program_id(0) == 0)
    def _build_bias():
        for bh in range(len(DIL_PAIRS) * N_HEADS):
            profile = jnp.broadcast_to(band_ref[bh:bh + 1, :], (span, 2 * span))
            bias_ref[bh] = pltpu.roll(profile, 0, 1, stride=1, stride_axis=0)
    in_current = lax.broadcasted_iota(jnp.int32, (span, 2 * span), 1) >= span

    def norm_tile(i, carry):
        rows = pl.ds(pl.multiple_of(i * DIL_TILE, DIL_TILE), DIL_TILE)
        for j in range(n_tiles):
            cols = lambda part: slice(part * GROUP_W + j * LANES, part * GROUP_W + (j + 1) * LANES)
            q = _pair_norm(x_ref[rows, cols(0)], gq_ref[...], same_head)
            q_scr[j, rows, :] = q * (HEAD_DIM ** -0.5)
            k_scr[j, rows, :] = _pair_norm(x_ref[rows, cols(1)], gk_ref[...], same_head)
            v_scr[j, rows, :] = x_ref[rows, cols(2)]
        return carry
    lax.fori_loop(0, seq // DIL_TILE, norm_tile, 0)

    for bi, (window, dil) in enumerate(DIL_PAIRS):
        sub_len = seq // dil
        nb = sub_len // span

        group = DIL_GROUP[bi]

        def block_group(i, carry, bi=bi, nb=nb, dil=dil, group=group):
            todo = []
            for u in range(group):
                t = i * group + u
                r = lax.shift_right_logical(t, int(math.log2(nb)))
                c = t & (nb - 1)
                first = r + c * (dil * span)
                stride = dil if dil > 1 else None
                rows = pl.ds(first, span, stride=stride)
                prev = pl.ds(jnp.maximum(first - dil * span, r), span, stride=stride)
                keep = jnp.logical_or(in_current, c != 0) if nb > 1 else None
                for j in range(n_tiles):
                    q2 = q_scr[j, rows, :]
                    if nb > 1:
                        k_cat = jnp.concatenate([k_scr[j, prev, :], k_scr[j, rows, :]], axis=0).astype(BF16)
                        v_cat = jnp.concatenate([v_scr[j, prev, :], v_scr[j, rows, :]], axis=0).astype(BF16)
                    else:
                        k_cat = k_scr[j, rows, :].astype(BF16)
                        v_cat = v_scr[j, rows, :].astype(BF16)
                    todo.append((rows, j, q2, k_cat, v_cat, keep))
            logits = []
            for rows, j, q2, k_cat, v_cat, keep in todo:
                for a in range(2):
                    qa = jnp.where(lo_blk if a == 0 else jnp.logical_not(lo_blk), q2, 0.0).astype(BF16)
                    bias = bias_ref[bi * N_HEADS + 2 * j + a]
                    lg = _dot_nt(qa, k_cat) + (bias if nb > 1 else bias[:, span:])
                    logits.append(jnp.where(keep, lg, NEG_INF) if nb > 1 else lg)
            m = [jnp.max(lg, axis=-1, keepdims=True) for lg in logits]
            p = [jnp.exp(lg - mx) for lg, mx in zip(logits, m)]
            l = [jnp.sum(px, axis=-1, keepdims=True) for px in p]
            o = [_dot(p[2 * n + a].astype(BF16), todo[n][4]) for n in range(len(todo)) for a in range(2)]
            for n, (rows, j, *_) in enumerate(todo):
                o_in = jnp.where(lo_blk, o[2 * n], o[2 * n + 1])
                m_in = jnp.where(lo_blk, m[2 * n], m[2 * n + 1])
                l_in = jnp.where(lo_blk, l[2 * n], l[2 * n + 1])
                if bi > 0:
                    m_old = ma[j, rows, :]
                    m_new = jnp.maximum(m_old, m_in)
                    w_old, w_in = jnp.exp(m_old - m_new), jnp.exp(m_in - m_new)
                    o_in = w_old * oa[j, rows, :] + w_in * o_in
                    l_in = w_old * la[j, rows, :] + w_in * l_in
                    m_in = m_new
                oa[j, rows, :] = o_in
                ma[j, rows, :] = m_in
                la[j, rows, :] = l_in
            return carry
        lax.fori_loop(0, n_blocks // group, block_group, 0)

    def finish(i, carry):
        rows = pl.ds(pl.multiple_of(i * DIL_TILE, DIL_TILE), DIL_TILE)
        for j in range(n_tiles):
            o_ref[rows, j * LANES:(j + 1) * LANES] = oa[j, rows, :] / la[j, rows, :]
        return carry
    lax.fori_loop(0, seq // DIL_TILE, finish, 0)


def _dil(slab, gq2, gk2, band, *, batch, seq):
    big = lambda: pltpu.VMEM((GROUP_W // LANES, seq, LANES), F32)
    n_bias = len(DIL_PAIRS) * N_HEADS
    return pl.pallas_call(
        functools.partial(_dil_body, seq=seq),
        grid=(batch,),
        in_specs=[pl.BlockSpec((seq, DIL_SLAB), lambda b: (0, b)),
                  _const_spec((1, LANES)), _const_spec((1, LANES)),
                  _const_spec(band.shape)],
        out_specs=pl.BlockSpec((seq, GROUP_W), lambda b: (0, b)),
        out_shape=jax.ShapeDtypeStruct((seq, batch * GROUP_W), F32),
        scratch_shapes=[pltpu.VMEM((n_bias, DIL_SPAN, 2 * DIL_SPAN), F32)] + [big() for _ in range(6)],
        compiler_params=_params("arbitrary"),
        name="dilated",
    )(slab, gq2, gk2, band)


def _t5_bucket(dist):
    exact = T5_BUCKETS // 2
    df = jnp.maximum(dist, 1).astype(F32)
    large = exact + (jnp.log(df / exact) / math.log(T5_MAX_DIST / exact) * (T5_BUCKETS - exact)).astype(jnp.int32)
    large = jnp.minimum(large, T5_BUCKETS - 1)
    return jnp.where(dist < exact, dist, large)


def _dil_band(table):
    span = DIL_SPAN
    delta = span - jnp.arange(2 * span, dtype=jnp.int32)
    rows = []
    for _, dil in DIL_PAIRS:
        bucket = _t5_bucket(jnp.clip(delta, 0, span) * dil)
        onehot = (bucket[:, None] == jnp.arange(T5_BUCKETS, dtype=jnp.int32)[None, :]).astype(F32)
        vals = jnp.dot(onehot, table.astype(F32), precision=HIGHEST)
        rows.append(jnp.where((delta >= 0)[:, None], vals, NEG_INF).T)
    return jnp.concatenate(rows, axis=0)


DN_TILE = 256
DN_GROUP = 4
PREP_STORE_DELAY = 12


def _softplus(x):
    return jnp.maximum(x, 0.0) + jnp.log1p(jnp.exp(-jnp.abs(x)))


def _pair_l2(x, lo_half):
    sq = x * x
    tot = jnp.sum(sq, axis=-1, keepdims=True)
    lo = jnp.sum(jnp.where(lo_half, sq, 0.0), axis=-1, keepdims=True)
    return x * lax.rsqrt(jnp.where(lo_half, lo, tot - lo) + EPS)


def _dn_body(x_ref, cw_ref, alog_ref, dtb_ref, on_ref, o_ref, q_scr, k_scr, v_scr, g_scr, b_scr, w_scr, a_scr,
             s_scr, *, seq):
    c = DN_CHUNK
    w = GROUP_W
    lo_tile = lax.broadcasted_iota(jnp.int32, (DN_TILE, LANES), 1) < HEAD_DIM

    def prep_steps(i):
        if isinstance(i, int):
            first, halo_first = i * DN_TILE, max(i * DN_TILE - 8, 0)
        else:
            first = pl.multiple_of(i * DN_TILE, DN_TILE)
            halo_first = pl.multiple_of(jnp.maximum(first - 8, 0), 8)
        rows = pl.ds(first, DN_TILE)
        cur = x_ref[rows, 0:3 * w]
        halo = jnp.where(i > 0, x_ref[pl.ds(halo_first, 8), 0:3 * w], 0.0)
        ext = jnp.concatenate([halo, cur], axis=0)
        acc = cw_ref[DN_CONV - 1:DN_CONV, :] * cur
        for j in range(DN_CONV - 1):
            acc = acc + cw_ref[j:j + 1, :] * pltpu.roll(ext, DN_CONV - 1 - j, 0)[8:, :]
        y = acc * jax.nn.sigmoid(acc)
        yield
        tiles = range(w // LANES)
        cols = lambda part, j: slice(part * w + j * LANES, part * w + (j + 1) * LANES)
        q = [_pair_l2(y[:, cols(0, j)], lo_tile) * (HEAD_DIM ** -0.5) for j in tiles]
        k = [_pair_l2(y[:, cols(1, j)], lo_tile) for j in tiles]
        yield
        ab = x_ref[rows, 3 * w:3 * w + LANES]
        g = -jnp.exp(alog_ref[...]) * _softplus(ab + dtb_ref[...])
        gc = _dot_split(chunk_tril, g, 3, split_rhs=True)
        yield
        gc = per_head_lanes(gc, 0)
        beta = per_head_lanes(jax.nn.sigmoid(ab), N_HEADS)
        for _ in range(PREP_STORE_DELAY):
            yield
        for j in tiles:
            q_scr[rows, j * LANES:(j + 1) * LANES] = q[j]
            k_scr[rows, j * LANES:(j + 1) * LANES] = k[j]
            v_scr[rows, j * LANES:(j + 1) * LANES] = y[:, cols(2, j)]
        g_scr[rows, :] = gc
        b_scr[rows, :] = beta

    ri = lax.broadcasted_iota(jnp.int32, (c, w), 0)
    ci = lax.broadcasted_iota(jnp.int32, (c, w), 1)
    cj = jnp.bitwise_and(ci, HEAD_DIM - 1)
    causal = ri >= cj
    strict = ri > cj
    eye4 = (ri == cj).astype(F32)
    same_sub = lax.shift_right_logical(ri, 4) == lax.shift_right_logical(cj, 4)
    bi_r = lax.broadcasted_iota(jnp.int32, (w, w), 0)
    bi_c = lax.broadcasted_iota(jnp.int32, (w, w), 1)
    same_head = lax.shift_right_logical(bi_r, 6) == lax.shift_right_logical(bi_c, 6)
    block_mask = same_head.astype(F32)
    chunk_tril = (same_head & (jnp.bitwise_and(bi_r, c - 1) >= jnp.bitwise_and(bi_c, c - 1))).astype(F32)
    head_of_lane = lax.shift_right_logical(lax.broadcasted_iota(jnp.int32, (DN_TILE, w), 1), 6)

    def per_head_lanes(x, first_lane):
        out = None
        for h in range(N_HEADS):
            col = jnp.broadcast_to(x[:, first_lane + h:first_lane + h + 1], (DN_TILE, w))
            out = col if out is None else jnp.where(head_of_lane == h, col, out)
        return out

    def per_head(a, b):
        bd = jnp.where(same_head, jnp.concatenate([b.astype(BF16)] * N_HEADS, axis=0), 0.0)
        return _dot(a.astype(BF16), bd)

    def group_rows(i):
        base = pl.multiple_of(i * (DN_GROUP * c), DN_GROUP * c)
        return [pl.ds(base + u * c, c) for u in range(DN_GROUP)]

    def solve_steps(i):
        rows = group_rows(i)
        each = lambda f, *lists: [f(*args) for args in zip(*lists)]
        q, k, v, gc, beta = ([ref[r, :] for r in rows] for ref in (q_scr, k_scr, v_scr, g_scr, b_scr))
        g_row = each(lambda g: jnp.sum(g * eye4, axis=0, keepdims=True), gc)
        decay = each(lambda g, gr: jnp.exp(jnp.where(causal, g - gr, NEG_INF)), gc, g_row)
        kb = each(jnp.multiply, k, beta)
        k_bd = each(lambda x: jnp.where(same_head, jnp.concatenate([x.astype(BF16)] * N_HEADS, axis=0), 0.0), k)
        lmat = each(lambda a, b, d: jnp.where(strict, _dot_nt(a.astype(BF16), b) * d, 0.0), kb, k_bd, decay)
        yield
        a_qk = each(lambda a, b, d: jnp.where(causal, _dot_nt(a.astype(BF16), b) * d, 0.0), q, k_bd, decay)
        yield
        p = each(lambda l_: jnp.where(same_sub, -l_, 0.0), lmat)
        t_diag = each(lambda x: eye4 + x, p)
        for _ in range(3):
            p = each(per_head, p, p)
            yield
            t_diag = each(lambda t, x: t + per_head(t, x), t_diag, p)
            yield
        nil = each(lambda t, l_: per_head(t, jnp.where(same_sub, 0.0, l_)), t_diag, lmat)
        yield
        nil2 = each(per_head, nil, nil)
        yield
        nil3 = each(per_head, nil, nil2)
        yield
        t_inv = each(lambda n1, n2, n3, t: per_head(eye4 - n1 + n2 - n3, t), nil, nil2, nil3, t_diag)
        yield
        eg = each(jnp.exp, gc)
        w_c = each(lambda t, a, e: per_head(t, a * e), t_inv, kb, eg)
        yield
        u_c = each(lambda t, a, b: per_head(t, a * b), t_inv, v, beta)
        yield
        q_dec = each(jnp.multiply, q, eg)
        k_dec = each(lambda x, g: x * jnp.exp(g[c - 1:c, :] - g), k, gc)
        for ref, vals in zip((w_scr, v_scr, a_scr, q_scr, k_scr), (w_c, u_c, a_qk, q_dec, k_dec)):
            for r, val in zip(rows, vals):
                ref[r, :] = val

    def state_steps(i):
        for rows in group_rows(i):
            state = s_scr[...]
            state_b = state.astype(BF16)
            w_s = _dot(w_scr[rows, :].astype(BF16), state_b)
            q_s = _dot(q_scr[rows, :].astype(BF16), state_b)
            yield
            v_new = v_scr[rows, :] - w_s
            o_intra = per_head(a_scr[rows, :], v_new)
            upd = lax.dot_general(k_scr[rows, :].astype(BF16), v_new.astype(BF16), (((0,), (0,)), ((), ())),
                                  preferred_element_type=F32)
            yield
            o_ref[rows, :] = q_s + o_intra
            s_scr[...] = state * jnp.exp(g_scr[rows, :][c - 1:c, :]) + upd * block_mask
            yield

    def weave(*steps):
        live = list(steps)
        while live:
            for gen in list(live):
                if next(gen, "done") == "done":
                    live.remove(gen)

    n_groups = seq // (DN_GROUP * c)
    s_scr[...] = jnp.zeros_like(s_scr)
    weave(prep_steps(0))
    weave(solve_steps(0), prep_steps(1))

    def group(i, carry):
        weave(solve_steps(i), state_steps(i - 1), prep_steps(i + 1))
        return carry
    lax.fori_loop(1, n_groups - 1, group, 0)
    weave(solve_steps(n_groups - 1), state_steps(n_groups - 2))
    weave(state_steps(n_groups - 1))

    def finish_tile(i, carry):
        rows = pl.ds(pl.multiple_of(i * DN_TILE, DN_TILE), DN_TILE)
        o = o_ref[rows, :]
        ms = _dot_split(o * o, block_mask, 2) * (1.0 / HEAD_DIM)
        gate = x_ref[rows, 3 * w + LANES:4 * w + LANES]
        o_ref[rows, :] = o * lax.rsqrt(ms + EPS) * on_ref[...] * (gate * jax.nn.sigmoid(gate))
        return carry
    lax.fori_loop(0, seq // DN_TILE, finish_tile, 0)


def _dn(slab, conv_w, a_log, dt_bias, o_norm, *, batch, seq):
    wide = lambda: pltpu.VMEM((seq, GROUP_W), F32)
    return pl.pallas_call(
        functools.partial(_dn_body, seq=seq),
        grid=(batch,),
        in_specs=[pl.BlockSpec((seq, DN_SLAB), lambda b: (0, b)),
                  _const_spec((DN_CONV, 3 * GROUP_W)), _const_spec((1, LANES)), _const_spec((1, LANES)),
                  _const_spec((1, GROUP_W))],
        out_specs=pl.BlockSpec((seq, GROUP_W), lambda b: (0, b)),
        out_shape=jax.ShapeDtypeStruct((seq, batch * GROUP_W), F32),
        scratch_shapes=[wide() for _ in range(7)] + [pltpu.VMEM((GROUP_W, GROUP_W), F32)],
        compiler_params=_params("parallel"),
        name="deltanet",
    )(slab, conv_w, a_log, dt_bias, o_norm)


def _row(v, width=None):
    v = v.astype(F32).reshape(1, -1)
    if width is not None and v.shape[1] < width:
        v = jnp.pad(v, ((0, 0), (0, width - v.shape[1])))
    return v


def _prep_w_in(w):
    w = w.astype(BF16)
    z = lambda n: jnp.zeros(w.shape[:2] + (n,), BF16)
    return jnp.concatenate([w[..., 0:416], z(96), w[..., 416:2216], z(120), w[..., 2216:2472]], axis=-1)


def _pad_heads(w, per_head, lo, hi):
    k = w.shape[0]
    w = w.reshape(k, N_HEADS, per_head)[:, :, lo:hi]
    w = jnp.pad(w, ((0, 0), (0, 0), (0, LANES - (hi - lo))))
    return w.reshape(k, N_HEADS * LANES).astype(BF16)


def _rope_tables(seq):
    half = MLA_ROPE // 2
    pos = jnp.arange(seq, dtype=F32)
    freqs = ROPE_THETA ** (-jnp.arange(half, dtype=F32) / half)
    ang = pos[:, None] * freqs[None, :]
    cos, sin = jnp.cos(ang), jnp.sin(ang)
    ones = jnp.ones((seq, MLA_NOPE), F32)
    zeros = jnp.zeros((seq, MLA_NOPE), F32)
    tail1 = jnp.ones((seq, LANES - MLA_DQK), F32)
    tail0 = jnp.zeros((seq, LANES - MLA_DQK), F32)
    return (jnp.concatenate([ones, cos, cos, tail1], axis=1),
            jnp.concatenate([zeros, -sin, sin, tail0], axis=1))


def _block_diag(blocks):
    g, r, c = blocks.shape
    eye = jnp.eye(g, dtype=blocks.dtype)
    return (blocks[:, :, None, :] * eye[:, None, :, None]).reshape(g * r, g * c)


def _mla_layer(slab, p, l, *, batch, seq):
    w_ukv = p["mla_w_ukv"][l]
    cos, sin = _rope_tables(seq)
    return _mla(slab, _row(p["mla_q_norm"][l]), _row(p["mla_kv_norm"][l]),
                _pad_heads(p["mla_w_uq"][l], MLA_DQK, 0, MLA_DQK),
                _pad_heads(w_ukv, MLA_NOPE + HEAD_DIM, 0, MLA_NOPE),
                w_ukv.reshape(MLA_KV_RANK, N_HEADS, MLA_NOPE + HEAD_DIM)[:, :, MLA_NOPE:]
                .reshape(MLA_KV_RANK, GROUP_W).astype(BF16),
                _row(p["mla_qk_q"][l], LANES), _row(p["mla_qk_k"][l], LANES), cos, sin, batch=batch, seq=seq)


def _dil_layer(slab, p, l, *, batch, seq):
    pair = lambda g: jnp.tile(g.astype(F32).reshape(1, HEAD_DIM), (1, LANES // HEAD_DIM))
    return _dil(slab, pair(p["dil_q_norm"][l]), pair(p["dil_k_norm"][l]), _dil_band(p["t5_bias"]),
                batch=batch, seq=seq)


def _dn_layer(slab, p, l, *, batch, seq):
    return _dn(slab, p["dn_conv"][l].astype(F32), _row(p["dn_a_log"][l], LANES), _row(p["dn_dt_bias"][l], LANES),
               jnp.tile(p["dn_o_norm"][l].astype(F32).reshape(1, HEAD_DIM), (1, N_HEADS)), batch=batch, seq=seq)


def _s5_layer(u, p, l, *, batch, seq):
    state_row = lambda v: v.astype(F32).reshape(1, S5_WIDTH)
    ldt = jnp.broadcast_to(p["s5_log_dt"][l][:, None], (S5_GROUPS, S5_STATE))
    bre = _block_diag(jnp.swapaxes(p["s5_b_re"][l], 1, 2).astype(F32))
    bim = _block_diag(jnp.swapaxes(p["s5_b_im"][l], 1, 2).astype(F32))
    cre = _block_diag(jnp.swapaxes(p["s5_c_re"][l], 1, 2)).astype(BF16)
    cim = _block_diag(jnp.swapaxes(p["s5_c_im"][l], 1, 2)).astype(BF16)
    return _s5(u, state_row(p["s5_lambda_re"][l]), state_row(p["s5_lambda_im"][l]), state_row(ldt),
               bre, bim, cre, cim, _row(p["s5_d"][l]), p["s5_w_glu"][l].astype(BF16), batch=batch, seq=seq,
               steps=min(256, seq))


def kernel(x, attn_norm, w_in, w_out, mla_q_norm, mla_kv_norm, mla_w_uq, mla_w_ukv, mla_qk_q, mla_qk_k,
           s5_lambda_re, s5_lambda_im, s5_log_dt, s5_b_re, s5_b_im, s5_c_re, s5_c_im, s5_d, s5_w_glu,
           dil_q_norm, dil_k_norm, t5_bias, dn_conv, dn_a_log, dn_dt_bias, dn_o_norm,
           ffn_norm, ffn_w1, ffn_w3, ffn_w2):
    p = dict(mla_q_norm=mla_q_norm, mla_kv_norm=mla_kv_norm, mla_w_uq=mla_w_uq, mla_w_ukv=mla_w_ukv,
             mla_qk_q=mla_qk_q, mla_qk_k=mla_qk_k, s5_lambda_re=s5_lambda_re, s5_lambda_im=s5_lambda_im,
             s5_log_dt=s5_log_dt, s5_b_re=s5_b_re, s5_b_im=s5_b_im, s5_c_re=s5_c_re, s5_c_im=s5_c_im, s5_d=s5_d,
             s5_w_glu=s5_w_glu, dil_q_norm=dil_q_norm, dil_k_norm=dil_k_norm, t5_bias=t5_bias, dn_conv=dn_conv,
             dn_a_log=dn_a_log, dn_dt_bias=dn_dt_bias, dn_o_norm=dn_o_norm)
    batch, seq, _ = x.shape
    h = x.reshape(batch * seq, D_MODEL)
    w_in_b, w_out_b = _prep_w_in(w_in), w_out.astype(BF16)
    w1_b, w3_b, w2_b = ffn_w1.astype(BF16), ffn_w3.astype(BF16), ffn_w2.astype(BF16)
    for l in range(attn_norm.shape[0]):
        mla_in, s5_in, dil_in, dn_in = _proj(h, _row(attn_norm[l]), w_in_b, l, batch=batch, seq=seq)
        ys = [_mla_layer(mla_in, p, l, batch=batch, seq=seq),
              _s5_layer(s5_in, p, l, batch=batch, seq=seq),
              _dil_layer(dil_in, p, l, batch=batch, seq=seq),
              _dn_layer(dn_in, p, l, batch=batch, seq=seq)]
        h = _out_ffn(h, ys, w_out_b, _row(ffn_norm[l]), w1_b, w3_b, w2_b, l, batch=batch, seq=seq)
    return h.reshape(batch, seq, D_MODEL)
```

```python
import functools
import math

import jax
import jax.numpy as jnp
from jax import lax
from jax.experimental import pallas as pl
from jax.experimental.pallas import tpu as pltpu

F32 = jnp.float32
BF16 = jnp.bfloat16
HIGHEST = lax.Precision.HIGHEST

D_MODEL = 1024
GROUP_W = 256
HEAD_DIM = 64
N_HEADS = 4
EPS = 1e-6
NEG_INF = -1e30
LOG2E = math.log2(math.e)

MLA_NOPE = 64
MLA_ROPE = 32
MLA_DQK = MLA_NOPE + MLA_ROPE
MLA_KV_RANK = 128
ROPE_THETA = 10000.0

S5_GROUPS = 16
S5_STATE = 64
S5_WIDTH = S5_GROUPS * S5_STATE

DIL_PAIRS = ((128, 1), (512, 4), (2048, 16))
DIL_SPAN = 128
T5_BUCKETS = 32
T5_MAX_DIST = 2048

DN_CONV = 4
DN_CHUNK = 64

FFN_HIDDEN = 2816
FFN_CHUNK = 2816

VMEM_LIMIT_BYTES = 56 * 1024 * 1024
LANES = 128

MLA_SLAB = 512
S5_SLAB = 256
DIL_SLAB = 768
DN_SLAB = 1152
PROJ_COLS = MLA_SLAB + S5_SLAB + DIL_SLAB + DN_SLAB


def _dot(a, b, precision=None):
    return jnp.dot(a, b, preferred_element_type=F32, precision=precision)


def _dot_nt(a, b, precision=None):
    return lax.dot_general(a, b, (((1,), (1,)), ((), ())), preferred_element_type=F32, precision=precision)


def _dot_split(a, b, terms, split_rhs=False):
    x = b if split_rhs else a
    mask = (a if split_rhs else b).astype(BF16)
    out = None
    for _ in range(terms):
        piece = x.astype(BF16)
        part = _dot(mask, piece) if split_rhs else _dot(piece, mask)
        out = part if out is None else out + part
        x = x - piece.astype(F32)
    return out


def _const_spec(shape):
    nd = len(shape)
    return pl.BlockSpec(shape, lambda *_: (0,) * nd, pipeline_mode=pl.Buffered(1))


def _params(*sem):
    return pltpu.CompilerParams(dimension_semantics=sem, vmem_limit_bytes=VMEM_LIMIT_BYTES)


def _proj_body(x_ref, g_ref, w_ref, mla_ref, s5_ref, dil_ref, dn_ref):
    x = x_ref[...]
    n = x * lax.rsqrt(jnp.mean(x * x, axis=-1, keepdims=True) + EPS) * g_ref[...]
    nb = n.astype(BF16)
    start = 0
    for ref in (mla_ref, s5_ref, dil_ref, dn_ref):
        width = ref.shape[-1]
        ref[...] = _dot(nb, w_ref[:, start:start + width])
        start += width


def _layer_spec(shape, layer):
    nd = len(shape)
    return pl.BlockSpec((None,) + tuple(shape), lambda *_: (layer,) + (0,) * nd, pipeline_mode=pl.Buffered(1))


def _proj(h, gain, w_big, layer, *, batch, seq, tm=1024):
    nt = seq // tm
    widths = (MLA_SLAB, S5_SLAB, DIL_SLAB, DN_SLAB)
    return pl.pallas_call(
        _proj_body,
        grid=(batch, nt),
        in_specs=[pl.BlockSpec((tm, D_MODEL), lambda b, i: (b * nt + i, 0)),
                  _const_spec((1, D_MODEL)),
                  _layer_spec((D_MODEL, PROJ_COLS), layer)],
        out_specs=[pl.BlockSpec((tm, w), lambda b, i: (i, b)) for w in widths],
        out_shape=[jax.ShapeDtypeStruct((seq, batch * w), F32) for w in widths],
        compiler_params=_params("parallel", "parallel"),
        name="proj",
    )(h, gain, w_big)


def _out_ffn_body(h_ref, y0_ref, y1_ref, y2_ref, y3_ref, wo_ref, g_ref, w1_ref, w3_ref, w2_ref, o_ref, acc_ref):
    mixed = jnp.concatenate([y_ref[...].astype(BF16) for y_ref in (y0_ref, y1_ref, y2_ref, y3_ref)], axis=1)
    h = h_ref[...] + _dot(mixed, wo_ref[...])
    n = h * lax.rsqrt(jnp.mean(h * h, axis=-1, keepdims=True) + EPS) * g_ref[...]
    nb = n.astype(BF16)
    acc_ref[...] = h

    def hidden_chunk(c, carry):
        cols = pl.ds(pl.multiple_of(c * FFN_CHUNK, FFN_CHUNK), FFN_CHUNK)
        a = _dot(nb, w1_ref[:, cols])
        b = _dot(nb, w3_ref[:, cols])
        z = (a * jax.nn.sigmoid(a) * b).astype(BF16)
        acc_ref[...] += _dot(z, w2_ref[cols, :])
        return carry
    lax.fori_loop(0, FFN_HIDDEN // FFN_CHUNK, hidden_chunk, 0)
    o_ref[...] = acc_ref[...]


def _out_ffn(h, ys, w_out, gain, w1, w3, w2, layer, *, batch, seq, tm=512):
    nt = seq // tm
    row = pl.BlockSpec((tm, D_MODEL), lambda b, i: (b * nt + i, 0))
    slab = pl.BlockSpec((tm, GROUP_W), lambda b, i: (i, b))
    return pl.pallas_call(
        _out_ffn_body,
        grid=(batch, nt),
        in_specs=[row] + [slab] * 4 + [
            _layer_spec((D_MODEL, D_MODEL), layer), _const_spec((1, D_MODEL)),
            _layer_spec((D_MODEL, FFN_HIDDEN), layer), _layer_spec((D_MODEL, FFN_HIDDEN), layer),
            _layer_spec((FFN_HIDDEN, D_MODEL), layer)],
        out_specs=row,
        out_shape=jax.ShapeDtypeStruct((batch * seq, D_MODEL), F32),
        scratch_shapes=[pltpu.VMEM((tm, D_MODEL), F32)],
        compiler_params=_params("parallel", "parallel"),
        name="out_ffn",
    )(h, *ys, w_out, gain, w1, w3, w2)


MLA_BLOCK = 256
MLA_SWEEP = (4, 2, 1)


def _mla_body(x_ref, qn_ref, kvn_ref, wuq_ref, wuk_ref, wuv_ref, gq_ref, gk_ref, cos_ref, sin_ref,
              o_ref, k_scr, vt_scr, q_scr, *, seq):
    blk = MLA_BLOCK
    n_pairs = N_HEADS // 2
    lane = lax.broadcasted_iota(jnp.int32, (blk, LANES), 1)

    def rope(x, c, s):
        rot = jnp.where(lane < MLA_NOPE + MLA_ROPE // 2, pltpu.roll(x, LANES - MLA_ROPE // 2, 1),
                        pltpu.roll(x, MLA_ROPE // 2, 1))
        return x * c + rot * s

    def norm_head(x, g):
        ssq = jnp.sum(x * x, axis=-1, keepdims=True)
        return x * lax.rsqrt(ssq * (1.0 / MLA_DQK) + EPS) * g

    heads = range(N_HEADS)

    def prepare_steps(i):
        rows = pl.ds(i * blk if isinstance(i, int) else pl.multiple_of(i * blk, blk), blk)
        ckv = x_ref[rows, 256:384]
        cq = x_ref[rows, 0:256]
        kvn = ckv * lax.rsqrt(jnp.mean(ckv * ckv, axis=-1, keepdims=True) + EPS) * kvn_ref[...]
        qn = cq * lax.rsqrt(jnp.mean(cq * cq, axis=-1, keepdims=True) + EPS) * qn_ref[...]
        kvn = kvn.astype(BF16)
        k_nope = _dot(kvn, wuk_ref[...])
        v = _dot(kvn, wuv_ref[...])
        q_all = _dot(qn.astype(BF16), wuq_ref[...])
        yield
        k_rope = pltpu.roll(x_ref[rows, 384:512], MLA_NOPE, 1)
        c = cos_ref[rows, :]
        s = sin_ref[rows, :]
        ks = [rope(norm_head(k_nope[:, h * LANES:(h + 1) * LANES] + k_rope, gk_ref[...]), c, s) for h in heads]
        yield
        qs_new = [rope(norm_head(q_all[:, h * LANES:(h + 1) * LANES], gq_ref[...]), c, s) * (MLA_DQK ** -0.5 * LOG2E)
                  for h in heads]
        vts = [v[:, pr * LANES:(pr + 1) * LANES].T for pr in range(n_pairs)]
        yield
        yield
        for h in heads:
            k_scr[h, rows, :] = ks[h].astype(BF16)
            q_scr[h] = qs_new[h].astype(BF16)
        for pr in range(n_pairs):
            vt_scr[pr, :, rows] = vts[pr].astype(BF16)

    def weave(*steps):
        live = list(steps)
        while live:
            for gen in list(live):
                if next(gen, "done") == "done":
                    live.remove(gen)

    key_pos = lax.broadcasted_iota(jnp.int32, (blk, blk), 0)
    query_pos = lax.broadcasted_iota(jnp.int32, (blk, blk), 1)
    first_of_pair = lax.broadcasted_iota(jnp.int32, (LANES, blk), 0) < HEAD_DIM

    def query_block(qi, _):
        rows = pl.ds(pl.multiple_of(qi * blk, blk), blk)
        qs = [q_scr[h] for h in heads]

        def kv_steps(first_key, width, carry, result, diagonal=False):
            m_old, l_old, acc_old = carry
            keys = pl.ds(pl.multiple_of(first_key, blk), width)
            logits = [_dot_nt(k_scr[h, keys, :], qs[h]) for h in heads]
            yield
            if diagonal:
                logits = [jnp.where(key_pos <= query_pos, x, NEG_INF) for x in logits]
            m_new = [jnp.maximum(m_old[h], jnp.max(logits[h], axis=0, keepdims=True)) for h in heads]
            alpha = [jnp.exp2(m_old[h] - m_new[h]) for h in heads]
            p = [jnp.exp2(logits[h] - m_new[h]) for h in heads]
            l_new = [alpha[h] * l_old[h] + jnp.sum(p[h], axis=0, keepdims=True) for h in heads]
            pv = [_dot(vt_scr[h // 2, :, keys], p[h].astype(BF16)) for h in heads]
            yield
            acc_new = [jnp.where(first_of_pair, alpha[2 * pr] * acc_old[pr] + pv[2 * pr],
                                 alpha[2 * pr + 1] * acc_old[pr] + pv[2 * pr + 1]) for pr in range(n_pairs)]
            result.append((tuple(m_new), tuple(l_new), tuple(acc_new)))

        def kv_step(first_key, width, carry):
            result = []
            weave(kv_steps(first_key, width, carry, result))
            return result[0]

        def last_steps(carry):
            result = []
            yield from kv_steps(qi * blk, blk, carry, result, diagonal=True)
            _, l, acc = result[0]
            for pr in range(n_pairs):
                out_t = acc[pr] / jnp.where(first_of_pair, l[2 * pr], l[2 * pr + 1])
                o_ref[rows, pr * LANES:(pr + 1) * LANES] = out_t.T

        carry = (tuple(jnp.full((1, blk), NEG_INF, F32) for _ in heads),
                 tuple(jnp.zeros((1, blk), F32) for _ in heads),
                 tuple(jnp.zeros((LANES, blk), F32) for _ in range(n_pairs)))
        done = 0
        for n_blk in MLA_SWEEP:
            todo = (qi - done) // n_blk
            carry = lax.fori_loop(0, todo, lambda j, cr, first=done, n_blk=n_blk:
                                  kv_step((first + j * n_blk) * blk, n_blk * blk, cr), carry)
            done = done + todo * n_blk

        @pl.when(qi + 1 < seq // blk)
        def _last_and_prepare_next():
            weave(last_steps(carry), prepare_steps(qi + 1))

        @pl.when(qi + 1 == seq // blk)
        def _last():
            weave(last_steps(carry))
        return 0

    weave(prepare_steps(0))
    lax.fori_loop(0, seq // blk, query_block, 0)


def _mla(slab, qn, kvn, wuq, wuk, wuv, gq, gk, cos, sin, *, batch, seq):
    nq = seq // MLA_BLOCK
    return pl.pallas_call(
        functools.partial(_mla_body, seq=seq),
        grid=(batch,),
        in_specs=[pl.BlockSpec((seq, MLA_SLAB), lambda b: (0, b)),
                  _const_spec((1, 256)), _const_spec((1, MLA_KV_RANK)),
                  _const_spec((256, N_HEADS * LANES)), _const_spec((MLA_KV_RANK, N_HEADS * LANES)),
                  _const_spec((MLA_KV_RANK, GROUP_W)),
                  _const_spec((1, LANES)), _const_spec((1, LANES)),
                  _const_spec((seq, LANES)), _const_spec((seq, LANES))],
        out_specs=pl.BlockSpec((seq, GROUP_W), lambda b: (0, b)),
        out_shape=jax.ShapeDtypeStruct((seq, batch * GROUP_W), F32),
        scratch_shapes=[pltpu.VMEM((N_HEADS, seq, LANES), BF16), pltpu.VMEM((N_HEADS // 2, LANES, seq), BF16),
                        pltpu.VMEM((N_HEADS, MLA_BLOCK, LANES), BF16)],
        compiler_params=_params("parallel"),
        name="mla",
    )(slab, qn, kvn, wuq, wuk, wuv, gq, gk, cos, sin)


def _s5_body(u_ref, lre_ref, lim_ref, ldt_ref, bre_ref, bim_ref, cre_ref, cim_ref, d_ref, wglu_ref, o_ref,
             a_scr, bbar_scr, h_scr, x_scr, io_scr, *, batch, steps):
    io_tiles = GROUP_W // LANES

    @pl.when(pl.program_id(0) == 0)
    def _discretise():
        lr = lre_ref[...]
        li = lim_ref[...]
        dt = jnp.exp(ldt_ref[...])
        mag = jnp.exp(lr * dt)
        ar = mag * jnp.cos(li * dt)
        ai = mag * jnp.sin(li * dt)
        den = lr * lr + li * li
        nr = ar - 1.0
        zr = (nr * lr + ai * li) / den
        zi = (ai * lr - nr * li) / den
        a_scr[0] = jnp.broadcast_to(ar, (batch, S5_WIDTH))
        a_scr[1] = jnp.broadcast_to(ai, (batch, S5_WIDTH))
        bre = bre_ref[...]
        bim = bim_ref[...]
        bbar_scr[0] = (zr * bre - zi * bim).astype(BF16)
        bbar_scr[1] = (zr * bim + zi * bre).astype(BF16)
        h_scr[...] = jnp.zeros_like(h_scr)

    for b in range(batch):
        for j in range(io_tiles):
            io_scr[j, pl.ds(b, steps, stride=batch), :] = u_ref[:, b * GROUP_W + j * LANES:b * GROUP_W + (j + 1) * LANES]
    u = jnp.concatenate([io_scr[j] for j in range(io_tiles)], axis=1)
    ub = u.astype(BF16)
    x_scr[0] = _dot(ub, bbar_scr[0])
    x_scr[1] = _dot(ub, bbar_scr[1])
    ar = a_scr[0]
    ai = a_scr[1]

    def step(t, carry):
        hr, hi = carry
        rows = pl.ds(pl.multiple_of(t * batch, batch), batch)
        nhr = ar * hr - ai * hi + x_scr[0, rows, :]
        nhi = ar * hi + ai * hr + x_scr[1, rows, :]
        x_scr[0, rows, :] = nhr
        x_scr[1, rows, :] = nhi
        return nhr, nhi

    hr, hi = lax.fori_loop(0, steps, step, (h_scr[0], h_scr[1]), unroll=8)
    h_scr[0] = hr
    h_scr[1] = hi
    y = _dot(x_scr[0].astype(BF16), cre_ref[...]) - _dot(x_scr[1].astype(BF16), cim_ref[...]) + d_ref[...] * u
    z = _dot(y.astype(BF16), wglu_ref[...])
    out = z[:, :GROUP_W] * jax.nn.sigmoid(z[:, GROUP_W:])
    for j in range(io_tiles):
        io_scr[j] = out[:, j * LANES:(j + 1) * LANES]
    for b in range(batch):
        for j in range(io_tiles):
            o_ref[:, b * GROUP_W + j * LANES:b * GROUP_W + (j + 1) * LANES] = io_scr[j, pl.ds(b, steps, stride=batch), :]


def _s5(u, lre, lim, ldt, bre, bim, cre, cim, d, wglu, *, batch, seq, steps=256):
    state = lambda rows: pltpu.VMEM((2, rows, S5_WIDTH), F32)
    return pl.pallas_call(
        functools.partial(_s5_body, batch=batch, steps=steps),
        grid=(seq // steps,),
        in_specs=[pl.BlockSpec((steps, batch * S5_SLAB), lambda i: (i, 0)),
                  _const_spec((1, S5_WIDTH)), _const_spec((1, S5_WIDTH)), _const_spec((1, S5_WIDTH)),
                  _const_spec((GROUP_W, S5_WIDTH)), _const_spec((GROUP_W, S5_WIDTH)),
                  _const_spec((S5_WIDTH, GROUP_W)), _const_spec((S5_WIDTH, GROUP_W)),
                  _const_spec((1, GROUP_W)), _const_spec((GROUP_W, 2 * GROUP_W))],
        out_specs=pl.BlockSpec((steps, batch * GROUP_W), lambda i: (i, 0)),
        out_shape=jax.ShapeDtypeStruct((seq, batch * GROUP_W), F32),
        scratch_shapes=[state(batch), pltpu.VMEM((2, GROUP_W, S5_WIDTH), BF16), state(batch),
                        state(steps * batch), pltpu.VMEM((GROUP_W // LANES, steps * batch, LANES), F32)],
        compiler_params=_params("arbitrary"),
        name="s5",
    )(u, lre, lim, ldt, bre, bim, cre, cim, d, wglu)


DIL_TILE = 256
DIL_GROUP = (4, 2, 2)


def _pair_norm(x, gain2, same_head):
    ms = _dot_split(x * x, same_head, 2) * (1.0 / HEAD_DIM)
    return x * lax.rsqrt(ms + EPS) * gain2


def _dil_body(x_ref, gq_ref, gk_ref, band_ref, o_ref, bias_ref, q_scr, k_scr, v_scr, oa, ma, la, *, seq):
    span = DIL_SPAN
    n_blocks = seq // span
    n_tiles = GROUP_W // LANES
    lo_blk = lax.broadcasted_iota(jnp.int32, (span, LANES), 1) < HEAD_DIM
    same_head = ((lax.broadcasted_iota(jnp.int32, (LANES, LANES), 0) < HEAD_DIM)
                 == (lax.broadcasted_iota(jnp.int32, (LANES, LANES), 1) < HEAD_DIM)).astype(F32)

    @pl.when(pl.program_id(0) == 0)
    def _build_bias():
        for bh in range(len(DIL_PAIRS) * N_HEADS):
            profile = jnp.broadcast_to(band_ref[bh:bh + 1, :], (span, 2 * span))
            bias_ref[bh] = pltpu.roll(profile, 0, 1, stride=1, stride_axis=0)
    in_current = lax.broadcasted_iota(jnp.int32, (span, 2 * span), 1) >= span

    def norm_tile(i, carry):
        rows = pl.ds(pl.multiple_of(i * DIL_TILE, DIL_TILE), DIL_TILE)
        for j in range(n_tiles):
            cols = lambda part: slice(part * GROUP_W + j * LANES, part * GROUP_W + (j + 1) * LANES)
            q = _pair_norm(x_ref[rows, cols(0)], gq_ref[...], same_head)
            q_scr[j, rows, :] = q * (HEAD_DIM ** -0.5 * LOG2E)
            k_scr[j, rows, :] = _pair_norm(x_ref[rows, cols(1)], gk_ref[...], same_head)
            v_scr[j, rows, :] = x_ref[rows, cols(2)]
        return carry
    lax.fori_loop(0, seq // DIL_TILE, norm_tile, 0)

    for bi, (window, dil) in enumerate(DIL_PAIRS):
        sub_len = seq // dil
        nb = sub_len // span

        group = DIL_GROUP[bi]

        def block_group(i, carry, bi=bi, nb=nb, dil=dil, group=group):
            todo = []
            for u in range(group):
                t = i * group + u
                r = lax.shift_right_logical(t, int(math.log2(nb)))
                c = t & (nb - 1)
                first = r + c * (dil * span)
                stride = dil if dil > 1 else None
                rows = pl.ds(first, span, stride=stride)
                prev = pl.ds(jnp.maximum(first - dil * span, r), span, stride=stride)
                keep = jnp.logical_or(in_current, c != 0) if nb > 1 else None
                for j in range(n_tiles):
                    q2 = q_scr[j, rows, :]
                    if nb > 1:
                        k_cat = jnp.concatenate([k_scr[j, prev, :], k_scr[j, rows, :]], axis=0).astype(BF16)
                        v_cat = jnp.concatenate([v_scr[j, prev, :], v_scr[j, rows, :]], axis=0).astype(BF16)
                    else:
                        k_cat = k_scr[j, rows, :].astype(BF16)
                        v_cat = v_scr[j, rows, :].astype(BF16)
                    todo.append((rows, j, q2, k_cat, v_cat, keep))
            logits = []
            for rows, j, q2, k_cat, v_cat, keep in todo:
                for a in range(2):
                    qa = jnp.where(lo_blk if a == 0 else jnp.logical_not(lo_blk), q2, 0.0).astype(BF16)
                    bias = bias_ref[bi * N_HEADS + 2 * j + a]
                    lg = _dot_nt(qa, k_cat) + (bias if nb > 1 else bias[:, span:])
                    logits.append(jnp.where(keep, lg, NEG_INF) if nb > 1 else lg)
            m = [jnp.max(lg, axis=-1, keepdims=True) for lg in logits]
            p = [jnp.exp2(lg - mx) for lg, mx in zip(logits, m)]
            l = [jnp.sum(px, axis=-1, keepdims=True) for px in p]
            o = [_dot(p[2 * n + a].astype(BF16), todo[n][4]) for n in range(len(todo)) for a in range(2)]
            for n, (rows, j, *_) in enumerate(todo):
                o_in = jnp.where(lo_blk, o[2 * n], o[2 * n + 1])
                m_in = jnp.where(lo_blk, m[2 * n], m[2 * n + 1])
                l_in = jnp.where(lo_blk, l[2 * n], l[2 * n + 1])
                if bi > 0:
                    m_old = ma[j, rows, :]
                    m_new = jnp.maximum(m_old, m_in)
                    w_old, w_in = jnp.exp2(m_old - m_new), jnp.exp2(m_in - m_new)
                    o_in = w_old * oa[j, rows, :] + w_in * o_in
                    l_in = w_old * la[j, rows, :] + w_in * l_in
                    m_in = m_new
                oa[j, rows, :] = o_in
                ma[j, rows, :] = m_in
                la[j, rows, :] = l_in
            return carry
        lax.fori_loop(0, n_blocks // group, block_group, 0)

    def finish(i, carry):
        rows = pl.ds(pl.multiple_of(i * DIL_TILE, DIL_TILE), DIL_TILE)
        for j in range(n_tiles):
            o_ref[rows, j * LANES:(j + 1) * LANES] = oa[j, rows, :] / la[j, rows, :]
        return carry
    lax.fori_loop(0, seq // DIL_TILE, finish, 0)


def _dil(slab, gq2, gk2, band, *, batch, seq):
    big = lambda: pltpu.VMEM((GROUP_W // LANES, seq, LANES), F32)
    n_bias = len(DIL_PAIRS) * N_HEADS
    return pl.pallas_call(
        functools.partial(_dil_body, seq=seq),
        grid=(batch,),
        in_specs=[pl.BlockSpec((seq, DIL_SLAB), lambda b: (0, b)),
                  _const_spec((1, LANES)), _const_spec((1, LANES)),
                  _const_spec(band.shape)],
        out_specs=pl.BlockSpec((seq, GROUP_W), lambda b: (0, b)),
        out_shape=jax.ShapeDtypeStruct((seq, batch * GROUP_W), F32),
        scratch_shapes=[pltpu.VMEM((n_bias, DIL_SPAN, 2 * DIL_SPAN), F32)] + [big() for _ in range(6)],
        compiler_params=_params("arbitrary"),
        name="dilated",
    )(slab, gq2, gk2, band)


def _t5_bucket(dist):
    exact = T5_BUCKETS // 2
    df = jnp.maximum(dist, 1).astype(F32)
    large = exact + (jnp.log(df / exact) / math.log(T5_MAX_DIST / exact) * (T5_BUCKETS - exact)).astype(jnp.int32)
    large = jnp.minimum(large, T5_BUCKETS - 1)
    return jnp.where(dist < exact, dist, large)


def _dil_band(table):
    span = DIL_SPAN
    delta = span - jnp.arange(2 * span, dtype=jnp.int32)
    rows = []
    for _, dil in DIL_PAIRS:
        bucket = _t5_bucket(jnp.clip(delta, 0, span) * dil)
        onehot = (bucket[:, None] == jnp.arange(T5_BUCKETS, dtype=jnp.int32)[None, :]).astype(F32)
        vals = jnp.dot(onehot, table.astype(F32), precision=HIGHEST)
        rows.append(jnp.where((delta >= 0)[:, None], vals * LOG2E, NEG_INF).T)
    return jnp.concatenate(rows, axis=0)


DN_TILE = 256
DN_GROUP = 4
PREP_STORE_DELAY = 12


def _softplus(x):
    return jnp.maximum(x, 0.0) + jnp.log1p(jnp.exp(-jnp.abs(x)))


def _pair_l2(x, lo_half):
    sq = x * x
    tot = jnp.sum(sq, axis=-1, keepdims=True)
    lo = jnp.sum(jnp.where(lo_half, sq, 0.0), axis=-1, keepdims=True)
    return x * lax.rsqrt(jnp.where(lo_half, lo, tot - lo) + EPS)


def _dn_body(x_ref, cw_ref, alog_ref, dtb_ref, on_ref, o_ref, q_scr, k_scr, v_scr, g_scr, b_scr, w_scr, a_scr,
             s_scr, *, seq):
    c = DN_CHUNK
    w = GROUP_W
    lo_tile = lax.broadcasted_iota(jnp.int32, (DN_TILE, LANES), 1) < HEAD_DIM

    def prep_steps(i):
        if isinstance(i, int):
            first, halo_first = i * DN_TILE, max(i * DN_TILE - 8, 0)
        else:
            first = pl.multiple_of(i * DN_TILE, DN_TILE)
            halo_first = pl.multiple_of(jnp.maximum(first - 8, 0), 8)
        rows = pl.ds(first, DN_TILE)
        cur = x_ref[rows, 0:3 * w]
        halo = jnp.where(i > 0, x_ref[pl.ds(halo_first, 8), 0:3 * w], 0.0)
        ext = jnp.concatenate([halo, cur], axis=0)
        acc = cw_ref[DN_CONV - 1:DN_CONV, :] * cur
        for j in range(DN_CONV - 1):
            acc = acc + cw_ref[j:j + 1, :] * pltpu.roll(ext, DN_CONV - 1 - j, 0)[8:, :]
        y = acc * jax.nn.sigmoid(acc)
        yield
        tiles = range(w // LANES)
        cols = lambda part, j: slice(part * w + j * LANES, part * w + (j + 1) * LANES)
        q = [_pair_l2(y[:, cols(0, j)], lo_tile) * (HEAD_DIM ** -0.5) for j in tiles]
        k = [_pair_l2(y[:, cols(1, j)], lo_tile) for j in tiles]
        yield
        ab = x_ref[rows, 3 * w:3 * w + LANES]
        g = -jnp.exp(alog_ref[...]) * _softplus(ab + dtb_ref[...])
        gc = _dot_split(chunk_tril, g, 3, split_rhs=True)
        yield
        gc = per_head_lanes(gc, 0)
        beta = per_head_lanes(jax.nn.sigmoid(ab), N_HEADS)
        for _ in range(PREP_STORE_DELAY):
            yield
        for j in tiles:
            q_scr[rows, j * LANES:(j + 1) * LANES] = q[j]
            k_scr[rows, j * LANES:(j + 1) * LANES] = k[j]
            v_scr[rows, j * LANES:(j + 1) * LANES] = y[:, cols(2, j)]
        g_scr[rows, :] = gc
        b_scr[rows, :] = beta

    ri = lax.broadcasted_iota(jnp.int32, (c, w), 0)
    ci = lax.broadcasted_iota(jnp.int32, (c, w), 1)
    cj = jnp.bitwise_and(ci, HEAD_DIM - 1)
    causal = ri >= cj
    strict = ri > cj
    eye4 = (ri == cj).astype(F32)
    same_sub = lax.shift_right_logical(ri, 4) == lax.shift_right_logical(cj, 4)
    bi_r = lax.broadcasted_iota(jnp.int32, (w, w), 0)
    bi_c = lax.broadcasted_iota(jnp.int32, (w, w), 1)
    same_head = lax.shift_right_logical(bi_r, 6) == lax.shift_right_logical(bi_c, 6)
    block_mask = same_head.astype(F32)
    chunk_tril = (same_head & (jnp.bitwise_and(bi_r, c - 1) >= jnp.bitwise_and(bi_c, c - 1))).astype(F32)
    head_of_lane = lax.shift_right_logical(lax.broadcasted_iota(jnp.int32, (DN_TILE, w), 1), 6)

    def per_head_lanes(x, first_lane):
        out = None
        for h in range(N_HEADS):
            col = jnp.broadcast_to(x[:, first_lane + h:first_lane + h + 1], (DN_TILE, w))
            out = col if out is None else jnp.where(head_of_lane == h, col, out)
        return out

    def per_head(a, b):
        bd = jnp.where(same_head, jnp.concatenate([b.astype(BF16)] * N_HEADS, axis=0), 0.0)
        return _dot(a.astype(BF16), bd)

    def group_rows(i):
        base = pl.multiple_of(i * (DN_GROUP * c), DN_GROUP * c)
        return [pl.ds(base + u * c, c) for u in range(DN_GROUP)]

    def solve_steps(i):
        rows = group_rows(i)
        each = lambda f, *lists: [f(*args) for args in zip(*lists)]
        q, k, v, gc, beta = ([ref[r, :] for r in rows] for ref in (q_scr, k_scr, v_scr, g_scr, b_scr))
        g_row = each(lambda g: jnp.sum(g * eye4, axis=0, keepdims=True), gc)
        decay = each(lambda g, gr: jnp.exp(jnp.where(causal, g - gr, NEG_INF)), gc, g_row)
        kb = each(jnp.multiply, k, beta)
        k_bd = each(lambda x: jnp.where(same_head, jnp.concatenate([x.astype(BF16)] * N_HEADS, axis=0), 0.0), k)
        lmat = each(lambda a, b, d: jnp.where(strict, _dot_nt(a.astype(BF16), b) * d, 0.0), kb, k_bd, decay)
        yield
        a_qk = each(lambda a, b, d: jnp.where(causal, _dot_nt(a.astype(BF16), b) * d, 0.0), q, k_bd, decay)
        yield
        p = each(lambda l_: jnp.where(same_sub, -l_, 0.0), lmat)
        t_diag = each(lambda x: eye4 + x, p)
        for _ in range(3):
            p = each(per_head, p, p)
            yield
            t_diag = each(lambda t, x: t + per_head(t, x), t_diag, p)
            yield
        nil = each(lambda t, l_: per_head(t, jnp.where(same_sub, 0.0, l_)), t_diag, lmat)
        yield
        nil2 = each(per_head, nil, nil)
        yield
        nil3 = each(per_head, nil, nil2)
        yield
        t_inv = each(lambda n1, n2, n3, t: per_head(eye4 - n1 + n2 - n3, t), nil, nil2, nil3, t_diag)
        yield
        eg = each(jnp.exp, gc)
        w_c = each(lambda t, a, e: per_head(t, a * e), t_inv, kb, eg)
        yield
        u_c = each(lambda t, a, b: per_head(t, a * b), t_inv, v, beta)
        yield
        q_dec = each(jnp.multiply, q, eg)
        k_dec = each(lambda x, g: x * jnp.exp(g[c - 1:c, :] - g), k, gc)
        for ref, vals in zip((w_scr, v_scr, a_scr, q_scr, k_scr), (w_c, u_c, a_qk, q_dec, k_dec)):
            for r, val in zip(rows, vals):
                ref[r, :] = val

    def state_steps(i):
        for rows in group_rows(i):
            state = s_scr[...]
            state_b = state.astype(BF16)
            w_s = _dot(w_scr[rows, :].astype(BF16), state_b)
            q_s = _dot(q_scr[rows, :].astype(BF16), state_b)
            yield
            v_new = v_scr[rows, :] - w_s
            o_intra = per_head(a_scr[rows, :], v_new)
            upd = lax.dot_general(k_scr[rows, :].astype(BF16), v_new.astype(BF16), (((0,), (0,)), ((), ())),
                                  preferred_element_type=F32)
            yield
            o_ref[rows, :] = q_s + o_intra
            s_scr[...] = state * jnp.exp(g_scr[rows, :][c - 1:c, :]) + upd * block_mask
            yield

    def weave(*steps):
        live = list(steps)
        while live:
            for gen in list(live):
                if next(gen, "done") == "done":
                    live.remove(gen)

    n_groups = seq // (DN_GROUP * c)
    s_scr[...] = jnp.zeros_like(s_scr)
    weave(prep_steps(0))
    weave(solve_steps(0), prep_steps(1))

    def group(i, carry):
        weave(solve_steps(i), state_steps(i - 1), prep_steps(i + 1))
        return carry
    lax.fori_loop(1, n_groups - 1, group, 0)
    weave(solve_steps(n_groups - 1), state_steps(n_groups - 2))
    weave(state_steps(n_groups - 1))

    def finish_tile(i, carry):
        rows = pl.ds(pl.multiple_of(i * DN_TILE, DN_TILE), DN_TILE)
        o = o_ref[rows, :]
        ms = _dot_split(o * o, block_mask, 2) * (1.0 / HEAD_DIM)
        gate = x_ref[rows, 3 * w + LANES:4 * w + LANES]
        o_ref[rows, :] = o * lax.rsqrt(ms + EPS) * on_ref[...] * (gate * jax.nn.sigmoid(gate))
        return carry
    lax.fori_loop(0, seq // DN_TILE, finish_tile, 0)


def _dn(slab, conv_w, a_log, dt_bias, o_norm, *, batch, seq):
    wide = lambda: pltpu.VMEM((seq, GROUP_W), F32)
    return pl.pallas_call(
        functools.partial(_dn_body, seq=seq),
        grid=(batch,),
        in_specs=[pl.BlockSpec((seq, DN_SLAB), lambda b: (0, b)),
                  _const_spec((DN_CONV, 3 * GROUP_W)), _const_spec((1, LANES)), _const_spec((1, LANES)),
                  _const_spec((1, GROUP_W))],
        out_specs=pl.BlockSpec((seq, GROUP_W), lambda b: (0, b)),
        out_shape=jax.ShapeDtypeStruct((seq, batch * GROUP_W), F32),
        scratch_shapes=[wide() for _ in range(7)] + [pltpu.VMEM((GROUP_W, GROUP_W), F32)],
        compiler_params=_params("parallel"),
        name="deltanet",
    )(slab, conv_w, a_log, dt_bias, o_norm)


def _row(v, width=None):
    v = v.astype(F32).reshape(1, -1)
    if width is not None and v.shape[1] < width:
        v = jnp.pad(v, ((0, 0), (0, width - v.shape[1])))
    return v


def _prep_w_in(w):
    w = w.astype(BF16)
    z = lambda n: jnp.zeros(w.shape[:2] + (n,), BF16)
    return jnp.concatenate([w[..., 0:416], z(96), w[..., 416:2216], z(120), w[..., 2216:2472]], axis=-1)


def _pad_heads(w, per_head, lo, hi):
    k = w.shape[0]
    w = w.reshape(k, N_HEADS, per_head)[:, :, lo:hi]
    w = jnp.pad(w, ((0, 0), (0, 0), (0, LANES - (hi - lo))))
    return w.reshape(k, N_HEADS * LANES).astype(BF16)


def _rope_tables(seq):
    half = MLA_ROPE // 2
    pos = jnp.arange(seq, dtype=F32)
    freqs = ROPE_THETA ** (-jnp.arange(half, dtype=F32) / half)
    ang = pos[:, None] * freqs[None, :]
    cos, sin = jnp.cos(ang), jnp.sin(ang)
    ones = jnp.ones((seq, MLA_NOPE), F32)
    zeros = jnp.zeros((seq, MLA_NOPE), F32)
    tail1 = jnp.ones((seq, LANES - MLA_DQK), F32)
    tail0 = jnp.zeros((seq, LANES - MLA_DQK), F32)
    return (jnp.concatenate([ones, cos, cos, tail1], axis=1),
            jnp.concatenate([zeros, -sin, sin, tail0], axis=1))


def _block_diag(blocks):
    g, r, c = blocks.shape
    eye = jnp.eye(g, dtype=blocks.dtype)
    return (blocks[:, :, None, :] * eye[:, None, :, None]).reshape(g * r, g * c)


def _mla_layer(slab, p, l, *, batch, seq):
    w_ukv = p["mla_w_ukv"][l]
    cos, sin = _rope_tables(seq)
    return _mla(slab, _row(p["mla_q_norm"][l]), _row(p["mla_kv_norm"][l]),
                _pad_heads(p["mla_w_uq"][l], MLA_DQK, 0, MLA_DQK),
                _pad_heads(w_ukv, MLA_NOPE + HEAD_DIM, 0, MLA_NOPE),
                w_ukv.reshape(MLA_KV_RANK, N_HEADS, MLA_NOPE + HEAD_DIM)[:, :, MLA_NOPE:]
                .reshape(MLA_KV_RANK, GROUP_W).astype(BF16),
                _row(p["mla_qk_q"][l], LANES), _row(p["mla_qk_k"][l], LANES), cos, sin, batch=batch, seq=seq)


def _dil_layer(slab, p, l, *, batch, seq):
    pair = lambda g: jnp.tile(g.astype(F32).reshape(1, HEAD_DIM), (1, LANES // HEAD_DIM))
    return _dil(slab, pair(p["dil_q_norm"][l]), pair(p["dil_k_norm"][l]), _dil_band(p["t5_bias"]),
                batch=batch, seq=seq)


def _dn_layer(slab, p, l, *, batch, seq):
    return _dn(slab, p["dn_conv"][l].astype(F32), _row(p["dn_a_log"][l], LANES), _row(p["dn_dt_bias"][l], LANES),
               jnp.tile(p["dn_o_norm"][l].astype(F32).reshape(1, HEAD_DIM), (1, N_HEADS)), batch=batch, seq=seq)


def _s5_layer(u, p, l, *, batch, seq):
    state_row = lambda v: v.astype(F32).reshape(1, S5_WIDTH)
    ldt = jnp.broadcast_to(p["s5_log_dt"][l][:, None], (S5_GROUPS, S5_STATE))
    bre = _block_diag(jnp.swapaxes(p["s5_b_re"][l], 1, 2).astype(F32))
    bim = _block_diag(jnp.swapaxes(p["s5_b_im"][l], 1, 2).astype(F32))
    cre = _block_diag(jnp.swapaxes(p["s5_c_re"][l], 1, 2)).astype(BF16)
    cim = _block_diag(jnp.swapaxes(p["s5_c_im"][l], 1, 2)).astype(BF16)
    return _s5(u, state_row(p["s5_lambda_re"][l]), state_row(p["s5_lambda_im"][l]), state_row(ldt),
               bre, bim, cre, cim, _row(p["s5_d"][l]), p["s5_w_glu"][l].astype(BF16), batch=batch, seq=seq,
               steps=min(256, seq))


def kernel(x, attn_norm, w_in, w_out, mla_q_norm, mla_kv_norm, mla_w_uq, mla_w_ukv, mla_qk_q, mla_qk_k,
           s5_lambda_re, s5_lambda_im, s5_log_dt, s5_b_re, s5_b_im, s5_c_re, s5_c_im, s5_d, s5_w_glu,
           dil_q_norm, dil_k_norm, t5_bias, dn_conv, dn_a_log, dn_dt_bias, dn_o_norm,
           ffn_norm, ffn_w1, ffn_w3, ffn_w2):
    p = dict(mla_q_norm=mla_q_norm, mla_kv_norm=mla_kv_norm, mla_w_uq=mla_w_uq, mla_w_ukv=mla_w_ukv,
             mla_qk_q=mla_qk_q, mla_qk_k=mla_qk_k, s5_lambda_re=s5_lambda_re, s5_lambda_im=s5_lambda_im,
             s5_log_dt=s5_log_dt, s5_b_re=s5_b_re, s5_b_im=s5_b_im, s5_c_re=s5_c_re, s5_c_im=s5_c_im, s5_d=s5_d,
             s5_w_glu=s5_w_glu, dil_q_norm=dil_q_norm, dil_k_norm=dil_k_norm, t5_bias=t5_bias, dn_conv=dn_conv,
             dn_a_log=dn_a_log, dn_dt_bias=dn_dt_bias, dn_o_norm=dn_o_norm)
    batch, seq, _ = x.shape
    h = x.reshape(batch * seq, D_MODEL)
    w_in_b, w_out_b = _prep_w_in(w_in), w_out.astype(BF16)
    w1_b, w3_b, w2_b = ffn_w1.astype(BF16), ffn_w3.astype(BF16), ffn_w2.astype(BF16)
    for l in range(attn_norm.shape[0]):
        mla_in, s5_in, dil_in, dn_in = _proj(h, _row(attn_norm[l]), w_in_b, l, batch=batch, seq=seq)
        ys = [_mla_layer(mla_in, p, l, batch=batch, seq=seq),
              _s5_layer(s5_in, p, l, batch=batch, seq=seq),
              _dil_layer(dil_in, p, l, batch=batch, seq=seq),
              _dn_layer(dn_in, p, l, batch=batch, seq=seq)]
        h = _out_ffn(h, ys, w_out_b, _row(ffn_norm[l]), w1_b, w3_b, w2_b, l, batch=batch, seq=seq)
    return h.reshape(batch, seq, D_MODEL)
```

```python
import functools
import math

import jax
import jax.numpy as jnp
from jax import lax
from jax.experimental import pallas as pl
from jax.experimental.pallas import tpu as pltpu

F32 = jnp.float32
BF16 = jnp.bfloat16
HIGHEST = lax.Precision.HIGHEST

D_MODEL = 1024
GROUP_W = 256
HEAD_DIM = 64
N_HEADS = 4
EPS = 1e-6
NEG_INF = -1e30
LOG2E = math.log2(math.e)

MLA_NOPE = 64
MLA_ROPE = 32
MLA_DQK = MLA_NOPE + MLA_ROPE
MLA_KV_RANK = 128
ROPE_THETA = 10000.0

S5_GROUPS = 16
S5_STATE = 64
S5_WIDTH = S5_GROUPS * S5_STATE

DIL_PAIRS = ((128, 1), (512, 4), (2048, 16))
DIL_SPAN = 128
T5_BUCKETS = 32
T5_MAX_DIST = 2048

DN_CONV = 4
DN_CHUNK = 64

FFN_HIDDEN = 2816
FFN_CHUNK = 2816

VMEM_LIMIT_BYTES = 56 * 1024 * 1024
LANES = 128

MLA_SLAB = 512
S5_SLAB = 256
DIL_SLAB = 768
DN_SLAB = 1152
PROJ_COLS = MLA_SLAB + S5_SLAB + DIL_SLAB + DN_SLAB


def _dot(a, b, precision=None):
    return jnp.dot(a, b, preferred_element_type=F32, precision=precision)


def _dot_nt(a, b, precision=None):
    return lax.dot_general(a, b, (((1,), (1,)), ((), ())), preferred_element_type=F32, precision=precision)


def _dot_split(a, b, terms, split_rhs=False):
    x = b if split_rhs else a
    mask = (a if split_rhs else b).astype(BF16)
    out = None
    for _ in range(terms):
        piece = x.astype(BF16)
        part = _dot(mask, piece) if split_rhs else _dot(piece, mask)
        out = part if out is None else out + part
        x = x - piece.astype(F32)
    return out


def _const_spec(shape):
    nd = len(shape)
    return pl.BlockSpec(shape, lambda *_: (0,) * nd, pipeline_mode=pl.Buffered(1))


def _params(*sem):
    return pltpu.CompilerParams(dimension_semantics=sem, vmem_limit_bytes=VMEM_LIMIT_BYTES)


def _proj_body(x_ref, g_ref, w_ref, mla_ref, s5_ref, dil_ref, dn_ref):
    x = x_ref[...]
    n = x * lax.rsqrt(jnp.mean(x * x, axis=-1, keepdims=True) + EPS) * g_ref[...]
    nb = n.astype(BF16)
    start = 0
    for ref in (mla_ref, s5_ref, dil_ref, dn_ref):
        width = ref.shape[-1]
        ref[...] = _dot(nb, w_ref[:, start:start + width])
        start += width


def _layer_spec(shape, layer):
    nd = len(shape)
    return pl.BlockSpec((None,) + tuple(shape), lambda *_: (layer,) + (0,) * nd, pipeline_mode=pl.Buffered(1))


def _proj(h, gain, w_big, layer, *, batch, seq, tm=1024):
    nt = seq // tm
    widths = (MLA_SLAB, S5_SLAB, DIL_SLAB, DN_SLAB)
    return pl.pallas_call(
        _proj_body,
        grid=(batch, nt),
        in_specs=[pl.BlockSpec((tm, D_MODEL), lambda b, i: (b * nt + i, 0)),
                  _const_spec((1, D_MODEL)),
                  _layer_spec((D_MODEL, PROJ_COLS), layer)],
        out_specs=[pl.BlockSpec((tm, w), lambda b, i: (i, b)) for w in widths],
        out_shape=[jax.ShapeDtypeStruct((seq, batch * w), F32) for w in widths],
        compiler_params=_params("parallel", "parallel"),
        name="proj",
    )(h, gain, w_big)


def _out_ffn_body(h_ref, y0_ref, y1_ref, y2_ref, y3_ref, wo_ref, g_ref, w1_ref, w3_ref, w2_ref, o_ref, acc_ref):
    mixed = jnp.concatenate([y_ref[...].astype(BF16) for y_ref in (y0_ref, y1_ref, y2_ref, y3_ref)], axis=1)
    h = h_ref[...] + _dot(mixed, wo_ref[...])
    n = h * lax.rsqrt(jnp.mean(h * h, axis=-1, keepdims=True) + EPS) * g_ref[...]
    nb = n.astype(BF16)
    acc_ref[...] = h

    def hidden_chunk(c, carry):
        cols = pl.ds(pl.multiple_of(c * FFN_CHUNK, FFN_CHUNK), FFN_CHUNK)
        a = _dot(nb, w1_ref[:, cols])
        b = _dot(nb, w3_ref[:, cols])
        z = (a * jax.nn.sigmoid(a) * b).astype(BF16)
        acc_ref[...] += _dot(z, w2_ref[cols, :])
        return carry
    lax.fori_loop(0, FFN_HIDDEN // FFN_CHUNK, hidden_chunk, 0)
    o_ref[...] = acc_ref[...]


def _out_ffn(h, ys, w_out, gain, w1, w3, w2, layer, *, batch, seq, tm=512):
    nt = seq // tm
    row = pl.BlockSpec((tm, D_MODEL), lambda b, i: (b * nt + i, 0))
    slab = pl.BlockSpec((tm, GROUP_W), lambda b, i: (i, b))
    return pl.pallas_call(
        _out_ffn_body,
        grid=(batch, nt),
        in_specs=[row] + [slab] * 4 + [
            _layer_spec((D_MODEL, D_MODEL), layer), _const_spec((1, D_MODEL)),
            _layer_spec((D_MODEL, FFN_HIDDEN), layer), _layer_spec((D_MODEL, FFN_HIDDEN), layer),
            _layer_spec((FFN_HIDDEN, D_MODEL), layer)],
        out_specs=row,
        out_shape=jax.ShapeDtypeStruct((batch * seq, D_MODEL), F32),
        scratch_shapes=[pltpu.VMEM((tm, D_MODEL), F32)],
        compiler_params=_params("parallel", "parallel"),
        name="out_ffn",
    )(h, *ys, w_out, gain, w1, w3, w2)


MLA_BLOCK = 256
MLA_PREP_PARTS = 1
MLA_SWEEP = (4, 2, 1)


def _mla_body(x_ref, qn_ref, kvn_ref, wuq_ref, wuk_ref, wuv_ref, gq_ref, gk_ref, cos_ref, sin_ref,
              o_ref, k_scr, vt_scr, q_scr, *, seq):
    blk = MLA_BLOCK
    n_pairs = N_HEADS // 2
    part_rows = blk // MLA_PREP_PARTS
    lane = lax.broadcasted_iota(jnp.int32, (part_rows, LANES), 1)

    def rope(x, c, s):
        rot = jnp.where(lane < MLA_NOPE + MLA_ROPE // 2, pltpu.roll(x, LANES - MLA_ROPE // 2, 1),
                        pltpu.roll(x, MLA_ROPE // 2, 1))
        return x * c + rot * s

    def norm_head(x, g):
        ssq = jnp.sum(x * x, axis=-1, keepdims=True)
        return x * lax.rsqrt(ssq * (1.0 / MLA_DQK) + EPS) * g

    heads = range(N_HEADS)

    def prepare_steps(i, part):
        first = i * blk + part * part_rows
        rows = pl.ds(first if isinstance(i, int) else pl.multiple_of(first, part_rows), part_rows)
        part_of_block = pl.ds(part * part_rows, part_rows)
        ckv = x_ref[rows, 256:384]
        cq = x_ref[rows, 0:256]
        kvn = ckv * lax.rsqrt(jnp.mean(ckv * ckv, axis=-1, keepdims=True) + EPS) * kvn_ref[...]
        qn = cq * lax.rsqrt(jnp.mean(cq * cq, axis=-1, keepdims=True) + EPS) * qn_ref[...]
        kvn = kvn.astype(BF16)
        k_nope = _dot(kvn, wuk_ref[...])
        v = _dot(kvn, wuv_ref[...])
        q_all = _dot(qn.astype(BF16), wuq_ref[...])
        yield
        k_rope = pltpu.roll(x_ref[rows, 384:512], MLA_NOPE, 1)
        c = cos_ref[rows, :]
        s = sin_ref[rows, :]
        ks = [rope(norm_head(k_nope[:, h * LANES:(h + 1) * LANES] + k_rope, gk_ref[...]), c, s) for h in heads]
        yield
        qs_new = [rope(norm_head(q_all[:, h * LANES:(h + 1) * LANES], gq_ref[...]), c, s) * (MLA_DQK ** -0.5 * LOG2E)
                  for h in heads]
        vts = [v[:, pr * LANES:(pr + 1) * LANES].T for pr in range(n_pairs)]
        yield
        yield
        for h in heads:
            k_scr[h, rows, :] = ks[h].astype(BF16)
            q_scr[h, part_of_block, :] = qs_new[h].astype(BF16)
        for pr in range(n_pairs):
            vt_scr[pr, :, rows] = vts[pr].astype(BF16)

    def weave(*steps):
        live = list(steps)
        while live:
            for gen in list(live):
                if next(gen, "done") == "done":
                    live.remove(gen)

    key_pos = lax.broadcasted_iota(jnp.int32, (blk, blk), 0)
    query_pos = lax.broadcasted_iota(jnp.int32, (blk, blk), 1)
    first_of_pair = lax.broadcasted_iota(jnp.int32, (LANES, blk), 0) < HEAD_DIM

    def query_block(qi, _):
        rows = pl.ds(pl.multiple_of(qi * blk, blk), blk)
        qs = [q_scr[h] for h in heads]

        def kv_steps(first_key, width, carry, result, diagonal=False):
            m_old, l_old, acc_old = carry
            keys = pl.ds(pl.multiple_of(first_key, blk), width)
            logits = [_dot_nt(k_scr[h, keys, :], qs[h]) for h in heads]
            yield
            if diagonal:
                logits = [jnp.where(key_pos <= query_pos, x, NEG_INF) for x in logits]
            m_new = [jnp.maximum(m_old[h], jnp.max(logits[h], axis=0, keepdims=True)) for h in heads]
            alpha = [jnp.exp2(m_old[h] - m_new[h]) for h in heads]
            p = [jnp.exp2(logits[h] - m_new[h]) for h in heads]
            l_new = [alpha[h] * l_old[h] + jnp.sum(p[h], axis=0, keepdims=True) for h in heads]
            pv = [_dot(vt_scr[h // 2, :, keys], p[h].astype(BF16)) for h in heads]
            yield
            acc_new = [jnp.where(first_of_pair, alpha[2 * pr] * acc_old[pr] + pv[2 * pr],
                                 alpha[2 * pr + 1] * acc_old[pr] + pv[2 * pr + 1]) for pr in range(n_pairs)]
            result.append((tuple(m_new), tuple(l_new), tuple(acc_new)))

        def kv_step(first_key, width, carry):
            result = []
            weave(kv_steps(first_key, width, carry, result))
            return result[0]

        def last_steps(carry):
            result = []
            yield from kv_steps(qi * blk, blk, carry, result, diagonal=True)
            _, l, acc = result[0]
            for pr in range(n_pairs):
                out_t = acc[pr] / jnp.where(first_of_pair, l[2 * pr], l[2 * pr + 1])
                o_ref[rows, pr * LANES:(pr + 1) * LANES] = out_t.T

        carry = (tuple(jnp.full((1, blk), NEG_INF, F32) for _ in heads),
                 tuple(jnp.zeros((1, blk), F32) for _ in heads),
                 tuple(jnp.zeros((LANES, blk), F32) for _ in range(n_pairs)))
        done = 0
        for n_blk in MLA_SWEEP:
            todo = (qi - done) // n_blk
            carry = lax.fori_loop(0, todo, lambda j, cr, first=done, n_blk=n_blk:
                                  kv_step((first + j * n_blk) * blk, n_blk * blk, cr), carry)
            done = done + todo * n_blk

        @pl.when(qi + 1 < seq // blk)
        def _last_and_prepare_next():
            weave(last_steps(carry), *(prepare_steps(qi + 1, part) for part in range(MLA_PREP_PARTS)))

        @pl.when(qi + 1 == seq // blk)
        def _last():
            weave(last_steps(carry))
        return 0

    weave(*(prepare_steps(0, part) for part in range(MLA_PREP_PARTS)))
    lax.fori_loop(0, seq // blk, query_block, 0)


def _mla(slab, qn, kvn, wuq, wuk, wuv, gq, gk, cos, sin, *, batch, seq):
    nq = seq // MLA_BLOCK
    return pl.pallas_call(
        functools.partial(_mla_body, seq=seq),
        grid=(batch,),
        in_specs=[pl.BlockSpec((seq, MLA_SLAB), lambda b: (0, b)),
                  _const_spec((1, 256)), _const_spec((1, MLA_KV_RANK)),
                  _const_spec((256, N_HEADS * LANES)), _const_spec((MLA_KV_RANK, N_HEADS * LANES)),
                  _const_spec((MLA_KV_RANK, GROUP_W)),
                  _const_spec((1, LANES)), _const_spec((1, LANES)),
                  _const_spec((seq, LANES)), _const_spec((seq, LANES))],
        out_specs=pl.BlockSpec((seq, GROUP_W), lambda b: (0, b)),
        out_shape=jax.ShapeDtypeStruct((seq, batch * GROUP_W), F32),
        scratch_shapes=[pltpu.VMEM((N_HEADS, seq, LANES), BF16), pltpu.VMEM((N_HEADS // 2, LANES, seq), BF16),
                        pltpu.VMEM((N_HEADS, MLA_BLOCK, LANES), BF16)],
        compiler_params=_params("parallel"),
        name="mla",
    )(slab, qn, kvn, wuq, wuk, wuv, gq, gk, cos, sin)


def _s5_body(u_ref, lre_ref, lim_ref, ldt_ref, bre_ref, bim_ref, cre_ref, cim_ref, d_ref, wglu_ref, o_ref,
             a_scr, bbar_scr, h_scr, x_scr, io_scr, *, batch, steps):
    io_tiles = GROUP_W // LANES

    @pl.when(pl.program_id(0) == 0)
    def _discretise():
        lr = lre_ref[...]
        li = lim_ref[...]
        dt = jnp.exp(ldt_ref[...])
        mag = jnp.exp(lr * dt)
        ar = mag * jnp.cos(li * dt)
        ai = mag * jnp.sin(li * dt)
        den = lr * lr + li * li
        nr = ar - 1.0
        zr = (nr * lr + ai * li) / den
        zi = (ai * lr - nr * li) / den
        a_scr[0] = jnp.broadcast_to(ar, (batch, S5_WIDTH))
        a_scr[1] = jnp.broadcast_to(ai, (batch, S5_WIDTH))
        bre = bre_ref[...]
        bim = bim_ref[...]
        bbar_scr[0] = (zr * bre - zi * bim).astype(BF16)
        bbar_scr[1] = (zr * bim + zi * bre).astype(BF16)
        h_scr[...] = jnp.zeros_like(h_scr)

    for b in range(batch):
        for j in range(io_tiles):
            io_scr[j, pl.ds(b, steps, stride=batch), :] = u_ref[:, b * GROUP_W + j * LANES:b * GROUP_W + (j + 1) * LANES]
    u = jnp.concatenate([io_scr[j] for j in range(io_tiles)], axis=1)
    ub = u.astype(BF16)
    x_scr[0] = _dot(ub, bbar_scr[0])
    x_scr[1] = _dot(ub, bbar_scr[1])
    ar = a_scr[0]
    ai = a_scr[1]

    def step(t, carry):
        hr, hi = carry
        rows = pl.ds(pl.multiple_of(t * batch, batch), batch)
        nhr = ar * hr - ai * hi + x_scr[0, rows, :]
        nhi = ar * hi + ai * hr + x_scr[1, rows, :]
        x_scr[0, rows, :] = nhr
        x_scr[1, rows, :] = nhi
        return nhr, nhi

    hr, hi = lax.fori_loop(0, steps, step, (h_scr[0], h_scr[1]), unroll=8)
    h_scr[0] = hr
    h_scr[1] = hi
    y = _dot(x_scr[0].astype(BF16), cre_ref[...]) - _dot(x_scr[1].astype(BF16), cim_ref[...]) + d_ref[...] * u
    z = _dot(y.astype(BF16), wglu_ref[...])
    out = z[:, :GROUP_W] * jax.nn.sigmoid(z[:, GROUP_W:])
    for j in range(io_tiles):
        io_scr[j] = out[:, j * LANES:(j + 1) * LANES]
    for b in range(batch):
        for j in range(io_tiles):
            o_ref[:, b * GROUP_W + j * LANES:b * GROUP_W + (j + 1) * LANES] = io_scr[j, pl.ds(b, steps, stride=batch), :]


def _s5(u, lre, lim, ldt, bre, bim, cre, cim, d, wglu, *, batch, seq, steps=256):
    state = lambda rows: pltpu.VMEM((2, rows, S5_WIDTH), F32)
    return pl.pallas_call(
        functools.partial(_s5_body, batch=batch, steps=steps),
        grid=(seq // steps,),
        in_specs=[pl.BlockSpec((steps, batch * S5_SLAB), lambda i: (i, 0)),
                  _const_spec((1, S5_WIDTH)), _const_spec((1, S5_WIDTH)), _const_spec((1, S5_WIDTH)),
                  _const_spec((GROUP_W, S5_WIDTH)), _const_spec((GROUP_W, S5_WIDTH)),
                  _const_spec((S5_WIDTH, GROUP_W)), _const_spec((S5_WIDTH, GROUP_W)),
                  _const_spec((1, GROUP_W)), _const_spec((GROUP_W, 2 * GROUP_W))],
        out_specs=pl.BlockSpec((steps, batch * GROUP_W), lambda i: (i, 0)),
        out_shape=jax.ShapeDtypeStruct((seq, batch * GROUP_W), F32),
        scratch_shapes=[state(batch), pltpu.VMEM((2, GROUP_W, S5_WIDTH), BF16), state(batch),
                        state(steps * batch), pltpu.VMEM((GROUP_W // LANES, steps * batch, LANES), F32)],
        compiler_params=_params("arbitrary"),
        name="s5",
    )(u, lre, lim, ldt, bre, bim, cre, cim, d, wglu)


DIL_TILE = 256
DIL_GROUP = (4, 2, 2)


def _pair_norm(x, gain2, same_head):
    ms = _dot_split(x * x, same_head, 2) * (1.0 / HEAD_DIM)
    return x * lax.rsqrt(ms + EPS) * gain2


def _dil_body(x_ref, gq_ref, gk_ref, band_ref, o_ref, bias_ref, q_scr, k_scr, v_scr, oa, ma, la, *, seq):
    span = DIL_SPAN
    n_blocks = seq // span
    n_tiles = GROUP_W // LANES
    lo_blk = lax.broadcasted_iota(jnp.int32, (span, LANES), 1) < HEAD_DIM
    same_head = ((lax.broadcasted_iota(jnp.int32, (LANES, LANES), 0) < HEAD_DIM)
                 == (lax.broadcasted_iota(jnp.int32, (LANES, LANES), 1) < HEAD_DIM)).astype(F32)

    @pl.when(pl.program_id(0) == 0)
    def _build_bias():
        for bh in range(len(DIL_PAIRS) * N_HEADS):
            profile = jnp.broadcast_to(band_ref[bh:bh + 1, :], (span, 2 * span))
            bias_ref[bh] = pltpu.roll(profile, 0, 1, stride=1, stride_axis=0)
    in_current = lax.broadcasted_iota(jnp.int32, (span, 2 * span), 1) >= span

    def norm_tile(i, carry):
        rows = pl.ds(pl.multiple_of(i * DIL_TILE, DIL_TILE), DIL_TILE)
        for j in range(n_tiles):
            cols = lambda part: slice(part * GROUP_W + j * LANES, part * GROUP_W + (j + 1) * LANES)
            q = _pair_norm(x_ref[rows, cols(0)], gq_ref[...], same_head)
            q_scr[j, rows, :] = q * (HEAD_DIM ** -0.5 * LOG2E)
            k_scr[j, rows, :] = _pair_norm(x_ref[rows, cols(1)], gk_ref[...], same_head)
            v_scr[j, rows, :] = x_ref[rows, cols(2)]
        return carry
    lax.fori_loop(0, seq // DIL_TILE, norm_tile, 0)

    for bi, (window, dil) in enumerate(DIL_PAIRS):
        sub_len = seq // dil
        nb = sub_len // span

        group = DIL_GROUP[bi]

        def block_group(i, carry, bi=bi, nb=nb, dil=dil, group=group):
            todo = []
            for u in range(group):
                t = i * group + u
                r = lax.shift_right_logical(t, int(math.log2(nb)))
                c = t & (nb - 1)
                first = r + c * (dil * span)
                stride = dil if dil > 1 else None
                rows = pl.ds(first, span, stride=stride)
                prev = pl.ds(jnp.maximum(first - dil * span, r), span, stride=stride)
                keep = jnp.logical_or(in_current, c != 0) if nb > 1 else None
                for j in range(n_tiles):
                    q2 = q_scr[j, rows, :]
                    if nb > 1:
                        k_cat = jnp.concatenate([k_scr[j, prev, :], k_scr[j, rows, :]], axis=0).astype(BF16)
                        v_cat = jnp.concatenate([v_scr[j, prev, :], v_scr[j, rows, :]], axis=0).astype(BF16)
                    else:
                        k_cat = k_scr[j, rows, :].astype(BF16)
                        v_cat = v_scr[j, rows, :].astype(BF16)
                    todo.append((rows, j, q2, k_cat, v_cat, keep))
            logits = []
            for rows, j, q2, k_cat, v_cat, keep in todo:
                for a in range(2):
                    qa = jnp.where(lo_blk if a == 0 else jnp.logical_not(lo_blk), q2, 0.0).astype(BF16)
                    bias = bias_ref[bi * N_HEADS + 2 * j + a]
                    lg = _dot_nt(qa, k_cat) + (bias if nb > 1 else bias[:, span:])
                    logits.append(jnp.where(keep, lg, NEG_INF) if nb > 1 else lg)
            m = [jnp.max(lg, axis=-1, keepdims=True) for lg in logits]
            p = [jnp.exp2(lg - mx) for lg, mx in zip(logits, m)]
            l = [jnp.sum(px, axis=-1, keepdims=True) for px in p]
            o = [_dot(p[2 * n + a].astype(BF16), todo[n][4]) for n in range(len(todo)) for a in range(2)]
            for n, (rows, j, *_) in enumerate(todo):
                o_in = jnp.where(lo_blk, o[2 * n], o[2 * n + 1])
                m_in = jnp.where(lo_blk, m[2 * n], m[2 * n + 1])
                l_in = jnp.where(lo_blk, l[2 * n], l[2 * n + 1])
                if bi > 0:
                    m_old = ma[j, rows, :]
                    m_new = jnp.maximum(m_old, m_in)
                    w_old, w_in = jnp.exp2(m_old - m_new), jnp.exp2(m_in - m_new)
                    o_in = w_old * oa[j, rows, :] + w_in * o_in
                    l_in = w_old * la[j, rows, :] + w_in * l_in
                    m_in = m_new
                oa[j, rows, :] = o_in
                ma[j, rows, :] = m_in
                la[j, rows, :] = l_in
            return carry
        lax.fori_loop(0, n_blocks // group, block_group, 0)

    def finish(i, carry):
        rows = pl.ds(pl.multiple_of(i * DIL_TILE, DIL_TILE), DIL_TILE)
        for j in range(n_tiles):
            o_ref[rows, j * LANES:(j + 1) * LANES] = oa[j, rows, :] / la[j, rows, :]
        return carry
    lax.fori_loop(0, seq // DIL_TILE, finish, 0)


def _dil(slab, gq2, gk2, band, *, batch, seq):
    big = lambda: pltpu.VMEM((GROUP_W // LANES, seq, LANES), F32)
    n_bias = len(DIL_PAIRS) * N_HEADS
    return pl.pallas_call(
        functools.partial(_dil_body, seq=seq),
        grid=(batch,),
        in_specs=[pl.BlockSpec((seq, DIL_SLAB), lambda b: (0, b)),
                  _const_spec((1, LANES)), _const_spec((1, LANES)),
                  _const_spec(band.shape)],
        out_specs=pl.BlockSpec((seq, GROUP_W), lambda b: (0, b)),
        out_shape=jax.ShapeDtypeStruct((seq, batch * GROUP_W), F32),
        scratch_shapes=[pltpu.VMEM((n_bias, DIL_SPAN, 2 * DIL_SPAN), F32)] + [big() for _ in range(6)],
        compiler_params=_params("arbitrary"),
        name="dilated",
    )(slab, gq2, gk2, band)


def _t5_bucket(dist):
    exact = T5_BUCKETS // 2
    df = jnp.maximum(dist, 1).astype(F32)
    large = exact + (jnp.log(df / exact) / math.log(T5_MAX_DIST / exact) * (T5_BUCKETS - exact)).astype(jnp.int32)
    large = jnp.minimum(large, T5_BUCKETS - 1)
    return jnp.where(dist < exact, dist, large)


def _dil_band(table):
    span = DIL_SPAN
    delta = span - jnp.arange(2 * span, dtype=jnp.int32)
    rows = []
    for _, dil in DIL_PAIRS:
        bucket = _t5_bucket(jnp.clip(delta, 0, span) * dil)
        onehot = (bucket[:, None] == jnp.arange(T5_BUCKETS, dtype=jnp.int32)[None, :]).astype(F32)
        vals = jnp.dot(onehot, table.astype(F32), precision=HIGHEST)
        rows.append(jnp.where((delta >= 0)[:, None], vals * LOG2E, NEG_INF).T)
    return jnp.concatenate(rows, axis=0)


DN_TILE = 256
DN_GROUP = 4
PREP_STORE_DELAY = 12


def _softplus(x):
    return jnp.maximum(x, 0.0) + jnp.log1p(jnp.exp(-jnp.abs(x)))


def _pair_l2(x, lo_half):
    sq = x * x
    tot = jnp.sum(sq, axis=-1, keepdims=True)
    lo = jnp.sum(jnp.where(lo_half, sq, 0.0), axis=-1, keepdims=True)
    return x * lax.rsqrt(jnp.where(lo_half, lo, tot - lo) + EPS)


def _dn_body(x_ref, cw_ref, alog_ref, dtb_ref, on_ref, o_ref, q_scr, k_scr, v_scr, g_scr, b_scr, w_scr, a_scr,
             s_scr, *, seq):
    c = DN_CHUNK
    w = GROUP_W
    lo_tile = lax.broadcasted_iota(jnp.int32, (DN_TILE, LANES), 1) < HEAD_DIM

    def prep_steps(i):
        if isinstance(i, int):
            first, halo_first = i * DN_TILE, max(i * DN_TILE - 8, 0)
        else:
            first = pl.multiple_of(i * DN_TILE, DN_TILE)
            halo_first = pl.multiple_of(jnp.maximum(first - 8, 0), 8)
        rows = pl.ds(first, DN_TILE)
        cur = x_ref[rows, 0:3 * w]
        halo = jnp.where(i > 0, x_ref[pl.ds(halo_first, 8), 0:3 * w], 0.0)
        ext = jnp.concatenate([halo, cur], axis=0)
        acc = cw_ref[DN_CONV - 1:DN_CONV, :] * cur
        for j in range(DN_CONV - 1):
            acc = acc + cw_ref[j:j + 1, :] * pltpu.roll(ext, DN_CONV - 1 - j, 0)[8:, :]
        y = acc * jax.nn.sigmoid(acc)
        yield
        tiles = range(w // LANES)
        cols = lambda part, j: slice(part * w + j * LANES, part * w + (j + 1) * LANES)
        q = [_pair_l2(y[:, cols(0, j)], lo_tile) * (HEAD_DIM ** -0.5) for j in tiles]
        k = [_pair_l2(y[:, cols(1, j)], lo_tile) for j in tiles]
        yield
        ab = x_ref[rows, 3 * w:3 * w + LANES]
        g = -jnp.exp(alog_ref[...]) * _softplus(ab + dtb_ref[...])
        gc = _dot_split(chunk_tril, g, 3, split_rhs=True)
        yield
        gc = per_head_lanes(gc, 0)
        beta = per_head_lanes(jax.nn.sigmoid(ab), N_HEADS)
        for _ in range(PREP_STORE_DELAY):
            yield
        for j in tiles:
            q_scr[rows, j * LANES:(j + 1) * LANES] = q[j]
            k_scr[rows, j * LANES:(j + 1) * LANES] = k[j]
            v_scr[rows, j * LANES:(j + 1) * LANES] = y[:, cols(2, j)]
        g_scr[rows, :] = gc
        b_scr[rows, :] = beta

    ri = lax.broadcasted_iota(jnp.int32, (c, w), 0)
    ci = lax.broadcasted_iota(jnp.int32, (c, w), 1)
    cj = jnp.bitwise_and(ci, HEAD_DIM - 1)
    causal = ri >= cj
    strict = ri > cj
    eye4 = (ri == cj).astype(F32)
    same_sub = lax.shift_right_logical(ri, 4) == lax.shift_right_logical(cj, 4)
    bi_r = lax.broadcasted_iota(jnp.int32, (w, w), 0)
    bi_c = lax.broadcasted_iota(jnp.int32, (w, w), 1)
    same_head = lax.shift_right_logical(bi_r, 6) == lax.shift_right_logical(bi_c, 6)
    block_mask = same_head.astype(F32)
    chunk_tril = (same_head & (jnp.bitwise_and(bi_r, c - 1) >= jnp.bitwise_and(bi_c, c - 1))).astype(F32)
    head_of_lane = lax.shift_right_logical(lax.broadcasted_iota(jnp.int32, (DN_TILE, w), 1), 6)

    def per_head_lanes(x, first_lane):
        out = None
        for h in range(N_HEADS):
            col = jnp.broadcast_to(x[:, first_lane + h:first_lane + h + 1], (DN_TILE, w))
            out = col if out is None else jnp.where(head_of_lane == h, col, out)
        return out

    def per_head(a, b):
        bd = jnp.where(same_head, jnp.concatenate([b.astype(BF16)] * N_HEADS, axis=0), 0.0)
        return _dot(a.astype(BF16), bd)

    def group_rows(i):
        base = pl.multiple_of(i * (DN_GROUP * c), DN_GROUP * c)
        return [pl.ds(base + u * c, c) for u in range(DN_GROUP)]

    def solve_steps(i):
        rows = group_rows(i)
        each = lambda f, *lists: [f(*args) for args in zip(*lists)]
        q, k, v, gc, beta = ([ref[r, :] for r in rows] for ref in (q_scr, k_scr, v_scr, g_scr, b_scr))
        g_row = each(lambda g: jnp.sum(g * eye4, axis=0, keepdims=True), gc)
        decay = each(lambda g, gr: jnp.exp(jnp.where(causal, g - gr, NEG_INF)), gc, g_row)
        kb = each(jnp.multiply, k, beta)
        k_bd = each(lambda x: jnp.where(same_head, jnp.concatenate([x.astype(BF16)] * N_HEADS, axis=0), 0.0), k)
        lmat = each(lambda a, b, d: jnp.where(strict, _dot_nt(a.astype(BF16), b) * d, 0.0), kb, k_bd, decay)
        yield
        a_qk = each(lambda a, b, d: jnp.where(causal, _dot_nt(a.astype(BF16), b) * d, 0.0), q, k_bd, decay)
        yield
        def pair_head(a0, a1, b):
            both = per_head(jnp.concatenate([a0, a1], axis=0), b)
            return both[:c], both[c:]

        p = each(lambda l_: jnp.where(same_sub, -l_, 0.0), lmat)
        t_diag = each(lambda x: eye4 + x, p)
        p = each(per_head, p, p)
        yield
        for _ in range(2):
            nxt = each(pair_head, p, t_diag, p)
            p = [x[0] for x in nxt]
            t_diag = each(lambda t, x: t + x[1], t_diag, nxt)
            yield
        t_diag = each(lambda t, x: t + per_head(t, x), t_diag, p)
        yield
        m1 = each(lambda l_, t: per_head(jnp.where(same_sub, 0.0, l_), t), lmat, t_diag)
        yield
        nxt = each(pair_head, t_diag, m1, m1)
        y = each(lambda t, x: t - x[0], t_diag, nxt)
        yield
        t_inv = each(lambda yy, x: yy + per_head(yy, x[1]), y, nxt)
        yield
        eg = each(jnp.exp, gc)
        w_c = each(lambda t, a, e: per_head(t, a * e), t_inv, kb, eg)
        yield
        u_c = each(lambda t, a, b: per_head(t, a * b), t_inv, v, beta)
        yield
        q_dec = each(jnp.multiply, q, eg)
        k_dec = each(lambda x, g: x * jnp.exp(g[c - 1:c, :] - g), k, gc)
        for ref, vals in zip((w_scr, v_scr, a_scr, q_scr, k_scr), (w_c, u_c, a_qk, q_dec, k_dec)):
            for r, val in zip(rows, vals):
                ref[r, :] = val

    def state_steps(i):
        for rows in group_rows(i):
            state = s_scr[...]
            state_b = state.astype(BF16)
            w_s = _dot(w_scr[rows, :].astype(BF16), state_b)
            q_s = _dot(q_scr[rows, :].astype(BF16), state_b)
            yield
            v_new = v_scr[rows, :] - w_s
            o_intra = per_head(a_scr[rows, :], v_new)
            upd = lax.dot_general(k_scr[rows, :].astype(BF16), v_new.astype(BF16), (((0,), (0,)), ((), ())),
                                  preferred_element_type=F32)
            yield
            o_ref[rows, :] = q_s + o_intra
            s_scr[...] = state * jnp.exp(g_scr[rows, :][c - 1:c, :]) + upd * block_mask
            yield

    def weave(*steps):
        live = list(steps)
        while live:
            for gen in list(live):
                if next(gen, "done") == "done":
                    live.remove(gen)

    n_groups = seq // (DN_GROUP * c)
    s_scr[...] = jnp.zeros_like(s_scr)
    weave(prep_steps(0))
    weave(solve_steps(0), prep_steps(1))

    def group(i, carry):
        weave(solve_steps(i), state_steps(i - 1), prep_steps(i + 1))
        return carry
    lax.fori_loop(1, n_groups - 1, group, 0)
    weave(solve_steps(n_groups - 1), state_steps(n_groups - 2))
    weave(state_steps(n_groups - 1))

    def finish_tile(i, carry):
        rows = pl.ds(pl.multiple_of(i * DN_TILE, DN_TILE), DN_TILE)
        o = o_ref[rows, :]
        ms = _dot_split(o * o, block_mask, 2) * (1.0 / HEAD_DIM)
        gate = x_ref[rows, 3 * w + LANES:4 * w + LANES]
        o_ref[rows, :] = o * lax.rsqrt(ms + EPS) * on_ref[...] * (gate * jax.nn.sigmoid(gate))
        return carry
    lax.fori_loop(0, seq // DN_TILE, finish_tile, 0)


def _dn(slab, conv_w, a_log, dt_bias, o_norm, *, batch, seq):
    wide = lambda: pltpu.VMEM((seq, GROUP_W), F32)
    return pl.pallas_call(
        functools.partial(_dn_body, seq=seq),
        grid=(batch,),
        in_specs=[pl.BlockSpec((seq, DN_SLAB), lambda b: (0, b)),
                  _const_spec((DN_CONV, 3 * GROUP_W)), _const_spec((1, LANES)), _const_spec((1, LANES)),
                  _const_spec((1, GROUP_W))],
        out_specs=pl.BlockSpec((seq, GROUP_W), lambda b: (0, b)),
        out_shape=jax.ShapeDtypeStruct((seq, batch * GROUP_W), F32),
        scratch_shapes=[wide() for _ in range(7)] + [pltpu.VMEM((GROUP_W, GROUP_W), F32)],
        compiler_params=_params("parallel"),
        name="deltanet",
    )(slab, conv_w, a_log, dt_bias, o_norm)


def _row(v, width=None):
    v = v.astype(F32).reshape(1, -1)
    if width is not None and v.shape[1] < width:
        v = jnp.pad(v, ((0, 0), (0, width - v.shape[1])))
    return v


def _prep_w_in(w):
    w = w.astype(BF16)
    z = lambda n: jnp.zeros(w.shape[:2] + (n,), BF16)
    return jnp.concatenate([w[..., 0:416], z(96), w[..., 416:2216], z(120), w[..., 2216:2472]], axis=-1)


def _pad_heads(w, per_head, lo, hi):
    k = w.shape[0]
    w = w.reshape(k, N_HEADS, per_head)[:, :, lo:hi]
    w = jnp.pad(w, ((0, 0), (0, 0), (0, LANES - (hi - lo))))
    return w.reshape(k, N_HEADS * LANES).astype(BF16)


def _rope_tables(seq):
    half = MLA_ROPE // 2
    pos = jnp.arange(seq, dtype=F32)
    freqs = ROPE_THETA ** (-jnp.arange(half, dtype=F32) / half)
    ang = pos[:, None] * freqs[None, :]
    cos, sin = jnp.cos(ang), jnp.sin(ang)
    ones = jnp.ones((seq, MLA_NOPE), F32)
    zeros = jnp.zeros((seq, MLA_NOPE), F32)
    tail1 = jnp.ones((seq, LANES - MLA_DQK), F32)
    tail0 = jnp.zeros((seq, LANES - MLA_DQK), F32)
    return (jnp.concatenate([ones, cos, cos, tail1], axis=1),
            jnp.concatenate([zeros, -sin, sin, tail0], axis=1))


def _block_diag(blocks):
    g, r, c = blocks.shape
    eye = jnp.eye(g, dtype=blocks.dtype)
    return (blocks[:, :, None, :] * eye[:, None, :, None]).reshape(g * r, g * c)


def _mla_layer(slab, p, l, *, batch, seq):
    w_ukv = p["mla_w_ukv"][l]
    cos, sin = _rope_tables(seq)
    return _mla(slab, _row(p["mla_q_norm"][l]), _row(p["mla_kv_norm"][l]),
                _pad_heads(p["mla_w_uq"][l], MLA_DQK, 0, MLA_DQK),
                _pad_heads(w_ukv, MLA_NOPE + HEAD_DIM, 0, MLA_NOPE),
                w_ukv.reshape(MLA_KV_RANK, N_HEADS, MLA_NOPE + HEAD_DIM)[:, :, MLA_NOPE:]
                .reshape(MLA_KV_RANK, GROUP_W).astype(BF16),
                _row(p["mla_qk_q"][l], LANES), _row(p["mla_qk_k"][l], LANES), cos, sin, batch=batch, seq=seq)


def _dil_layer(slab, p, l, *, batch, seq):
    pair = lambda g: jnp.tile(g.astype(F32).reshape(1, HEAD_DIM), (1, LANES // HEAD_DIM))
    return _dil(slab, pair(p["dil_q_norm"][l]), pair(p["dil_k_norm"][l]), _dil_band(p["t5_bias"]),
                batch=batch, seq=seq)


def _dn_layer(slab, p, l, *, batch, seq):
    return _dn(slab, p["dn_conv"][l].astype(F32), _row(p["dn_a_log"][l], LANES), _row(p["dn_dt_bias"][l], LANES),
               jnp.tile(p["dn_o_norm"][l].astype(F32).reshape(1, HEAD_DIM), (1, N_HEADS)), batch=batch, seq=seq)


def _s5_layer(u, p, l, *, batch, seq):
    state_row = lambda v: v.astype(F32).reshape(1, S5_WIDTH)
    ldt = jnp.broadcast_to(p["s5_log_dt"][l][:, None], (S5_GROUPS, S5_STATE))
    bre = _block_diag(jnp.swapaxes(p["s5_b_re"][l], 1, 2).astype(F32))
    bim = _block_diag(jnp.swapaxes(p["s5_b_im"][l], 1, 2).astype(F32))
    cre = _block_diag(jnp.swapaxes(p["s5_c_re"][l], 1, 2)).astype(BF16)
    cim = _block_diag(jnp.swapaxes(p["s5_c_im"][l], 1, 2)).astype(BF16)
    return _s5(u, state_row(p["s5_lambda_re"][l]), state_row(p["s5_lambda_im"][l]), state_row(ldt),
               bre, bim, cre, cim, _row(p["s5_d"][l]), p["s5_w_glu"][l].astype(BF16), batch=batch, seq=seq,
               steps=min(256, seq))


def kernel(x, attn_norm, w_in, w_out, mla_q_norm, mla_kv_norm, mla_w_uq, mla_w_ukv, mla_qk_q, mla_qk_k,
           s5_lambda_re, s5_lambda_im, s5_log_dt, s5_b_re, s5_b_im, s5_c_re, s5_c_im, s5_d, s5_w_glu,
           dil_q_norm, dil_k_norm, t5_bias, dn_conv, dn_a_log, dn_dt_bias, dn_o_norm,
           ffn_norm, ffn_w1, ffn_w3, ffn_w2):
    p = dict(mla_q_norm=mla_q_norm, mla_kv_norm=mla_kv_norm, mla_w_uq=mla_w_uq, mla_w_ukv=mla_w_ukv,
             mla_qk_q=mla_qk_q, mla_qk_k=mla_qk_k, s5_lambda_re=s5_lambda_re, s5_lambda_im=s5_lambda_im,
             s5_log_dt=s5_log_dt, s5_b_re=s5_b_re, s5_b_im=s5_b_im, s5_c_re=s5_c_re, s5_c_im=s5_c_im, s5_d=s5_d,
             s5_w_glu=s5_w_glu, dil_q_norm=dil_q_norm, dil_k_norm=dil_k_norm, t5_bias=t5_bias, dn_conv=dn_conv,
             dn_a_log=dn_a_log, dn_dt_bias=dn_dt_bias, dn_o_norm=dn_o_norm)
    batch, seq, _ = x.shape
    h = x.reshape(batch * seq, D_MODEL)
    w_in_b, w_out_b = _prep_w_in(w_in), w_out.astype(BF16)
    w1_b, w3_b, w2_b = ffn_w1.astype(BF16), ffn_w3.astype(BF16), ffn_w2.astype(BF16)
    for l in range(attn_norm.shape[0]):
        mla_in, s5_in, dil_in, dn_in = _proj(h, _row(attn_norm[l]), w_in_b, l, batch=batch, seq=seq)
        ys = [_mla_layer(mla_in, p, l, batch=batch, seq=seq),
              _s5_layer(s5_in, p, l, batch=batch, seq=seq),
              _dil_layer(dil_in, p, l, batch=batch, seq=seq),
              _dn_layer(dn_in, p, l, batch=batch, seq=seq)]
        h = _out_ffn(h, ys, w_out_b, _row(ffn_norm[l]), w1_b, w3_b, w2_b, l, batch=batch, seq=seq)
    return h.reshape(batch, seq, D_MODEL)
```

```python
import functools
import math

import jax
import jax.numpy as jnp
from jax import lax
from jax.experimental import pallas as pl
from jax.experimental.pallas import tpu as pltpu

F32 = jnp.float32
BF16 = jnp.bfloat16
HIGHEST = lax.Precision.HIGHEST

D_MODEL = 1024
GROUP_W = 256
HEAD_DIM = 64
N_HEADS = 4
EPS = 1e-6
NEG_INF = -1e30
LOG2E = math.log2(math.e)

MLA_NOPE = 64
MLA_ROPE = 32
MLA_DQK = MLA_NOPE + MLA_ROPE
MLA_KV_RANK = 128
ROPE_THETA = 10000.0

S5_GROUPS = 16
S5_STATE = 64
S5_WIDTH = S5_GROUPS * S5_STATE

DIL_PAIRS = ((128, 1), (512, 4), (2048, 16))
DIL_SPAN = 128
T5_BUCKETS = 32
T5_MAX_DIST = 2048

DN_CONV = 4
DN_CHUNK = 64

FFN_HIDDEN = 2816
FFN_CHUNK = 2816

VMEM_LIMIT_BYTES = 56 * 1024 * 1024
LANES = 128

MLA_SLAB = 512
S5_SLAB = 256
DIL_SLAB = 768
DN_SLAB = 1152
PROJ_COLS = MLA_SLAB + S5_SLAB + DIL_SLAB + DN_SLAB


def _dot(a, b, precision=None):
    return jnp.dot(a, b, preferred_element_type=F32, precision=precision)


def _dot_nt(a, b, precision=None):
    return lax.dot_general(a, b, (((1,), (1,)), ((), ())), preferred_element_type=F32, precision=precision)


def _dot_split(a, b, terms, split_rhs=False):
    x = b if split_rhs else a
    mask = (a if split_rhs else b).astype(BF16)
    out = None
    for _ in range(terms):
        piece = x.astype(BF16)
        part = _dot(mask, piece) if split_rhs else _dot(piece, mask)
        out = part if out is None else out + part
        x = x - piece.astype(F32)
    return out


def _const_spec(shape):
    nd = len(shape)
    return pl.BlockSpec(shape, lambda *_: (0,) * nd, pipeline_mode=pl.Buffered(1))


def _params(*sem):
    return pltpu.CompilerParams(dimension_semantics=sem, vmem_limit_bytes=VMEM_LIMIT_BYTES)


def _proj_body(x_ref, g_ref, w_ref, mla_ref, s5_ref, dil_ref, dn_ref):
    x = x_ref[...]
    n = x * lax.rsqrt(jnp.mean(x * x, axis=-1, keepdims=True) + EPS) * g_ref[...]
    nb = n.astype(BF16)
    start = 0
    for ref in (mla_ref, s5_ref, dil_ref, dn_ref):
        width = ref.shape[-1]
        ref[...] = _dot(nb, w_ref[:, start:start + width])
        start += width


def _layer_spec(shape, layer):
    nd = len(shape)
    return pl.BlockSpec((None,) + tuple(shape), lambda *_: (layer,) + (0,) * nd, pipeline_mode=pl.Buffered(1))


def _proj(h, gain, w_big, layer, *, batch, seq, tm=1024):
    nt = seq // tm
    widths = (MLA_SLAB, S5_SLAB, DIL_SLAB, DN_SLAB)
    return pl.pallas_call(
        _proj_body,
        grid=(batch, nt),
        in_specs=[pl.BlockSpec((tm, D_MODEL), lambda b, i: (b * nt + i, 0)),
                  _const_spec((1, D_MODEL)),
                  _layer_spec((D_MODEL, PROJ_COLS), layer)],
        out_specs=[pl.BlockSpec((tm, w), lambda b, i: (i, b)) for w in widths],
        out_shape=[jax.ShapeDtypeStruct((seq, batch * w), F32) for w in widths],
        compiler_params=_params("parallel", "parallel"),
        name="proj",
    )(h, gain, w_big)


def _out_ffn_body(h_ref, y0_ref, y1_ref, y2_ref, y3_ref, wo_ref, g_ref, w1_ref, w3_ref, w2_ref, o_ref, acc_ref):
    mixed = jnp.concatenate([y_ref[...].astype(BF16) for y_ref in (y0_ref, y1_ref, y2_ref, y3_ref)], axis=1)
    h = h_ref[...] + _dot(mixed, wo_ref[...])
    n = h * lax.rsqrt(jnp.mean(h * h, axis=-1, keepdims=True) + EPS) * g_ref[...]
    nb = n.astype(BF16)
    acc_ref[...] = h

    def hidden_chunk(c, carry):
        cols = pl.ds(pl.multiple_of(c * FFN_CHUNK, FFN_CHUNK), FFN_CHUNK)
        a = _dot(nb, w1_ref[:, cols])
        b = _dot(nb, w3_ref[:, cols])
        z = (a * jax.nn.sigmoid(a) * b).astype(BF16)
        acc_ref[...] += _dot(z, w2_ref[cols, :])
        return carry
    lax.fori_loop(0, FFN_HIDDEN // FFN_CHUNK, hidden_chunk, 0)
    o_ref[...] = acc_ref[...]


def _out_ffn(h, ys, w_out, gain, w1, w3, w2, layer, *, batch, seq, tm=512):
    nt = seq // tm
    row = pl.BlockSpec((tm, D_MODEL), lambda b, i: (b * nt + i, 0))
    slab = pl.BlockSpec((tm, GROUP_W), lambda b, i: (i, b))
    return pl.pallas_call(
        _out_ffn_body,
        grid=(batch, nt),
        in_specs=[row] + [slab] * 4 + [
            _layer_spec((D_MODEL, D_MODEL), layer), _const_spec((1, D_MODEL)),
            _layer_spec((D_MODEL, FFN_HIDDEN), layer), _layer_spec((D_MODEL, FFN_HIDDEN), layer),
            _layer_spec((FFN_HIDDEN, D_MODEL), layer)],
        out_specs=row,
        out_shape=jax.ShapeDtypeStruct((batch * seq, D_MODEL), F32),
        scratch_shapes=[pltpu.VMEM((tm, D_MODEL), F32)],
        compiler_params=_params("parallel", "parallel"),
        name="out_ffn",
    )(h, *ys, w_out, gain, w1, w3, w2)


MLA_BLOCK = 256
MLA_PREP_PARTS = 1
MLA_SWEEP = (4, 2, 1)


def _mla_body(x_ref, qn_ref, kvn_ref, wuq_ref, wuk_ref, wuv_ref, gq_ref, gk_ref, cos_ref, sin_ref,
              o_ref, k_scr, vt_scr, q_scr, *, seq):
    blk = MLA_BLOCK
    n_pairs = N_HEADS // 2
    part_rows = blk // MLA_PREP_PARTS

    def rope(x, c, s):
        return x * c + pltpu.roll(x, LANES // 2, 1) * s

    def norm_head(x, g):
        ssq = jnp.sum(x * x, axis=-1, keepdims=True)
        return x * lax.rsqrt(ssq * (1.0 / MLA_DQK) + EPS) * g

    heads = range(N_HEADS)

    def prepare_steps(i, part):
        first = i * blk + part * part_rows
        rows = pl.ds(first if isinstance(i, int) else pl.multiple_of(first, part_rows), part_rows)
        part_of_block = pl.ds(part * part_rows, part_rows)
        ckv = x_ref[rows, 256:384]
        cq = x_ref[rows, 0:256]
        kvn = ckv * lax.rsqrt(jnp.mean(ckv * ckv, axis=-1, keepdims=True) + EPS) * kvn_ref[...]
        qn = cq * lax.rsqrt(jnp.mean(cq * cq, axis=-1, keepdims=True) + EPS) * qn_ref[...]
        kvn = kvn.astype(BF16)
        k_nope = _dot(kvn, wuk_ref[...])
        v_t = _dot_nt(wuv_ref[...], kvn)
        q_all = _dot(qn.astype(BF16), wuq_ref[...])
        yield
        k_rope = x_ref[rows, 384:512]
        c = cos_ref[rows, :]
        s = sin_ref[rows, :]
        ks = [rope(norm_head(k_nope[:, h * LANES:(h + 1) * LANES] + k_rope, gk_ref[...]), c, s) for h in heads]
        yield
        qs_new = [rope(norm_head(q_all[:, h * LANES:(h + 1) * LANES], gq_ref[...]), c, s) * (MLA_DQK ** -0.5 * LOG2E)
                  for h in heads]
        vts = [v_t[pr * LANES:(pr + 1) * LANES, :] for pr in range(n_pairs)]
        yield
        yield
        for h in heads:
            k_scr[h, rows, :] = ks[h].astype(BF16)
            q_scr[h, part_of_block, :] = qs_new[h].astype(BF16)
        for pr in range(n_pairs):
            vt_scr[pr, :, rows] = vts[pr].astype(BF16)

    def weave(*steps):
        live = list(steps)
        while live:
            for gen in list(live):
                if next(gen, "done") == "done":
                    live.remove(gen)

    key_pos = lax.broadcasted_iota(jnp.int32, (blk, blk), 0)
    query_pos = lax.broadcasted_iota(jnp.int32, (blk, blk), 1)
    first_of_pair = lax.broadcasted_iota(jnp.int32, (LANES, blk), 0) < HEAD_DIM

    def query_block(qi, _):
        rows = pl.ds(pl.multiple_of(qi * blk, blk), blk)
        qs = [q_scr[h] for h in heads]

        def kv_steps(first_key, width, carry, result, diagonal=False):
            m_old, l_old, acc_old = carry
            keys = pl.ds(pl.multiple_of(first_key, blk), width)
            logits = [_dot_nt(k_scr[h, keys, :], qs[h]) for h in heads]
            yield
            if diagonal:
                logits = [jnp.where(key_pos <= query_pos, x, NEG_INF) for x in logits]
            m_new = [jnp.maximum(m_old[h], jnp.max(logits[h], axis=0, keepdims=True)) for h in heads]
            alpha = [jnp.exp2(m_old[h] - m_new[h]) for h in heads]
            p = [jnp.exp2(logits[h] - m_new[h]) for h in heads]
            l_new = [alpha[h] * l_old[h] + jnp.sum(p[h], axis=0, keepdims=True) for h in heads]
            pv = [_dot(vt_scr[h // 2, :, keys], p[h].astype(BF16)) for h in heads]
            yield
            acc_new = [jnp.where(first_of_pair, alpha[2 * pr] * acc_old[pr] + pv[2 * pr],
                                 alpha[2 * pr + 1] * acc_old[pr] + pv[2 * pr + 1]) for pr in range(n_pairs)]
            result.append((tuple(m_new), tuple(l_new), tuple(acc_new)))

        def kv_step(first_key, width, carry):
            result = []
            weave(kv_steps(first_key, width, carry, result))
            return result[0]

        def last_steps(carry):
            result = []
            yield from kv_steps(qi * blk, blk, carry, result, diagonal=True)
            _, l, acc = result[0]
            for pr in range(n_pairs):
                out_t = acc[pr] / jnp.where(first_of_pair, l[2 * pr], l[2 * pr + 1])
                o_ref[rows, pr * LANES:(pr + 1) * LANES] = out_t.T

        carry = (tuple(jnp.full((1, blk), NEG_INF, F32) for _ in heads),
                 tuple(jnp.zeros((1, blk), F32) for _ in heads),
                 tuple(jnp.zeros((LANES, blk), F32) for _ in range(n_pairs)))
        done = 0
        for n_blk in MLA_SWEEP:
            todo = (qi - done) // n_blk
            carry = lax.fori_loop(0, todo, lambda j, cr, first=done, n_blk=n_blk:
                                  kv_step((first + j * n_blk) * blk, n_blk * blk, cr), carry)
            done = done + todo * n_blk

        @pl.when(qi + 1 < seq // blk)
        def _last_and_prepare_next():
            weave(last_steps(carry), *(prepare_steps(qi + 1, part) for part in range(MLA_PREP_PARTS)))

        @pl.when(qi + 1 == seq // blk)
        def _last():
            weave(last_steps(carry))
        return 0

    weave(*(prepare_steps(0, part) for part in range(MLA_PREP_PARTS)))
    lax.fori_loop(0, seq // blk, query_block, 0)


def _mla(slab, qn, kvn, wuq, wuk, wuv, gq, gk, cos, sin, *, batch, seq):
    nq = seq // MLA_BLOCK
    return pl.pallas_call(
        functools.partial(_mla_body, seq=seq),
        grid=(batch,),
        in_specs=[pl.BlockSpec((seq, MLA_SLAB), lambda b: (0, b)),
                  _const_spec((1, 256)), _const_spec((1, MLA_KV_RANK)),
                  _const_spec((256, N_HEADS * LANES)), _const_spec((MLA_KV_RANK, N_HEADS * LANES)),
                  _const_spec((GROUP_W, MLA_KV_RANK)),
                  _const_spec((1, LANES)), _const_spec((1, LANES)),
                  _const_spec((seq, LANES)), _const_spec((seq, LANES))],
        out_specs=pl.BlockSpec((seq, GROUP_W), lambda b: (0, b)),
        out_shape=jax.ShapeDtypeStruct((seq, batch * GROUP_W), F32),
        scratch_shapes=[pltpu.VMEM((N_HEADS, seq, LANES), BF16), pltpu.VMEM((N_HEADS // 2, LANES, seq), BF16),
                        pltpu.VMEM((N_HEADS, MLA_BLOCK, LANES), BF16)],
        compiler_params=_params("parallel"),
        name="mla",
    )(slab, qn, kvn, wuq, wuk, wuv, gq, gk, cos, sin)


def _s5_body(u_ref, lre_ref, lim_ref, ldt_ref, bre_ref, bim_ref, cre_ref, cim_ref, d_ref, wglu_ref, o_ref,
             a_scr, bbar_scr, h_scr, x_scr, io_scr, *, batch, steps):
    io_tiles = GROUP_W // LANES

    @pl.when(pl.program_id(0) == 0)
    def _discretise():
        lr = lre_ref[...]
        li = lim_ref[...]
        dt = jnp.exp(ldt_ref[...])
        mag = jnp.exp(lr * dt)
        ar = mag * jnp.cos(li * dt)
        ai = mag * jnp.sin(li * dt)
        den = lr * lr + li * li
        nr = ar - 1.0
        zr = (nr * lr + ai * li) / den
        zi = (ai * lr - nr * li) / den
        a_scr[0] = jnp.broadcast_to(ar, (batch, S5_WIDTH))
        a_scr[1] = jnp.broadcast_to(ai, (batch, S5_WIDTH))
        bre = bre_ref[...]
        bim = bim_ref[...]
        bbar_scr[0] = (zr * bre - zi * bim).astype(BF16)
        bbar_scr[1] = (zr * bim + zi * bre).astype(BF16)
        h_scr[...] = jnp.zeros_like(h_scr)

    for b in range(batch):
        for j in range(io_tiles):
            io_scr[j, pl.ds(b, steps, stride=batch), :] = u_ref[:, b * GROUP_W + j * LANES:b * GROUP_W + (j + 1) * LANES]
    u = jnp.concatenate([io_scr[j] for j in range(io_tiles)], axis=1)
    ub = u.astype(BF16)
    x_scr[0] = _dot(ub, bbar_scr[0])
    x_scr[1] = _dot(ub, bbar_scr[1])
    ar = a_scr[0]
    ai = a_scr[1]

    def step(t, carry):
        hr, hi = carry
        rows = pl.ds(pl.multiple_of(t * batch, batch), batch)
        nhr = ar * hr - ai * hi + x_scr[0, rows, :]
        nhi = ar * hi + ai * hr + x_scr[1, rows, :]
        x_scr[0, rows, :] = nhr
        x_scr[1, rows, :] = nhi
        return nhr, nhi

    hr, hi = lax.fori_loop(0, steps, step, (h_scr[0], h_scr[1]), unroll=8)
    h_scr[0] = hr
    h_scr[1] = hi
    y = _dot(x_scr[0].astype(BF16), cre_ref[...]) - _dot(x_scr[1].astype(BF16), cim_ref[...]) + d_ref[...] * u
    z = _dot(y.astype(BF16), wglu_ref[...])
    out = z[:, :GROUP_W] * jax.nn.sigmoid(z[:, GROUP_W:])
    for j in range(io_tiles):
        io_scr[j] = out[:, j * LANES:(j + 1) * LANES]
    for b in range(batch):
        for j in range(io_tiles):
            o_ref[:, b * GROUP_W + j * LANES:b * GROUP_W + (j + 1) * LANES] = io_scr[j, pl.ds(b, steps, stride=batch), :]


def _s5(u, lre, lim, ldt, bre, bim, cre, cim, d, wglu, *, batch, seq, steps=256):
    state = lambda rows: pltpu.VMEM((2, rows, S5_WIDTH), F32)
    return pl.pallas_call(
        functools.partial(_s5_body, batch=batch, steps=steps),
        grid=(seq // steps,),
        in_specs=[pl.BlockSpec((steps, batch * S5_SLAB), lambda i: (i, 0)),
                  _const_spec((1, S5_WIDTH)), _const_spec((1, S5_WIDTH)), _const_spec((1, S5_WIDTH)),
                  _const_spec((GROUP_W, S5_WIDTH)), _const_spec((GROUP_W, S5_WIDTH)),
                  _const_spec((S5_WIDTH, GROUP_W)), _const_spec((S5_WIDTH, GROUP_W)),
                  _const_spec((1, GROUP_W)), _const_spec((GROUP_W, 2 * GROUP_W))],
        out_specs=pl.BlockSpec((steps, batch * GROUP_W), lambda i: (i, 0)),
        out_shape=jax.ShapeDtypeStruct((seq, batch * GROUP_W), F32),
        scratch_shapes=[state(batch), pltpu.VMEM((2, GROUP_W, S5_WIDTH), BF16), state(batch),
                        state(steps * batch), pltpu.VMEM((GROUP_W // LANES, steps * batch, LANES), F32)],
        compiler_params=_params("arbitrary"),
        name="s5",
    )(u, lre, lim, ldt, bre, bim, cre, cim, d, wglu)


DIL_TILE = 256
DIL_GROUP = (4, 2, 2)


def _pair_norm(x, gain2, same_head):
    ms = _dot_split(x * x, same_head, 2) * (1.0 / HEAD_DIM)
    return x * lax.rsqrt(ms + EPS) * gain2


def _dil_body(x_ref, gq_ref, gk_ref, band_ref, o_ref, bias_ref, q_scr, k_scr, v_scr, oa, ma, la, *, seq):
    span = DIL_SPAN
    n_blocks = seq // span
    n_tiles = GROUP_W // LANES
    lo_blk = lax.broadcasted_iota(jnp.int32, (span, LANES), 1) < HEAD_DIM
    same_head = ((lax.broadcasted_iota(jnp.int32, (LANES, LANES), 0) < HEAD_DIM)
                 == (lax.broadcasted_iota(jnp.int32, (LANES, LANES), 1) < HEAD_DIM)).astype(F32)

    @pl.when(pl.program_id(0) == 0)
    def _build_bias():
        for bh in range(len(DIL_PAIRS) * N_HEADS):
            profile = jnp.broadcast_to(band_ref[bh:bh + 1, :], (span, 2 * span))
            bias_ref[bh] = pltpu.roll(profile, 0, 1, stride=1, stride_axis=0)
    in_current = lax.broadcasted_iota(jnp.int32, (span, 2 * span), 1) >= span

    def norm_tile(i, carry):
        rows = pl.ds(pl.multiple_of(i * DIL_TILE, DIL_TILE), DIL_TILE)
        for j in range(n_tiles):
            cols = lambda part: slice(part * GROUP_W + j * LANES, part * GROUP_W + (j + 1) * LANES)
            q = _pair_norm(x_ref[rows, cols(0)], gq_ref[...], same_head)
            q_scr[j, rows, :] = q * (HEAD_DIM ** -0.5 * LOG2E)
            k_scr[j, rows, :] = _pair_norm(x_ref[rows, cols(1)], gk_ref[...], same_head)
            v_scr[j, rows, :] = x_ref[rows, cols(2)]
        return carry
    lax.fori_loop(0, seq // DIL_TILE, norm_tile, 0)

    for bi, (window, dil) in enumerate(DIL_PAIRS):
        sub_len = seq // dil
        nb = sub_len // span

        group = DIL_GROUP[bi]

        def block_group(i, carry, bi=bi, nb=nb, dil=dil, group=group):
            todo = []
            for u in range(group):
                t = i * group + u
                r = lax.shift_right_logical(t, int(math.log2(nb)))
                c = t & (nb - 1)
                first = r + c * (dil * span)
                stride = dil if dil > 1 else None
                rows = pl.ds(first, span, stride=stride)
                prev = pl.ds(jnp.maximum(first - dil * span, r), span, stride=stride)
                keep = jnp.logical_or(in_current, c != 0) if nb > 1 else None
                for j in range(n_tiles):
                    q2 = q_scr[j, rows, :]
                    if nb > 1:
                        k_cat = jnp.concatenate([k_scr[j, prev, :], k_scr[j, rows, :]], axis=0).astype(BF16)
                        v_cat = jnp.concatenate([v_scr[j, prev, :], v_scr[j, rows, :]], axis=0).astype(BF16)
                    else:
                        k_cat = k_scr[j, rows, :].astype(BF16)
                        v_cat = v_scr[j, rows, :].astype(BF16)
                    todo.append((rows, j, q2, k_cat, v_cat, keep))
            logits = []
            for rows, j, q2, k_cat, v_cat, keep in todo:
                for a in range(2):
                    qa = jnp.where(lo_blk if a == 0 else jnp.logical_not(lo_blk), q2, 0.0).astype(BF16)
                    bias = bias_ref[bi * N_HEADS + 2 * j + a]
                    lg = _dot_nt(qa, k_cat) + (bias if nb > 1 else bias[:, span:])
                    logits.append(jnp.where(keep, lg, NEG_INF) if nb > 1 else lg)
            m = [jnp.max(lg, axis=-1, keepdims=True) for lg in logits]
            p = [jnp.exp2(lg - mx) for lg, mx in zip(logits, m)]
            l = [jnp.sum(px, axis=-1, keepdims=True) for px in p]
            o = [_dot(p[2 * n + a].astype(BF16), todo[n][4]) for n in range(len(todo)) for a in range(2)]
            for n, (rows, j, *_) in enumerate(todo):
                o_in = jnp.where(lo_blk, o[2 * n], o[2 * n + 1])
                m_in = jnp.where(lo_blk, m[2 * n], m[2 * n + 1])
                l_in = jnp.where(lo_blk, l[2 * n], l[2 * n + 1])
                if bi > 0:
                    m_old = ma[j, rows, :]
                    m_new = jnp.maximum(m_old, m_in)
                    w_old, w_in = jnp.exp2(m_old - m_new), jnp.exp2(m_in - m_new)
                    o_in = w_old * oa[j, rows, :] + w_in * o_in
                    l_in = w_old * la[j, rows, :] + w_in * l_in
                    m_in = m_new
                oa[j, rows, :] = o_in
                ma[j, rows, :] = m_in
                la[j, rows, :] = l_in
            return carry
        lax.fori_loop(0, n_blocks // group, block_group, 0)

    def finish(i, carry):
        rows = pl.ds(pl.multiple_of(i * DIL_TILE, DIL_TILE), DIL_TILE)
        for j in range(n_tiles):
            o_ref[rows, j * LANES:(j + 1) * LANES] = oa[j, rows, :] / la[j, rows, :]
        return carry
    lax.fori_loop(0, seq // DIL_TILE, finish, 0)


def _dil(slab, gq2, gk2, band, *, batch, seq):
    big = lambda: pltpu.VMEM((GROUP_W // LANES, seq, LANES), F32)
    n_bias = len(DIL_PAIRS) * N_HEADS
    return pl.pallas_call(
        functools.partial(_dil_body, seq=seq),
        grid=(batch,),
        in_specs=[pl.BlockSpec((seq, DIL_SLAB), lambda b: (0, b)),
                  _const_spec((1, LANES)), _const_spec((1, LANES)),
                  _const_spec(band.shape)],
        out_specs=pl.BlockSpec((seq, GROUP_W), lambda b: (0, b)),
        out_shape=jax.ShapeDtypeStruct((seq, batch * GROUP_W), F32),
        scratch_shapes=[pltpu.VMEM((n_bias, DIL_SPAN, 2 * DIL_SPAN), F32)] + [big() for _ in range(6)],
        compiler_params=_params("arbitrary"),
        name="dilated",
    )(slab, gq2, gk2, band)


def _t5_bucket(dist):
    exact = T5_BUCKETS // 2
    df = jnp.maximum(dist, 1).astype(F32)
    large = exact + (jnp.log(df / exact) / math.log(T5_MAX_DIST / exact) * (T5_BUCKETS - exact)).astype(jnp.int32)
    large = jnp.minimum(large, T5_BUCKETS - 1)
    return jnp.where(dist < exact, dist, large)


def _dil_band(table):
    span = DIL_SPAN
    delta = span - jnp.arange(2 * span, dtype=jnp.int32)
    rows = []
    for _, dil in DIL_PAIRS:
        bucket = _t5_bucket(jnp.clip(delta, 0, span) * dil)
        onehot = (bucket[:, None] == jnp.arange(T5_BUCKETS, dtype=jnp.int32)[None, :]).astype(F32)
        vals = jnp.dot(onehot, table.astype(F32), precision=HIGHEST)
        rows.append(jnp.where((delta >= 0)[:, None], vals * LOG2E, NEG_INF).T)
    return jnp.concatenate(rows, axis=0)


DN_TILE = 256
DN_GROUP = 4
PREP_STORE_DELAY = 12


def _softplus(x):
    return jnp.maximum(x, 0.0) + jnp.log1p(jnp.exp(-jnp.abs(x)))


def _pair_l2(x, lo_half):
    sq = x * x
    tot = jnp.sum(sq, axis=-1, keepdims=True)
    lo = jnp.sum(jnp.where(lo_half, sq, 0.0), axis=-1, keepdims=True)
    return x * lax.rsqrt(jnp.where(lo_half, lo, tot - lo) + EPS)


def _dn_body(x_ref, cw_ref, alog_ref, dtb_ref, on_ref, o_ref, q_scr, k_scr, v_scr, g_scr, b_scr, w_scr, a_scr,
             s_scr, *, seq):
    c = DN_CHUNK
    w = GROUP_W
    lo_tile = lax.broadcasted_iota(jnp.int32, (DN_TILE, LANES), 1) < HEAD_DIM

    def prep_steps(i):
        if isinstance(i, int):
            first, halo_first = i * DN_TILE, max(i * DN_TILE - 8, 0)
        else:
            first = pl.multiple_of(i * DN_TILE, DN_TILE)
            halo_first = pl.multiple_of(jnp.maximum(first - 8, 0), 8)
        rows = pl.ds(first, DN_TILE)
        cur = x_ref[rows, 0:3 * w]
        halo = jnp.where(i > 0, x_ref[pl.ds(halo_first, 8), 0:3 * w], 0.0)
        ext = jnp.concatenate([halo, cur], axis=0)
        acc = cw_ref[DN_CONV - 1:DN_CONV, :] * cur
        for j in range(DN_CONV - 1):
            acc = acc + cw_ref[j:j + 1, :] * pltpu.roll(ext, DN_CONV - 1 - j, 0)[8:, :]
        y = acc * jax.nn.sigmoid(acc)
        yield
        tiles = range(w // LANES)
        cols = lambda part, j: slice(part * w + j * LANES, part * w + (j + 1) * LANES)
        q = [_pair_l2(y[:, cols(0, j)], lo_tile) * (HEAD_DIM ** -0.5) for j in tiles]
        k = [_pair_l2(y[:, cols(1, j)], lo_tile) for j in tiles]
        yield
        ab = x_ref[rows, 3 * w:3 * w + LANES]
        g = -jnp.exp(alog_ref[...]) * _softplus(ab + dtb_ref[...])
        gc = _dot_split(chunk_tril, g, 3, split_rhs=True)
        yield
        gc = per_head_lanes(gc, 0)
        beta = per_head_lanes(jax.nn.sigmoid(ab), N_HEADS)
        for _ in range(PREP_STORE_DELAY):
            yield
        for j in tiles:
            q_scr[rows, j * LANES:(j + 1) * LANES] = q[j]
            k_scr[rows, j * LANES:(j + 1) * LANES] = k[j]
            v_scr[rows, j * LANES:(j + 1) * LANES] = y[:, cols(2, j)]
        g_scr[rows, :] = gc
        b_scr[rows, :] = beta

    ri = lax.broadcasted_iota(jnp.int32, (c, w), 0)
    ci = lax.broadcasted_iota(jnp.int32, (c, w), 1)
    cj = jnp.bitwise_and(ci, HEAD_DIM - 1)
    causal = ri >= cj
    strict = ri > cj
    eye4 = (ri == cj).astype(F32)
    same_sub = lax.shift_right_logical(ri, 4) == lax.shift_right_logical(cj, 4)
    bi_r = lax.broadcasted_iota(jnp.int32, (w, w), 0)
    bi_c = lax.broadcasted_iota(jnp.int32, (w, w), 1)
    same_head = lax.shift_right_logical(bi_r, 6) == lax.shift_right_logical(bi_c, 6)
    block_mask = same_head.astype(F32)
    chunk_tril = (same_head & (jnp.bitwise_and(bi_r, c - 1) >= jnp.bitwise_and(bi_c, c - 1))).astype(F32)
    head_of_lane = lax.shift_right_logical(lax.broadcasted_iota(jnp.int32, (DN_TILE, w), 1), 6)

    def per_head_lanes(x, first_lane):
        out = None
        for h in range(N_HEADS):
            col = jnp.broadcast_to(x[:, first_lane + h:first_lane + h + 1], (DN_TILE, w))
            out = col if out is None else jnp.where(head_of_lane == h, col, out)
        return out

    def per_head(a, b):
        bd = jnp.where(same_head, jnp.concatenate([b.astype(BF16)] * N_HEADS, axis=0), 0.0)
        return _dot(a.astype(BF16), bd)

    def group_rows(i):
        base = pl.multiple_of(i * (DN_GROUP * c), DN_GROUP * c)
        return [pl.ds(base + u * c, c) for u in range(DN_GROUP)]

    def solve_steps(i):
        rows = group_rows(i)
        each = lambda f, *lists: [f(*args) for args in zip(*lists)]
        q, k, v, gc, beta = ([ref[r, :] for r in rows] for ref in (q_scr, k_scr, v_scr, g_scr, b_scr))
        g_row = each(lambda g: jnp.sum(g * eye4, axis=0, keepdims=True), gc)
        decay = each(lambda g, gr: jnp.exp(jnp.where(causal, g - gr, NEG_INF)), gc, g_row)
        kb = each(jnp.multiply, k, beta)
        k_bd = each(lambda x: jnp.where(same_head, jnp.concatenate([x.astype(BF16)] * N_HEADS, axis=0), 0.0), k)
        lmat = each(lambda a, b, d: jnp.where(strict, _dot_nt(a.astype(BF16), b) * d, 0.0), kb, k_bd, decay)
        yield
        a_qk = each(lambda a, b, d: jnp.where(causal, _dot_nt(a.astype(BF16), b) * d, 0.0), q, k_bd, decay)
        yield
        def pair_head(a0, a1, b):
            both = per_head(jnp.concatenate([a0, a1], axis=0), b)
            return both[:c], both[c:]

        p = each(lambda l_: jnp.where(same_sub, -l_, 0.0), lmat)
        t_diag = each(lambda x: eye4 + x, p)
        p = each(per_head, p, p)
        yield
        for _ in range(2):
            nxt = each(pair_head, p, t_diag, p)
            p = [x[0] for x in nxt]
            t_diag = each(lambda t, x: t + x[1], t_diag, nxt)
            yield
        t_diag = each(lambda t, x: t + per_head(t, x), t_diag, p)
        yield
        m1 = each(lambda l_, t: per_head(jnp.where(same_sub, 0.0, l_), t), lmat, t_diag)
        yield
        nxt = each(pair_head, t_diag, m1, m1)
        y = each(lambda t, x: t - x[0], t_diag, nxt)
        yield
        t_inv = each(lambda yy, x: yy + per_head(yy, x[1]), y, nxt)
        yield
        eg = each(jnp.exp, gc)
        w_c = each(lambda t, a, e: per_head(t, a * e), t_inv, kb, eg)
        yield
        u_c = each(lambda t, a, b: per_head(t, a * b), t_inv, v, beta)
        yield
        q_dec = each(jnp.multiply, q, eg)
        k_dec = each(lambda x, g: x * jnp.exp(g[c - 1:c, :] - g), k, gc)
        for ref, vals in zip((w_scr, v_scr, a_scr, q_scr, k_scr), (w_c, u_c, a_qk, q_dec, k_dec)):
            for r, val in zip(rows, vals):
                ref[r, :] = val

    def state_steps(i):
        for rows in group_rows(i):
            state = s_scr[...]
            state_b = state.astype(BF16)
            w_s = _dot(w_scr[rows, :].astype(BF16), state_b)
            q_s = _dot(q_scr[rows, :].astype(BF16), state_b)
            yield
            v_new = v_scr[rows, :] - w_s
            o_intra = per_head(a_scr[rows, :], v_new)
            upd = lax.dot_general(k_scr[rows, :].astype(BF16), v_new.astype(BF16), (((0,), (0,)), ((), ())),
                                  preferred_element_type=F32)
            yield
            o_ref[rows, :] = q_s + o_intra
            s_scr[...] = state * jnp.exp(g_scr[rows, :][c - 1:c, :]) + upd * block_mask
            yield

    def weave(*steps):
        live = list(steps)
        while live:
            for gen in list(live):
                if next(gen, "done") == "done":
                    live.remove(gen)

    n_groups = seq // (DN_GROUP * c)
    s_scr[...] = jnp.zeros_like(s_scr)
    weave(prep_steps(0))
    weave(solve_steps(0), prep_steps(1))

    def group(i, carry):
        weave(solve_steps(i), state_steps(i - 1), prep_steps(i + 1))
        return carry
    lax.fori_loop(1, n_groups - 1, group, 0)
    weave(solve_steps(n_groups - 1), state_steps(n_groups - 2))
    weave(state_steps(n_groups - 1))

    def finish_tile(i, carry):
        rows = pl.ds(pl.multiple_of(i * DN_TILE, DN_TILE), DN_TILE)
        o = o_ref[rows, :]
        ms = _dot_split(o * o, block_mask, 2) * (1.0 / HEAD_DIM)
        gate = x_ref[rows, 3 * w + LANES:4 * w + LANES]
        o_ref[rows, :] = o * lax.rsqrt(ms + EPS) * on_ref[...] * (gate * jax.nn.sigmoid(gate))
        return carry
    lax.fori_loop(0, seq // DN_TILE, finish_tile, 0)


def _dn(slab, conv_w, a_log, dt_bias, o_norm, *, batch, seq):
    wide = lambda: pltpu.VMEM((seq, GROUP_W), F32)
    return pl.pallas_call(
        functools.partial(_dn_body, seq=seq),
        grid=(batch,),
        in_specs=[pl.BlockSpec((seq, DN_SLAB), lambda b: (0, b)),
                  _const_spec((DN_CONV, 3 * GROUP_W)), _const_spec((1, LANES)), _const_spec((1, LANES)),
                  _const_spec((1, GROUP_W))],
        out_specs=pl.BlockSpec((seq, GROUP_W), lambda b: (0, b)),
        out_shape=jax.ShapeDtypeStruct((seq, batch * GROUP_W), F32),
        scratch_shapes=[wide() for _ in range(7)] + [pltpu.VMEM((GROUP_W, GROUP_W), F32)],
        compiler_params=_params("parallel"),
        name="deltanet",
    )(slab, conv_w, a_log, dt_bias, o_norm)


def _row(v, width=None):
    v = v.astype(F32).reshape(1, -1)
    if width is not None and v.shape[1] < width:
        v = jnp.pad(v, ((0, 0), (0, width - v.shape[1])))
    return v


def _prep_w_in(w):
    w = w.astype(BF16)
    z = lambda n: jnp.zeros(w.shape[:2] + (n,), BF16)
    return jnp.concatenate([w[..., 0:400], z(48), w[..., 400:416], z(48), w[..., 416:2216], z(120),
                            w[..., 2216:2472]], axis=-1)


def _mla_head_lanes(x):
    half = MLA_ROPE // 2
    nope, rope = x[..., :MLA_NOPE], x[..., MLA_NOPE:]
    split = HEAD_DIM - half
    zeros = jnp.zeros(x.shape[:-1] + (LANES - MLA_DQK,), x.dtype)
    return jnp.concatenate([rope[..., :half], nope[..., :split], rope[..., half:], nope[..., split:], zeros], axis=-1)


def _pad_heads(w, per_head, width):
    k = w.shape[0]
    w = w.reshape(k, N_HEADS, per_head)[:, :, :width]
    w = jnp.pad(w, ((0, 0), (0, 0), (0, MLA_DQK - width)))
    return _mla_head_lanes(w).reshape(k, N_HEADS * LANES).astype(BF16)


def _rope_tables(seq):
    half = MLA_ROPE // 2
    pos = jnp.arange(seq, dtype=F32)
    freqs = ROPE_THETA ** (-jnp.arange(half, dtype=F32) / half)
    ang = pos[:, None] * freqs[None, :]
    cos, sin = jnp.cos(ang), jnp.sin(ang)
    ones = jnp.ones((seq, MLA_NOPE), F32)
    zeros = jnp.zeros((seq, MLA_NOPE), F32)
    return (_mla_head_lanes(jnp.concatenate([ones, cos, cos], axis=1)),
            _mla_head_lanes(jnp.concatenate([zeros, -sin, sin], axis=1)))


def _block_diag(blocks):
    g, r, c = blocks.shape
    eye = jnp.eye(g, dtype=blocks.dtype)
    return (blocks[:, :, None, :] * eye[:, None, :, None]).reshape(g * r, g * c)


def _mla_layer(slab, p, l, *, batch, seq):
    w_ukv = p["mla_w_ukv"][l]
    cos, sin = _rope_tables(seq)
    return _mla(slab, _row(p["mla_q_norm"][l]), _row(p["mla_kv_norm"][l]),
                _pad_heads(p["mla_w_uq"][l], MLA_DQK, MLA_DQK),
                _pad_heads(w_ukv, MLA_NOPE + HEAD_DIM, MLA_NOPE),
                w_ukv.reshape(MLA_KV_RANK, N_HEADS, MLA_NOPE + HEAD_DIM)[:, :, MLA_NOPE:]
                .reshape(MLA_KV_RANK, GROUP_W).T.astype(BF16),
                _mla_head_lanes(_row(p["mla_qk_q"][l])), _mla_head_lanes(_row(p["mla_qk_k"][l])), cos, sin,
                batch=batch, seq=seq)


def _dil_layer(slab, p, l, *, batch, seq):
    pair = lambda g: jnp.tile(g.astype(F32).reshape(1, HEAD_DIM), (1, LANES // HEAD_DIM))
    return _dil(slab, pair(p["dil_q_norm"][l]), pair(p["dil_k_norm"][l]), _dil_band(p["t5_bias"]),
                batch=batch, seq=seq)


def _dn_layer(slab, p, l, *, batch, seq):
    return _dn(slab, p["dn_conv"][l].astype(F32), _row(p["dn_a_log"][l], LANES), _row(p["dn_dt_bias"][l], LANES),
               jnp.tile(p["dn_o_norm"][l].astype(F32).reshape(1, HEAD_DIM), (1, N_HEADS)), batch=batch, seq=seq)


def _s5_layer(u, p, l, *, batch, seq):
    state_row = lambda v: v.astype(F32).reshape(1, S5_WIDTH)
    ldt = jnp.broadcast_to(p["s5_log_dt"][l][:, None], (S5_GROUPS, S5_STATE))
    bre = _block_diag(jnp.swapaxes(p["s5_b_re"][l], 1, 2).astype(F32))
    bim = _block_diag(jnp.swapaxes(p["s5_b_im"][l], 1, 2).astype(F32))
    cre = _block_diag(jnp.swapaxes(p["s5_c_re"][l], 1, 2)).astype(BF16)
    cim = _block_diag(jnp.swapaxes(p["s5_c_im"][l], 1, 2)).astype(BF16)
    return _s5(u, state_row(p["s5_lambda_re"][l]), state_row(p["s5_lambda_im"][l]), state_row(ldt),
               bre, bim, cre, cim, _row(p["s5_d"][l]), p["s5_w_glu"][l].astype(BF16), batch=batch, seq=seq,
               steps=min(256, seq))


def kernel(x, attn_norm, w_in, w_out, mla_q_norm, mla_kv_norm, mla_w_uq, mla_w_ukv, mla_qk_q, mla_qk_k,
           s5_lambda_re, s5_lambda_im, s5_log_dt, s5_b_re, s5_b_im, s5_c_re, s5_c_im, s5_d, s5_w_glu,
           dil_q_norm, dil_k_norm, t5_bias, dn_conv, dn_a_log, dn_dt_bias, dn_o_norm,
           ffn_norm, ffn_w1, ffn_w3, ffn_w2):
    p = dict(mla_q_norm=mla_q_norm, mla_kv_norm=mla_kv_norm, mla_w_uq=mla_w_uq, mla_w_ukv=mla_w_ukv,
             mla_qk_q=mla_qk_q, mla_qk_k=mla_qk_k, s5_lambda_re=s5_lambda_re, s5_lambda_im=s5_lambda_im,
             s5_log_dt=s5_log_dt, s5_b_re=s5_b_re, s5_b_im=s5_b_im, s5_c_re=s5_c_re, s5_c_im=s5_c_im, s5_d=s5_d,
             s5_w_glu=s5_w_glu, dil_q_norm=dil_q_norm, dil_k_norm=dil_k_norm, t5_bias=t5_bias, dn_conv=dn_conv,
             dn_a_log=dn_a_log, dn_dt_bias=dn_dt_bias, dn_o_norm=dn_o_norm)
    batch, seq, _ = x.shape
    h = x.reshape(batch * seq, D_MODEL)
    w_in_b, w_out_b = _prep_w_in(w_in), w_out.astype(BF16)
    w1_b, w3_b, w2_b = ffn_w1.astype(BF16), ffn_w3.astype(BF16), ffn_w2.astype(BF16)
    for l in range(attn_norm.shape[0]):
        mla_in, s5_in, dil_in, dn_in = _proj(h, _row(attn_norm[l]), w_in_b, l, batch=batch, seq=seq)
        ys = [_mla_layer(mla_in, p, l, batch=batch, seq=seq),
              _s5_layer(s5_in, p, l, batch=batch, seq=seq),
              _dil_layer(dil_in, p, l, batch=batch, seq=seq),
              _dn_layer(dn_in, p, l, batch=batch, seq=seq)]
        h = _out_ffn(h, ys, w_out_b, _row(ffn_norm[l]), w1_b, w3_b, w2_b, l, batch=batch, seq=seq)
    return h.reshape(batch, seq, D_MODEL)
```

```python
import functools
import math

import jax
import jax.numpy as jnp
from jax import lax
from jax.experimental import pallas as pl
from jax.experimental.pallas import tpu as pltpu

F32 = jnp.float32
BF16 = jnp.bfloat16
HIGHEST = lax.Precision.HIGHEST

D_MODEL = 1024
GROUP_W = 256
HEAD_DIM = 64
N_HEADS = 4
EPS = 1e-6
NEG_INF = -1e30
LOG2E = math.log2(math.e)

MLA_NOPE = 64
MLA_ROPE = 32
MLA_DQK = MLA_NOPE + MLA_ROPE
MLA_KV_RANK = 128
ROPE_THETA = 10000.0

S5_GROUPS = 16
S5_STATE = 64
S5_WIDTH = S5_GROUPS * S5_STATE

DIL_PAIRS = ((128, 1), (512, 4), (2048, 16))
DIL_SPAN = 128
T5_BUCKETS = 32
T5_MAX_DIST = 2048

DN_CONV = 4
DN_CHUNK = 64

FFN_HIDDEN = 2816
FFN_CHUNK = 2816

VMEM_LIMIT_BYTES = 56 * 1024 * 1024
LANES = 128

MLA_SLAB = 512
S5_SLAB = 256
DIL_SLAB = 768
DN_SLAB = 1152
PROJ_COLS = MLA_SLAB + S5_SLAB + DIL_SLAB + DN_SLAB


def _dot(a, b, precision=None):
    return jnp.dot(a, b, preferred_element_type=F32, precision=precision)


def _dot_nt(a, b, precision=None):
    return lax.dot_general(a, b, (((1,), (1,)), ((), ())), preferred_element_type=F32, precision=precision)


def _dot_split(a, b, terms, split_rhs=False):
    x = b if split_rhs else a
    mask = (a if split_rhs else b).astype(BF16)
    out = None
    for _ in range(terms):
        piece = x.astype(BF16)
        part = _dot(mask, piece) if split_rhs else _dot(piece, mask)
        out = part if out is None else out + part
        x = x - piece.astype(F32)
    return out


def _const_spec(shape):
    nd = len(shape)
    return pl.BlockSpec(shape, lambda *_: (0,) * nd, pipeline_mode=pl.Buffered(1))


def _params(*sem):
    return pltpu.CompilerParams(dimension_semantics=sem, vmem_limit_bytes=VMEM_LIMIT_BYTES)


def _proj_body(x_ref, g_ref, w_ref, mla_ref, s5_ref, dil_ref, dn_ref):
    x = x_ref[...]
    n = x * lax.rsqrt(jnp.mean(x * x, axis=-1, keepdims=True) + EPS) * g_ref[...]
    nb = n.astype(BF16)
    start = 0
    for ref in (mla_ref, s5_ref, dil_ref, dn_ref):
        width = ref.shape[-1]
        ref[...] = _dot(nb, w_ref[:, start:start + width])
        start += width


def _layer_spec(shape, layer):
    nd = len(shape)
    return pl.BlockSpec((None,) + tuple(shape), lambda *_: (layer,) + (0,) * nd, pipeline_mode=pl.Buffered(1))


def _proj(h, gain, w_big, layer, *, batch, seq, tm=1024):
    nt = seq // tm
    widths = (MLA_SLAB, S5_SLAB, DIL_SLAB, DN_SLAB)
    return pl.pallas_call(
        _proj_body,
        grid=(batch, nt),
        in_specs=[pl.BlockSpec((tm, D_MODEL), lambda b, i: (b * nt + i, 0)),
                  _const_spec((1, D_MODEL)),
                  _layer_spec((D_MODEL, PROJ_COLS), layer)],
        out_specs=[pl.BlockSpec((tm, w), lambda b, i: (i, b)) for w in widths],
        out_shape=[jax.ShapeDtypeStruct((seq, batch * w), F32) for w in widths],
        compiler_params=_params("parallel", "parallel"),
        name="proj",
    )(h, gain, w_big)


def _out_ffn_body(h_ref, y0_ref, y1_ref, y2_ref, y3_ref, wo_ref, g_ref, w1_ref, w3_ref, w2_ref, o_ref, acc_ref):
    mixed = jnp.concatenate([y_ref[...].astype(BF16) for y_ref in (y0_ref, y1_ref, y2_ref, y3_ref)], axis=1)
    h = h_ref[...] + _dot(mixed, wo_ref[...])
    n = h * lax.rsqrt(jnp.mean(h * h, axis=-1, keepdims=True) + EPS) * g_ref[...]
    nb = n.astype(BF16)
    acc_ref[...] = h

    def hidden_chunk(c, carry):
        cols = pl.ds(pl.multiple_of(c * FFN_CHUNK, FFN_CHUNK), FFN_CHUNK)
        a = _dot(nb, w1_ref[:, cols])
        b = _dot(nb, w3_ref[:, cols])
        z = (a * jax.nn.sigmoid(a) * b).astype(BF16)
        acc_ref[...] += _dot(z, w2_ref[cols, :])
        return carry
    lax.fori_loop(0, FFN_HIDDEN // FFN_CHUNK, hidden_chunk, 0)
    o_ref[...] = acc_ref[...]


def _out_ffn(h, ys, w_out, gain, w1, w3, w2, layer, *, batch, seq, tm=512):
    nt = seq // tm
    row = pl.BlockSpec((tm, D_MODEL), lambda b, i: (b * nt + i, 0))
    slab = pl.BlockSpec((tm, GROUP_W), lambda b, i: (i, b))
    return pl.pallas_call(
        _out_ffn_body,
        grid=(batch, nt),
        in_specs=[row] + [slab] * 4 + [
            _layer_spec((D_MODEL, D_MODEL), layer), _const_spec((1, D_MODEL)),
            _layer_spec((D_MODEL, FFN_HIDDEN), layer), _layer_spec((D_MODEL, FFN_HIDDEN), layer),
            _layer_spec((FFN_HIDDEN, D_MODEL), layer)],
        out_specs=row,
        out_shape=jax.ShapeDtypeStruct((batch * seq, D_MODEL), F32),
        scratch_shapes=[pltpu.VMEM((tm, D_MODEL), F32)],
        compiler_params=_params("parallel", "parallel"),
        name="out_ffn",
    )(h, *ys, w_out, gain, w1, w3, w2)


MLA_BLOCK = 256
MLA_PREP_PARTS = 1
MLA_SWEEP = (4, 2, 1)


def _mla_body(x_ref, qn_ref, kvn_ref, wuq_ref, wuk_ref, wuv_ref, gq_ref, gk_ref, cos_ref, sin_ref,
              o_ref, k_scr, vt_scr, q_scr, *, seq):
    blk = MLA_BLOCK
    n_pairs = N_HEADS // 2
    part_rows = blk // MLA_PREP_PARTS

    def rope(x, c, s):
        return x * c + pltpu.roll(x, LANES // 2, 1) * s

    all_lanes = jnp.ones((LANES, LANES), F32)

    def norm_head(x, g):
        ssq = _dot_split(x * x, all_lanes, 2)
        return x * lax.rsqrt(ssq * (1.0 / MLA_DQK) + EPS) * g

    heads = range(N_HEADS)

    def prepare_steps(i, part):
        first = i * blk + part * part_rows
        rows = pl.ds(first if isinstance(i, int) else pl.multiple_of(first, part_rows), part_rows)
        part_of_block = pl.ds(part * part_rows, part_rows)
        ckv = x_ref[rows, 256:384]
        cq = x_ref[rows, 0:256]
        kvn = ckv * lax.rsqrt(jnp.mean(ckv * ckv, axis=-1, keepdims=True) + EPS) * kvn_ref[...]
        qn = cq * lax.rsqrt(jnp.mean(cq * cq, axis=-1, keepdims=True) + EPS) * qn_ref[...]
        kvn = kvn.astype(BF16)
        k_nope = _dot(kvn, wuk_ref[...])
        v_t = _dot_nt(wuv_ref[...], kvn)
        q_all = _dot(qn.astype(BF16), wuq_ref[...])
        yield
        k_rope = x_ref[rows, 384:512]
        c = cos_ref[rows, :]
        s = sin_ref[rows, :]
        ks = [rope(norm_head(k_nope[:, h * LANES:(h + 1) * LANES] + k_rope, gk_ref[...]), c, s) for h in heads]
        yield
        qs_new = [rope(norm_head(q_all[:, h * LANES:(h + 1) * LANES], gq_ref[...]), c, s) * (MLA_DQK ** -0.5 * LOG2E)
                  for h in heads]
        vts = [v_t[pr * LANES:(pr + 1) * LANES, :] for pr in range(n_pairs)]
        yield
        yield
        for h in heads:
            k_scr[h, rows, :] = ks[h].astype(BF16)
            q_scr[h, part_of_block, :] = qs_new[h].astype(BF16)
        for pr in range(n_pairs):
            vt_scr[pr, :, rows] = vts[pr].astype(BF16)

    def weave(*steps):
        live = list(steps)
        while live:
            for gen in list(live):
                if next(gen, "done") == "done":
                    live.remove(gen)

    key_pos = lax.broadcasted_iota(jnp.int32, (blk, blk), 0)
    query_pos = lax.broadcasted_iota(jnp.int32, (blk, blk), 1)
    first_of_pair = lax.broadcasted_iota(jnp.int32, (LANES, blk), 0) < HEAD_DIM

    def query_block(qi, _):
        rows = pl.ds(pl.multiple_of(qi * blk, blk), blk)
        qs = [q_scr[h] for h in heads]

        def kv_steps(first_key, width, carry, result, diagonal=False):
            m_old, l_old, acc_old = carry
            keys = pl.ds(pl.multiple_of(first_key, blk), width)
            logits = [_dot_nt(k_scr[h, keys, :], qs[h]) for h in heads]
            yield
            if diagonal:
                logits = [jnp.where(key_pos <= query_pos, x, NEG_INF) for x in logits]
            m_new = [jnp.maximum(m_old[h], jnp.max(logits[h], axis=0, keepdims=True)) for h in heads]
            alpha = [jnp.exp2(m_old[h] - m_new[h]) for h in heads]
            p = [jnp.exp2(logits[h] - m_new[h]) for h in heads]
            l_new = [alpha[h] * l_old[h] + jnp.sum(p[h], axis=0, keepdims=True) for h in heads]
            pv = [_dot(vt_scr[h // 2, :, keys], p[h].astype(BF16)) for h in heads]
            yield
            acc_new = [jnp.where(first_of_pair, alpha[2 * pr] * acc_old[pr] + pv[2 * pr],
                                 alpha[2 * pr + 1] * acc_old[pr] + pv[2 * pr + 1]) for pr in range(n_pairs)]
            result.append((tuple(m_new), tuple(l_new), tuple(acc_new)))

        def kv_step(first_key, width, carry):
            result = []
            weave(kv_steps(first_key, width, carry, result))
            return result[0]

        def last_steps(carry):
            result = []
            yield from kv_steps(qi * blk, blk, carry, result, diagonal=True)
            _, l, acc = result[0]
            for pr in range(n_pairs):
                out_t = acc[pr] / jnp.where(first_of_pair, l[2 * pr], l[2 * pr + 1])
                o_ref[rows, pr * LANES:(pr + 1) * LANES] = out_t.T

        carry = (tuple(jnp.full((1, blk), NEG_INF, F32) for _ in heads),
                 tuple(jnp.zeros((1, blk), F32) for _ in heads),
                 tuple(jnp.zeros((LANES, blk), F32) for _ in range(n_pairs)))
        done = 0
        for n_blk in MLA_SWEEP:
            todo = (qi - done) // n_blk
            carry = lax.fori_loop(0, todo, lambda j, cr, first=done, n_blk=n_blk:
                                  kv_step((first + j * n_blk) * blk, n_blk * blk, cr), carry)
            done = done + todo * n_blk

        @pl.when(qi + 1 < seq // blk)
        def _last_and_prepare_next():
            weave(last_steps(carry), *(prepare_steps(qi + 1, part) for part in range(MLA_PREP_PARTS)))

        @pl.when(qi + 1 == seq // blk)
        def _last():
            weave(last_steps(carry))
        return 0

    weave(*(prepare_steps(0, part) for part in range(MLA_PREP_PARTS)))
    lax.fori_loop(0, seq // blk, query_block, 0)


def _mla(slab, qn, kvn, wuq, wuk, wuv, gq, gk, cos, sin, *, batch, seq):
    nq = seq // MLA_BLOCK
    return pl.pallas_call(
        functools.partial(_mla_body, seq=seq),
        grid=(batch,),
        in_specs=[pl.BlockSpec((seq, MLA_SLAB), lambda b: (0, b)),
                  _const_spec((1, 256)), _const_spec((1, MLA_KV_RANK)),
                  _const_spec((256, N_HEADS * LANES)), _const_spec((MLA_KV_RANK, N_HEADS * LANES)),
                  _const_spec((GROUP_W, MLA_KV_RANK)),
                  _const_spec((1, LANES)), _const_spec((1, LANES)),
                  _const_spec((seq, LANES)), _const_spec((seq, LANES))],
        out_specs=pl.BlockSpec((seq, GROUP_W), lambda b: (0, b)),
        out_shape=jax.ShapeDtypeStruct((seq, batch * GROUP_W), F32),
        scratch_shapes=[pltpu.VMEM((N_HEADS, seq, LANES), BF16), pltpu.VMEM((N_HEADS // 2, LANES, seq), BF16),
                        pltpu.VMEM((N_HEADS, MLA_BLOCK, LANES), BF16)],
        compiler_params=_params("parallel"),
        name="mla",
    )(slab, qn, kvn, wuq, wuk, wuv, gq, gk, cos, sin)


def _s5_body(u_ref, lre_ref, lim_ref, ldt_ref, bre_ref, bim_ref, cre_ref, cim_ref, d_ref, wglu_ref, o_ref,
             a_scr, bbar_scr, h_scr, x_scr, io_scr, *, batch, steps):
    io_tiles = GROUP_W // LANES

    @pl.when(pl.program_id(0) == 0)
    def _discretise():
        lr = lre_ref[...]
        li = lim_ref[...]
        dt = jnp.exp(ldt_ref[...])
        mag = jnp.exp(lr * dt)
        ar = mag * jnp.cos(li * dt)
        ai = mag * jnp.sin(li * dt)
        den = lr * lr + li * li
        nr = ar - 1.0
        zr = (nr * lr + ai * li) / den
        zi = (ai * lr - nr * li) / den
        a_scr[0] = jnp.broadcast_to(ar, (batch, S5_WIDTH))
        a_scr[1] = jnp.broadcast_to(ai, (batch, S5_WIDTH))
        bre = bre_ref[...]
        bim = bim_ref[...]
        bbar_scr[0] = (zr * bre - zi * bim).astype(BF16)
        bbar_scr[1] = (zr * bim + zi * bre).astype(BF16)
        h_scr[...] = jnp.zeros_like(h_scr)

    for b in range(batch):
        for j in range(io_tiles):
            io_scr[j, pl.ds(b, steps, stride=batch), :] = u_ref[:, b * GROUP_W + j * LANES:b * GROUP_W + (j + 1) * LANES]
    u = jnp.concatenate([io_scr[j] for j in range(io_tiles)], axis=1)
    ub = u.astype(BF16)
    x_scr[0] = _dot(ub, bbar_scr[0])
    x_scr[1] = _dot(ub, bbar_scr[1])
    ar = a_scr[0]
    ai = a_scr[1]

    def step(t, carry):
        hr, hi = carry
        rows = pl.ds(pl.multiple_of(t * batch, batch), batch)
        nhr = ar * hr - ai * hi + x_scr[0, rows, :]
        nhi = ar * hi + ai * hr + x_scr[1, rows, :]
        x_scr[0, rows, :] = nhr
        x_scr[1, rows, :] = nhi
        return nhr, nhi

    hr, hi = lax.fori_loop(0, steps, step, (h_scr[0], h_scr[1]), unroll=8)
    h_scr[0] = hr
    h_scr[1] = hi
    y = _dot(x_scr[0].astype(BF16), cre_ref[...]) - _dot(x_scr[1].astype(BF16), cim_ref[...]) + d_ref[...] * u
    z = _dot(y.astype(BF16), wglu_ref[...])
    out = z[:, :GROUP_W] * jax.nn.sigmoid(z[:, GROUP_W:])
    for j in range(io_tiles):
        io_scr[j] = out[:, j * LANES:(j + 1) * LANES]
    for b in range(batch):
        for j in range(io_tiles):
            o_ref[:, b * GROUP_W + j * LANES:b * GROUP_W + (j + 1) * LANES] = io_scr[j, pl.ds(b, steps, stride=batch), :]


def _s5(u, lre, lim, ldt, bre, bim, cre, cim, d, wglu, *, batch, seq, steps=256):
    state = lambda rows: pltpu.VMEM((2, rows, S5_WIDTH), F32)
    return pl.pallas_call(
        functools.partial(_s5_body, batch=batch, steps=steps),
        grid=(seq // steps,),
        in_specs=[pl.BlockSpec((steps, batch * S5_SLAB), lambda i: (i, 0)),
                  _const_spec((1, S5_WIDTH)), _const_spec((1, S5_WIDTH)), _const_spec((1, S5_WIDTH)),
                  _const_spec((GROUP_W, S5_WIDTH)), _const_spec((GROUP_W, S5_WIDTH)),
                  _const_spec((S5_WIDTH, GROUP_W)), _const_spec((S5_WIDTH, GROUP_W)),
                  _const_spec((1, GROUP_W)), _const_spec((GROUP_W, 2 * GROUP_W))],
        out_specs=pl.BlockSpec((steps, batch * GROUP_W), lambda i: (i, 0)),
        out_shape=jax.ShapeDtypeStruct((seq, batch * GROUP_W), F32),
        scratch_shapes=[state(batch), pltpu.VMEM((2, GROUP_W, S5_WIDTH), BF16), state(batch),
                        state(steps * batch), pltpu.VMEM((GROUP_W // LANES, steps * batch, LANES), F32)],
        compiler_params=_params("arbitrary"),
        name="s5",
    )(u, lre, lim, ldt, bre, bim, cre, cim, d, wglu)


DIL_TILE = 256
DIL_GROUP = (4, 2, 2)


def _pair_norm(x, gain2, same_head):
    ms = _dot_split(x * x, same_head, 2) * (1.0 / HEAD_DIM)
    return x * lax.rsqrt(ms + EPS) * gain2


def _dil_body(x_ref, gq_ref, gk_ref, band_ref, o_ref, bias_ref, q_scr, k_scr, v_scr, oa, ma, la, *, seq):
    span = DIL_SPAN
    n_blocks = seq // span
    n_tiles = GROUP_W // LANES
    lo_blk = lax.broadcasted_iota(jnp.int32, (span, LANES), 1) < HEAD_DIM
    same_head = ((lax.broadcasted_iota(jnp.int32, (LANES, LANES), 0) < HEAD_DIM)
                 == (lax.broadcasted_iota(jnp.int32, (LANES, LANES), 1) < HEAD_DIM)).astype(F32)

    @pl.when(pl.program_id(0) == 0)
    def _build_bias():
        for bh in range(len(DIL_PAIRS) * N_HEADS):
            profile = jnp.broadcast_to(band_ref[bh:bh + 1, :], (span, 2 * span))
            bias_ref[bh] = pltpu.roll(profile, 0, 1, stride=1, stride_axis=0)
    in_current = lax.broadcasted_iota(jnp.int32, (span, 2 * span), 1) >= span

    def norm_tile(i, carry):
        rows = pl.ds(pl.multiple_of(i * DIL_TILE, DIL_TILE), DIL_TILE)
        for j in range(n_tiles):
            cols = lambda part: slice(part * GROUP_W + j * LANES, part * GROUP_W + (j + 1) * LANES)
            q = _pair_norm(x_ref[rows, cols(0)], gq_ref[...], same_head)
            q_scr[j, rows, :] = q * (HEAD_DIM ** -0.5 * LOG2E)
            k_scr[j, rows, :] = _pair_norm(x_ref[rows, cols(1)], gk_ref[...], same_head)
            v_scr[j, rows, :] = x_ref[rows, cols(2)]
        return carry
    lax.fori_loop(0, seq // DIL_TILE, norm_tile, 0)

    for bi, (window, dil) in enumerate(DIL_PAIRS):
        sub_len = seq // dil
        nb = sub_len // span

        group = DIL_GROUP[bi]

        def block_group(i, carry, bi=bi, nb=nb, dil=dil, group=group):
            todo = []
            for u in range(group):
                t = i * group + u
                r = lax.shift_right_logical(t, int(math.log2(nb)))
                c = t & (nb - 1)
                first = r + c * (dil * span)
                stride = dil if dil > 1 else None
                rows = pl.ds(first, span, stride=stride)
                prev = pl.ds(jnp.maximum(first - dil * span, r), span, stride=stride)
                keep = jnp.logical_or(in_current, c != 0) if nb > 1 else None
                for j in range(n_tiles):
                    q2 = q_scr[j, rows, :]
                    if nb > 1:
                        k_cat = jnp.concatenate([k_scr[j, prev, :], k_scr[j, rows, :]], axis=0).astype(BF16)
                        v_cat = jnp.concatenate([v_scr[j, prev, :], v_scr[j, rows, :]], axis=0).astype(BF16)
                    else:
                        k_cat = k_scr[j, rows, :].astype(BF16)
                        v_cat = v_scr[j, rows, :].astype(BF16)
                    todo.append((rows, j, q2, k_cat, v_cat, keep))
            logits = []
            for rows, j, q2, k_cat, v_cat, keep in todo:
                for a in range(2):
                    qa = jnp.where(lo_blk if a == 0 else jnp.logical_not(lo_blk), q2, 0.0).astype(BF16)
                    bias = bias_ref[bi * N_HEADS + 2 * j + a]
                    lg = _dot_nt(qa, k_cat) + (bias if nb > 1 else bias[:, span:])
                    logits.append(jnp.where(keep, lg, NEG_INF) if nb > 1 else lg)
            m = [jnp.max(lg, axis=-1, keepdims=True) for lg in logits]
            p = [jnp.exp2(lg - mx) for lg, mx in zip(logits, m)]
            l = [jnp.sum(px, axis=-1, keepdims=True) for px in p]
            o = [_dot(p[2 * n + a].astype(BF16), todo[n][4]) for n in range(len(todo)) for a in range(2)]
            for n, (rows, j, *_) in enumerate(todo):
                o_in = jnp.where(lo_blk, o[2 * n], o[2 * n + 1])
                m_in = jnp.where(lo_blk, m[2 * n], m[2 * n + 1])
                l_in = jnp.where(lo_blk, l[2 * n], l[2 * n + 1])
                if bi > 0:
                    m_old = ma[j, rows, :]
                    m_new = jnp.maximum(m_old, m_in)
                    w_old, w_in = jnp.exp2(m_old - m_new), jnp.exp2(m_in - m_new)
                    o_in = w_old * oa[j, rows, :] + w_in * o_in
                    l_in = w_old * la[j, rows, :] + w_in * l_in
                    m_in = m_new
                oa[j, rows, :] = o_in
                ma[j, rows, :] = m_in
                la[j, rows, :] = l_in
            return carry
        lax.fori_loop(0, n_blocks // group, block_group, 0)

    def finish(i, carry):
        rows = pl.ds(pl.multiple_of(i * DIL_TILE, DIL_TILE), DIL_TILE)
        for j in range(n_tiles):
            o_ref[rows, j * LANES:(j + 1) * LANES] = oa[j, rows, :] / la[j, rows, :]
        return carry
    lax.fori_loop(0, seq // DIL_TILE, finish, 0)


def _dil(slab, gq2, gk2, band, *, batch, seq):
    big = lambda: pltpu.VMEM((GROUP_W // LANES, seq, LANES), F32)
    n_bias = len(DIL_PAIRS) * N_HEADS
    return pl.pallas_call(
        functools.partial(_dil_body, seq=seq),
        grid=(batch,),
        in_specs=[pl.BlockSpec((seq, DIL_SLAB), lambda b: (0, b)),
                  _const_spec((1, LANES)), _const_spec((1, LANES)),
                  _const_spec(band.shape)],
        out_specs=pl.BlockSpec((seq, GROUP_W), lambda b: (0, b)),
        out_shape=jax.ShapeDtypeStruct((seq, batch * GROUP_W), F32),
        scratch_shapes=[pltpu.VMEM((n_bias, DIL_SPAN, 2 * DIL_SPAN), F32)] + [big() for _ in range(6)],
        compiler_params=_params("arbitrary"),
        name="dilated",
    )(slab, gq2, gk2, band)


def _t5_bucket(dist):
    exact = T5_BUCKETS // 2
    df = jnp.maximum(dist, 1).astype(F32)
    large = exact + (jnp.log(df / exact) / math.log(T5_MAX_DIST / exact) * (T5_BUCKETS - exact)).astype(jnp.int32)
    large = jnp.minimum(large, T5_BUCKETS - 1)
    return jnp.where(dist < exact, dist, large)


def _dil_band(table):
    span = DIL_SPAN
    delta = span - jnp.arange(2 * span, dtype=jnp.int32)
    rows = []
    for _, dil in DIL_PAIRS:
        bucket = _t5_bucket(jnp.clip(delta, 0, span) * dil)
        onehot = (bucket[:, None] == jnp.arange(T5_BUCKETS, dtype=jnp.int32)[None, :]).astype(F32)
        vals = jnp.dot(onehot, table.astype(F32), precision=HIGHEST)
        rows.append(jnp.where((delta >= 0)[:, None], vals * LOG2E, NEG_INF).T)
    return jnp.concatenate(rows, axis=0)


DN_TILE = 256
DN_GROUP = 4
PREP_STORE_DELAY = 12


def _softplus(x):
    return jnp.maximum(x, 0.0) + jnp.log1p(jnp.exp(-jnp.abs(x)))


def _pair_l2(x, lo_half):
    sq = x * x
    tot = jnp.sum(sq, axis=-1, keepdims=True)
    lo = jnp.sum(jnp.where(lo_half, sq, 0.0), axis=-1, keepdims=True)
    return x * lax.rsqrt(jnp.where(lo_half, lo, tot - lo) + EPS)


def _dn_body(x_ref, cw_ref, alog_ref, dtb_ref, on_ref, o_ref, q_scr, k_scr, v_scr, g_scr, b_scr, w_scr, a_scr,
             s_scr, *, seq):
    c = DN_CHUNK
    w = GROUP_W
    lo_tile = lax.broadcasted_iota(jnp.int32, (DN_TILE, LANES), 1) < HEAD_DIM

    def prep_steps(i):
        if isinstance(i, int):
            first, halo_first = i * DN_TILE, max(i * DN_TILE - 8, 0)
        else:
            first = pl.multiple_of(i * DN_TILE, DN_TILE)
            halo_first = pl.multiple_of(jnp.maximum(first - 8, 0), 8)
        rows = pl.ds(first, DN_TILE)
        cur = x_ref[rows, 0:3 * w]
        halo = jnp.where(i > 0, x_ref[pl.ds(halo_first, 8), 0:3 * w], 0.0)
        ext = jnp.concatenate([halo, cur], axis=0)
        acc = cw_ref[DN_CONV - 1:DN_CONV, :] * cur
        for j in range(DN_CONV - 1):
            acc = acc + cw_ref[j:j + 1, :] * pltpu.roll(ext, DN_CONV - 1 - j, 0)[8:, :]
        y = acc * jax.nn.sigmoid(acc)
        yield
        tiles = range(w // LANES)
        cols = lambda part, j: slice(part * w + j * LANES, part * w + (j + 1) * LANES)
        q = [_pair_l2(y[:, cols(0, j)], lo_tile) * (HEAD_DIM ** -0.5) for j in tiles]
        k = [_pair_l2(y[:, cols(1, j)], lo_tile) for j in tiles]
        yield
        ab = x_ref[rows, 3 * w:3 * w + LANES]
        g = -jnp.exp(alog_ref[...]) * _softplus(ab + dtb_ref[...])
        gc = _dot_split(chunk_tril, g, 3, split_rhs=True)
        yield
        gc = per_head_lanes(gc, 0)
        beta = per_head_lanes(jax.nn.sigmoid(ab), N_HEADS)
        for _ in range(PREP_STORE_DELAY):
            yield
        for j in tiles:
            q_scr[rows, j * LANES:(j + 1) * LANES] = q[j]
            k_scr[rows, j * LANES:(j + 1) * LANES] = k[j]
            v_scr[rows, j * LANES:(j + 1) * LANES] = y[:, cols(2, j)]
        g_scr[rows, :] = gc
        b_scr[rows, :] = beta

    ri = lax.broadcasted_iota(jnp.int32, (c, w), 0)
    ci = lax.broadcasted_iota(jnp.int32, (c, w), 1)
    cj = jnp.bitwise_and(ci, HEAD_DIM - 1)
    causal = ri >= cj
    strict = ri > cj
    eye4 = (ri == cj).astype(F32)
    same_sub = lax.shift_right_logical(ri, 4) == lax.shift_right_logical(cj, 4)
    bi_r = lax.broadcasted_iota(jnp.int32, (w, w), 0)
    bi_c = lax.broadcasted_iota(jnp.int32, (w, w), 1)
    same_head = lax.shift_right_logical(bi_r, 6) == lax.shift_right_logical(bi_c, 6)
    block_mask = same_head.astype(F32)
    chunk_tril = (same_head & (jnp.bitwise_and(bi_r, c - 1) >= jnp.bitwise_and(bi_c, c - 1))).astype(F32)
    head_of_lane = lax.shift_right_logical(lax.broadcasted_iota(jnp.int32, (DN_TILE, w), 1), 6)

    def per_head_lanes(x, first_lane):
        out = None
        for h in range(N_HEADS):
            col = jnp.broadcast_to(x[:, first_lane + h:first_lane + h + 1], (DN_TILE, w))
            out = col if out is None else jnp.where(head_of_lane == h, col, out)
        return out

    def per_head(a, b):
        bd = jnp.where(same_head, jnp.concatenate([b.astype(BF16)] * N_HEADS, axis=0), 0.0)
        return _dot(a.astype(BF16), bd)

    def group_rows(i):
        base = pl.multiple_of(i * (DN_GROUP * c), DN_GROUP * c)
        return [pl.ds(base + u * c, c) for u in range(DN_GROUP)]

    def solve_steps(i):
        rows = group_rows(i)
        each = lambda f, *lists: [f(*args) for args in zip(*lists)]
        q, k, v, gc, beta = ([ref[r, :] for r in rows] for ref in (q_scr, k_scr, v_scr, g_scr, b_scr))
        g_row = each(lambda g: jnp.sum(g * eye4, axis=0, keepdims=True), gc)
        decay = each(lambda g, gr: jnp.exp(jnp.where(causal, g - gr, NEG_INF)), gc, g_row)
        kb = each(jnp.multiply, k, beta)
        k_bd = each(lambda x: jnp.where(same_head, jnp.concatenate([x.astype(BF16)] * N_HEADS, axis=0), 0.0), k)
        lmat = each(lambda a, b, d: jnp.where(strict, _dot_nt(a.astype(BF16), b) * d, 0.0), kb, k_bd, decay)
        yield
        a_qk = each(lambda a, b, d: jnp.where(causal, _dot_nt(a.astype(BF16), b) * d, 0.0), q, k_bd, decay)
        yield
        def pair_head(a0, a1, b):
            both = per_head(jnp.concatenate([a0, a1], axis=0), b)
            return both[:c], both[c:]

        p = each(lambda l_: jnp.where(same_sub, -l_, 0.0), lmat)
        t_diag = each(lambda x: eye4 + x, p)
        p = each(per_head, p, p)
        yield
        for _ in range(2):
            nxt = each(pair_head, p, t_diag, p)
            p = [x[0] for x in nxt]
            t_diag = each(lambda t, x: t + x[1], t_diag, nxt)
            yield
        t_diag = each(lambda t, x: t + per_head(t, x), t_diag, p)
        yield
        m1 = each(lambda l_, t: per_head(jnp.where(same_sub, 0.0, l_), t), lmat, t_diag)
        yield
        nxt = each(pair_head, t_diag, m1, m1)
        y = each(lambda t, x: t - x[0], t_diag, nxt)
        yield
        t_inv = each(lambda yy, x: yy + per_head(yy, x[1]), y, nxt)
        yield
        eg = each(jnp.exp, gc)
        w_c = each(lambda t, a, e: per_head(t, a * e), t_inv, kb, eg)
        yield
        u_c = each(lambda t, a, b: per_head(t, a * b), t_inv, v, beta)
        yield
        q_dec = each(jnp.multiply, q, eg)
        k_dec = each(lambda x, g: x * jnp.exp(g[c - 1:c, :] - g), k, gc)
        for ref, vals in zip((w_scr, v_scr, a_scr, q_scr, k_scr), (w_c, u_c, a_qk, q_dec, k_dec)):
            for r, val in zip(rows, vals):
                ref[r, :] = val

    def state_steps(i):
        for rows in group_rows(i):
            state = s_scr[...]
            state_b = state.astype(BF16)
            w_s = _dot(w_scr[rows, :].astype(BF16), state_b)
            q_s = _dot(q_scr[rows, :].astype(BF16), state_b)
            yield
            v_new = v_scr[rows, :] - w_s
            o_intra = per_head(a_scr[rows, :], v_new)
            upd = lax.dot_general(k_scr[rows, :].astype(BF16), v_new.astype(BF16), (((0,), (0,)), ((), ())),
                                  preferred_element_type=F32)
            yield
            o_ref[rows, :] = q_s + o_intra
            s_scr[...] = state * jnp.exp(g_scr[rows, :][c - 1:c, :]) + upd * block_mask
            yield

    def weave(*steps):
        live = list(steps)
        while live:
            for gen in list(live):
                if next(gen, "done") == "done":
                    live.remove(gen)

    n_groups = seq // (DN_GROUP * c)
    s_scr[...] = jnp.zeros_like(s_scr)
    weave(prep_steps(0))
    weave(solve_steps(0), prep_steps(1))

    def group(i, carry):
        weave(solve_steps(i), state_steps(i - 1), prep_steps(i + 1))
        return carry
    lax.fori_loop(1, n_groups - 1, group, 0)
    weave(solve_steps(n_groups - 1), state_steps(n_groups - 2))
    weave(state_steps(n_groups - 1))

    def finish_tile(i, carry):
        rows = pl.ds(pl.multiple_of(i * DN_TILE, DN_TILE), DN_TILE)
        o = o_ref[rows, :]
        ms = _dot_split(o * o, block_mask, 2) * (1.0 / HEAD_DIM)
        gate = x_ref[rows, 3 * w + LANES:4 * w + LANES]
        o_ref[rows, :] = o * lax.rsqrt(ms + EPS) * on_ref[...] * (gate * jax.nn.sigmoid(gate))
        return carry
    lax.fori_loop(0, seq // DN_TILE, finish_tile, 0)


def _dn(slab, conv_w, a_log, dt_bias, o_norm, *, batch, seq):
    wide = lambda: pltpu.VMEM((seq, GROUP_W), F32)
    return pl.pallas_call(
        functools.partial(_dn_body, seq=seq),
        grid=(batch,),
        in_specs=[pl.BlockSpec((seq, DN_SLAB), lambda b: (0, b)),
                  _const_spec((DN_CONV, 3 * GROUP_W)), _const_spec((1, LANES)), _const_spec((1, LANES)),
                  _const_spec((1, GROUP_W))],
        out_specs=pl.BlockSpec((seq, GROUP_W), lambda b: (0, b)),
        out_shape=jax.ShapeDtypeStruct((seq, batch * GROUP_W), F32),
        scratch_shapes=[wide() for _ in range(7)] + [pltpu.VMEM((GROUP_W, GROUP_W), F32)],
        compiler_params=_params("parallel"),
        name="deltanet",
    )(slab, conv_w, a_log, dt_bias, o_norm)


def _row(v, width=None):
    v = v.astype(F32).reshape(1, -1)
    if width is not None and v.shape[1] < width:
        v = jnp.pad(v, ((0, 0), (0, width - v.shape[1])))
    return v


def _prep_w_in(w):
    w = w.astype(BF16)
    z = lambda n: jnp.zeros(w.shape[:2] + (n,), BF16)
    return jnp.concatenate([w[..., 0:400], z(48), w[..., 400:416], z(48), w[..., 416:2216], z(120),
                            w[..., 2216:2472]], axis=-1)


def _mla_head_lanes(x):
    half = MLA_ROPE // 2
    nope, rope = x[..., :MLA_NOPE], x[..., MLA_NOPE:]
    split = HEAD_DIM - half
    zeros = jnp.zeros(x.shape[:-1] + (LANES - MLA_DQK,), x.dtype)
    return jnp.concatenate([rope[..., :half], nope[..., :split], rope[..., half:], nope[..., split:], zeros], axis=-1)


def _pad_heads(w, per_head, width):
    k = w.shape[0]
    w = w.reshape(k, N_HEADS, per_head)[:, :, :width]
    w = jnp.pad(w, ((0, 0), (0, 0), (0, MLA_DQK - width)))
    return _mla_head_lanes(w).reshape(k, N_HEADS * LANES).astype(BF16)


def _rope_tables(seq):
    half = MLA_ROPE // 2
    pos = jnp.arange(seq, dtype=F32)
    freqs = ROPE_THETA ** (-jnp.arange(half, dtype=F32) / half)
    ang = pos[:, None] * freqs[None, :]
    cos, sin = jnp.cos(ang), jnp.sin(ang)
    ones = jnp.ones((seq, MLA_NOPE), F32)
    zeros = jnp.zeros((seq, MLA_NOPE), F32)
    return (_mla_head_lanes(jnp.concatenate([ones, cos, cos], axis=1)),
            _mla_head_lanes(jnp.concatenate([zeros, -sin, sin], axis=1)))


def _block_diag(blocks):
    g, r, c = blocks.shape
    eye = jnp.eye(g, dtype=blocks.dtype)
    return (blocks[:, :, None, :] * eye[:, None, :, None]).reshape(g * r, g * c)


def _mla_layer(slab, p, l, *, batch, seq):
    w_ukv = p["mla_w_ukv"][l]
    cos, sin = _rope_tables(seq)
    return _mla(slab, _row(p["mla_q_norm"][l]), _row(p["mla_kv_norm"][l]),
                _pad_heads(p["mla_w_uq"][l], MLA_DQK, MLA_DQK),
                _pad_heads(w_ukv, MLA_NOPE + HEAD_DIM, MLA_NOPE),
                w_ukv.reshape(MLA_KV_RANK, N_HEADS, MLA_NOPE + HEAD_DIM)[:, :, MLA_NOPE:]
                .reshape(MLA_KV_RANK, GROUP_W).T.astype(BF16),
                _mla_head_lanes(_row(p["mla_qk_q"][l])), _mla_head_lanes(_row(p["mla_qk_k"][l])), cos, sin,
                batch=batch, seq=seq)


def _dil_layer(slab, p, l, *, batch, seq):
    pair = lambda g: jnp.tile(g.astype(F32).reshape(1, HEAD_DIM), (1, LANES // HEAD_DIM))
    return _dil(slab, pair(p["dil_q_norm"][l]), pair(p["dil_k_norm"][l]), _dil_band(p["t5_bias"]),
                batch=batch, seq=seq)


def _dn_layer(slab, p, l, *, batch, seq):
    return _dn(slab, p["dn_conv"][l].astype(F32), _row(p["dn_a_log"][l], LANES), _row(p["dn_dt_bias"][l], LANES),
               jnp.tile(p["dn_o_norm"][l].astype(F32).reshape(1, HEAD_DIM), (1, N_HEADS)), batch=batch, seq=seq)


def _s5_layer(u, p, l, *, batch, seq):
    state_row = lambda v: v.astype(F32).reshape(1, S5_WIDTH)
    ldt = jnp.broadcast_to(p["s5_log_dt"][l][:, None], (S5_GROUPS, S5_STATE))
    bre = _block_diag(jnp.swapaxes(p["s5_b_re"][l], 1, 2).astype(F32))
    bim = _block_diag(jnp.swapaxes(p["s5_b_im"][l], 1, 2).astype(F32))
    cre = _block_diag(jnp.swapaxes(p["s5_c_re"][l], 1, 2)).astype(BF16)
    cim = _block_diag(jnp.swapaxes(p["s5_c_im"][l], 1, 2)).astype(BF16)
    return _s5(u, state_row(p["s5_lambda_re"][l]), state_row(p["s5_lambda_im"][l]), state_row(ldt),
               bre, bim, cre, cim, _row(p["s5_d"][l]), p["s5_w_glu"][l].astype(BF16), batch=batch, seq=seq,
               steps=min(256, seq))


def kernel(x, attn_norm, w_in, w_out, mla_q_norm, mla_kv_norm, mla_w_uq, mla_w_ukv, mla_qk_q, mla_qk_k,
           s5_lambda_re, s5_lambda_im, s5_log_dt, s5_b_re, s5_b_im, s5_c_re, s5_c_im, s5_d, s5_w_glu,
           dil_q_norm, dil_k_norm, t5_bias, dn_conv, dn_a_log, dn_dt_bias, dn_o_norm,
           ffn_norm, ffn_w1, ffn_w3, ffn_w2):
    p = dict(mla_q_norm=mla_q_norm, mla_kv_norm=mla_kv_norm, mla_w_uq=mla_w_uq, mla_w_ukv=mla_w_ukv,
             mla_qk_q=mla_qk_q, mla_qk_k=mla_qk_k, s5_lambda_re=s5_lambda_re, s5_lambda_im=s5_lambda_im,
             s5_log_dt=s5_log_dt, s5_b_re=s5_b_re, s5_b_im=s5_b_im, s5_c_re=s5_c_re, s5_c_im=s5_c_im, s5_d=s5_d,
             s5_w_glu=s5_w_glu, dil_q_norm=dil_q_norm, dil_k_norm=dil_k_norm, t5_bias=t5_bias, dn_conv=dn_conv,
             dn_a_log=dn_a_log, dn_dt_bias=dn_dt_bias, dn_o_norm=dn_o_norm)
    batch, seq, _ = x.shape
    h = x.reshape(batch * seq, D_MODEL)
    w_in_b, w_out_b = _prep_w_in(w_in), w_out.astype(BF16)
    w1_b, w3_b, w2_b = ffn_w1.astype(BF16), ffn_w3.astype(BF16), ffn_w2.astype(BF16)
    for l in range(attn_norm.shape[0]):
        mla_in, s5_in, dil_in, dn_in = _proj(h, _row(attn_norm[l]), w_in_b, l, batch=batch, seq=seq)
        ys = [_mla_layer(mla_in, p, l, batch=batch, seq=seq),
              _s5_layer(s5_in, p, l, batch=batch, seq=seq),
              _dil_layer(dil_in, p, l, batch=batch, seq=seq),
              _dn_layer(dn_in, p, l, batch=batch, seq=seq)]
        h = _out_ffn(h, ys, w_out_b, _row(ffn_norm[l]), w1_b, w3_b, w2_b, l, batch=batch, seq=seq)
    return h.reshape(batch, seq, D_MODEL)
```
